```python
import math
import jax
import jax.numpy as jnp
from jax import lax
import numpy as np

D_MODEL = 1024
BATCH = 16
SEQ = 2048
DEPTH = 2

N_META = 16
CHUNK = 128
EPS = 1e-6

SSD_HEADS = 16
SSD_HEAD_DIM = 64
SSD_INNER = SSD_HEADS * SSD_HEAD_DIM
SSD_GROUPS = 4
SSD_STATE = 128
SSD_CONV = 4
SSD_XBC = SSD_INNER + 2 * SSD_GROUPS * SSD_STATE

RET_HEADS = 4
RET_QK_DIM = 256
RET_V_DIM = 256
RET_QK_WIDTH = RET_HEADS * RET_QK_DIM
RET_WIDTH = RET_HEADS * RET_V_DIM

SB_HEADS = 16
SB_HEAD_DIM = 64
SB_WIDTH = SB_HEADS * SB_HEAD_DIM

LRU_WIDTH = 1024
LRU_BLOCKS = 8
LRU_BLOCK = LRU_WIDTH // LRU_BLOCKS
LRU_CONV = 4
LRU_C = 8.0

FFN_DIM = 2816
FFN_CONV = 3

MIX0_IN = SSD_INNER + SSD_XBC + SSD_HEADS + 2 * RET_QK_WIDTH + 2 * RET_WIDTH
MIX0_OUT = SSD_INNER + RET_WIDTH
MIX1_IN = 3 * SB_WIDTH + 2 * LRU_WIDTH
MIX1_OUT = SB_WIDTH + LRU_WIDTH

kernel_name = 'hybrid_ssd_retention_stickbreak_rglru_block'


def rmsnorm(x, g):
    xf = x.astype(jnp.float32)
    y = xf * lax.rsqrt(jnp.mean(xf * xf, -1, keepdims=True) + EPS)
    return (y * g).astype(x.dtype)


def causal_dwconv(x, w, b):
    K, C = w.shape
    y = lax.conv_general_dilated(x, w[:, None, :], window_strides=(1,), padding=[(K - 1, 0)],
                                 dimension_numbers=('NWC', 'WIO', 'NWC'), feature_group_count=C)
    return y + b


def split_cols(u, sizes):
    return jnp.split(u, np.cumsum(sizes)[:-1].tolist(), axis=-1)


def front_pad(t, pad):
    return jnp.pad(t, ((0, 0), (pad, 0)) + ((0, 0),) * (t.ndim - 2))


def segsum(a):
    L = a.shape[-1]
    cs = jnp.cumsum(a, -1)
    seg = cs[..., :, None] - cs[..., None, :]
    return jnp.where(jnp.tril(jnp.ones((L, L), bool)), seg, -jnp.inf)


def ssd_group(z, xbc, dt_raw, conv_w, conv_b, dt_bias, a_log, d_skip, norm_g):
    Bsz, T, _ = xbc.shape
    f32 = jnp.float32
    E = SSD_HEADS // SSD_GROUPS
    xbc = jax.nn.silu(causal_dwconv(xbc, conv_w, conv_b)).astype(f32)
    xs, bm, cm = jnp.split(xbc, [SSD_INNER, SSD_INNER + SSD_GROUPS * SSD_STATE], -1)
    dt = jax.nn.softplus(dt_raw.astype(f32) + dt_bias.astype(f32))
    a = -jnp.exp(a_log.astype(f32))
    pad = (-T) % CHUNK
    P = T + pad
    nc = P // CHUNK
    X = front_pad(xs * jnp.repeat(dt, SSD_HEAD_DIM, -1), pad).reshape(Bsz, nc, CHUNK, SSD_GROUPS, E, SSD_HEAD_DIM)
    Bm = front_pad(bm, pad).reshape(Bsz, nc, CHUNK, SSD_GROUPS, SSD_STATE)
    Cm = front_pad(cm, pad).reshape(Bsz, nc, CHUNK, SSD_GROUPS, SSD_STATE)
    a_dt = front_pad(dt * a, pad).reshape(Bsz, nc, CHUNK, SSD_GROUPS, E).transpose(0, 3, 4, 1, 2)
    a_cs = jnp.cumsum(a_dt, -1)
    cb = jnp.einsum('bclgn,bcsgn->bgcls', Cm, Bm)
    y_diag = jnp.einsum('bgecls,bcsgep->bclgep', cb[:, :, None] * jnp.exp(segsum(a_dt)), X)
    decay_states = jnp.exp(a_cs[..., -1:] - a_cs).transpose(0, 3, 4, 1, 2)
    states = jnp.einsum('bclgn,bclgep->bcgepn', Bm, X * decay_states[..., None])
    chunk_tot = jnp.pad(a_cs[..., -1], ((0, 0), (0, 0), (0, 0), (1, 0)))
    decay_chunk = jnp.exp(segsum(chunk_tot))
    states = jnp.concatenate([jnp.zeros_like(states[:, :1]), states], 1)
    prev = jnp.einsum('bgezc,bcgepn->bzgepn', decay_chunk, states)[:, :-1]
    y_off = jnp.einsum('bclgn,bcgepn->bclgep', Cm, prev) * jnp.exp(a_cs).transpose(0, 3, 4, 1, 2)[..., None]
    y = (y_diag + y_off).reshape(Bsz, P, SSD_INNER)[:, pad:]
    y = y + xs * jnp.repeat(d_skip.astype(f32), SSD_HEAD_DIM)
    y = (y * jax.nn.silu(z.astype(f32))).reshape(Bsz, T, SSD_GROUPS, SSD_INNER // SSD_GROUPS)
    y = y * lax.rsqrt(jnp.mean(y * y, -1, keepdims=True) + EPS)
    return (y.reshape(Bsz, T, SSD_INNER) * norm_g).astype(z.dtype)


def rotate_retnet(x, pos):
    half = x.shape[-1] // 2
    inv_freq = 1.0 / (10000.0 ** (jnp.arange(half, dtype=jnp.float32) / (half - 1)))
    ang = pos[:, None] * inv_freq[None, :]
    cos = jnp.cos(ang)[None, :, None, :]
    sin = jnp.sin(ang)[None, :, None, :]
    x1, x2 = x[..., :half], x[..., half:]
    return jnp.concatenate([x1 * cos - x2 * sin, x1 * sin + x2 * cos], -1)


def retention_group(q, k, v, g, norm_g):
    Bsz, T, _ = q.shape
    f32 = jnp.float32
    pos = jnp.arange(T, dtype=f32)
    q = rotate_retnet(q.astype(f32).reshape(Bsz, T, RET_HEADS, RET_QK_DIM), pos)
    k = rotate_retnet(k.astype(f32).reshape(Bsz, T, RET_HEADS, RET_QK_DIM), pos) * RET_QK_DIM ** -0.5
    v = v.astype(f32).reshape(Bsz, T, RET_HEADS, RET_V_DIM)
    pad = (-T) % CHUNK
    P = T + pad
    nc = P // CHUNK
    q = front_pad(q, pad).reshape(Bsz, nc, CHUNK, RET_HEADS, RET_QK_DIM)
    k = front_pad(k, pad).reshape(Bsz, nc, CHUNK, RET_HEADS, RET_QK_DIM)
    v = front_pad(v, pad).reshape(Bsz, nc, CHUNK, RET_HEADS, RET_V_DIM)
    log_gamma = jnp.log1p(-jnp.exp2(-5.0 - jnp.arange(RET_HEADS, dtype=f32)))
    idx = jnp.arange(CHUNK, dtype=f32)
    diff = idx[:, None] - idx[None, :]
    decay = jnp.where(diff >= 0, jnp.exp(log_gamma[:, None, None] * jnp.maximum(diff, 0.0)), 0.0)
    scores = jnp.einsum('bclhd,bcshd->bhcls', q, k) * decay[None, :, None]
    inner = jnp.einsum('bhcls,bcshe->bclhe', scores, v)
    zeta = jnp.exp(log_gamma[None, :] * (CHUNK - 1 - idx)[:, None])
    kv = jnp.einsum('bclhd,bclhe->bchde', k * zeta[..., None], v)
    chunk_decay = jnp.exp(CHUNK * log_gamma)[None, :, None, None]

    def step(R, kv_c):
        return chunk_decay * R + kv_c, R

    _, R_prev = lax.scan(step, jnp.zeros_like(kv[:, 0]), jnp.moveaxis(kv, 1, 0))
    R_prev = jnp.moveaxis(R_prev, 0, 1)
    xi = jnp.exp(log_gamma[None, :] * (idx + 1.0)[:, None])
    cross = jnp.einsum('bclhd,bchde->bclhe', q, R_prev) * xi[..., None]
    o = (inner + cross).reshape(Bsz, P, RET_HEADS, RET_V_DIM)[:, pad:]
    o = o - jnp.mean(o, -1, keepdims=True)
    o = o * lax.rsqrt(jnp.mean(o * o, -1, keepdims=True) + EPS)
    o = o.reshape(Bsz, T, RET_WIDTH) * norm_g
    return (jax.nn.silu(g.astype(f32)) * o).astype(g.dtype)


def stick_breaking_attention(q, k, v):
    Bsz, T, H, D = q.shape
    f32 = jnp.float32
    scale = D ** -0.5
    bounds = [(0, N_META)] + [(s, min(s + CHUNK, T)) for s in range(N_META, T, CHUNK)]
    outs = []
    for s0, s1 in bounds:
        z = jnp.einsum('bqhd,bkhd->bhqk', q[:, s0:s1].astype(f32), k[:, :s1].astype(f32)) * scale
        strict = jnp.arange(s1)[None, :] < jnp.arange(s0, s1)[:, None]
        log_1m = jnp.where(strict, jax.nn.log_sigmoid(-z), 0.0)
        after = lax.cumsum(log_1m, axis=3, reverse=True) - log_1m
        w = jnp.where(strict, jnp.exp(jax.nn.log_sigmoid(z) + after), 0.0)
        outs.append(jnp.einsum('bhqk,bkhd->bqhd', w, v[:, :s1].astype(f32)))
    return jnp.concatenate(outs, 1)


def rg_lru(x, w_a, b_a, w_x, b_x, lam):
    Bsz, T, W = x.shape
    f32 = jnp.float32
    x = x.astype(f32)
    xb = x.reshape(Bsz, T, LRU_BLOCKS, LRU_BLOCK)
    r = jax.nn.sigmoid(jnp.einsum('btni,nij->btnj', xb, w_a.astype(f32)).reshape(Bsz, T, W) + b_a)
    i = jax.nn.sigmoid(jnp.einsum('btni,nij->btnj', xb, w_x.astype(f32)).reshape(Bsz, T, W) + b_x)
    log_a = -LRU_C * r * jax.nn.softplus(-lam.astype(f32))
    a = jnp.exp(log_a)
    b = jnp.sqrt(jnp.maximum(-jnp.expm1(2.0 * log_a), 0.0)) * (i * x)

    def combine(left, right):
        a_l, b_l = left
        a_r, b_r = right
        return a_l * a_r, a_r * b_l + b_r

    _, hs = lax.associative_scan(combine, (a, b), axis=1)
    return hs


def ssd_retention_mixer(h, w_in, ssd_conv_w, ssd_conv_b, ssd_dt_bias, ssd_a_log, ssd_d, ssd_norm, ret_norm, w_out):
    u = h @ w_in
    z, xbc, dt_raw, q, k, v, g = split_cols(u, (SSD_INNER, SSD_XBC, SSD_HEADS, RET_QK_WIDTH, RET_QK_WIDTH, RET_WIDTH, RET_WIDTH))
    y_ssd = ssd_group(z, xbc, dt_raw, ssd_conv_w, ssd_conv_b, ssd_dt_bias, ssd_a_log, ssd_d, ssd_norm)
    y_ret = retention_group(q, k, v, g, ret_norm)
    return jnp.concatenate([y_ssd, y_ret], -1) @ w_out


def sb_lru_mixer(h, w_in, lru_conv_w, lru_conv_b, lru_wa, lru_ba, lru_wx, lru_bx, lru_lambda, w_out):
    Bsz, T, _ = h.shape
    u = h @ w_in
    q, k, v, gate, xr = split_cols(u, (SB_WIDTH, SB_WIDTH, SB_WIDTH, LRU_WIDTH, LRU_WIDTH))
    shp = (Bsz, T, SB_HEADS, SB_HEAD_DIM)
    y_sb = stick_breaking_attention(q.reshape(shp), k.reshape(shp), v.reshape(shp)).reshape(Bsz, T, SB_WIDTH)
    xr = causal_dwconv(xr, lru_conv_w, lru_conv_b)
    y_lru = rg_lru(xr, lru_wa, lru_ba, lru_wx, lru_bx, lru_lambda) * jax.nn.gelu(gate.astype(jnp.float32))
    return jnp.concatenate([y_sb.astype(h.dtype), y_lru.astype(h.dtype)], -1) @ w_out


def conv_ffn(h, w_in, conv_w, conv_b, w_out):
    u = causal_dwconv(h @ w_in, conv_w, conv_b)
    g, up = jnp.split(u, 2, -1)
    return (jax.nn.silu(g) * up) @ w_out


def _fwd_setup_inputs(seed: int = 0) -> dict:
    key = jax.random.key(seed)
    ks = iter(jax.random.split(key, 64))
    f32 = jnp.float32

    def nrm(shape, scale):
        return jax.random.normal(next(ks), shape, f32) * scale

    def gain(n):
        return 1.0 + nrm((n,), 0.02)

    dt0 = jnp.exp(jax.random.uniform(next(ks), (SSD_HEADS,), f32, math.log(1e-3), math.log(1e-1)))
    dt_bias = dt0 + jnp.log(-jnp.expm1(-dt0))
    a_log = jnp.log(jax.random.uniform(next(ks), (SSD_HEADS,), f32, 1.0, 16.0))
    a_lru = jax.random.uniform(next(ks), (LRU_WIDTH,), f32, 0.9, 0.999) ** (1.0 / LRU_C)
    lam = jnp.log(a_lru) - jnp.log1p(-a_lru)
    return {
        'x': nrm((BATCH, SEQ, D_MODEL), 1.0),
        'meta_tokens': nrm((N_META, D_MODEL), 1.0),
        'l0_mix_norm': gain(D_MODEL),
        'l0_w_in': nrm((D_MODEL, MIX0_IN), D_MODEL ** -0.5),
        'l0_ssd_conv_w': nrm((SSD_CONV, SSD_XBC), SSD_CONV ** -0.5),
        'l0_ssd_conv_b': nrm((SSD_XBC,), 0.01),
        'l0_ssd_dt_bias': dt_bias,
        'l0_ssd_a_log': a_log,
        'l0_ssd_d': gain(SSD_HEADS),
        'l0_ssd_norm': gain(SSD_INNER),
        'l0_ret_norm': gain(RET_WIDTH),
        'l0_w_out': nrm((MIX0_OUT, D_MODEL), MIX0_OUT ** -0.5),
        'l0_ffn_norm': gain(D_MODEL),
        'l0_ffn_w_in': nrm((D_MODEL, 2 * FFN_DIM), D_MODEL ** -0.5),
        'l0_ffn_conv_w': nrm((FFN_CONV, 2 * FFN_DIM), FFN_CONV ** -0.5),
        'l0_ffn_conv_b': nrm((2 * FFN_DIM,), 0.01),
        'l0_ffn_w_out': nrm((FFN_DIM, D_MODEL), FFN_DIM ** -0.5),
        'l1_mix_norm': gain(D_MODEL),
        'l1_w_in': nrm((D_MODEL, MIX1_IN), D_MODEL ** -0.5),
        'l1_lru_conv_w': nrm((LRU_CONV, LRU_WIDTH), LRU_CONV ** -0.5),
        'l1_lru_conv_b': nrm((LRU_WIDTH,), 0.01),
        'l1_lru_wa': nrm((LRU_BLOCKS, LRU_BLOCK, LRU_BLOCK), LRU_BLOCK ** -0.5),
        'l1_lru_ba': nrm((LRU_WIDTH,), 0.01),
        'l1_lru_wx': nrm((LRU_BLOCKS, LRU_BLOCK, LRU_BLOCK), LRU_BLOCK ** -0.5),
        'l1_lru_bx': nrm((LRU_WIDTH,), 0.01),
        'l1_lru_lambda': lam,
        'l1_w_out': nrm((MIX1_OUT, D_MODEL), MIX1_OUT ** -0.5),
        'l1_ffn_norm': gain(D_MODEL),
        'l1_ffn_w_in': nrm((D_MODEL, 2 * FFN_DIM), D_MODEL ** -0.5),
        'l1_ffn_conv_w': nrm((FFN_CONV, 2 * FFN_DIM), FFN_CONV ** -0.5),
        'l1_ffn_conv_b': nrm((2 * FFN_DIM,), 0.01),
        'l1_ffn_w_out': nrm((FFN_DIM, D_MODEL), FFN_DIM ** -0.5),
        'final_norm': gain(D_MODEL),
    }


def _fwd_reference(x, meta_tokens, l0_mix_norm, l0_w_in, l0_ssd_conv_w, l0_ssd_conv_b, l0_ssd_dt_bias, l0_ssd_a_log,
              l0_ssd_d, l0_ssd_norm, l0_ret_norm, l0_w_out, l0_ffn_norm, l0_ffn_w_in, l0_ffn_conv_w, l0_ffn_conv_b,
              l0_ffn_w_out, l1_mix_norm, l1_w_in, l1_lru_conv_w, l1_lru_conv_b, l1_lru_wa, l1_lru_ba, l1_lru_wx,
              l1_lru_bx, l1_lru_lambda, l1_w_out, l1_ffn_norm, l1_ffn_w_in, l1_ffn_conv_w, l1_ffn_conv_b,
              l1_ffn_w_out, final_norm):
    Bsz = x.shape[0]
    meta = jnp.broadcast_to(meta_tokens[None].astype(x.dtype), (Bsz, N_META, D_MODEL))
    h = jnp.concatenate([meta, x], 1)
    even_layers = [(l0_mix_norm,
                    (l0_w_in, l0_ssd_conv_w, l0_ssd_conv_b, l0_ssd_dt_bias, l0_ssd_a_log, l0_ssd_d, l0_ssd_norm, l0_ret_norm, l0_w_out),
                    l0_ffn_norm, (l0_ffn_w_in, l0_ffn_conv_w, l0_ffn_conv_b, l0_ffn_w_out))]
    odd_layers = [(l1_mix_norm,
                   (l1_w_in, l1_lru_conv_w, l1_lru_conv_b, l1_lru_wa, l1_lru_ba, l1_lru_wx, l1_lru_bx, l1_lru_lambda, l1_w_out),
                   l1_ffn_norm, (l1_ffn_w_in, l1_ffn_conv_w, l1_ffn_conv_b, l1_ffn_w_out))]
    for layer in range(DEPTH):
        if layer % 2 == 0:
            mix_norm, mix_params, ffn_norm, ffn_params = even_layers[layer // 2]
            h = h + ssd_retention_mixer(rmsnorm(h, mix_norm), *mix_params)
        else:
            mix_norm, mix_params, ffn_norm, ffn_params = odd_layers[layer // 2]
            h = h + sb_lru_mixer(rmsnorm(h, mix_norm), *mix_params)
        h = h + conv_ffn(rmsnorm(h, ffn_norm), *ffn_params)
    return rmsnorm(h, final_norm)[:, N_META:]


import jax as _jax
import jax.numpy as _jnp

TWIN_FORMAT = 'train_step'
FWD_PARAMS = ['x', 'meta_tokens', 'l0_mix_norm', 'l0_w_in', 'l0_ssd_conv_w', 'l0_ssd_conv_b', 'l0_ssd_dt_bias', 'l0_ssd_a_log', 'l0_ssd_d', 'l0_ssd_norm', 'l0_ret_norm', 'l0_w_out', 'l0_ffn_norm', 'l0_ffn_w_in', 'l0_ffn_conv_w', 'l0_ffn_conv_b', 'l0_ffn_w_out', 'l1_mix_norm', 'l1_w_in', 'l1_lru_conv_w', 'l1_lru_conv_b', 'l1_lru_wa', 'l1_lru_ba', 'l1_lru_wx', 'l1_lru_bx', 'l1_lru_lambda', 'l1_w_out', 'l1_ffn_norm', 'l1_ffn_w_in', 'l1_ffn_conv_w', 'l1_ffn_conv_b', 'l1_ffn_w_out', 'final_norm']
TWIN_WEIGHTS = ['meta_tokens', 'l0_mix_norm', 'l0_w_in', 'l0_ssd_conv_w', 'l0_ssd_conv_b', 'l0_ssd_dt_bias', 'l0_ssd_a_log', 'l0_ssd_d', 'l0_ssd_norm', 'l0_ret_norm', 'l0_w_out', 'l0_ffn_norm', 'l0_ffn_w_in', 'l0_ffn_conv_w', 'l0_ffn_conv_b', 'l0_ffn_w_out', 'l1_mix_norm', 'l1_w_in', 'l1_lru_conv_w', 'l1_lru_conv_b', 'l1_lru_wa', 'l1_lru_ba', 'l1_lru_wx', 'l1_lru_bx', 'l1_lru_lambda', 'l1_w_out', 'l1_ffn_norm', 'l1_ffn_w_in', 'l1_ffn_conv_w', 'l1_ffn_conv_b', 'l1_ffn_w_out', 'final_norm']
TWIN_DIFF_INPUT = 'x'
TWIN_INPUTS = ['x', 'meta_tokens', 'l0_mix_norm', 'l0_w_in', 'l0_ssd_conv_w', 'l0_ssd_conv_b', 'l0_ssd_dt_bias', 'l0_ssd_a_log', 'l0_ssd_d', 'l0_ssd_norm', 'l0_ret_norm', 'l0_w_out', 'l0_ffn_norm', 'l0_ffn_w_in', 'l0_ffn_conv_w', 'l0_ffn_conv_b', 'l0_ffn_w_out', 'l1_mix_norm', 'l1_w_in', 'l1_lru_conv_w', 'l1_lru_conv_b', 'l1_lru_wa', 'l1_lru_ba', 'l1_lru_wx', 'l1_lru_bx', 'l1_lru_lambda', 'l1_w_out', 'l1_ffn_norm', 'l1_ffn_w_in', 'l1_ffn_conv_w', 'l1_ffn_conv_b', 'l1_ffn_w_out', 'final_norm', 'loss_target', 'm_meta_tokens', 'm_l0_mix_norm', 'm_l0_w_in', 'm_l0_ssd_conv_w', 'm_l0_ssd_conv_b', 'm_l0_ssd_dt_bias', 'm_l0_ssd_a_log', 'm_l0_ssd_d', 'm_l0_ssd_norm', 'm_l0_ret_norm', 'm_l0_w_out', 'm_l0_ffn_norm', 'm_l0_ffn_w_in', 'm_l0_ffn_conv_w', 'm_l0_ffn_conv_b', 'm_l0_ffn_w_out', 'm_l1_mix_norm', 'm_l1_w_in', 'm_l1_lru_conv_w', 'm_l1_lru_conv_b', 'm_l1_lru_wa', 'm_l1_lru_ba', 'm_l1_lru_wx', 'm_l1_lru_bx', 'm_l1_lru_lambda', 'm_l1_w_out', 'm_l1_ffn_norm', 'm_l1_ffn_w_in', 'm_l1_ffn_conv_w', 'm_l1_ffn_conv_b', 'm_l1_ffn_w_out', 'm_final_norm', 'v_meta_tokens', 'v_l0_mix_norm', 'v_l0_w_in', 'v_l0_ssd_conv_w', 'v_l0_ssd_conv_b', 'v_l0_ssd_dt_bias', 'v_l0_ssd_a_log', 'v_l0_ssd_d', 'v_l0_ssd_norm', 'v_l0_ret_norm', 'v_l0_w_out', 'v_l0_ffn_norm', 'v_l0_ffn_w_in', 'v_l0_ffn_conv_w', 'v_l0_ffn_conv_b', 'v_l0_ffn_w_out', 'v_l1_mix_norm', 'v_l1_w_in', 'v_l1_lru_conv_w', 'v_l1_lru_conv_b', 'v_l1_lru_wa', 'v_l1_lru_ba', 'v_l1_lru_wx', 'v_l1_lru_bx', 'v_l1_lru_lambda', 'v_l1_w_out', 'v_l1_ffn_norm', 'v_l1_ffn_w_in', 'v_l1_ffn_conv_w', 'v_l1_ffn_conv_b', 'v_l1_ffn_w_out', 'v_final_norm']
TWIN_OUTPUTS = ['loss', 'grad_x', 'grad_meta_tokens', 'grad_l0_mix_norm', 'grad_l0_w_in', 'grad_l0_ssd_conv_w', 'grad_l0_ssd_conv_b', 'grad_l0_ssd_dt_bias', 'grad_l0_ssd_a_log', 'grad_l0_ssd_d', 'grad_l0_ssd_norm', 'grad_l0_ret_norm', 'grad_l0_w_out', 'grad_l0_ffn_norm', 'grad_l0_ffn_w_in', 'grad_l0_ffn_conv_w', 'grad_l0_ffn_conv_b', 'grad_l0_ffn_w_out', 'grad_l1_mix_norm', 'grad_l1_w_in', 'grad_l1_lru_conv_w', 'grad_l1_lru_conv_b', 'grad_l1_lru_wa', 'grad_l1_lru_ba', 'grad_l1_lru_wx', 'grad_l1_lru_bx', 'grad_l1_lru_lambda', 'grad_l1_w_out', 'grad_l1_ffn_norm', 'grad_l1_ffn_w_in', 'grad_l1_ffn_conv_w', 'grad_l1_ffn_conv_b', 'grad_l1_ffn_w_out', 'grad_final_norm', 'delta_meta_tokens', 'delta_l0_mix_norm', 'delta_l0_w_in', 'delta_l0_ssd_conv_w', 'delta_l0_ssd_conv_b', 'delta_l0_ssd_dt_bias', 'delta_l0_ssd_a_log', 'delta_l0_ssd_d', 'delta_l0_ssd_norm', 'delta_l0_ret_norm', 'delta_l0_w_out', 'delta_l0_ffn_norm', 'delta_l0_ffn_w_in', 'delta_l0_ffn_conv_w', 'delta_l0_ffn_conv_b', 'delta_l0_ffn_w_out', 'delta_l1_mix_norm', 'delta_l1_w_in', 'delta_l1_lru_conv_w', 'delta_l1_lru_conv_b', 'delta_l1_lru_wa', 'delta_l1_lru_ba', 'delta_l1_lru_wx', 'delta_l1_lru_bx', 'delta_l1_lru_lambda', 'delta_l1_w_out', 'delta_l1_ffn_norm', 'delta_l1_ffn_w_in', 'delta_l1_ffn_conv_w', 'delta_l1_ffn_conv_b', 'delta_l1_ffn_w_out', 'delta_final_norm', 'new_m_meta_tokens', 'new_m_l0_mix_norm', 'new_m_l0_w_in', 'new_m_l0_ssd_conv_w', 'new_m_l0_ssd_conv_b', 'new_m_l0_ssd_dt_bias', 'new_m_l0_ssd_a_log', 'new_m_l0_ssd_d', 'new_m_l0_ssd_norm', 'new_m_l0_ret_norm', 'new_m_l0_w_out', 'new_m_l0_ffn_norm', 'new_m_l0_ffn_w_in', 'new_m_l0_ffn_conv_w', 'new_m_l0_ffn_conv_b', 'new_m_l0_ffn_w_out', 'new_m_l1_mix_norm', 'new_m_l1_w_in', 'new_m_l1_lru_conv_w', 'new_m_l1_lru_conv_b', 'new_m_l1_lru_wa', 'new_m_l1_lru_ba', 'new_m_l1_lru_wx', 'new_m_l1_lru_bx', 'new_m_l1_lru_lambda', 'new_m_l1_w_out', 'new_m_l1_ffn_norm', 'new_m_l1_ffn_w_in', 'new_m_l1_ffn_conv_w', 'new_m_l1_ffn_conv_b', 'new_m_l1_ffn_w_out', 'new_m_final_norm', 'new_v_meta_tokens', 'new_v_l0_mix_norm', 'new_v_l0_w_in', 'new_v_l0_ssd_conv_w', 'new_v_l0_ssd_conv_b', 'new_v_l0_ssd_dt_bias', 'new_v_l0_ssd_a_log', 'new_v_l0_ssd_d', 'new_v_l0_ssd_norm', 'new_v_l0_ret_norm', 'new_v_l0_w_out', 'new_v_l0_ffn_norm', 'new_v_l0_ffn_w_in', 'new_v_l0_ffn_conv_w', 'new_v_l0_ffn_conv_b', 'new_v_l0_ffn_w_out', 'new_v_l1_mix_norm', 'new_v_l1_w_in', 'new_v_l1_lru_conv_w', 'new_v_l1_lru_conv_b', 'new_v_l1_lru_wa', 'new_v_l1_lru_ba', 'new_v_l1_lru_wx', 'new_v_l1_lru_bx', 'new_v_l1_lru_lambda', 'new_v_l1_w_out', 'new_v_l1_ffn_norm', 'new_v_l1_ffn_w_in', 'new_v_l1_ffn_conv_w', 'new_v_l1_ffn_conv_b', 'new_v_l1_ffn_w_out', 'new_v_final_norm']
TWIN_LEAF_KINDS = {'loss': 'loss', 'grad_x': 'grad_x', 'grad_meta_tokens': 'grad_w', 'grad_l0_mix_norm': 'grad_w', 'grad_l0_w_in': 'grad_w', 'grad_l0_ssd_conv_w': 'grad_w', 'grad_l0_ssd_conv_b': 'grad_w', 'grad_l0_ssd_dt_bias': 'grad_w', 'grad_l0_ssd_a_log': 'grad_w', 'grad_l0_ssd_d': 'grad_w', 'grad_l0_ssd_norm': 'grad_w', 'grad_l0_ret_norm': 'grad_w', 'grad_l0_w_out': 'grad_w', 'grad_l0_ffn_norm': 'grad_w', 'grad_l0_ffn_w_in': 'grad_w', 'grad_l0_ffn_conv_w': 'grad_w', 'grad_l0_ffn_conv_b': 'grad_w', 'grad_l0_ffn_w_out': 'grad_w', 'grad_l1_mix_norm': 'grad_w', 'grad_l1_w_in': 'grad_w', 'grad_l1_lru_conv_w': 'grad_w', 'grad_l1_lru_conv_b': 'grad_w', 'grad_l1_lru_wa': 'grad_w', 'grad_l1_lru_ba': 'grad_w', 'grad_l1_lru_wx': 'grad_w', 'grad_l1_lru_bx': 'grad_w', 'grad_l1_lru_lambda': 'grad_w', 'grad_l1_w_out': 'grad_w', 'grad_l1_ffn_norm': 'grad_w', 'grad_l1_ffn_w_in': 'grad_w', 'grad_l1_ffn_conv_w': 'grad_w', 'grad_l1_ffn_conv_b': 'grad_w', 'grad_l1_ffn_w_out': 'grad_w', 'grad_final_norm': 'grad_w', 'delta_meta_tokens': 'delta_w', 'delta_l0_mix_norm': 'delta_w', 'delta_l0_w_in': 'delta_w', 'delta_l0_ssd_conv_w': 'delta_w', 'delta_l0_ssd_conv_b': 'delta_w', 'delta_l0_ssd_dt_bias': 'delta_w', 'delta_l0_ssd_a_log': 'delta_w', 'delta_l0_ssd_d': 'delta_w', 'delta_l0_ssd_norm': 'delta_w', 'delta_l0_ret_norm': 'delta_w', 'delta_l0_w_out': 'delta_w', 'delta_l0_ffn_norm': 'delta_w', 'delta_l0_ffn_w_in': 'delta_w', 'delta_l0_ffn_conv_w': 'delta_w', 'delta_l0_ffn_conv_b': 'delta_w', 'delta_l0_ffn_w_out': 'delta_w', 'delta_l1_mix_norm': 'delta_w', 'delta_l1_w_in': 'delta_w', 'delta_l1_lru_conv_w': 'delta_w', 'delta_l1_lru_conv_b': 'delta_w', 'delta_l1_lru_wa': 'delta_w', 'delta_l1_lru_ba': 'delta_w', 'delta_l1_lru_wx': 'delta_w', 'delta_l1_lru_bx': 'delta_w', 'delta_l1_lru_lambda': 'delta_w', 'delta_l1_w_out': 'delta_w', 'delta_l1_ffn_norm': 'delta_w', 'delta_l1_ffn_w_in': 'delta_w', 'delta_l1_ffn_conv_w': 'delta_w', 'delta_l1_ffn_conv_b': 'delta_w', 'delta_l1_ffn_w_out': 'delta_w', 'delta_final_norm': 'delta_w', 'new_m_meta_tokens': 'new_m', 'new_m_l0_mix_norm': 'new_m', 'new_m_l0_w_in': 'new_m', 'new_m_l0_ssd_conv_w': 'new_m', 'new_m_l0_ssd_conv_b': 'new_m', 'new_m_l0_ssd_dt_bias': 'new_m', 'new_m_l0_ssd_a_log': 'new_m', 'new_m_l0_ssd_d': 'new_m', 'new_m_l0_ssd_norm': 'new_m', 'new_m_l0_ret_norm': 'new_m', 'new_m_l0_w_out': 'new_m', 'new_m_l0_ffn_norm': 'new_m', 'new_m_l0_ffn_w_in': 'new_m', 'new_m_l0_ffn_conv_w': 'new_m', 'new_m_l0_ffn_conv_b': 'new_m', 'new_m_l0_ffn_w_out': 'new_m', 'new_m_l1_mix_norm': 'new_m', 'new_m_l1_w_in': 'new_m', 'new_m_l1_lru_conv_w': 'new_m', 'new_m_l1_lru_conv_b': 'new_m', 'new_m_l1_lru_wa': 'new_m', 'new_m_l1_lru_ba': 'new_m', 'new_m_l1_lru_wx': 'new_m', 'new_m_l1_lru_bx': 'new_m', 'new_m_l1_lru_lambda': 'new_m', 'new_m_l1_w_out': 'new_m', 'new_m_l1_ffn_norm': 'new_m', 'new_m_l1_ffn_w_in': 'new_m', 'new_m_l1_ffn_conv_w': 'new_m', 'new_m_l1_ffn_conv_b': 'new_m', 'new_m_l1_ffn_w_out': 'new_m', 'new_m_final_norm': 'new_m', 'new_v_meta_tokens': 'new_v', 'new_v_l0_mix_norm': 'new_v', 'new_v_l0_w_in': 'new_v', 'new_v_l0_ssd_conv_w': 'new_v', 'new_v_l0_ssd_conv_b': 'new_v', 'new_v_l0_ssd_dt_bias': 'new_v', 'new_v_l0_ssd_a_log': 'new_v', 'new_v_l0_ssd_d': 'new_v', 'new_v_l0_ssd_norm': 'new_v', 'new_v_l0_ret_norm': 'new_v', 'new_v_l0_w_out': 'new_v', 'new_v_l0_ffn_norm': 'new_v', 'new_v_l0_ffn_w_in': 'new_v', 'new_v_l0_ffn_conv_w': 'new_v', 'new_v_l0_ffn_conv_b': 'new_v', 'new_v_l0_ffn_w_out': 'new_v', 'new_v_l1_mix_norm': 'new_v', 'new_v_l1_w_in': 'new_v', 'new_v_l1_lru_conv_w': 'new_v', 'new_v_l1_lru_conv_b': 'new_v', 'new_v_l1_lru_wa': 'new_v', 'new_v_l1_lru_ba': 'new_v', 'new_v_l1_lru_wx': 'new_v', 'new_v_l1_lru_bx': 'new_v', 'new_v_l1_lru_lambda': 'new_v', 'new_v_l1_w_out': 'new_v', 'new_v_l1_ffn_norm': 'new_v', 'new_v_l1_ffn_w_in': 'new_v', 'new_v_l1_ffn_conv_w': 'new_v', 'new_v_l1_ffn_conv_b': 'new_v', 'new_v_l1_ffn_w_out': 'new_v', 'new_v_final_norm': 'new_v'}


def _forward(args):
    return _fwd_reference(*[args[k] for k in FWD_PARAMS])


def _output_shape():
    out = _jax.eval_shape(lambda: _forward(_fwd_setup_inputs(0)))
    return out.shape, out.dtype

N_MICROBATCH = 1
ADAM_LR = 0.001
ADAM_B1 = 0.9
ADAM_B2 = 0.999
ADAM_EPS = 1e-08
ADAM_WD = 0.01
ADAM_STEP = 10
PER_EXAMPLE_BATCH_AXIS = {'x': 0, 'loss_target': 0}
SHARED_INPUTS = []
_WEIGHT_DTYPES = {'meta_tokens': _jnp.float32, 'l0_mix_norm': _jnp.float32, 'l0_w_in': _jnp.float32, 'l0_ssd_conv_w': _jnp.float32, 'l0_ssd_conv_b': _jnp.float32, 'l0_ssd_dt_bias': _jnp.float32, 'l0_ssd_a_log': _jnp.float32, 'l0_ssd_d': _jnp.float32, 'l0_ssd_norm': _jnp.float32, 'l0_ret_norm': _jnp.float32, 'l0_w_out': _jnp.float32, 'l0_ffn_norm': _jnp.float32, 'l0_ffn_w_in': _jnp.float32, 'l0_ffn_conv_w': _jnp.float32, 'l0_ffn_conv_b': _jnp.float32, 'l0_ffn_w_out': _jnp.float32, 'l1_mix_norm': _jnp.float32, 'l1_w_in': _jnp.float32, 'l1_lru_conv_w': _jnp.float32, 'l1_lru_conv_b': _jnp.float32, 'l1_lru_wa': _jnp.float32, 'l1_lru_ba': _jnp.float32, 'l1_lru_wx': _jnp.float32, 'l1_lru_bx': _jnp.float32, 'l1_lru_lambda': _jnp.float32, 'l1_w_out': _jnp.float32, 'l1_ffn_norm': _jnp.float32, 'l1_ffn_w_in': _jnp.float32, 'l1_ffn_conv_w': _jnp.float32, 'l1_ffn_conv_b': _jnp.float32, 'l1_ffn_w_out': _jnp.float32, 'final_norm': _jnp.float32}
MOMENT_SCALE = {'meta_tokens': 9.528074e-03, 'l0_mix_norm': 2.363920e-01, 'l0_w_in': 8.635439e-02, 'l0_ssd_conv_w': 9.345771e-02, 'l0_ssd_conv_b': 1.726948e-01, 'l0_ssd_dt_bias': 2.892195e-01, 'l0_ssd_a_log': 4.429460e-01, 'l0_ssd_d': 5.090369e-01, 'l0_ssd_norm': 1.310812e-01, 'l0_ret_norm': 7.217757e-02, 'l0_w_out': 1.457592e-01, 'l0_ffn_norm': 1.339666e-01, 'l0_ffn_w_in': 5.253606e-02, 'l0_ffn_conv_w': 5.177982e-02, 'l0_ffn_conv_b': 6.124453e-02, 'l0_ffn_w_out': 8.627699e-02, 'l1_mix_norm': 1.006514e-01, 'l1_w_in': 4.664536e-02, 'l1_lru_conv_w': 5.699460e-02, 'l1_lru_conv_b': 4.718289e-01, 'l1_lru_wa': 1.321843e-02, 'l1_lru_ba': 1.078393e-02, 'l1_lru_wx': 2.377254e-02, 'l1_lru_bx': 2.038642e-02, 'l1_lru_lambda': 2.348095e-02, 'l1_w_out': 7.634835e-02, 'l1_ffn_norm': 1.000618e-01, 'l1_ffn_w_in': 4.100494e-02, 'l1_ffn_conv_w': 4.087739e-02, 'l1_ffn_conv_b': 3.928525e-02, 'l1_ffn_w_out': 6.696065e-02, 'final_norm': 3.197157e+01}


def _to_microbatches(a, axis):
    t = _jnp.moveaxis(a, axis, 0)
    t = t.reshape((N_MICROBATCH, t.shape[0] // N_MICROBATCH) + t.shape[1:])
    return _jnp.moveaxis(t, 1, axis + 1)


def setup_inputs(seed: int = 0) -> dict:
    inp = _fwd_setup_inputs(seed)
    key = _jax.random.fold_in(_jax.random.key(seed), 7919)
    shape, _ = _output_shape()
    out = dict(inp)
    out["loss_target"] = _jax.random.normal(_jax.random.fold_in(key, 0), shape, _jnp.float32)
    for i, name in enumerate(TWIN_WEIGHTS):
        w = inp[name].astype(_jnp.float32)
        if MOMENT_SCALE is None:
            s = _jnp.sqrt(_jnp.mean(_jnp.square(w)) + 1e-30)
        else:
            s = MOMENT_SCALE[name]
        km, kv = _jax.random.split(_jax.random.fold_in(key, i + 1))
        out[name] = w
        out["m_" + name] = s * _jax.random.normal(km, w.shape, _jnp.float32)
        out["v_" + name] = (s * s) * _jax.random.uniform(kv, w.shape, _jnp.float32, 0.5, 1.5)
    if N_MICROBATCH > 1:
        for name, axis in PER_EXAMPLE_BATCH_AXIS.items():
            out[name] = _to_microbatches(out[name], axis)
    return {'x': out['x'], 'meta_tokens': out['meta_tokens'], 'l0_mix_norm': out['l0_mix_norm'], 'l0_w_in': out['l0_w_in'], 'l0_ssd_conv_w': out['l0_ssd_conv_w'], 'l0_ssd_conv_b': out['l0_ssd_conv_b'], 'l0_ssd_dt_bias': out['l0_ssd_dt_bias'], 'l0_ssd_a_log': out['l0_ssd_a_log'], 'l0_ssd_d': out['l0_ssd_d'], 'l0_ssd_norm': out['l0_ssd_norm'], 'l0_ret_norm': out['l0_ret_norm'], 'l0_w_out': out['l0_w_out'], 'l0_ffn_norm': out['l0_ffn_norm'], 'l0_ffn_w_in': out['l0_ffn_w_in'], 'l0_ffn_conv_w': out['l0_ffn_conv_w'], 'l0_ffn_conv_b': out['l0_ffn_conv_b'], 'l0_ffn_w_out': out['l0_ffn_w_out'], 'l1_mix_norm': out['l1_mix_norm'], 'l1_w_in': out['l1_w_in'], 'l1_lru_conv_w': out['l1_lru_conv_w'], 'l1_lru_conv_b': out['l1_lru_conv_b'], 'l1_lru_wa': out['l1_lru_wa'], 'l1_lru_ba': out['l1_lru_ba'], 'l1_lru_wx': out['l1_lru_wx'], 'l1_lru_bx': out['l1_lru_bx'], 'l1_lru_lambda': out['l1_lru_lambda'], 'l1_w_out': out['l1_w_out'], 'l1_ffn_norm': out['l1_ffn_norm'], 'l1_ffn_w_in': out['l1_ffn_w_in'], 'l1_ffn_conv_w': out['l1_ffn_conv_w'], 'l1_ffn_conv_b': out['l1_ffn_conv_b'], 'l1_ffn_w_out': out['l1_ffn_w_out'], 'final_norm': out['final_norm'], 'loss_target': out['loss_target'], 'm_meta_tokens': out['m_meta_tokens'], 'm_l0_mix_norm': out['m_l0_mix_norm'], 'm_l0_w_in': out['m_l0_w_in'], 'm_l0_ssd_conv_w': out['m_l0_ssd_conv_w'], 'm_l0_ssd_conv_b': out['m_l0_ssd_conv_b'], 'm_l0_ssd_dt_bias': out['m_l0_ssd_dt_bias'], 'm_l0_ssd_a_log': out['m_l0_ssd_a_log'], 'm_l0_ssd_d': out['m_l0_ssd_d'], 'm_l0_ssd_norm': out['m_l0_ssd_norm'], 'm_l0_ret_norm': out['m_l0_ret_norm'], 'm_l0_w_out': out['m_l0_w_out'], 'm_l0_ffn_norm': out['m_l0_ffn_norm'], 'm_l0_ffn_w_in': out['m_l0_ffn_w_in'], 'm_l0_ffn_conv_w': out['m_l0_ffn_conv_w'], 'm_l0_ffn_conv_b': out['m_l0_ffn_conv_b'], 'm_l0_ffn_w_out': out['m_l0_ffn_w_out'], 'm_l1_mix_norm': out['m_l1_mix_norm'], 'm_l1_w_in': out['m_l1_w_in'], 'm_l1_lru_conv_w': out['m_l1_lru_conv_w'], 'm_l1_lru_conv_b': out['m_l1_lru_conv_b'], 'm_l1_lru_wa': out['m_l1_lru_wa'], 'm_l1_lru_ba': out['m_l1_lru_ba'], 'm_l1_lru_wx': out['m_l1_lru_wx'], 'm_l1_lru_bx': out['m_l1_lru_bx'], 'm_l1_lru_lambda': out['m_l1_lru_lambda'], 'm_l1_w_out': out['m_l1_w_out'], 'm_l1_ffn_norm': out['m_l1_ffn_norm'], 'm_l1_ffn_w_in': out['m_l1_ffn_w_in'], 'm_l1_ffn_conv_w': out['m_l1_ffn_conv_w'], 'm_l1_ffn_conv_b': out['m_l1_ffn_conv_b'], 'm_l1_ffn_w_out': out['m_l1_ffn_w_out'], 'm_final_norm': out['m_final_norm'], 'v_meta_tokens': out['v_meta_tokens'], 'v_l0_mix_norm': out['v_l0_mix_norm'], 'v_l0_w_in': out['v_l0_w_in'], 'v_l0_ssd_conv_w': out['v_l0_ssd_conv_w'], 'v_l0_ssd_conv_b': out['v_l0_ssd_conv_b'], 'v_l0_ssd_dt_bias': out['v_l0_ssd_dt_bias'], 'v_l0_ssd_a_log': out['v_l0_ssd_a_log'], 'v_l0_ssd_d': out['v_l0_ssd_d'], 'v_l0_ssd_norm': out['v_l0_ssd_norm'], 'v_l0_ret_norm': out['v_l0_ret_norm'], 'v_l0_w_out': out['v_l0_w_out'], 'v_l0_ffn_norm': out['v_l0_ffn_norm'], 'v_l0_ffn_w_in': out['v_l0_ffn_w_in'], 'v_l0_ffn_conv_w': out['v_l0_ffn_conv_w'], 'v_l0_ffn_conv_b': out['v_l0_ffn_conv_b'], 'v_l0_ffn_w_out': out['v_l0_ffn_w_out'], 'v_l1_mix_norm': out['v_l1_mix_norm'], 'v_l1_w_in': out['v_l1_w_in'], 'v_l1_lru_conv_w': out['v_l1_lru_conv_w'], 'v_l1_lru_conv_b': out['v_l1_lru_conv_b'], 'v_l1_lru_wa': out['v_l1_lru_wa'], 'v_l1_lru_ba': out['v_l1_lru_ba'], 'v_l1_lru_wx': out['v_l1_lru_wx'], 'v_l1_lru_bx': out['v_l1_lru_bx'], 'v_l1_lru_lambda': out['v_l1_lru_lambda'], 'v_l1_w_out': out['v_l1_w_out'], 'v_l1_ffn_norm': out['v_l1_ffn_norm'], 'v_l1_ffn_w_in': out['v_l1_ffn_w_in'], 'v_l1_ffn_conv_w': out['v_l1_ffn_conv_w'], 'v_l1_ffn_conv_b': out['v_l1_ffn_conv_b'], 'v_l1_ffn_w_out': out['v_l1_ffn_w_out'], 'v_final_norm': out['v_final_norm']}


def _loss(weights, diff, rest, loss_target):
    with _jax.named_scope("forward"):
        args = {**rest, TWIN_DIFF_INPUT: diff, **{k: w.astype(_WEIGHT_DTYPES[k]) for k, w in weights.items()}}
        y = _forward(args)
    with _jax.named_scope("loss_head"):
        err = _jnp.square(y.astype(_jnp.float32) - loss_target)
        return 0.5 * _jnp.sum(_jnp.mean(err, axis=-1)) if err.ndim else 0.5 * err


def _adamw(w, g, m, v):
    m = ADAM_B1 * m + (1.0 - ADAM_B1) * g
    v = ADAM_B2 * v + (1.0 - ADAM_B2) * _jnp.square(g)
    m_hat = m / (1.0 - ADAM_B1 ** ADAM_STEP)
    v_hat = v / (1.0 - ADAM_B2 ** ADAM_STEP)
    delta = -ADAM_LR * (m_hat / (_jnp.sqrt(v_hat) + ADAM_EPS) + ADAM_WD * w)
    return delta, m, v


def reference(x, meta_tokens, l0_mix_norm, l0_w_in, l0_ssd_conv_w, l0_ssd_conv_b, l0_ssd_dt_bias, l0_ssd_a_log, l0_ssd_d, l0_ssd_norm, l0_ret_norm, l0_w_out, l0_ffn_norm, l0_ffn_w_in, l0_ffn_conv_w, l0_ffn_conv_b, l0_ffn_w_out, l1_mix_norm, l1_w_in, l1_lru_conv_w, l1_lru_conv_b, l1_lru_wa, l1_lru_ba, l1_lru_wx, l1_lru_bx, l1_lru_lambda, l1_w_out, l1_ffn_norm, l1_ffn_w_in, l1_ffn_conv_w, l1_ffn_conv_b, l1_ffn_w_out, final_norm, loss_target, m_meta_tokens, m_l0_mix_norm, m_l0_w_in, m_l0_ssd_conv_w, m_l0_ssd_conv_b, m_l0_ssd_dt_bias, m_l0_ssd_a_log, m_l0_ssd_d, m_l0_ssd_norm, m_l0_ret_norm, m_l0_w_out, m_l0_ffn_norm, m_l0_ffn_w_in, m_l0_ffn_conv_w, m_l0_ffn_conv_b, m_l0_ffn_w_out, m_l1_mix_norm, m_l1_w_in, m_l1_lru_conv_w, m_l1_lru_conv_b, m_l1_lru_wa, m_l1_lru_ba, m_l1_lru_wx, m_l1_lru_bx, m_l1_lru_lambda, m_l1_w_out, m_l1_ffn_norm, m_l1_ffn_w_in, m_l1_ffn_conv_w, m_l1_ffn_conv_b, m_l1_ffn_w_out, m_final_norm, v_meta_tokens, v_l0_mix_norm, v_l0_w_in, v_l0_ssd_conv_w, v_l0_ssd_conv_b, v_l0_ssd_dt_bias, v_l0_ssd_a_log, v_l0_ssd_d, v_l0_ssd_norm, v_l0_ret_norm, v_l0_w_out, v_l0_ffn_norm, v_l0_ffn_w_in, v_l0_ffn_conv_w, v_l0_ffn_conv_b, v_l0_ffn_w_out, v_l1_mix_norm, v_l1_w_in, v_l1_lru_conv_w, v_l1_lru_conv_b, v_l1_lru_wa, v_l1_lru_ba, v_l1_lru_wx, v_l1_lru_bx, v_l1_lru_lambda, v_l1_w_out, v_l1_ffn_norm, v_l1_ffn_w_in, v_l1_ffn_conv_w, v_l1_ffn_conv_b, v_l1_ffn_w_out, v_final_norm):
    given = dict(x=x, meta_tokens=meta_tokens, l0_mix_norm=l0_mix_norm, l0_w_in=l0_w_in, l0_ssd_conv_w=l0_ssd_conv_w, l0_ssd_conv_b=l0_ssd_conv_b, l0_ssd_dt_bias=l0_ssd_dt_bias, l0_ssd_a_log=l0_ssd_a_log, l0_ssd_d=l0_ssd_d, l0_ssd_norm=l0_ssd_norm, l0_ret_norm=l0_ret_norm, l0_w_out=l0_w_out, l0_ffn_norm=l0_ffn_norm, l0_ffn_w_in=l0_ffn_w_in, l0_ffn_conv_w=l0_ffn_conv_w, l0_ffn_conv_b=l0_ffn_conv_b, l0_ffn_w_out=l0_ffn_w_out, l1_mix_norm=l1_mix_norm, l1_w_in=l1_w_in, l1_lru_conv_w=l1_lru_conv_w, l1_lru_conv_b=l1_lru_conv_b, l1_lru_wa=l1_lru_wa, l1_lru_ba=l1_lru_ba, l1_lru_wx=l1_lru_wx, l1_lru_bx=l1_lru_bx, l1_lru_lambda=l1_lru_lambda, l1_w_out=l1_w_out, l1_ffn_norm=l1_ffn_norm, l1_ffn_w_in=l1_ffn_w_in, l1_ffn_conv_w=l1_ffn_conv_w, l1_ffn_conv_b=l1_ffn_conv_b, l1_ffn_w_out=l1_ffn_w_out, final_norm=final_norm, loss_target=loss_target, m_meta_tokens=m_meta_tokens, m_l0_mix_norm=m_l0_mix_norm, m_l0_w_in=m_l0_w_in, m_l0_ssd_conv_w=m_l0_ssd_conv_w, m_l0_ssd_conv_b=m_l0_ssd_conv_b, m_l0_ssd_dt_bias=m_l0_ssd_dt_bias, m_l0_ssd_a_log=m_l0_ssd_a_log, m_l0_ssd_d=m_l0_ssd_d, m_l0_ssd_norm=m_l0_ssd_norm, m_l0_ret_norm=m_l0_ret_norm, m_l0_w_out=m_l0_w_out, m_l0_ffn_norm=m_l0_ffn_norm, m_l0_ffn_w_in=m_l0_ffn_w_in, m_l0_ffn_conv_w=m_l0_ffn_conv_w, m_l0_ffn_conv_b=m_l0_ffn_conv_b, m_l0_ffn_w_out=m_l0_ffn_w_out, m_l1_mix_norm=m_l1_mix_norm, m_l1_w_in=m_l1_w_in, m_l1_lru_conv_w=m_l1_lru_conv_w, m_l1_lru_conv_b=m_l1_lru_conv_b, m_l1_lru_wa=m_l1_lru_wa, m_l1_lru_ba=m_l1_lru_ba, m_l1_lru_wx=m_l1_lru_wx, m_l1_lru_bx=m_l1_lru_bx, m_l1_lru_lambda=m_l1_lru_lambda, m_l1_w_out=m_l1_w_out, m_l1_ffn_norm=m_l1_ffn_norm, m_l1_ffn_w_in=m_l1_ffn_w_in, m_l1_ffn_conv_w=m_l1_ffn_conv_w, m_l1_ffn_conv_b=m_l1_ffn_conv_b, m_l1_ffn_w_out=m_l1_ffn_w_out, m_final_norm=m_final_norm, v_meta_tokens=v_meta_tokens, v_l0_mix_norm=v_l0_mix_norm, v_l0_w_in=v_l0_w_in, v_l0_ssd_conv_w=v_l0_ssd_conv_w, v_l0_ssd_conv_b=v_l0_ssd_conv_b, v_l0_ssd_dt_bias=v_l0_ssd_dt_bias, v_l0_ssd_a_log=v_l0_ssd_a_log, v_l0_ssd_d=v_l0_ssd_d, v_l0_ssd_norm=v_l0_ssd_norm, v_l0_ret_norm=v_l0_ret_norm, v_l0_w_out=v_l0_w_out, v_l0_ffn_norm=v_l0_ffn_norm, v_l0_ffn_w_in=v_l0_ffn_w_in, v_l0_ffn_conv_w=v_l0_ffn_conv_w, v_l0_ffn_conv_b=v_l0_ffn_conv_b, v_l0_ffn_w_out=v_l0_ffn_w_out, v_l1_mix_norm=v_l1_mix_norm, v_l1_w_in=v_l1_w_in, v_l1_lru_conv_w=v_l1_lru_conv_w, v_l1_lru_conv_b=v_l1_lru_conv_b, v_l1_lru_wa=v_l1_lru_wa, v_l1_lru_ba=v_l1_lru_ba, v_l1_lru_wx=v_l1_lru_wx, v_l1_lru_bx=v_l1_lru_bx, v_l1_lru_lambda=v_l1_lru_lambda, v_l1_w_out=v_l1_w_out, v_l1_ffn_norm=v_l1_ffn_norm, v_l1_ffn_w_in=v_l1_ffn_w_in, v_l1_ffn_conv_w=v_l1_ffn_conv_w, v_l1_ffn_conv_b=v_l1_ffn_conv_b, v_l1_ffn_w_out=v_l1_ffn_w_out, v_final_norm=v_final_norm)
    weights = {n: given[n] for n in TWIN_WEIGHTS}
    shared = {n: given[n] for n in SHARED_INPUTS}
    per_example = {n: given[n] for n in ['x']}
    grad_fn = _jax.value_and_grad(_loss, argnums=(0, 1))

    def one_microbatch(ex, loss_target):
        ex = dict(ex)
        diff = ex.pop(TWIN_DIFF_INPUT)
        return grad_fn(weights, diff, {**shared, **ex}, loss_target)

    if N_MICROBATCH == 1:
        loss, (grad_w, grad_x) = one_microbatch(per_example, given["loss_target"])
    else:
        def body(carry, xs):
            loss_sum, grad_sum = carry
            l_k, (gw_k, gx_k) = one_microbatch(xs[0], xs[1])
            with _jax.named_scope("update"):
                return (loss_sum + l_k, _jax.tree.map(_jnp.add, grad_sum, gw_k)), gx_k

        init = (_jnp.zeros((), _jnp.float32), _jax.tree.map(_jnp.zeros_like, weights))
        (loss, grad_w), grad_x = _jax.lax.scan(body, init, (per_example, given["loss_target"]))
    with _jax.named_scope("update"):
        delta_w, new_m, new_v = {}, {}, {}
        for n in TWIN_WEIGHTS:
            delta_w[n], new_m[n], new_v[n] = _adamw(weights[n], grad_w[n], given["m_" + n], given["v_" + n])
    return (loss, grad_x, *[grad_w[n] for n in TWIN_WEIGHTS], *[delta_w[n] for n in TWIN_WEIGHTS],
            *[new_m[n] for n in TWIN_WEIGHTS], *[new_v[n] for n in TWIN_WEIGHTS])
```

```python
import functools
from typing import NamedTuple

import numpy as np
import jax
import jax.numpy as jnp
from jax import lax
from jax.experimental import pallas as pl
from jax.experimental.pallas import tpu as pltpu

F32 = jnp.float32
BF16 = jnp.bfloat16
EPS = 1e-6
LRU_C = 8.0
NEG = -1e30
SUBLANES = 8
LANES = 128
VMEM_LIMIT = 56 * 1024 * 1024

ADAM_LR = 0.001
ADAM_B1 = 0.9
ADAM_B2 = 0.999
ADAM_EPS = 1e-08
ADAM_WD = 0.01
ADAM_STEP = 10

WEIGHTS = ['meta_tokens', 'l0_mix_norm', 'l0_w_in', 'l0_ssd_conv_w', 'l0_ssd_conv_b', 'l0_ssd_dt_bias', 'l0_ssd_a_log',
           'l0_ssd_d', 'l0_ssd_norm', 'l0_ret_norm', 'l0_w_out', 'l0_ffn_norm', 'l0_ffn_w_in', 'l0_ffn_conv_w',
           'l0_ffn_conv_b', 'l0_ffn_w_out', 'l1_mix_norm', 'l1_w_in', 'l1_lru_conv_w', 'l1_lru_conv_b', 'l1_lru_wa',
           'l1_lru_ba', 'l1_lru_wx', 'l1_lru_bx', 'l1_lru_lambda', 'l1_w_out', 'l1_ffn_norm', 'l1_ffn_w_in',
           'l1_ffn_conv_w', 'l1_ffn_conv_b', 'l1_ffn_w_out', 'final_norm']
BIG_COL = ['l0_w_in', 'l0_ffn_w_in', 'l1_w_in', 'l1_ffn_w_in']
BIG_ROW = ['l0_w_out', 'l0_ffn_w_out', 'l1_w_out', 'l1_ffn_w_out']
BIG = BIG_COL + BIG_ROW
SMALL_SHARDED = ['meta_tokens', 'l0_ssd_conv_w', 'l0_ffn_conv_w', 'l1_lru_conv_w', 'l1_ffn_conv_w']
REPLICATED = [w for w in WEIGHTS if w not in BIG and w not in SMALL_SHARDED]


class Config(NamedTuple):
    d_model: int = 1024
    seq: int = 2048
    n_meta: int = 16
    chunk: int = 128
    ssd_heads: int = 16
    ssd_head_dim: int = 64
    ssd_groups: int = 4
    ssd_state: int = 128
    ret_heads: int = 4
    ret_dim: int = 256
    sb_heads: int = 16
    sb_head_dim: int = 64
    lru_width: int = 1024
    lru_blocks: int = 8
    ffn_dim: int = 2816
    n_dev: int = 8

    @property
    def t(self):
        return self.n_meta + self.seq

    @property
    def pad(self):
        return (-self.t) % self.chunk

    @property
    def tp(self):
        return self.t + self.pad

    @property
    def nc(self):
        return self.tp // self.chunk

    @property
    def ssd_inner(self):
        return self.ssd_heads * self.ssd_head_dim

    @property
    def ssd_gw(self):
        return self.ssd_inner // self.ssd_groups

    @property
    def ssd_bc(self):
        return self.ssd_groups * self.ssd_state

    @property
    def ret_w(self):
        return self.ret_heads * self.ret_dim

    @property
    def sb_w(self):
        return self.sb_heads * self.sb_head_dim

    @property
    def mix0_segs(self):
        return (self.ssd_inner, self.ssd_inner + 2 * self.ssd_bc, self.ssd_heads, self.ret_w, self.ret_w, self.ret_w, self.ret_w)


CFG = Config()


def _dg(a, b, ca, cb):
    return lax.dot_general(a, b, (((ca,), (cb,)), ((), ())), preferred_element_type=F32)


def _cot_a(b, ca, cb, g):
    return _dg(g, b, 1, 1 - cb) if ca == 1 else _dg(b, g, 1 - cb, 1)


def _cot_b(a, ca, cb, g):
    return _dg(a, g, 1 - ca, 0) if cb == 0 else _dg(g, a, 0, 1 - ca)


@functools.partial(jax.custom_vjp, nondiff_argnums=(2, 3))
def _bdot(a, b, ca, cb):
    return _dg(a.astype(BF16), b.astype(BF16), ca, cb)


def _bdot_fwd(a, b, ca, cb):
    return _bdot(a, b, ca, cb), (a, b)


def _bdot_bwd(ca, cb, res, g):
    a, b = res
    gb = g.astype(BF16)
    return _cot_a(b.astype(BF16), ca, cb, gb).astype(a.dtype), _cot_b(a.astype(BF16), ca, cb, gb).astype(b.dtype)


_bdot.defvjp(_bdot_fwd, _bdot_bwd)


def _dot(a, b):
    return _bdot(a, b, 1, 0)


def _dot_nt(a, b):
    return _bdot(a, b, 1, 1)


def _dot_tn(a, b):
    return _bdot(a, b, 0, 0)


def _split3(a):
    hi = a.astype(BF16)
    r1 = a - hi.astype(F32)
    mid = r1.astype(BF16)
    return hi, mid, (r1 - mid.astype(F32)).astype(BF16)


@functools.partial(jax.custom_vjp, nondiff_argnums=(2, 3, 4))
def _edot(a, b, ca, cb, const):
    if const == 'b':
        bb = b.astype(BF16)
        return sum(_dg(p, bb, ca, cb) for p in _split3(a))
    ab = a.astype(BF16)
    return sum(_dg(ab, p, ca, cb) for p in _split3(b))


def _edot_fwd(a, b, ca, cb, const):
    return _edot(a, b, ca, cb, const), (a, b)


def _edot_bwd(ca, cb, const, res, g):
    a, b = res
    if const == 'b':
        bb = b.astype(BF16)
        return sum(_cot_a(bb, ca, cb, p) for p in _split3(g)), jnp.zeros_like(b)
    ab = a.astype(BF16)
    return jnp.zeros_like(a), sum(_cot_b(ab, ca, cb, p) for p in _split3(g))


_edot.defvjp(_edot_fwd, _edot_bwd)


def _dot_exact01(a, sel):
    return _edot(a, sel, 1, 0, 'b')


def _silu(x):
    return x * jax.nn.sigmoid(x)


def _softplus(x):
    return jnp.maximum(x, 0.0) + jnp.log1p(jnp.exp(-jnp.abs(x)))


def _neg_expm1(x):
    series = -x * (1.0 + x * 0.5 * (1.0 + x / 3.0 * (1.0 + x * 0.25)))
    return jnp.where(x > -0.01, series, 1.0 - jnp.exp(x))


def _causal_conv(x, tail, w, b):
    taps, rows = w.shape[0], x.shape[0]
    xx = jnp.concatenate([tail, x], axis=0)
    y = b
    for k in range(taps):
        off = SUBLANES - (taps - 1 - k)
        y = y + w[k:k + 1, :] * xx[off:off + rows, :]
    return y


def _valid_chunk(cfg, chunk_idx):
    first = (chunk_idx % cfg.nc) == 0
    rid = lax.broadcasted_iota(jnp.int32, (cfg.chunk, 1), 0)
    return jnp.where(jnp.logical_and(first, rid < cfg.pad), 0.0, 1.0).astype(F32)


def _cparams(**kw):
    return pltpu.CompilerParams(vmem_limit_bytes=VMEM_LIMIT, **kw)


class Row(NamedTuple):
    arr: jax.Array
    width: int
    blk: int = 0
    tail: bool = False
    diff: bool = True


class Par(NamedTuple):
    arr: jax.Array
    diff: bool = True


def _row_specs(cfg, rows, order):
    tr = cfg.chunk
    specs, args = [], []
    for r in rows:
        specs.append(pl.BlockSpec((tr, r.width), functools.partial(lambda i, b, o: (o(i), b), b=r.blk, o=order)))
        args.append(r.arr)
        if r.tail:
            per = tr // SUBLANES
            specs.append(pl.BlockSpec((SUBLANES, r.width),
                                      functools.partial(lambda i, b, o: (jnp.maximum(o(i) * per - 1, 0), b), b=r.blk, o=order)))
            args.append(r.arr)
    return specs, args


def _par_specs(pars):
    specs = [pl.BlockSpec(p.arr.shape, functools.partial(lambda i, nd: (0,) * nd, nd=p.arr.ndim)) for p in pars]
    return specs, [p.arr for p in pars]


def stage_fwd(name, cfg, fn, rows, pars, outs):
    tr = cfg.chunk
    n_rows = rows[0].arr.shape[0]
    rspecs, rargs = _row_specs(cfg, rows, lambda i: i)
    pspecs, pargs = _par_specs(pars)
    n_in = len(rargs) + len(pargs)

    def body(*refs):
        valid = _valid_chunk(cfg, pl.program_id(0))
        vals = [r[...].astype(F32) for r in refs[:n_in]]
        res = fn(valid, *vals)
        for o, v in zip(refs[n_in:], res):
            o[...] = v.astype(o.dtype)

    return pl.pallas_call(
        body, grid=(n_rows // tr,), in_specs=rspecs + pspecs,
        out_specs=[pl.BlockSpec((tr, w), lambda i: (i, 0)) for w, _ in outs],
        out_shape=[jax.ShapeDtypeStruct((n_rows, w), dt) for w, dt in outs],
        name=name, compiler_params=_cparams())(*rargs, *pargs)


def stage_bwd(name, cfg, fn, rows, pars, douts, drow_dtypes):
    tr = cfg.chunk
    n_rows = rows[0].arr.shape[0]
    n_blk = n_rows // tr
    order = lambda i: n_blk - 1 - i
    rspecs, rargs = _row_specs(cfg, rows, order)
    pspecs, pargs = _par_specs(pars)
    pieces = [p for out in douts for p in out]
    dspecs, dargs = _row_specs(cfg, pieces, order)
    n_r, n_p, n_d = len(rargs), len(pargs), len(dargs)
    diff_rows = [r for r in rows if r.diff]
    diff_pars = [p for p in pars if p.diff]
    tails = [r for r in diff_rows if r.tail]

    def body(*refs):
        in_refs = refs[:n_r + n_p]
        d_refs = refs[n_r + n_p:n_r + n_p + n_d]
        o_refs = refs[n_r + n_p + n_d:]
        drow_refs = o_refs[:len(diff_rows)]
        dpar_refs = o_refs[len(diff_rows):len(diff_rows) + len(diff_pars)]
        carry_refs = o_refs[len(diff_rows) + len(diff_pars):]
        step = pl.program_id(0)
        valid = _valid_chunk(cfg, order(step))
        vals = [r[...].astype(F32) for r in in_refs]
        slots, pos = [], 0
        for r in rows:
            if r.diff:
                slots.append(pos)
                if r.tail:
                    slots.append(pos + 1)
            pos += 2 if r.tail else 1
        for p in pars:
            if p.diff:
                slots.append(pos)
            pos += 1

        def g(*dv):
            full = list(vals)
            for s, v in zip(slots, dv):
                full[s] = v
            return tuple(fn(valid, *full))

        _, vjp = jax.vjp(g, *[vals[s] for s in slots])
        cts, k = [], 0
        for out in douts:
            parts = [d_refs[k + j][...].astype(F32) for j in range(len(out))]
            k += len(out)
            cts.append(parts[0] if len(parts) == 1 else jnp.concatenate(parts, axis=1))
        grads = list(vjp(tuple(cts)))

        @pl.when(step == 0)
        def _():
            for c in carry_refs:
                c[...] = jnp.zeros_like(c)
            for d in dpar_refs:
                d[...] = jnp.zeros_like(d)

        gi, ci = 0, 0
        for r, o in zip(diff_rows, drow_refs):
            dx = grads[gi]
            gi += 1
            if r.tail:
                dtail = grads[gi]
                gi += 1
                c = carry_refs[ci]
                ci += 1
                dx = dx + jnp.concatenate([jnp.zeros((tr - SUBLANES, r.width), F32), c[...]], axis=0)
                c[...] = dtail
            o[...] = dx.astype(o.dtype)
        for d in dpar_refs:
            d[...] += grads[gi]
            gi += 1

    out_specs = [pl.BlockSpec((tr, r.width), lambda i: (order(i), 0)) for r in diff_rows]
    out_shape = [jax.ShapeDtypeStruct((n_rows, r.width), dt) for r, dt in zip(diff_rows, drow_dtypes)]
    for p in diff_pars:
        out_specs.append(pl.BlockSpec(p.arr.shape, functools.partial(lambda i, nd: (0,) * nd, nd=p.arr.ndim)))
        out_shape.append(jax.ShapeDtypeStruct(p.arr.shape, F32))
    res = pl.pallas_call(
        body, grid=(n_blk,), in_specs=rspecs + pspecs + dspecs, out_specs=out_specs, out_shape=out_shape,
        scratch_shapes=[pltpu.VMEM((SUBLANES, r.width), F32) for r in tails],
        name=name, compiler_params=_cparams(dimension_semantics=("arbitrary",)))(*rargs, *pargs, *dargs)
    return list(res[:len(diff_rows)]), list(res[len(diff_rows):])


def _tile(n, cap):
    if n % LANES:
        return n
    q = n // LANES
    best = 1
    for k in range(1, q + 1):
        if q % k == 0 and k * LANES <= cap:
            best = k
    return best * LANES


def matmul(name, a, b, *, ta=False, tb=False, out_dtype=F32, resid=None, cfg=None, tm_cap=256, tn_cap=1024, tk_cap=2816):
    m, k = (a.shape[1], a.shape[0]) if ta else a.shape
    n = b.shape[0] if tb else b.shape[1]
    tm, tn, tk = _tile(m, tm_cap), _tile(n, tn_cap), _tile(k, tk_cap)
    nk = k // tk
    a_spec = pl.BlockSpec((tk, tm), lambda i, j, l: (l, i)) if ta else pl.BlockSpec((tm, tk), lambda i, j, l: (i, l))
    b_spec = pl.BlockSpec((tn, tk), lambda i, j, l: (j, l)) if tb else pl.BlockSpec((tk, tn), lambda i, j, l: (l, j))
    dims = (((0 if ta else 1,), (1 if tb else 0,)), ((), ()))
    in_specs, args = [a_spec, b_spec], [a, b]
    if resid is not None:
        in_specs.append(pl.BlockSpec((tm, tn), lambda i, j, l: (i, j)))
        args.append(resid)

    def body(*refs):
        a_ref, b_ref = refs[0], refs[1]
        o_ref, acc = refs[-2], refs[-1]
        l = pl.program_id(2)
        part = lax.dot_general(a_ref[...].astype(BF16), b_ref[...].astype(BF16), dims, preferred_element_type=F32)

        @pl.when(l == 0)
        def _():
            acc[...] = part

        @pl.when(l > 0)
        def _():
            acc[...] += part

        @pl.when(l == nk - 1)
        def _():
            res = acc[...]
            if resid is not None:
                row = pl.program_id(0) * tm + lax.broadcasted_iota(jnp.int32, (tm, 1), 0)
                inpad = jnp.zeros((tm, 1), jnp.bool_)
                for s in range(m // cfg.tp):
                    inpad = jnp.logical_or(inpad, jnp.logical_and(row >= s * cfg.tp, row < s * cfg.tp + cfg.pad))
                res = refs[2][...] + jnp.where(inpad, 0.0, res)
            o_ref[...] = res.astype(o_ref.dtype)

    return pl.pallas_call(
        body, grid=(m // tm, n // tn, nk), in_specs=in_specs, out_specs=pl.BlockSpec((tm, tn), lambda i, j, l: (i, j)),
        out_shape=jax.ShapeDtypeStruct((m, n), out_dtype), scratch_shapes=[pltpu.VMEM((tm, tn), F32)],
        name=name, compiler_params=_cparams(dimension_semantics=("parallel", "parallel", "arbitrary")))(*args)


def _norm_fn(valid, h, g):
    y = h * lax.rsqrt(jnp.mean(h * h, axis=-1, keepdims=True) + EPS)
    return (valid * (y * g),)


def _make_ssd_pre(cfg):
    nh = cfg.ssd_heads

    def fn(valid, xs, xs_t, bm, bm_t, cm, cm_t, dtr, w_xs, w_b, w_c, b_xs, b_b, b_c, dt_bias, a_log, sel_hd, sel_lane):
        xs = valid * _silu(_causal_conv(xs, xs_t, w_xs, b_xs))
        bm = valid * _silu(_causal_conv(bm, bm_t, w_b, b_b))
        cm = valid * _silu(_causal_conv(cm, cm_t, w_c, b_c))
        lane = lax.broadcasted_iota(jnp.int32, (1, LANES), 1)
        head = jnp.where(lane < nh, 1.0, 0.0).astype(F32)
        dt = valid * head * _softplus(dtr + dt_bias)
        adt = dt * (-jnp.exp(a_log))
        x = xs * _dot_exact01(dt, sel_hd)
        adt_b = _dot_exact01(adt, sel_lane)
        return xs, bm, cm, x, adt_b

    return fn


def _make_ssd_post(cfg):
    gw = cfg.ssd_gw

    def fn(valid, y_raw, xs, z, d_skip, norm_g, sel_hd):
        d_rep = _dot_exact01(jnp.broadcast_to(d_skip, (SUBLANES, LANES)), sel_hd)[0:1]
        y = (y_raw + xs * d_rep) * _silu(z)
        parts = []
        for g in range(cfg.ssd_groups):
            yg = y[:, g * gw:(g + 1) * gw]
            parts.append(yg * lax.rsqrt(jnp.mean(yg * yg, axis=-1, keepdims=True) + EPS))
        return (jnp.concatenate(parts, axis=1) * norm_g,)

    return fn


def _ffn_act_fn(valid, ug, ug_t, uu, uu_t, w_g, w_u, b_g, b_u):
    return (valid * _silu(_causal_conv(ug, ug_t, w_g, b_g)) * _causal_conv(uu, uu_t, w_u, b_u),)


def _make_lru_pre(cfg):
    nb = cfg.lru_blocks
    bw = cfg.lru_width // nb

    def fn(valid, xr, xr_t, w_conv, b_conv, wa, ba, wx, bx, lam):
        xc = _causal_conv(xr, xr_t, w_conv, b_conv)
        a_parts, b_parts = [], []
        for n in range(nb):
            sl = slice(n * bw, (n + 1) * bw)
            xn = xc[:, sl]
            r = jax.nn.sigmoid(_dot(xn, wa[n]) + ba[:, sl])
            i = jax.nn.sigmoid(_dot(xn, wx[n]) + bx[:, sl])
            log_a = -LRU_C * r * _softplus(-lam[:, sl])
            a_parts.append(jnp.exp(log_a))
            b_parts.append(valid * jnp.sqrt(jnp.maximum(_neg_expm1(2.0 * log_a), 0.0)) * (i * xn))
        return jnp.concatenate(a_parts, axis=1), jnp.concatenate(b_parts, axis=1)

    return fn


def _lru_post_fn(valid, hs, gate):
    return (hs * jax.nn.gelu(gate),)


def _tri_consts(n):
    r = lax.broadcasted_iota(jnp.int32, (n, n), 0)
    c = lax.broadcasted_iota(jnp.int32, (n, n), 1)
    return r, c


def _ssd_chunk(x, bm, cm, a, s, tril, lower):
    rows = x.shape[0]
    heads = a.shape[1] // LANES
    p = x.shape[1] // heads
    cb = _dot_nt(cm, bm)
    ys, ss = [], []
    for e in range(heads):
        ae = a[:, e * LANES:(e + 1) * LANES]
        cs = _edot(tril, ae, 1, 0, 'a')
        lmat = jnp.exp(jnp.where(lower, cs - cs.T, NEG))
        xe = x[:, e * p:(e + 1) * p]
        se = s[e * p:(e + 1) * p, :]
        tot = cs[rows - 1:rows, :]
        y_diag = _dot(cb * lmat, xe)
        st = _dot_tn(xe * jnp.exp(tot - cs)[:, :p], bm)
        y_off = _dot_nt(cm, se) * jnp.exp(cs)[:, :p]
        ys.append(y_diag + y_off)
        ss.append(jnp.exp(tot[:, :1]) * se + st)
    return jnp.concatenate(ys, axis=1), jnp.concatenate(ss, axis=0)


def _ssd_specs(cfg, x, bm, cm, adt_b):
    gw, ns = cfg.ssd_gw, cfg.ssd_state
    hpg = cfg.ssd_heads // cfg.ssd_groups
    specs = [pl.BlockSpec((cfg.tp, gw), lambda b, g: (b, g)),
             pl.BlockSpec((cfg.tp, ns), lambda b, g: (b, g)),
             pl.BlockSpec((cfg.tp, ns), lambda b, g: (b, g)),
             pl.BlockSpec((cfg.tp, hpg * LANES), lambda b, g: (b, g))]
    return specs, [x, bm, cm, adt_b]


def ssd_scan_fwd(cfg, x, bm, cm, adt_b):
    nb = x.shape[0] // cfg.tp
    L, nc = cfg.chunk, cfg.nc
    specs, args = _ssd_specs(cfg, x, bm, cm, adt_b)

    def body(x_ref, b_ref, c_ref, a_ref, y_ref, s_ref):
        r, c = _tri_consts(L)
        lower = r >= c
        tril = lower.astype(F32)
        s_ref[...] = jnp.zeros_like(s_ref)

        def step(ci, carry):
            rows = pl.ds(pl.multiple_of(ci * L, L), L)
            y, s_new = _ssd_chunk(x_ref[rows, :], b_ref[rows, :], c_ref[rows, :], a_ref[rows, :], s_ref[...], tril, lower)
            y_ref[rows, :] = y
            s_ref[...] = s_new
            return carry

        lax.fori_loop(0, nc, step, 0)

    return pl.pallas_call(
        body, grid=(nb, cfg.ssd_groups), in_specs=specs, out_specs=specs[0],
        out_shape=jax.ShapeDtypeStruct(x.shape, F32), scratch_shapes=[pltpu.VMEM((cfg.ssd_gw, cfg.ssd_state), F32)],
        name="ssd_scan_fwd", compiler_params=_cparams())(*args)


def ssd_scan_bwd(cfg, x, bm, cm, adt_b, dy):
    nb = x.shape[0] // cfg.tp
    L, nc = cfg.chunk, cfg.nc
    specs, args = _ssd_specs(cfg, x, bm, cm, adt_b)

    def body(x_ref, b_ref, c_ref, a_ref, dy_ref, dx_ref, db_ref, dc_ref, da_ref, s_all, ds_ref):
        r, c = _tri_consts(L)
        lower = r >= c
        tril = lower.astype(F32)
        chunk = functools.partial(_ssd_chunk, tril=tril, lower=lower)
        ds_ref[...] = jnp.zeros_like(ds_ref)

        def fwd(ci, carry):
            rows = pl.ds(pl.multiple_of(ci * L, L), L)
            s_all[ci] = ds_ref[...]
            _, s_new = chunk(x_ref[rows, :], b_ref[rows, :], c_ref[rows, :], a_ref[rows, :], ds_ref[...])
            ds_ref[...] = s_new
            return carry

        lax.fori_loop(0, nc, fwd, 0)
        ds_ref[...] = jnp.zeros_like(ds_ref)

        def bwd(k, carry):
            ci = nc - 1 - k
            rows = pl.ds(pl.multiple_of(ci * L, L), L)
            _, vjp = jax.vjp(chunk, x_ref[rows, :], b_ref[rows, :], c_ref[rows, :], a_ref[rows, :], s_all[ci])
            dx, db, dc, da, ds = vjp((dy_ref[rows, :], ds_ref[...]))
            dx_ref[rows, :] = dx
            db_ref[rows, :] = db
            dc_ref[rows, :] = dc
            da_ref[rows, :] = da
            ds_ref[...] = ds
            return carry

        lax.fori_loop(0, nc, bwd, 0)

    return pl.pallas_call(
        body, grid=(nb, cfg.ssd_groups), in_specs=specs + [specs[0]], out_specs=specs,
        out_shape=[jax.ShapeDtypeStruct(t.shape, F32) for t in (x, bm, cm, adt_b)],
        scratch_shapes=[pltpu.VMEM((nc, cfg.ssd_gw, cfg.ssd_state), F32), pltpu.VMEM((cfg.ssd_gw, cfg.ssd_state), F32)],
        name="ssd_scan_bwd", compiler_params=_cparams())(*args, dy)


def _ret_chunk(q, k, v, g, norm_g, r_prev, cos, sin, valid, decay, zeta, xi, cdec, scale):
    half = q.shape[1] // 2

    def rot(t):
        t1, t2 = t[:, :half], t[:, half:]
        return jnp.concatenate([t1 * cos - t2 * sin, t1 * sin + t2 * cos], axis=1)

    qr = valid * rot(q)
    kr = valid * rot(k) * scale
    v = valid * v
    inner = _dot(_dot_nt(qr, kr) * decay, v)
    kv = _dot_tn(kr * zeta, v)
    cross = _dot(qr, r_prev) * xi
    o = inner + cross
    o = o - jnp.mean(o, axis=-1, keepdims=True)
    o = o * lax.rsqrt(jnp.mean(o * o, axis=-1, keepdims=True) + EPS)
    return _silu(g) * (o * norm_g), cdec * r_prev + kv


def _ret_consts(cfg):
    f32 = jnp.float32
    log_gamma = jnp.log1p(-jnp.exp2(-5.0 - jnp.arange(cfg.ret_heads, dtype=f32)))
    idx = jnp.arange(cfg.chunk, dtype=f32)
    diff = idx[:, None] - idx[None, :]
    decay = jnp.where(diff >= 0, jnp.exp(log_gamma[:, None, None] * jnp.maximum(diff, 0.0)), 0.0)
    zeta = jnp.exp(log_gamma[:, None] * (cfg.chunk - 1 - idx)[None, :])[..., None]
    xi = jnp.exp(log_gamma[:, None] * (idx + 1.0)[None, :])[..., None]
    cdec = jnp.exp(cfg.chunk * log_gamma)[:, None, None]
    half = cfg.ret_dim // 2
    inv_freq = 1.0 / (10000.0 ** (jnp.arange(half, dtype=f32) / (half - 1)))
    pos = jnp.arange(cfg.tp, dtype=f32) - cfg.pad
    ang = pos[:, None] * inv_freq[None, :]
    return dict(decay=decay, zeta=zeta, xi=xi, cdec=cdec, cos=jnp.cos(ang), sin=jnp.sin(ang))


def _ret_specs(cfg, u, col0, norm_g, consts):
    dk, nh, L = cfg.ret_dim, cfg.ret_heads, cfg.chunk
    base = col0 // dk
    seg = lambda s: pl.BlockSpec((cfg.tp, dk), functools.partial(lambda h, b, s: (b, base + s * nh + h), s=s))
    half = dk // 2
    specs = [seg(0), seg(1), seg(2), seg(3),
             pl.BlockSpec((1, dk), lambda h, b: (0, h)),
             pl.BlockSpec((cfg.tp, half), lambda h, b: (0, 0)),
             pl.BlockSpec((cfg.tp, half), lambda h, b: (0, 0)),
             pl.BlockSpec((None, L, L), lambda h, b: (h, 0, 0)),
             pl.BlockSpec((None, L, 1), lambda h, b: (h, 0, 0)),
             pl.BlockSpec((None, L, 1), lambda h, b: (h, 0, 0)),
             pl.BlockSpec((None, 1, 1), lambda h, b: (h, 0, 0))]
    args = [u, u, u, u, norm_g, consts["cos"], consts["sin"], consts["decay"], consts["zeta"], consts["xi"], consts["cdec"]]
    return specs, args


def _ret_chunk_at(cfg, refs, ci):
    L = cfg.chunk
    cos_ref, sin_ref, decay_ref, zeta_ref, xi_ref, cdec_ref = refs
    rows = pl.ds(pl.multiple_of(ci * L, L), L)
    fn = functools.partial(_ret_chunk, cos=cos_ref[rows, :], sin=sin_ref[rows, :], valid=_valid_chunk(cfg, ci),
                           decay=decay_ref[...], zeta=zeta_ref[...], xi=xi_ref[...], cdec=cdec_ref[...],
                           scale=cfg.ret_dim ** -0.5)
    return fn, rows


def ret_fwd(cfg, u, col0, norm_g, consts):
    nb = u.shape[0] // cfg.tp
    dk, nc = cfg.ret_dim, cfg.nc
    specs, args = _ret_specs(cfg, u, col0, norm_g, consts)

    def body(q_ref, k_ref, v_ref, g_ref, ng_ref, *rest):
        y_ref, r_ref = rest[-2], rest[-1]
        r_ref[...] = jnp.zeros_like(r_ref)

        def step(ci, carry):
            fn, rows = _ret_chunk_at(cfg, rest[:6], ci)
            y, r_new = fn(q_ref[rows, :], k_ref[rows, :], v_ref[rows, :], g_ref[rows, :], ng_ref[...], r_ref[...])
            y_ref[rows, :] = y.astype(y_ref.dtype)
            r_ref[...] = r_new
            return carry

        lax.fori_loop(0, nc, step, 0)

    return pl.pallas_call(
        body, grid=(cfg.ret_heads, nb), in_specs=specs, out_specs=pl.BlockSpec((cfg.tp, dk), lambda h, b: (b, h)),
        out_shape=jax.ShapeDtypeStruct((u.shape[0], cfg.ret_w), BF16), scratch_shapes=[pltpu.VMEM((dk, dk), F32)],
        name="ret_fwd", compiler_params=_cparams())(*args)


def ret_bwd(cfg, u, col0, norm_g, consts, dy, dy_col0):
    nb = u.shape[0] // cfg.tp
    dk, nc, nh = cfg.ret_dim, cfg.nc, cfg.ret_heads
    specs, args = _ret_specs(cfg, u, col0, norm_g, consts)
    dy_base = dy_col0 // dk
    specs.append(pl.BlockSpec((cfg.tp, dk), lambda h, b: (b, dy_base + h)))
    seg_out = lambda s: pl.BlockSpec((cfg.tp, dk), functools.partial(lambda h, b, s: (b, s * nh + h), s=s))

    def body(q_ref, k_ref, v_ref, g_ref, ng_ref, *rest):
        consts_refs, dy_ref = rest[:6], rest[6]
        dq_ref, dk_ref, dv_ref, dg_ref, dng_ref, r_all, dr_ref = rest[7:]
        dr_ref[...] = jnp.zeros_like(dr_ref)

        def fwd(ci, carry):
            fn, rows = _ret_chunk_at(cfg, consts_refs, ci)
            r_all[ci] = dr_ref[...]
            _, r_new = fn(q_ref[rows, :], k_ref[rows, :], v_ref[rows, :], g_ref[rows, :], ng_ref[...], dr_ref[...])
            dr_ref[...] = r_new
            return carry

        lax.fori_loop(0, nc, fwd, 0)
        dr_ref[...] = jnp.zeros_like(dr_ref)

        @pl.when(pl.program_id(1) == 0)
        def _():
            dng_ref[...] = jnp.zeros_like(dng_ref)

        def bwd(kk, carry):
            ci = nc - 1 - kk
            fn, rows = _ret_chunk_at(cfg, consts_refs, ci)
            _, vjp = jax.vjp(fn, q_ref[rows, :], k_ref[rows, :], v_ref[rows, :], g_ref[rows, :], ng_ref[...], r_all[ci])
            dq, dkk, dv, dg, dng, dr = vjp((dy_ref[rows, :].astype(F32), dr_ref[...]))
            dq_ref[rows, :] = dq.astype(dq_ref.dtype)
            dk_ref[rows, :] = dkk.astype(dk_ref.dtype)
            dv_ref[rows, :] = dv.astype(dv_ref.dtype)
            dg_ref[rows, :] = dg.astype(dg_ref.dtype)
            dng_ref[...] += dng
            dr_ref[...] = dr
            return carry

        lax.fori_loop(0, nc, bwd, 0)

    seg_shape = jax.ShapeDtypeStruct((u.shape[0], cfg.ret_w), BF16)
    res = pl.pallas_call(
        body, grid=(nh, nb), in_specs=specs,
        out_specs=[pl.BlockSpec((cfg.tp, dk), lambda h, b: (b, h))] * 4 + [pl.BlockSpec((1, dk), lambda h, b: (0, h))],
        out_shape=[seg_shape] * 4 + [jax.ShapeDtypeStruct((1, cfg.ret_w), F32)],
        scratch_shapes=[pltpu.VMEM((nc, dk, dk), F32), pltpu.VMEM((dk, dk), F32)],
        name="ret_bwd", compiler_params=_cparams(dimension_semantics=("arbitrary", "arbitrary")))(*args, dy)
    return res[:4], res[4]


def _dot3(a, tri):
    return _edot(a, tri, 1, 0, 'b')


def _sb_specs(cfg, u, col0):
    base = col0 // LANES
    per = cfg.sb_w // LANES
    seg = lambda s: pl.BlockSpec((cfg.tp, LANES), functools.partial(lambda b, p, s: (b, base + s * per + p), s=s))
    return [seg(0), seg(1), seg(2)], [u, u, u]


def _sb_tile(cfg, qh, k, i, j, rid, cid):
    L = cfg.chunk
    z = _dot_nt(qh, k) * (cfg.sb_head_dim ** -0.5)
    kpos = j * L + cid
    mask = jnp.logical_and(kpos < i * L + rid, kpos >= cfg.pad)
    sp = _softplus(z)
    return z, sp, mask, jnp.where(mask, -sp, 0.0)


def sb_fwd(cfg, u, col0):
    nb = u.shape[0] // cfg.tp
    L, nc = cfg.chunk, cfg.nc
    hd = cfg.sb_head_dim
    specs, args = _sb_specs(cfg, u, col0)

    def body(q_ref, k_ref, v_ref, y_ref, tot_ref):
        rid, cid = _tri_consts(L)
        after_tri = (rid > cid).astype(BF16)
        lane = lax.broadcasted_iota(jnp.int32, (1, LANES), 1)

        def qloop(i, carry):
            qrows = pl.ds(pl.multiple_of(i * L, L), L)
            q = q_ref[qrows, :]
            accs = []
            for hh in range(2):
                mine = jnp.logical_and(lane >= hh * hd, lane < (hh + 1) * hd)
                qh = jnp.where(mine, q, 0.0).astype(BF16)

                def kloop(jj, c):
                    acc, run = c
                    j = i - jj
                    krows = pl.ds(pl.multiple_of(j * L, L), L)
                    z, sp, mask, l1m = _sb_tile(cfg, qh, k_ref[krows, :], i, j, rid, cid)
                    after = _dot3(l1m, after_tri) + run
                    w = jnp.where(mask, jnp.exp(z - sp + after), 0.0)
                    return acc + _dot(w, v_ref[krows, :]), run + jnp.sum(l1m, axis=1, keepdims=True)

                acc, run = lax.fori_loop(0, i + 1, kloop, (jnp.zeros((L, LANES), F32), jnp.zeros((L, 1), F32)))
                accs.append(acc)
                tot_ref[0, hh, qrows, :] = run
            y_ref[qrows, :] = jnp.where(lane < hd, accs[0], accs[1]).astype(y_ref.dtype)
            return carry

        lax.fori_loop(0, nc, qloop, 0)

    return pl.pallas_call(
        body, grid=(nb, cfg.sb_w // LANES), in_specs=specs,
        out_specs=[pl.BlockSpec((cfg.tp, LANES), lambda b, p: (b, p)),
                   pl.BlockSpec((1, 2, cfg.tp, 1), lambda b, p: (b, p, 0, 0))],
        out_shape=[jax.ShapeDtypeStruct((u.shape[0], cfg.sb_w), BF16),
                   jax.ShapeDtypeStruct((nb, cfg.sb_heads, cfg.tp, 1), F32)],
        name="sb_fwd", compiler_params=_cparams())(*args)


def sb_bwd(cfg, u, col0, tot, dy, dy_col0):
    nb = u.shape[0] // cfg.tp
    L, nc = cfg.chunk, cfg.nc
    hd = cfg.sb_head_dim
    scale = hd ** -0.5
    specs, args = _sb_specs(cfg, u, col0)
    dy_base = dy_col0 // LANES
    specs += [pl.BlockSpec((1, 2, cfg.tp, 1), lambda b, p: (b, p, 0, 0)),
              pl.BlockSpec((cfg.tp, LANES), lambda b, p: (b, dy_base + p))]

    def body(q_ref, k_ref, v_ref, tot_ref, dy_ref, dq_ref, dk_ref, dv_ref, dk_acc, dv_acc):
        rid, cid = _tri_consts(L)
        upto_tri = (rid <= cid).astype(BF16)
        before_tri = (rid < cid).astype(BF16)
        lane = lax.broadcasted_iota(jnp.int32, (1, LANES), 1)
        dk_acc[...] = jnp.zeros_like(dk_acc)
        dv_acc[...] = jnp.zeros_like(dv_acc)

        def qloop(i, carry):
            qrows = pl.ds(pl.multiple_of(i * L, L), L)
            q = q_ref[qrows, :]
            dy = dy_ref[qrows, :].astype(F32)
            dq_sum = jnp.zeros((L, LANES), F32)
            for hh in range(2):
                mine = jnp.logical_and(lane >= hh * hd, lane < (hh + 1) * hd)
                qh = jnp.where(mine, q, 0.0).astype(BF16)
                dyh = jnp.where(mine, dy, 0.0).astype(BF16)
                tot_q = tot_ref[0, hh, qrows, :]

                def kloop(j, c):
                    dq, run_l, run_a = c
                    krows = pl.ds(pl.multiple_of(j * L, L), L)
                    k = k_ref[krows, :]
                    v = v_ref[krows, :]
                    z, sp, mask, l1m = _sb_tile(cfg, qh, k, i, j, rid, cid)
                    after = tot_q - (run_l + _dot3(l1m, upto_tri))
                    w = jnp.where(mask, jnp.exp(z - sp + after), 0.0)
                    da = w * _dot_nt(dyh, v)
                    d_l1m = _dot3(da, before_tri) + run_a
                    sg = jax.nn.sigmoid(z)
                    dz = (jnp.where(mask, da * (1.0 - sg) - d_l1m * sg, 0.0) * scale).astype(BF16)
                    dv_acc[krows, :] += _dot_tn(w, dyh)
                    dk_acc[krows, :] += _dot_tn(dz, qh)
                    dq = dq + _dot(dz, jnp.where(mine, k, 0.0))
                    return dq, run_l + jnp.sum(l1m, axis=1, keepdims=True), run_a + jnp.sum(da, axis=1, keepdims=True)

                zero = jnp.zeros((L, 1), F32)
                dq_h, _, _ = lax.fori_loop(0, i + 1, kloop, (jnp.zeros((L, LANES), F32), zero, zero))
                dq_sum = dq_sum + dq_h
            dq_ref[qrows, :] = dq_sum.astype(dq_ref.dtype)
            return carry

        lax.fori_loop(0, nc, qloop, 0)
        dk_ref[...] = dk_acc[...].astype(dk_ref.dtype)
        dv_ref[...] = dv_acc[...].astype(dv_ref.dtype)

    out_spec = pl.BlockSpec((cfg.tp, LANES), lambda b, p: (b, p))
    seg_shape = jax.ShapeDtypeStruct((u.shape[0], cfg.sb_w), BF16)
    return pl.pallas_call(
        body, grid=(nb, cfg.sb_w // LANES), in_specs=specs, out_specs=[out_spec] * 3, out_shape=[seg_shape] * 3,
        scratch_shapes=[pltpu.VMEM((cfg.tp, LANES), F32), pltpu.VMEM((cfg.tp, LANES), F32)],
        name="sb_bwd", compiler_params=_cparams())(*args, tot, dy)


def _lru_cols(cfg):
    return _tile(cfg.lru_width, 256)


def lru_scan_fwd(cfg, a, b):
    nb = a.shape[0] // cfg.tp
    cw = _lru_cols(cfg)
    spec = pl.BlockSpec((cfg.tp, cw), lambda s, c: (s, c))

    def body(a_ref, b_ref, h_ref):
        def step(i, h):
            rows = pl.ds(pl.multiple_of(i * SUBLANES, SUBLANES), SUBLANES)
            at, bt = a_ref[rows, :], b_ref[rows, :]
            outs = []
            for r in range(SUBLANES):
                h = at[r:r + 1, :] * h + bt[r:r + 1, :]
                outs.append(h)
            h_ref[rows, :] = jnp.concatenate(outs, axis=0)
            return h

        lax.fori_loop(0, cfg.tp // SUBLANES, step, jnp.zeros((1, cw), F32))

    return pl.pallas_call(
        body, grid=(nb, cfg.lru_width // cw), in_specs=[spec, spec], out_specs=spec,
        out_shape=jax.ShapeDtypeStruct(a.shape, F32), name="lru_scan_fwd", compiler_params=_cparams())(a, b)


def lru_scan_bwd(cfg, a, h, dh):
    nb = a.shape[0] // cfg.tp
    cw = _lru_cols(cfg)
    nt = cfg.tp // SUBLANES
    spec = pl.BlockSpec((cfg.tp, cw), lambda s, c: (s, c))

    def body(a_ref, h_ref, dh_ref, da_ref, db_ref):
        def step(k, c):
            i = nt - 1 - k
            rows = pl.ds(pl.multiple_of(i * SUBLANES, SUBLANES), SUBLANES)
            prev = pl.ds(pl.multiple_of(jnp.maximum(i - 1, 0) * SUBLANES, SUBLANES), SUBLANES)
            at, ht, dht = a_ref[rows, :], h_ref[rows, :], dh_ref[rows, :]
            h_before = jnp.where(i > 0, h_ref[prev, :][SUBLANES - 1:SUBLANES, :], 0.0)
            das, dbs = [None] * SUBLANES, [None] * SUBLANES
            for r in range(SUBLANES - 1, -1, -1):
                g = dht[r:r + 1, :] + c
                dbs[r] = g
                das[r] = g * (ht[r - 1:r, :] if r > 0 else h_before)
                c = at[r:r + 1, :] * g
            da_ref[rows, :] = jnp.concatenate(das, axis=0)
            db_ref[rows, :] = jnp.concatenate(dbs, axis=0)
            return c

        lax.fori_loop(0, nt, step, jnp.zeros((1, cw), F32))

    return pl.pallas_call(
        body, grid=(nb, cfg.lru_width // cw), in_specs=[spec] * 3, out_specs=[spec] * 2,
        out_shape=[jax.ShapeDtypeStruct(a.shape, F32)] * 2, name="lru_scan_bwd", compiler_params=_cparams())(a, h, dh)


def final_loss(cfg, h, target, norm_g):
    assert cfg.seq % cfg.chunk == 0 and cfg.n_meta + cfg.pad == cfg.chunk
    L, nc, d = cfg.chunk, cfg.nc, cfg.d_model
    per_seq = cfg.seq // L

    def tgt_map(i):
        return ((i // nc) * per_seq + jnp.maximum(i % nc - 1, 0), 0)

    def body(h_ref, t_ref, g_ref, loss_ref, dh_ref, dg_ref):
        i = pl.program_id(0)
        real = jnp.where(i % nc == 0, 0.0, 1.0)
        tgt = t_ref[...]

        def loss_fn(hv, g):
            y = hv * lax.rsqrt(jnp.mean(hv * hv, axis=-1, keepdims=True) + EPS) * g
            return 0.5 * real * jnp.sum(jnp.mean(jnp.square(y - tgt), axis=-1))

        val, (dh, dg) = jax.value_and_grad(loss_fn, argnums=(0, 1))(h_ref[...], g_ref[...])

        @pl.when(i == 0)
        def _():
            loss_ref[...] = jnp.zeros_like(loss_ref)
            dg_ref[...] = jnp.zeros_like(dg_ref)

        loss_ref[...] += jnp.broadcast_to(val, loss_ref.shape)
        dg_ref[...] += dg
        dh_ref[...] = dh

    return pl.pallas_call(
        body, grid=(h.shape[0] // L,),
        in_specs=[pl.BlockSpec((L, d), lambda i: (i, 0)), pl.BlockSpec((L, d), tgt_map), pl.BlockSpec((1, d), lambda i: (0, 0))],
        out_specs=[pl.BlockSpec((SUBLANES, LANES), lambda i: (0, 0)), pl.BlockSpec((L, d), lambda i: (i, 0)),
                   pl.BlockSpec((1, d), lambda i: (0, 0))],
        out_shape=[jax.ShapeDtypeStruct((SUBLANES, LANES), F32), jax.ShapeDtypeStruct(h.shape, F32),
                   jax.ShapeDtypeStruct((1, d), F32)],
        name="final_loss", compiler_params=_cparams(dimension_semantics=("arbitrary",)))(h, target, norm_g)


def exchange(name, arrays, same):
    n = len(arrays)
    n_peer = 7
    hbm = pl.BlockSpec(memory_space=pl.ANY)
    out_shape = [jax.ShapeDtypeStruct((8,) + (a.shape if s else a.shape[1:]), a.dtype) for a, s in zip(arrays, same)]

    def body(*refs):
        ins, outs = refs[:n], refs[n:2 * n]
        send_sems, recv_sems, local_sems = refs[2 * n:]
        x, y, c = lax.axis_index("x"), lax.axis_index("y"), lax.axis_index("c")
        me = 4 * x + 2 * y + c

        def slab(w, dest):
            return ins[w] if same[w] else ins[w].at[dest]

        local = [pltpu.make_async_copy(slab(w, me), outs[w].at[me], local_sems.at[w]) for w in range(n)]
        for cp in local:
            cp.start()
        remote = []
        for r in range(1, n_peer + 1):
            px = 1 - x if r & 4 else x
            py = 1 - y if r & 2 else y
            pc = 1 - c if r & 1 else c
            pidx = 4 * px + 2 * py + pc
            for w in range(n):
                send = pltpu.make_async_remote_copy(
                    src_ref=slab(w, pidx), dst_ref=outs[w].at[me], send_sem=send_sems.at[w, r - 1],
                    recv_sem=recv_sems.at[w, r - 1], device_id=(px, py, pc), device_id_type=pl.DeviceIdType.MESH)
                recv = pltpu.make_async_remote_copy(
                    src_ref=slab(w, pidx), dst_ref=outs[w].at[pidx], send_sem=send_sems.at[w, r - 1],
                    recv_sem=recv_sems.at[w, r - 1], device_id=(px, py, pc), device_id_type=pl.DeviceIdType.MESH)
                send.start()
                remote.append((send, recv))
        for send, recv in remote:
            recv.wait_recv()
        for send, recv in remote:
            send.wait_send()
        for cp in local:
            cp.wait()

    return pl.pallas_call(
        body, in_specs=[hbm] * n, out_specs=[hbm] * n, out_shape=out_shape,
        scratch_shapes=[pltpu.SemaphoreType.DMA((n, n_peer)), pltpu.SemaphoreType.DMA((n, n_peer)),
                        pltpu.SemaphoreType.DMA((n,))],
        name=name, compiler_params=pltpu.CompilerParams(has_side_effects=True))(*arrays)


def _row_tile(r, mult, cap):
    best = None
    for t in range(mult, min(r, cap) + 1, mult):
        if r % t == 0:
            best = t
    return best if best is not None else r


def adamw(name, parts, w, m, v):
    r, c = w.shape
    tr = _row_tile(r, 16, 256)
    spec = pl.BlockSpec((tr, c), lambda i: (i, 0))

    def body(p_ref, w_ref, m_ref, v_ref, g_ref, d_ref, m2_ref, v2_ref):
        g = p_ref[0].astype(F32)
        for dev in range(1, 8):
            g = g + p_ref[dev].astype(F32)
        m2 = ADAM_B1 * m_ref[...] + (1.0 - ADAM_B1) * g
        v2 = ADAM_B2 * v_ref[...] + (1.0 - ADAM_B2) * jnp.square(g)
        m_hat = m2 / (1.0 - ADAM_B1 ** ADAM_STEP)
        v_hat = v2 / (1.0 - ADAM_B2 ** ADAM_STEP)
        g_ref[...] = g
        d_ref[...] = -ADAM_LR * (m_hat / (jnp.sqrt(v_hat) + ADAM_EPS) + ADAM_WD * w_ref[...])
        m2_ref[...] = m2
        v2_ref[...] = v2

    return pl.pallas_call(
        body, grid=(r // tr,), in_specs=[pl.BlockSpec((8, tr, c), lambda i: (0, i, 0)), spec, spec, spec],
        out_specs=[spec] * 4, out_shape=[jax.ShapeDtypeStruct((r, c), F32)] * 4, name=name, compiler_params=_cparams())(parts, w, m, v)


PACK_ROWS = 256


def _pack(arrs):
    flat = jnp.concatenate([a.reshape(-1).astype(F32) for a in arrs])
    quantum = PACK_ROWS * LANES
    total = -(-flat.shape[0] // quantum) * quantum
    return jnp.pad(flat, (0, total - flat.shape[0])).reshape(-1, LANES)


def _unpack(packed, shapes):
    flat = packed.reshape(-1)
    out, off = [], 0
    for s in shapes:
        n = int(np.prod(s))
        out.append(flat[off:off + n].reshape(s))
        off += n
    return out


def _pad_lanes(vec):
    return jnp.pad(vec.astype(F32), (0, LANES - vec.shape[0]))[None, :]


def make_step(cfg):
    I, BC, H, RW, SW, LW, F, D = (cfg.ssd_inner, cfg.ssd_bc, cfg.ssd_heads, cfg.ret_w, cfg.sb_w, cfg.lru_width,
                                  cfg.ffn_dim, cfg.d_model)
    ND = cfg.n_dev
    q_off = 2 * I + 2 * BC
    dt_off = q_off + 4 * RW

    def gather_cols(g):
        return jnp.transpose(g, (1, 0, 2)).reshape(g.shape[1], -1)

    def scatter_cols(full):
        r, c = full.shape
        return jnp.transpose(full.reshape(r, ND, c // ND), (1, 0, 2))

    def step(p, m_in, v_in, x, loss_target):
        nb = x.shape[0]
        rows = nb * cfg.tp

        small_shapes = [p[n].shape for n in SMALL_SHARDED]
        gathered = exchange("gather_weights", [p[n].astype(BF16) for n in BIG] + [_pack([p[n] for n in SMALL_SHARDED])],
                            [True] * (len(BIG) + 1))
        wfull = {}
        for n, g in zip(BIG, gathered):
            wfull[n] = gather_cols(g) if n in BIG_COL else g.reshape(-1, g.shape[-1])
        small_parts = [_unpack(gathered[-1][d], small_shapes) for d in range(ND)]
        sfull = {n: jnp.concatenate([small_parts[d][k] for d in range(ND)], axis=1) for k, n in enumerate(SMALL_SHARDED)}

        w0 = wfull['l0_w_in']
        w0a = jnp.concatenate([w0[:, :q_off], w0[:, q_off + H:], jnp.pad(w0[:, q_off:q_off + H], ((0, 0), (0, LANES - H)))], axis=1)

        sel_hd = (jnp.arange(LANES)[:, None] == (jnp.arange(I)[None, :] // cfg.ssd_head_dim)).astype(F32)
        sel_lane = (jnp.arange(LANES)[:, None] == (jnp.arange(H * LANES)[None, :] // LANES)).astype(F32)
        rconst = _ret_consts(cfg)
        row2 = lambda vec: vec.astype(F32)[None, :]

        meta = jnp.broadcast_to(sfull['meta_tokens'][None], (nb, cfg.n_meta, D))
        h0 = jnp.concatenate([jnp.zeros((nb, cfg.pad, D), F32), meta, x], axis=1).reshape(rows, D)

        def norm_fwd(name, h, g):
            return stage_fwd(name, cfg, _norm_fn, [Row(h, D)], [Par(row2(g))], [(D, BF16)])[0]

        def norm_bwd(name, h, g, dhn, dh_next):
            fn = lambda valid, hv, gv: (_norm_fn(valid, hv, gv)[0], hv)
            (dh,), (dg,) = stage_bwd(name, cfg, fn, [Row(h, D)], [Par(row2(g))], [[Row(dhn, D)], [Row(dh_next, D)]], [F32])
            return dh, dg[0]

        hn0 = norm_fwd("l0_mix_norm", h0, p['l0_mix_norm'])
        u0 = matmul("l0_in_proj", hn0, w0a)
        cw, cb = sfull['l0_ssd_conv_w'], p['l0_ssd_conv_b']
        ssd_pre = _make_ssd_pre(cfg)
        ssd_post = _make_ssd_post(cfg)
        pre_rows = [Row(u0, I, 1, tail=True), Row(u0, BC, (2 * I) // BC, tail=True), Row(u0, BC, (2 * I) // BC + 1, tail=True),
                    Row(u0, LANES, dt_off // LANES)]
        pre_pars = [Par(cw[:, :I]), Par(cw[:, I:I + BC]), Par(cw[:, I + BC:]), Par(row2(cb[:I])), Par(row2(cb[I:I + BC])),
                    Par(row2(cb[I + BC:])), Par(_pad_lanes(p['l0_ssd_dt_bias'])), Par(_pad_lanes(p['l0_ssd_a_log'])),
                    Par(sel_hd, diff=False), Par(sel_lane, diff=False)]
        xs, bm, cm, xdt, adt_b = stage_fwd("ssd_pre", cfg, ssd_pre, pre_rows, pre_pars,
                                           [(I, F32), (BC, F32), (BC, F32), (I, F32), (H * LANES, F32)])
        y_raw = ssd_scan_fwd(cfg, xdt, bm, cm, adt_b)
        post_rows = [Row(y_raw, I), Row(xs, I), Row(u0, I, 0)]
        post_pars = [Par(_pad_lanes(p['l0_ssd_d'])), Par(row2(p['l0_ssd_norm'])), Par(sel_hd, diff=False)]
        (y_ssd,) = stage_fwd("ssd_post", cfg, ssd_post, post_rows, post_pars, [(I, BF16)])
        ret_g = row2(p['l0_ret_norm'])
        y_ret = ret_fwd(cfg, u0, q_off, ret_g, rconst)
        ycat0 = jnp.concatenate([y_ssd, y_ret], axis=1)
        h1 = matmul("l0_out_proj", ycat0, wfull['l0_w_out'], resid=h0, cfg=cfg)

        def ffn_fwd(tag, h, norm_g, w_in, conv_w, conv_b, w_out):
            hn = norm_fwd(tag + "_ffn_norm", h, norm_g)
            u = matmul(tag + "_ffn_in", hn, w_in)
            rws = [Row(u, F, 0, tail=True), Row(u, F, 1, tail=True)]
            prs = [Par(conv_w[:, :F]), Par(conv_w[:, F:]), Par(row2(conv_b[:F])), Par(row2(conv_b[F:]))]
            (act,) = stage_fwd(tag + "_ffn_act", cfg, _ffn_act_fn, rws, prs, [(F, BF16)])
            h_out = matmul(tag + "_ffn_out", act, w_out, resid=h, cfg=cfg)
            return h_out, (hn, u, rws, prs, act)

        h2, ffn0_saved = ffn_fwd("l0", h1, p['l0_ffn_norm'], wfull['l0_ffn_w_in'], sfull['l0_ffn_conv_w'], p['l0_ffn_conv_b'],
                                 wfull['l0_ffn_w_out'])

        hn2 = norm_fwd("l1_mix_norm", h2, p['l1_mix_norm'])
        u1 = matmul("l1_in_proj", hn2, wfull['l1_w_in'])
        y_sb, sb_tot = sb_fwd(cfg, u1, 0)
        lru_pre = _make_lru_pre(cfg)
        gate_blk = (3 * SW) // LW
        lpre_rows = [Row(u1, LW, gate_blk + 1, tail=True)]
        lpre_pars = [Par(sfull['l1_lru_conv_w']), Par(row2(p['l1_lru_conv_b'])), Par(p['l1_lru_wa']), Par(row2(p['l1_lru_ba'])),
                     Par(p['l1_lru_wx']), Par(row2(p['l1_lru_bx'])), Par(row2(p['l1_lru_lambda']))]
        lru_a, lru_b = stage_fwd("lru_pre", cfg, lru_pre, lpre_rows, lpre_pars, [(LW, F32), (LW, F32)])
        lru_h = lru_scan_fwd(cfg, lru_a, lru_b)
        lpost_rows = [Row(lru_h, LW), Row(u1, LW, gate_blk)]
        (y_lru,) = stage_fwd("lru_post", cfg, _lru_post_fn, lpost_rows, [], [(LW, BF16)])
        ycat1 = jnp.concatenate([y_sb, y_lru], axis=1)
        h3 = matmul("l1_out_proj", ycat1, wfull['l1_w_out'], resid=h2, cfg=cfg)
        h4, ffn1_saved = ffn_fwd("l1", h3, p['l1_ffn_norm'], wfull['l1_ffn_w_in'], sfull['l1_ffn_conv_w'], p['l1_ffn_conv_b'],
                                 wfull['l1_ffn_w_out'])

        loss_part, dh4, d_final = final_loss(cfg, h4, loss_target.reshape(nb * cfg.seq, D), row2(p['final_norm']))
        loss = lax.psum(loss_part[0, 0], ("x", "y", "c"))
        gfull, grep = {}, {'final_norm': d_final[0]}

        def ffn_bwd(tag, h, norm_g, w_in, w_out, saved, dh_out):
            hn, u, rws, prs, act = saved
            dact = matmul(tag + "_ffn_out_dx", dh_out, w_out, tb=True, out_dtype=BF16)
            gfull[tag + '_ffn_w_out'] = matmul(tag + "_ffn_out_dw", act, dh_out, ta=True, out_dtype=BF16, tm_cap=512, tn_cap=512, tk_cap=4352)
            (dug, duu), (dwg, dwu, dbg, dbu) = stage_bwd(tag + "_ffn_act_bwd", cfg, _ffn_act_fn, rws, prs, [[Row(dact, F)]], [BF16, BF16])
            du = jnp.concatenate([dug, duu], axis=1)
            gfull[tag + '_ffn_conv_w'] = jnp.concatenate([dwg, dwu], axis=1)
            grep[tag + '_ffn_conv_b'] = jnp.concatenate([dbg, dbu], axis=1)[0]
            dhn = matmul(tag + "_ffn_in_dx", du, w_in, tb=True)
            gfull[tag + '_ffn_w_in'] = matmul(tag + "_ffn_in_dw", hn, du, ta=True, out_dtype=BF16, tm_cap=512, tn_cap=512, tk_cap=4352)
            dh, grep[tag + '_ffn_norm'] = norm_bwd(tag + "_ffn_norm_bwd", h, norm_g, dhn, dh_out)
            return dh

        dh3 = ffn_bwd("l1", h3, p['l1_ffn_norm'], wfull['l1_ffn_w_in'], wfull['l1_ffn_w_out'], ffn1_saved, dh4)

        dycat1 = matmul("l1_out_dx", dh3, wfull['l1_w_out'], tb=True, out_dtype=BF16)
        gfull['l1_w_out'] = matmul("l1_out_dw", ycat1, dh3, ta=True, out_dtype=BF16, tm_cap=512, tn_cap=512, tk_cap=4352)
        (d_lru_h, d_gate), _ = stage_bwd("lru_post_bwd", cfg, _lru_post_fn, lpost_rows, [], [[Row(dycat1, LW, SW // LW)]], [F32, BF16])
        d_lru_a, d_lru_b = lru_scan_bwd(cfg, lru_a, lru_h, d_lru_h)
        (d_xr,), lgr = stage_bwd("lru_pre_bwd", cfg, lru_pre, lpre_rows, lpre_pars, [[Row(d_lru_a, LW)], [Row(d_lru_b, LW)]], [BF16])
        gfull['l1_lru_conv_w'] = lgr[0]
        grep.update({'l1_lru_conv_b': lgr[1][0], 'l1_lru_wa': lgr[2], 'l1_lru_ba': lgr[3][0], 'l1_lru_wx': lgr[4],
                     'l1_lru_bx': lgr[5][0], 'l1_lru_lambda': lgr[6][0]})
        d_q, d_k, d_v = sb_bwd(cfg, u1, 0, sb_tot, dycat1, 0)
        du1 = jnp.concatenate([d_q, d_k, d_v, d_gate, d_xr], axis=1)
        dhn2 = matmul("l1_in_dx", du1, wfull['l1_w_in'], tb=True)
        gfull['l1_w_in'] = matmul("l1_in_dw", hn2, du1, ta=True, out_dtype=BF16, tm_cap=512, tn_cap=512, tk_cap=4352)
        dh2, grep['l1_mix_norm'] = norm_bwd("l1_mix_norm_bwd", h2, p['l1_mix_norm'], dhn2, dh3)

        dh1 = ffn_bwd("l0", h1, p['l0_ffn_norm'], wfull['l0_ffn_w_in'], wfull['l0_ffn_w_out'], ffn0_saved, dh2)

        dycat0 = matmul("l0_out_dx", dh1, wfull['l0_w_out'], tb=True, out_dtype=BF16)
        gfull['l0_w_out'] = matmul("l0_out_dw", ycat0, dh1, ta=True, out_dtype=BF16, tm_cap=512, tn_cap=512, tk_cap=4352)
        (d_yraw, d_xs1, d_z), (d_dskip, d_ssdnorm) = stage_bwd("ssd_post_bwd", cfg, ssd_post, post_rows, post_pars,
                                                               [[Row(dycat0, I, 0)]], [F32, F32, BF16])
        d_xdt, d_bm, d_cm, d_adt = ssd_scan_bwd(cfg, xdt, bm, cm, adt_b, d_yraw)
        (d_xs, d_b, d_c, d_dt), sgr = stage_bwd("ssd_pre_bwd", cfg, ssd_pre, pre_rows, pre_pars,
                                                 [[Row(d_xs1, I)], [Row(d_bm, BC)], [Row(d_cm, BC)], [Row(d_xdt, I)], [Row(d_adt, H * LANES)]],
                                                 [BF16] * 4)
        gfull['l0_ssd_conv_w'] = jnp.concatenate(sgr[0:3], axis=1)
        grep.update({'l0_ssd_conv_b': jnp.concatenate(sgr[3:6], axis=1)[0], 'l0_ssd_dt_bias': sgr[6][0, :H],
                     'l0_ssd_a_log': sgr[7][0, :H], 'l0_ssd_d': d_dskip[0, :H], 'l0_ssd_norm': d_ssdnorm[0]})
        (d_rq, d_rk, d_rv, d_rg), d_retnorm = ret_bwd(cfg, u0, q_off, ret_g, rconst, dycat0, I)
        grep['l0_ret_norm'] = d_retnorm[0]
        du0 = jnp.concatenate([d_z, d_xs, d_b, d_c, d_rq, d_rk, d_rv, d_rg, d_dt], axis=1)
        dhn0 = matmul("l0_in_dx", du0, w0a, tb=True, tk_cap=2432)
        dw0a = matmul("l0_in_dw", hn0, du0, ta=True, out_dtype=BF16, tm_cap=512, tn_cap=512, tk_cap=4352)
        gfull['l0_w_in'] = jnp.concatenate([dw0a[:, :q_off], dw0a[:, dt_off:dt_off + H], dw0a[:, q_off:dt_off]], axis=1)
        dh0, grep['l0_mix_norm'] = norm_bwd("l0_mix_norm_bwd", h0, p['l0_mix_norm'], dhn0, dh1)

        dh0 = dh0.reshape(nb, cfg.tp, D)
        grad_x = dh0[:, cfg.chunk:, :]
        gfull['meta_tokens'] = jnp.sum(dh0[:, cfg.pad:cfg.chunk, :], axis=0)

        big_send = [scatter_cols(gfull[n]) if n in BIG_COL else gfull[n].reshape(ND, -1, gfull[n].shape[-1]) for n in BIG]
        small_send = jnp.stack([_pack([scatter_cols(gfull[n])[d] for n in SMALL_SHARDED]) for d in range(ND)])
        rep_send = _pack([grep[n] for n in REPLICATED])
        got = exchange("exchange_grads", big_send + [small_send, rep_send], [False] * (len(BIG) + 1) + [True])

        grad, delta, new_m, new_v = {}, {}, {}, {}
        for n, parts in zip(BIG, got):
            grad[n], delta[n], new_m[n], new_v[n] = adamw("adamw_" + n, parts, p[n], m_in[n], v_in[n])
        for names, parts in ((SMALL_SHARDED, got[len(BIG)]), (REPLICATED, got[len(BIG) + 1])):
            shapes = [p[n].shape for n in names]
            res = adamw("adamw_small_sharded" if names is SMALL_SHARDED else "adamw_replicated", parts,
                        _pack([p[n] for n in names]), _pack([m_in[n] for n in names]), _pack([v_in[n] for n in names]))
            for dst, packed in zip((grad, delta, new_m, new_v), res):
                for n, a in zip(names, _unpack(packed, shapes)):
                    dst[n] = a
        return (loss, grad_x, *[grad[n] for n in WEIGHTS], *[delta[n] for n in WEIGHTS], *[new_m[n] for n in WEIGHTS],
                *[new_v[n] for n in WEIGHTS])

    return step


_STEP = make_step(CFG)


def kernel(x, meta_tokens, l0_mix_norm, l0_w_in, l0_ssd_conv_w, l0_ssd_conv_b, l0_ssd_dt_bias, l0_ssd_a_log, l0_ssd_d, l0_ssd_norm, l0_ret_norm, l0_w_out, l0_ffn_norm, l0_ffn_w_in, l0_ffn_conv_w, l0_ffn_conv_b, l0_ffn_w_out, l1_mix_norm, l1_w_in, l1_lru_conv_w, l1_lru_conv_b, l1_lru_wa, l1_lru_ba, l1_lru_wx, l1_lru_bx, l1_lru_lambda, l1_w_out, l1_ffn_norm, l1_ffn_w_in, l1_ffn_conv_w, l1_ffn_conv_b, l1_ffn_w_out, final_norm, loss_target, m_meta_tokens, m_l0_mix_norm, m_l0_w_in, m_l0_ssd_conv_w, m_l0_ssd_conv_b, m_l0_ssd_dt_bias, m_l0_ssd_a_log, m_l0_ssd_d, m_l0_ssd_norm, m_l0_ret_norm, m_l0_w_out, m_l0_ffn_norm, m_l0_ffn_w_in, m_l0_ffn_conv_w, m_l0_ffn_conv_b, m_l0_ffn_w_out, m_l1_mix_norm, m_l1_w_in, m_l1_lru_conv_w, m_l1_lru_conv_b, m_l1_lru_wa, m_l1_lru_ba, m_l1_lru_wx, m_l1_lru_bx, m_l1_lru_lambda, m_l1_w_out, m_l1_ffn_norm, m_l1_ffn_w_in, m_l1_ffn_conv_w, m_l1_ffn_conv_b, m_l1_ffn_w_out, m_final_norm, v_meta_tokens, v_l0_mix_norm, v_l0_w_in, v_l0_ssd_conv_w, v_l0_ssd_conv_b, v_l0_ssd_dt_bias, v_l0_ssd_a_log, v_l0_ssd_d, v_l0_ssd_norm, v_l0_ret_norm, v_l0_w_out, v_l0_ffn_norm, v_l0_ffn_w_in, v_l0_ffn_conv_w, v_l0_ffn_conv_b, v_l0_ffn_w_out, v_l1_mix_norm, v_l1_w_in, v_l1_lru_conv_w, v_l1_lru_conv_b, v_l1_lru_wa, v_l1_lru_ba, v_l1_lru_wx, v_l1_lru_bx, v_l1_lru_lambda, v_l1_w_out, v_l1_ffn_norm, v_l1_ffn_w_in, v_l1_ffn_conv_w, v_l1_ffn_conv_b, v_l1_ffn_w_out, v_final_norm):
    args = locals()
    p = {n: args[n] for n in WEIGHTS}
    m_in = {n: args["m_" + n] for n in WEIGHTS}
    v_in = {n: args["v_" + n] for n in WEIGHTS}
    return _STEP(p, m_in, v_in, x, loss_target)
```

```python
import functools
from typing import NamedTuple

import numpy as np
import jax
import jax.numpy as jnp
from jax import lax
from jax.experimental import pallas as pl
from jax.experimental.pallas import tpu as pltpu

F32 = jnp.float32
BF16 = jnp.bfloat16
EPS = 1e-6
LRU_C = 8.0
NEG = -1e30
SUBLANES = 8
LANES = 128
VMEM_LIMIT = 56 * 1024 * 1024

ADAM_LR = 0.001
ADAM_B1 = 0.9
ADAM_B2 = 0.999
ADAM_EPS = 1e-08
ADAM_WD = 0.01
ADAM_STEP = 10

WEIGHTS = ['meta_tokens', 'l0_mix_norm', 'l0_w_in', 'l0_ssd_conv_w', 'l0_ssd_conv_b', 'l0_ssd_dt_bias', 'l0_ssd_a_log',
           'l0_ssd_d', 'l0_ssd_norm', 'l0_ret_norm', 'l0_w_out', 'l0_ffn_norm', 'l0_ffn_w_in', 'l0_ffn_conv_w',
           'l0_ffn_conv_b', 'l0_ffn_w_out', 'l1_mix_norm', 'l1_w_in', 'l1_lru_conv_w', 'l1_lru_conv_b', 'l1_lru_wa',
           'l1_lru_ba', 'l1_lru_wx', 'l1_lru_bx', 'l1_lru_lambda', 'l1_w_out', 'l1_ffn_norm', 'l1_ffn_w_in',
           'l1_ffn_conv_w', 'l1_ffn_conv_b', 'l1_ffn_w_out', 'final_norm']
BIG_COL = ['l0_w_in', 'l0_ffn_w_in', 'l1_w_in', 'l1_ffn_w_in']
BIG_ROW = ['l0_w_out', 'l0_ffn_w_out', 'l1_w_out', 'l1_ffn_w_out']
BIG = BIG_COL + BIG_ROW
SMALL_SHARDED = ['meta_tokens', 'l0_ssd_conv_w', 'l0_ffn_conv_w', 'l1_lru_conv_w', 'l1_ffn_conv_w']
REPLICATED = [w for w in WEIGHTS if w not in BIG and w not in SMALL_SHARDED]


class Config(NamedTuple):
    d_model: int = 1024
    seq: int = 2048
    n_meta: int = 16
    chunk: int = 128
    ssd_heads: int = 16
    ssd_head_dim: int = 64
    ssd_groups: int = 4
    ssd_state: int = 128
    ret_heads: int = 4
    ret_dim: int = 256
    sb_heads: int = 16
    sb_head_dim: int = 64
    lru_width: int = 1024
    lru_blocks: int = 8
    ffn_dim: int = 2816
    n_dev: int = 8

    @property
    def t(self):
        return self.n_meta + self.seq

    @property
    def pad(self):
        return (-self.t) % self.chunk

    @property
    def tp(self):
        return self.t + self.pad

    @property
    def nc(self):
        return self.tp // self.chunk

    @property
    def ssd_inner(self):
        return self.ssd_heads * self.ssd_head_dim

    @property
    def ssd_gw(self):
        return self.ssd_inner // self.ssd_groups

    @property
    def ssd_bc(self):
        return self.ssd_groups * self.ssd_state

    @property
    def ret_w(self):
        return self.ret_heads * self.ret_dim

    @property
    def sb_w(self):
        return self.sb_heads * self.sb_head_dim

    @property
    def mix0_segs(self):
        return (self.ssd_inner, self.ssd_inner + 2 * self.ssd_bc, self.ssd_heads, self.ret_w, self.ret_w, self.ret_w, self.ret_w)


CFG = Config()


def _dg(a, b, ca, cb):
    return lax.dot_general(a, b, (((ca,), (cb,)), ((), ())), preferred_element_type=F32)


def _cot_a(b, ca, cb, g):
    return _dg(g, b, 1, 1 - cb) if ca == 1 else _dg(b, g, 1 - cb, 1)


def _cot_b(a, ca, cb, g):
    return _dg(a, g, 1 - ca, 0) if cb == 0 else _dg(g, a, 0, 1 - ca)


@functools.partial(jax.custom_vjp, nondiff_argnums=(2, 3))
def _bdot(a, b, ca, cb):
    return _dg(a.astype(BF16), b.astype(BF16), ca, cb)


def _bdot_fwd(a, b, ca, cb):
    return _bdot(a, b, ca, cb), (a, b)


def _bdot_bwd(ca, cb, res, g):
    a, b = res
    gb = g.astype(BF16)
    return _cot_a(b.astype(BF16), ca, cb, gb).astype(a.dtype), _cot_b(a.astype(BF16), ca, cb, gb).astype(b.dtype)


_bdot.defvjp(_bdot_fwd, _bdot_bwd)


def _dot(a, b):
    return _bdot(a, b, 1, 0)


def _dot_nt(a, b):
    return _bdot(a, b, 1, 1)


def _dot_tn(a, b):
    return _bdot(a, b, 0, 0)


def _split3(a):
    hi = a.astype(BF16)
    r1 = a - hi.astype(F32)
    mid = r1.astype(BF16)
    return hi, mid, (r1 - mid.astype(F32)).astype(BF16)


@functools.partial(jax.custom_vjp, nondiff_argnums=(2, 3, 4))
def _edot(a, b, ca, cb, const):
    if const == 'b':
        bb = b.astype(BF16)
        return sum(_dg(p, bb, ca, cb) for p in _split3(a))
    ab = a.astype(BF16)
    return sum(_dg(ab, p, ca, cb) for p in _split3(b))


def _edot_fwd(a, b, ca, cb, const):
    return _edot(a, b, ca, cb, const), (a, b)


def _edot_bwd(ca, cb, const, res, g):
    a, b = res
    if const == 'b':
        bb = b.astype(BF16)
        return sum(_cot_a(bb, ca, cb, p) for p in _split3(g)), jnp.zeros_like(b)
    ab = a.astype(BF16)
    return jnp.zeros_like(a), sum(_cot_b(ab, ca, cb, p) for p in _split3(g))


_edot.defvjp(_edot_fwd, _edot_bwd)


def _dot_exact01(a, sel):
    return _edot(a, sel, 1, 0, 'b')


def _silu(x):
    return x * jax.nn.sigmoid(x)


def _softplus(x):
    return jnp.maximum(x, 0.0) + jnp.log1p(jnp.exp(-jnp.abs(x)))


def _neg_expm1(x):
    series = -x * (1.0 + x * 0.5 * (1.0 + x / 3.0 * (1.0 + x * 0.25)))
    return jnp.where(x > -0.01, series, 1.0 - jnp.exp(x))


def _causal_conv(x, tail, w, b):
    taps, rows = w.shape[0], x.shape[0]
    xx = jnp.concatenate([tail, x], axis=0)
    y = b
    for k in range(taps):
        off = SUBLANES - (taps - 1 - k)
        y = y + w[k:k + 1, :] * xx[off:off + rows, :]
    return y


def _valid_chunk(cfg, chunk_idx):
    first = (chunk_idx % cfg.nc) == 0
    rid = lax.broadcasted_iota(jnp.int32, (cfg.chunk, 1), 0)
    return jnp.where(jnp.logical_and(first, rid < cfg.pad), 0.0, 1.0).astype(F32)


def _cparams(**kw):
    return pltpu.CompilerParams(vmem_limit_bytes=VMEM_LIMIT, **kw)


class Row(NamedTuple):
    arr: jax.Array
    width: int
    blk: int = 0
    tail: bool = False
    diff: bool = True


class Par(NamedTuple):
    arr: jax.Array
    diff: bool = True


def _row_specs(cfg, rows, order):
    tr = cfg.chunk
    specs, args = [], []
    for r in rows:
        specs.append(pl.BlockSpec((tr, r.width), functools.partial(lambda i, b, o: (o(i), b), b=r.blk, o=order)))
        args.append(r.arr)
        if r.tail:
            per = tr // SUBLANES
            specs.append(pl.BlockSpec((SUBLANES, r.width),
                                      functools.partial(lambda i, b, o: (jnp.maximum(o(i) * per - 1, 0), b), b=r.blk, o=order)))
            args.append(r.arr)
    return specs, args


def _par_specs(pars):
    specs = [pl.BlockSpec(p.arr.shape, functools.partial(lambda i, nd: (0,) * nd, nd=p.arr.ndim)) for p in pars]
    return specs, [p.arr for p in pars]


def stage_fwd(name, cfg, fn, rows, pars, outs):
    tr = cfg.chunk
    n_rows = rows[0].arr.shape[0]
    rspecs, rargs = _row_specs(cfg, rows, lambda i: i)
    pspecs, pargs = _par_specs(pars)
    n_in = len(rargs) + len(pargs)

    def body(*refs):
        valid = _valid_chunk(cfg, pl.program_id(0))
        vals = [r[...].astype(F32) for r in refs[:n_in]]
        res = fn(valid, *vals)
        for o, v in zip(refs[n_in:], res):
            o[...] = v.astype(o.dtype)

    return pl.pallas_call(
        body, grid=(n_rows // tr,), in_specs=rspecs + pspecs,
        out_specs=[pl.BlockSpec((tr, w), lambda i: (i, 0)) for w, _ in outs],
        out_shape=[jax.ShapeDtypeStruct((n_rows, w), dt) for w, dt in outs],
        name=name, compiler_params=_cparams())(*rargs, *pargs)


def stage_bwd(name, cfg, fn, rows, pars, douts, drow_dtypes):
    tr = cfg.chunk
    n_rows = rows[0].arr.shape[0]
    n_blk = n_rows // tr
    order = lambda i: n_blk - 1 - i
    rspecs, rargs = _row_specs(cfg, rows, order)
    pspecs, pargs = _par_specs(pars)
    pieces = [p for out in douts for p in out]
    dspecs, dargs = _row_specs(cfg, pieces, order)
    n_r, n_p, n_d = len(rargs), len(pargs), len(dargs)
    diff_rows = [r for r in rows if r.diff]
    diff_pars = [p for p in pars if p.diff]
    tails = [r for r in diff_rows if r.tail]

    def body(*refs):
        in_refs = refs[:n_r + n_p]
        d_refs = refs[n_r + n_p:n_r + n_p + n_d]
        o_refs = refs[n_r + n_p + n_d:]
        drow_refs = o_refs[:len(diff_rows)]
        dpar_refs = o_refs[len(diff_rows):len(diff_rows) + len(diff_pars)]
        carry_refs = o_refs[len(diff_rows) + len(diff_pars):]
        step = pl.program_id(0)
        valid = _valid_chunk(cfg, order(step))
        vals = [r[...].astype(F32) for r in in_refs]
        slots, pos = [], 0
        for r in rows:
            if r.diff:
                slots.append(pos)
                if r.tail:
                    slots.append(pos + 1)
            pos += 2 if r.tail else 1
        for p in pars:
            if p.diff:
                slots.append(pos)
            pos += 1

        def g(*dv):
            full = list(vals)
            for s, v in zip(slots, dv):
                full[s] = v
            return tuple(fn(valid, *full))

        _, vjp = jax.vjp(g, *[vals[s] for s in slots])
        cts, k = [], 0
        for out in douts:
            parts = [d_refs[k + j][...].astype(F32) for j in range(len(out))]
            k += len(out)
            cts.append(parts[0] if len(parts) == 1 else jnp.concatenate(parts, axis=1))
        grads = list(vjp(tuple(cts)))

        @pl.when(step == 0)
        def _():
            for c in carry_refs:
                c[...] = jnp.zeros_like(c)
            for d in dpar_refs:
                d[...] = jnp.zeros_like(d)

        gi, ci = 0, 0
        for r, o in zip(diff_rows, drow_refs):
            dx = grads[gi]
            gi += 1
            if r.tail:
                dtail = grads[gi]
                gi += 1
                c = carry_refs[ci]
                ci += 1
                dx = dx + jnp.concatenate([jnp.zeros((tr - SUBLANES, r.width), F32), c[...]], axis=0)
                c[...] = dtail
            o[...] = dx.astype(o.dtype)
        for d in dpar_refs:
            d[...] += grads[gi]
            gi += 1

    out_specs = [pl.BlockSpec((tr, r.width), lambda i: (order(i), 0)) for r in diff_rows]
    out_shape = [jax.ShapeDtypeStruct((n_rows, r.width), dt) for r, dt in zip(diff_rows, drow_dtypes)]
    for p in diff_pars:
        out_specs.append(pl.BlockSpec(p.arr.shape, functools.partial(lambda i, nd: (0,) * nd, nd=p.arr.ndim)))
        out_shape.append(jax.ShapeDtypeStruct(p.arr.shape, F32))
    res = pl.pallas_call(
        body, grid=(n_blk,), in_specs=rspecs + pspecs + dspecs, out_specs=out_specs, out_shape=out_shape,
        scratch_shapes=[pltpu.VMEM((SUBLANES, r.width), F32) for r in tails],
        name=name, compiler_params=_cparams(dimension_semantics=("arbitrary",)))(*rargs, *pargs, *dargs)
    return list(res[:len(diff_rows)]), list(res[len(diff_rows):])


def _tile(n, cap):
    if n % LANES:
        return n
    q = n // LANES
    best = 1
    for k in range(1, q + 1):
        if q % k == 0 and k * LANES <= cap:
            best = k
    return best * LANES


def _mm_vmem(tm, tn, tk, a_bytes, b_bytes, out_bytes, resid, nk):
    total = 2 * (tm * tk * a_bytes + tk * tn * b_bytes + tm * tn * out_bytes) + tm * tn * 4
    total += 2 * tm * tn * 4 if resid else 0
    total += (tm * tk * 2 if a_bytes == 4 else 0) + (tk * tn * 2 if b_bytes == 4 else 0)
    return total + (tm * tn * 4 if nk > 1 else 0)


def matmul(name, a, b, *, ta=False, tb=False, out_dtype=F32, resid=None, cfg=None, tm_cap=2176, tn_cap=1024, tk_cap=2816):
    m, k = (a.shape[1], a.shape[0]) if ta else a.shape
    n = b.shape[0] if tb else b.shape[1]
    tm, tn, tk = _tile(m, tm_cap), _tile(n, tn_cap), _tile(k, tk_cap)
    budget = VMEM_LIMIT - 8 * 1024 * 1024
    sizes = (a.dtype.itemsize, b.dtype.itemsize, jnp.dtype(out_dtype).itemsize, resid is not None)
    while _mm_vmem(tm, tn, tk, *sizes, k // tk) > budget:
        if tm > 256:
            tm = _tile(m, tm - 1)
        elif tk > 512:
            tk = _tile(k, tk - 1)
        else:
            tn = _tile(n, tn - 1)
    nk = k // tk
    a_spec = pl.BlockSpec((tk, tm), lambda i, j, l: (l, i)) if ta else pl.BlockSpec((tm, tk), lambda i, j, l: (i, l))
    b_spec = pl.BlockSpec((tn, tk), lambda i, j, l: (j, l)) if tb else pl.BlockSpec((tk, tn), lambda i, j, l: (l, j))
    dims = (((0 if ta else 1,), (1 if tb else 0,)), ((), ()))
    in_specs, args = [a_spec, b_spec], [a, b]
    if resid is not None:
        in_specs.append(pl.BlockSpec((tm, tn), lambda i, j, l: (i, j)))
        args.append(resid)

    def body(*refs):
        a_ref, b_ref = refs[0], refs[1]
        o_ref = refs[3] if resid is not None else refs[2]
        part = lax.dot_general(a_ref[...].astype(BF16), b_ref[...].astype(BF16), dims, preferred_element_type=F32)

        def finish(res):
            if resid is not None:
                row = pl.program_id(0) * tm + lax.broadcasted_iota(jnp.int32, (tm, 1), 0)
                inpad = jnp.zeros((tm, 1), jnp.bool_)
                for s in range(m // cfg.tp):
                    inpad = jnp.logical_or(inpad, jnp.logical_and(row >= s * cfg.tp, row < s * cfg.tp + cfg.pad))
                res = refs[2][...] + jnp.where(inpad, 0.0, res)
            o_ref[...] = res.astype(o_ref.dtype)

        if nk == 1:
            finish(part)
            return
        acc = refs[-1]
        l = pl.program_id(2)

        @pl.when(l == 0)
        def _():
            acc[...] = part

        @pl.when(jnp.logical_and(l > 0, l < nk - 1))
        def _():
            acc[...] += part

        @pl.when(l == nk - 1)
        def _():
            finish(acc[...] + part)

    return pl.pallas_call(
        body, grid=(m // tm, n // tn, nk), in_specs=in_specs, out_specs=pl.BlockSpec((tm, tn), lambda i, j, l: (i, j)),
        out_shape=jax.ShapeDtypeStruct((m, n), out_dtype), scratch_shapes=[pltpu.VMEM((tm, tn), F32)] if nk > 1 else [],
        name=name, compiler_params=_cparams(dimension_semantics=("parallel", "parallel", "arbitrary")))(*args)


def _norm_fn(valid, h, g):
    y = h * lax.rsqrt(jnp.mean(h * h, axis=-1, keepdims=True) + EPS)
    return (valid * (y * g),)


def _make_ssd_pre(cfg):
    nh = cfg.ssd_heads

    def fn(valid, xs, xs_t, bm, bm_t, cm, cm_t, dtr, w_xs, w_b, w_c, b_xs, b_b, b_c, dt_bias, a_log, sel_hd, sel_lane):
        xs = valid * _silu(_causal_conv(xs, xs_t, w_xs, b_xs))
        bm = valid * _silu(_causal_conv(bm, bm_t, w_b, b_b))
        cm = valid * _silu(_causal_conv(cm, cm_t, w_c, b_c))
        lane = lax.broadcasted_iota(jnp.int32, (1, LANES), 1)
        head = jnp.where(lane < nh, 1.0, 0.0).astype(F32)
        dt = valid * head * _softplus(dtr + dt_bias)
        adt = dt * (-jnp.exp(a_log))
        x = xs * _dot_exact01(dt, sel_hd)
        adt_b = _dot_exact01(adt, sel_lane)
        return xs, bm, cm, x, adt_b

    return fn


def _make_ssd_post(cfg):
    gw = cfg.ssd_gw

    def fn(valid, y_raw, xs, z, d_skip, norm_g, sel_hd):
        d_rep = _dot_exact01(jnp.broadcast_to(d_skip, (SUBLANES, LANES)), sel_hd)[0:1]
        y = (y_raw + xs * d_rep) * _silu(z)
        parts = []
        for g in range(cfg.ssd_groups):
            yg = y[:, g * gw:(g + 1) * gw]
            parts.append(yg * lax.rsqrt(jnp.mean(yg * yg, axis=-1, keepdims=True) + EPS))
        return (jnp.concatenate(parts, axis=1) * norm_g,)

    return fn


def _ffn_act_fn(valid, ug, ug_t, uu, uu_t, w_g, w_u, b_g, b_u):
    return (valid * _silu(_causal_conv(ug, ug_t, w_g, b_g)) * _causal_conv(uu, uu_t, w_u, b_u),)


def _make_lru_pre(cfg):
    nb = cfg.lru_blocks
    bw = cfg.lru_width // nb

    def fn(valid, xr, xr_t, w_conv, b_conv, wa, ba, wx, bx, lam):
        xc = _causal_conv(xr, xr_t, w_conv, b_conv)
        a_parts, b_parts = [], []
        for n in range(nb):
            sl = slice(n * bw, (n + 1) * bw)
            xn = xc[:, sl]
            r = jax.nn.sigmoid(_dot(xn, wa[n]) + ba[:, sl])
            i = jax.nn.sigmoid(_dot(xn, wx[n]) + bx[:, sl])
            log_a = -LRU_C * r * _softplus(-lam[:, sl])
            a_parts.append(jnp.exp(log_a))
            b_parts.append(valid * jnp.sqrt(jnp.maximum(_neg_expm1(2.0 * log_a), 0.0)) * (i * xn))
        return jnp.concatenate(a_parts, axis=1), jnp.concatenate(b_parts, axis=1)

    return fn


def _lru_post_fn(valid, hs, gate):
    return (hs * jax.nn.gelu(gate),)


def _tri_consts(n):
    r = lax.broadcasted_iota(jnp.int32, (n, n), 0)
    c = lax.broadcasted_iota(jnp.int32, (n, n), 1)
    return r, c


def _ssd_chunk(x, bm, cm, a, s, tril, lower):
    rows = x.shape[0]
    heads = a.shape[1] // LANES
    p = x.shape[1] // heads
    cb = _dot_nt(cm, bm)
    ys, ss = [], []
    for e in range(heads):
        ae = a[:, e * LANES:(e + 1) * LANES]
        cs = _edot(tril, ae, 1, 0, 'a')
        lmat = jnp.exp(jnp.where(lower, cs - cs.T, NEG))
        xe = x[:, e * p:(e + 1) * p]
        se = s[e * p:(e + 1) * p, :]
        tot = cs[rows - 1:rows, :]
        y_diag = _dot(cb * lmat, xe)
        st = _dot_tn(xe * jnp.exp(tot - cs)[:, :p], bm)
        y_off = _dot_nt(cm, se) * jnp.exp(cs)[:, :p]
        ys.append(y_diag + y_off)
        ss.append(jnp.exp(tot[:, :1]) * se + st)
    return jnp.concatenate(ys, axis=1), jnp.concatenate(ss, axis=0)


def _ssd_specs(cfg, x, bm, cm, adt_b):
    gw, ns = cfg.ssd_gw, cfg.ssd_state
    hpg = cfg.ssd_heads // cfg.ssd_groups
    specs = [pl.BlockSpec((cfg.tp, gw), lambda b, g: (b, g)),
             pl.BlockSpec((cfg.tp, ns), lambda b, g: (b, g)),
             pl.BlockSpec((cfg.tp, ns), lambda b, g: (b, g)),
             pl.BlockSpec((cfg.tp, hpg * LANES), lambda b, g: (b, g))]
    return specs, [x, bm, cm, adt_b]


def ssd_scan_fwd(cfg, x, bm, cm, adt_b):
    nb = x.shape[0] // cfg.tp
    L, nc = cfg.chunk, cfg.nc
    specs, args = _ssd_specs(cfg, x, bm, cm, adt_b)

    def body(x_ref, b_ref, c_ref, a_ref, y_ref, s_ref):
        r, c = _tri_consts(L)
        lower = r >= c
        tril = lower.astype(F32)
        s_ref[...] = jnp.zeros_like(s_ref)

        def step(ci, carry):
            rows = pl.ds(pl.multiple_of(ci * L, L), L)
            y, s_new = _ssd_chunk(x_ref[rows, :], b_ref[rows, :], c_ref[rows, :], a_ref[rows, :], s_ref[...], tril, lower)
            y_ref[rows, :] = y
            s_ref[...] = s_new
            return carry

        lax.fori_loop(0, nc, step, 0)

    return pl.pallas_call(
        body, grid=(nb, cfg.ssd_groups), in_specs=specs, out_specs=specs[0],
        out_shape=jax.ShapeDtypeStruct(x.shape, F32), scratch_shapes=[pltpu.VMEM((cfg.ssd_gw, cfg.ssd_state), F32)],
        name="ssd_scan_fwd", compiler_params=_cparams())(*args)


def ssd_scan_bwd(cfg, x, bm, cm, adt_b, dy):
    nb = x.shape[0] // cfg.tp
    L, nc = cfg.chunk, cfg.nc
    specs, args = _ssd_specs(cfg, x, bm, cm, adt_b)

    def body(x_ref, b_ref, c_ref, a_ref, dy_ref, dx_ref, db_ref, dc_ref, da_ref, s_all, ds_ref):
        r, c = _tri_consts(L)
        lower = r >= c
        tril = lower.astype(F32)
        chunk = functools.partial(_ssd_chunk, tril=tril, lower=lower)
        ds_ref[...] = jnp.zeros_like(ds_ref)

        def fwd(ci, carry):
            rows = pl.ds(pl.multiple_of(ci * L, L), L)
            s_all[ci] = ds_ref[...]
            _, s_new = chunk(x_ref[rows, :], b_ref[rows, :], c_ref[rows, :], a_ref[rows, :], ds_ref[...])
            ds_ref[...] = s_new
            return carry

        lax.fori_loop(0, nc, fwd, 0)
        ds_ref[...] = jnp.zeros_like(ds_ref)

        def bwd(k, carry):
            ci = nc - 1 - k
            rows = pl.ds(pl.multiple_of(ci * L, L), L)
            _, vjp = jax.vjp(chunk, x_ref[rows, :], b_ref[rows, :], c_ref[rows, :], a_ref[rows, :], s_all[ci])
            dx, db, dc, da, ds = vjp((dy_ref[rows, :], ds_ref[...]))
            dx_ref[rows, :] = dx
            db_ref[rows, :] = db
            dc_ref[rows, :] = dc
            da_ref[rows, :] = da
            ds_ref[...] = ds
            return carry

        lax.fori_loop(0, nc, bwd, 0)

    return pl.pallas_call(
        body, grid=(nb, cfg.ssd_groups), in_specs=specs + [specs[0]], out_specs=specs,
        out_shape=[jax.ShapeDtypeStruct(t.shape, F32) for t in (x, bm, cm, adt_b)],
        scratch_shapes=[pltpu.VMEM((nc, cfg.ssd_gw, cfg.ssd_state), F32), pltpu.VMEM((cfg.ssd_gw, cfg.ssd_state), F32)],
        name="ssd_scan_bwd", compiler_params=_cparams())(*args, dy)


def _ret_chunk(q, k, v, g, norm_g, r_prev, cos, sin, valid, decay, zeta, xi, cdec, scale):
    half = q.shape[1] // 2

    def rot(t):
        t1, t2 = t[:, :half], t[:, half:]
        return jnp.concatenate([t1 * cos - t2 * sin, t1 * sin + t2 * cos], axis=1)

    qr = valid * rot(q)
    kr = valid * rot(k) * scale
    v = valid * v
    inner = _dot(_dot_nt(qr, kr) * decay, v)
    kv = _dot_tn(kr * zeta, v)
    cross = _dot(qr, r_prev) * xi
    o = inner + cross
    o = o - jnp.mean(o, axis=-1, keepdims=True)
    o = o * lax.rsqrt(jnp.mean(o * o, axis=-1, keepdims=True) + EPS)
    return _silu(g) * (o * norm_g), cdec * r_prev + kv


def _ret_consts(cfg):
    f32 = jnp.float32
    log_gamma = jnp.log1p(-jnp.exp2(-5.0 - jnp.arange(cfg.ret_heads, dtype=f32)))
    idx = jnp.arange(cfg.chunk, dtype=f32)
    diff = idx[:, None] - idx[None, :]
    decay = jnp.where(diff >= 0, jnp.exp(log_gamma[:, None, None] * jnp.maximum(diff, 0.0)), 0.0)
    zeta = jnp.exp(log_gamma[:, None] * (cfg.chunk - 1 - idx)[None, :])[..., None]
    xi = jnp.exp(log_gamma[:, None] * (idx + 1.0)[None, :])[..., None]
    cdec = jnp.exp(cfg.chunk * log_gamma)[:, None, None]
    half = cfg.ret_dim // 2
    inv_freq = 1.0 / (10000.0 ** (jnp.arange(half, dtype=f32) / (half - 1)))
    pos = jnp.arange(cfg.tp, dtype=f32) - cfg.pad
    ang = pos[:, None] * inv_freq[None, :]
    return dict(decay=decay, zeta=zeta, xi=xi, cdec=cdec, cos=jnp.cos(ang), sin=jnp.sin(ang))


def _ret_specs(cfg, u, col0, norm_g, consts):
    dk, nh, L = cfg.ret_dim, cfg.ret_heads, cfg.chunk
    base = col0 // dk
    seg = lambda s: pl.BlockSpec((cfg.tp, dk), functools.partial(lambda h, b, s: (b, base + s * nh + h), s=s))
    half = dk // 2
    specs = [seg(0), seg(1), seg(2), seg(3),
             pl.BlockSpec((1, dk), lambda h, b: (0, h)),
             pl.BlockSpec((cfg.tp, half), lambda h, b: (0, 0)),
             pl.BlockSpec((cfg.tp, half), lambda h, b: (0, 0)),
             pl.BlockSpec((None, L, L), lambda h, b: (h, 0, 0)),
             pl.BlockSpec((None, L, 1), lambda h, b: (h, 0, 0)),
             pl.BlockSpec((None, L, 1), lambda h, b: (h, 0, 0)),
             pl.BlockSpec((None, 1, 1), lambda h, b: (h, 0, 0))]
    args = [u, u, u, u, norm_g, consts["cos"], consts["sin"], consts["decay"], consts["zeta"], consts["xi"], consts["cdec"]]
    return specs, args


def _ret_chunk_at(cfg, refs, ci):
    L = cfg.chunk
    cos_ref, sin_ref, decay_ref, zeta_ref, xi_ref, cdec_ref = refs
    rows = pl.ds(pl.multiple_of(ci * L, L), L)
    fn = functools.partial(_ret_chunk, cos=cos_ref[rows, :], sin=sin_ref[rows, :], valid=_valid_chunk(cfg, ci),
                           decay=decay_ref[...], zeta=zeta_ref[...], xi=xi_ref[...], cdec=cdec_ref[...],
                           scale=cfg.ret_dim ** -0.5)
    return fn, rows


def ret_fwd(cfg, u, col0, norm_g, consts):
    nb = u.shape[0] // cfg.tp
    dk, nc = cfg.ret_dim, cfg.nc
    specs, args = _ret_specs(cfg, u, col0, norm_g, consts)

    def body(q_ref, k_ref, v_ref, g_ref, ng_ref, *rest):
        y_ref, r_ref = rest[-2], rest[-1]
        r_ref[...] = jnp.zeros_like(r_ref)

        def step(ci, carry):
            fn, rows = _ret_chunk_at(cfg, rest[:6], ci)
            y, r_new = fn(q_ref[rows, :], k_ref[rows, :], v_ref[rows, :], g_ref[rows, :], ng_ref[...], r_ref[...])
            y_ref[rows, :] = y.astype(y_ref.dtype)
            r_ref[...] = r_new
            return carry

        lax.fori_loop(0, nc, step, 0)

    return pl.pallas_call(
        body, grid=(cfg.ret_heads, nb), in_specs=specs, out_specs=pl.BlockSpec((cfg.tp, dk), lambda h, b: (b, h)),
        out_shape=jax.ShapeDtypeStruct((u.shape[0], cfg.ret_w), BF16), scratch_shapes=[pltpu.VMEM((dk, dk), F32)],
        name="ret_fwd", compiler_params=_cparams())(*args)


def ret_bwd(cfg, u, col0, norm_g, consts, dy, dy_col0):
    nb = u.shape[0] // cfg.tp
    dk, nc, nh = cfg.ret_dim, cfg.nc, cfg.ret_heads
    specs, args = _ret_specs(cfg, u, col0, norm_g, consts)
    dy_base = dy_col0 // dk
    specs.append(pl.BlockSpec((cfg.tp, dk), lambda h, b: (b, dy_base + h)))
    seg_out = lambda s: pl.BlockSpec((cfg.tp, dk), functools.partial(lambda h, b, s: (b, s * nh + h), s=s))

    def body(q_ref, k_ref, v_ref, g_ref, ng_ref, *rest):
        consts_refs, dy_ref = rest[:6], rest[6]
        dq_ref, dk_ref, dv_ref, dg_ref, dng_ref, r_all, dr_ref = rest[7:]
        dr_ref[...] = jnp.zeros_like(dr_ref)

        def fwd(ci, carry):
            fn, rows = _ret_chunk_at(cfg, consts_refs, ci)
            r_all[ci] = dr_ref[...]
            _, r_new = fn(q_ref[rows, :], k_ref[rows, :], v_ref[rows, :], g_ref[rows, :], ng_ref[...], dr_ref[...])
            dr_ref[...] = r_new
            return carry

        lax.fori_loop(0, nc, fwd, 0)
        dr_ref[...] = jnp.zeros_like(dr_ref)

        @pl.when(pl.program_id(1) == 0)
        def _():
            dng_ref[...] = jnp.zeros_like(dng_ref)

        def bwd(kk, carry):
            ci = nc - 1 - kk
            fn, rows = _ret_chunk_at(cfg, consts_refs, ci)
            _, vjp = jax.vjp(fn, q_ref[rows, :], k_ref[rows, :], v_ref[rows, :], g_ref[rows, :], ng_ref[...], r_all[ci])
            dq, dkk, dv, dg, dng, dr = vjp((dy_ref[rows, :].astype(F32), dr_ref[...]))
            dq_ref[rows, :] = dq.astype(dq_ref.dtype)
            dk_ref[rows, :] = dkk.astype(dk_ref.dtype)
            dv_ref[rows, :] = dv.astype(dv_ref.dtype)
            dg_ref[rows, :] = dg.astype(dg_ref.dtype)
            dng_ref[...] += dng
            dr_ref[...] = dr
            return carry

        lax.fori_loop(0, nc, bwd, 0)

    seg_shape = jax.ShapeDtypeStruct((u.shape[0], cfg.ret_w), BF16)
    res = pl.pallas_call(
        body, grid=(nh, nb), in_specs=specs,
        out_specs=[pl.BlockSpec((cfg.tp, dk), lambda h, b: (b, h))] * 4 + [pl.BlockSpec((1, dk), lambda h, b: (0, h))],
        out_shape=[seg_shape] * 4 + [jax.ShapeDtypeStruct((1, cfg.ret_w), F32)],
        scratch_shapes=[pltpu.VMEM((nc, dk, dk), F32), pltpu.VMEM((dk, dk), F32)],
        name="ret_bwd", compiler_params=_cparams(dimension_semantics=("arbitrary", "arbitrary")))(*args, dy)
    return res[:4], res[4]


def _dot2(a, tri):
    hi = a.astype(BF16)
    lo = (a - hi.astype(F32)).astype(BF16)
    return _dg(hi, tri, 1, 0) + _dg(lo, tri, 1, 0)


def _sb_group(cfg):
    return max(g for g in (4, 2, 1) if g <= cfg.nc)


def _sb_specs(cfg, u, col0):
    base = col0 // LANES
    per = cfg.sb_w // LANES
    seg = lambda s: pl.BlockSpec((cfg.tp, LANES), functools.partial(lambda b, p, s: (b, base + s * per + p), s=s))
    return [seg(0), seg(1), seg(2)], [u, u, u]


def _sb_scores(cfg, qh, ks, mask):
    z = _dg(qh, ks, 1, 1) * (cfg.sb_head_dim ** -0.5)
    sp = _softplus(z)
    return z, sp, jnp.where(mask, -sp, 0.0)


def _sb_running(x, tri, run, order):
    L = x.shape[0]
    parts = [None] * (x.shape[1] // L)
    for b in order:
        blk = x[:, b * L:(b + 1) * L]
        parts[b] = _dot2(blk, tri) + run
        run = run + jnp.sum(blk, axis=1, keepdims=True)
    return jnp.concatenate(parts, axis=1) if len(parts) > 1 else parts[0], run


def sb_fwd(cfg, u, col0):
    nb = u.shape[0] // cfg.tp
    L, nc = cfg.chunk, cfg.nc
    hd = cfg.sb_head_dim
    G = _sb_group(cfg)
    specs, args = _sb_specs(cfg, u, col0)

    def body(q_ref, k_ref, v_ref, y_ref, tot_ref):
        tr, tc = _tri_consts(L)
        after_tri = (tr > tc).astype(BF16)
        rid = lax.broadcasted_iota(jnp.int32, (L, G * L), 0)
        cid = lax.broadcasted_iota(jnp.int32, (L, G * L), 1)
        lane = lax.broadcasted_iota(jnp.int32, (1, LANES), 1)
        mine = [jnp.logical_and(lane >= hh * hd, lane < (hh + 1) * hd) for hh in range(2)]

        def qloop(i, carry):
            qrows = pl.ds(pl.multiple_of(i * L, L), L)
            q = q_ref[qrows, :]
            qhs = [jnp.where(mine[hh], q, 0.0).astype(BF16) for hh in range(2)]

            def gloop(gg, c):
                accs, runs = c
                hi = i - gg * G
                start = jnp.maximum(hi - (G - 1), 0)
                krows = pl.ds(pl.multiple_of(start * L, L), G * L)
                ks = k_ref[krows, :].astype(BF16)
                vs = v_ref[krows, :].astype(BF16)
                kpos = start * L + cid
                mask = jnp.logical_and(jnp.logical_and(kpos < i * L + rid, kpos >= cfg.pad), kpos < (hi + 1) * L)
                new_accs, new_runs = [], []
                for hh in range(2):
                    z, sp, l1m = _sb_scores(cfg, qhs[hh], ks, mask)
                    after, run = _sb_running(l1m, after_tri, runs[hh], range(G - 1, -1, -1))
                    w = jnp.where(mask, jnp.exp(z - sp + after), 0.0)
                    new_accs.append(accs[hh] + _dg(w.astype(BF16), vs, 1, 0))
                    new_runs.append(run)
                return tuple(new_accs), tuple(new_runs)

            zero_acc, zero_run = jnp.zeros((L, LANES), F32), jnp.zeros((L, 1), F32)
            accs, runs = lax.fori_loop(0, (i + G) // G, gloop, ((zero_acc, zero_acc), (zero_run, zero_run)))
            for hh in range(2):
                tot_ref[0, hh, qrows, :] = runs[hh]
            y_ref[qrows, :] = jnp.where(lane < hd, accs[0], accs[1]).astype(y_ref.dtype)
            return carry

        lax.fori_loop(0, nc, qloop, 0)

    return pl.pallas_call(
        body, grid=(nb, cfg.sb_w // LANES), in_specs=specs,
        out_specs=[pl.BlockSpec((cfg.tp, LANES), lambda b, p: (b, p)),
                   pl.BlockSpec((1, 2, cfg.tp, 1), lambda b, p: (b, p, 0, 0))],
        out_shape=[jax.ShapeDtypeStruct((u.shape[0], cfg.sb_w), BF16),
                   jax.ShapeDtypeStruct((nb, cfg.sb_heads, cfg.tp, 1), F32)],
        name="sb_fwd", compiler_params=_cparams())(*args)


def sb_bwd(cfg, u, col0, tot, dy, dy_col0):
    nb = u.shape[0] // cfg.tp
    L, nc = cfg.chunk, cfg.nc
    hd = cfg.sb_head_dim
    scale = hd ** -0.5
    specs, args = _sb_specs(cfg, u, col0)
    dy_base = dy_col0 // LANES
    specs += [pl.BlockSpec((1, 2, cfg.tp, 1), lambda b, p: (b, p, 0, 0)),
              pl.BlockSpec((cfg.tp, LANES), lambda b, p: (b, dy_base + p))]

    G = _sb_group(cfg)

    def body(q_ref, k_ref, v_ref, tot_ref, dy_ref, dq_ref, dk_ref, dv_ref, dk_acc, dv_acc):
        tr, tc = _tri_consts(L)
        upto_tri = (tr <= tc).astype(BF16)
        before_tri = (tr < tc).astype(BF16)
        rid = lax.broadcasted_iota(jnp.int32, (L, G * L), 0)
        cid = lax.broadcasted_iota(jnp.int32, (L, G * L), 1)
        lane = lax.broadcasted_iota(jnp.int32, (1, LANES), 1)
        mine = [jnp.logical_and(lane >= hh * hd, lane < (hh + 1) * hd) for hh in range(2)]
        dk_acc[...] = jnp.zeros_like(dk_acc)
        dv_acc[...] = jnp.zeros_like(dv_acc)

        def qloop(i, carry):
            qrows = pl.ds(pl.multiple_of(i * L, L), L)
            q = q_ref[qrows, :]
            dy = dy_ref[qrows, :].astype(F32)
            qhs = [jnp.where(mine[hh], q, 0.0).astype(BF16) for hh in range(2)]
            dyhs = [jnp.where(mine[hh], dy, 0.0).astype(BF16) for hh in range(2)]
            tots = [tot_ref[0, hh, qrows, :] for hh in range(2)]

            def gloop(g, c):
                dq, run_ls, run_as = c
                lo = g * G
                start = jnp.minimum(lo, nc - G)
                krows = pl.ds(pl.multiple_of(start * L, L), G * L)
                k = k_ref[krows, :]
                ks = k.astype(BF16)
                vs = v_ref[krows, :].astype(BF16)
                kpos = start * L + cid
                mask = jnp.logical_and(jnp.logical_and(kpos < i * L + rid, kpos >= cfg.pad), kpos >= lo * L)
                new_ls, new_as = [], []
                dv_add = jnp.zeros((G * L, LANES), F32)
                dk_add = jnp.zeros((G * L, LANES), F32)
                for hh in range(2):
                    z, sp, l1m = _sb_scores(cfg, qhs[hh], ks, mask)
                    upto, run_l = _sb_running(l1m, upto_tri, run_ls[hh], range(G))
                    w = jnp.where(mask, jnp.exp(z - sp + (tots[hh] - upto)), 0.0)
                    da = w * _dg(dyhs[hh], vs, 1, 1)
                    d_l1m, run_a = _sb_running(da, before_tri, run_as[hh], range(G))
                    sg = jnp.exp(z - sp)
                    dz = (jnp.where(mask, da * (1.0 - sg) - d_l1m * sg, 0.0) * scale).astype(BF16)
                    dv_add = dv_add + _dg(w.astype(BF16), dyhs[hh], 0, 0)
                    dk_add = dk_add + _dg(dz, qhs[hh], 0, 0)
                    dq = dq + _dg(dz, jnp.where(mine[hh], k, 0.0).astype(BF16), 1, 0)
                    new_ls.append(run_l)
                    new_as.append(run_a)
                dv_acc[krows, :] += dv_add
                dk_acc[krows, :] += dk_add
                return dq, tuple(new_ls), tuple(new_as)

            zero = jnp.zeros((L, 1), F32)
            dq, _, _ = lax.fori_loop(0, (i + G) // G, gloop, (jnp.zeros((L, LANES), F32), (zero, zero), (zero, zero)))
            dq_ref[qrows, :] = dq.astype(dq_ref.dtype)
            return carry

        lax.fori_loop(0, nc, qloop, 0)
        dk_ref[...] = dk_acc[...].astype(dk_ref.dtype)
        dv_ref[...] = dv_acc[...].astype(dv_ref.dtype)

    out_spec = pl.BlockSpec((cfg.tp, LANES), lambda b, p: (b, p))
    seg_shape = jax.ShapeDtypeStruct((u.shape[0], cfg.sb_w), BF16)
    return pl.pallas_call(
        body, grid=(nb, cfg.sb_w // LANES), in_specs=specs, out_specs=[out_spec] * 3, out_shape=[seg_shape] * 3,
        scratch_shapes=[pltpu.VMEM((cfg.tp, LANES), F32), pltpu.VMEM((cfg.tp, LANES), F32)],
        name="sb_bwd", compiler_params=_cparams())(*args, tot, dy)


def _lru_cols(cfg):
    return _tile(cfg.lru_width, 256)


def lru_scan_fwd(cfg, a, b):
    nb = a.shape[0] // cfg.tp
    cw = _lru_cols(cfg)
    spec = pl.BlockSpec((cfg.tp, cw), lambda s, c: (s, c))

    def body(a_ref, b_ref, h_ref):
        def step(i, h):
            rows = pl.ds(pl.multiple_of(i * SUBLANES, SUBLANES), SUBLANES)
            at, bt = a_ref[rows, :], b_ref[rows, :]
            outs = []
            for r in range(SUBLANES):
                h = at[r:r + 1, :] * h + bt[r:r + 1, :]
                outs.append(h)
            h_ref[rows, :] = jnp.concatenate(outs, axis=0)
            return h

        lax.fori_loop(0, cfg.tp // SUBLANES, step, jnp.zeros((1, cw), F32))

    return pl.pallas_call(
        body, grid=(nb, cfg.lru_width // cw), in_specs=[spec, spec], out_specs=spec,
        out_shape=jax.ShapeDtypeStruct(a.shape, F32), name="lru_scan_fwd", compiler_params=_cparams())(a, b)


def lru_scan_bwd(cfg, a, h, dh):
    nb = a.shape[0] // cfg.tp
    cw = _lru_cols(cfg)
    nt = cfg.tp // SUBLANES
    spec = pl.BlockSpec((cfg.tp, cw), lambda s, c: (s, c))

    def body(a_ref, h_ref, dh_ref, da_ref, db_ref):
        def step(k, c):
            i = nt - 1 - k
            rows = pl.ds(pl.multiple_of(i * SUBLANES, SUBLANES), SUBLANES)
            prev = pl.ds(pl.multiple_of(jnp.maximum(i - 1, 0) * SUBLANES, SUBLANES), SUBLANES)
            at, ht, dht = a_ref[rows, :], h_ref[rows, :], dh_ref[rows, :]
            h_before = jnp.where(i > 0, h_ref[prev, :][SUBLANES - 1:SUBLANES, :], 0.0)
            das, dbs = [None] * SUBLANES, [None] * SUBLANES
            for r in range(SUBLANES - 1, -1, -1):
                g = dht[r:r + 1, :] + c
                dbs[r] = g
                das[r] = g * (ht[r - 1:r, :] if r > 0 else h_before)
                c = at[r:r + 1, :] * g
            da_ref[rows, :] = jnp.concatenate(das, axis=0)
            db_ref[rows, :] = jnp.concatenate(dbs, axis=0)
            return c

        lax.fori_loop(0, nt, step, jnp.zeros((1, cw), F32))

    return pl.pallas_call(
        body, grid=(nb, cfg.lru_width // cw), in_specs=[spec] * 3, out_specs=[spec] * 2,
        out_shape=[jax.ShapeDtypeStruct(a.shape, F32)] * 2, name="lru_scan_bwd", compiler_params=_cparams())(a, h, dh)


def final_loss(cfg, h, target, norm_g):
    assert cfg.seq % cfg.chunk == 0 and cfg.n_meta + cfg.pad == cfg.chunk
    L, nc, d = cfg.chunk, cfg.nc, cfg.d_model
    per_seq = cfg.seq // L

    def tgt_map(i):
        return ((i // nc) * per_seq + jnp.maximum(i % nc - 1, 0), 0)

    def body(h_ref, t_ref, g_ref, loss_ref, dh_ref, dg_ref):
        i = pl.program_id(0)
        real = jnp.where(i % nc == 0, 0.0, 1.0)
        tgt = t_ref[...]

        def loss_fn(hv, g):
            y = hv * lax.rsqrt(jnp.mean(hv * hv, axis=-1, keepdims=True) + EPS) * g
            return 0.5 * real * jnp.sum(jnp.mean(jnp.square(y - tgt), axis=-1))

        val, (dh, dg) = jax.value_and_grad(loss_fn, argnums=(0, 1))(h_ref[...], g_ref[...])

        @pl.when(i == 0)
        def _():
            loss_ref[...] = jnp.zeros_like(loss_ref)
            dg_ref[...] = jnp.zeros_like(dg_ref)

        loss_ref[...] += jnp.broadcast_to(val, loss_ref.shape)
        dg_ref[...] += dg
        dh_ref[...] = dh

    return pl.pallas_call(
        body, grid=(h.shape[0] // L,),
        in_specs=[pl.BlockSpec((L, d), lambda i: (i, 0)), pl.BlockSpec((L, d), tgt_map), pl.BlockSpec((1, d), lambda i: (0, 0))],
        out_specs=[pl.BlockSpec((SUBLANES, LANES), lambda i: (0, 0)), pl.BlockSpec((L, d), lambda i: (i, 0)),
                   pl.BlockSpec((1, d), lambda i: (0, 0))],
        out_shape=[jax.ShapeDtypeStruct((SUBLANES, LANES), F32), jax.ShapeDtypeStruct(h.shape, F32),
                   jax.ShapeDtypeStruct((1, d), F32)],
        name="final_loss", compiler_params=_cparams(dimension_semantics=("arbitrary",)))(h, target, norm_g)


def exchange(name, arrays, same):
    n = len(arrays)
    n_peer = 7
    hbm = pl.BlockSpec(memory_space=pl.ANY)
    out_shape = [jax.ShapeDtypeStruct((8,) + (a.shape if s else a.shape[1:]), a.dtype) for a, s in zip(arrays, same)]

    def body(*refs):
        ins, outs = refs[:n], refs[n:2 * n]
        send_sems, recv_sems, local_sems = refs[2 * n:]
        x, y, c = lax.axis_index("x"), lax.axis_index("y"), lax.axis_index("c")
        me = 4 * x + 2 * y + c

        def slab(w, dest):
            return ins[w] if same[w] else ins[w].at[dest]

        local = [pltpu.make_async_copy(slab(w, me), outs[w].at[me], local_sems.at[w]) for w in range(n)]
        for cp in local:
            cp.start()
        remote = []
        for r in range(1, n_peer + 1):
            px = 1 - x if r & 4 else x
            py = 1 - y if r & 2 else y
            pc = 1 - c if r & 1 else c
            pidx = 4 * px + 2 * py + pc
            for w in range(n):
                send = pltpu.make_async_remote_copy(
                    src_ref=slab(w, pidx), dst_ref=outs[w].at[me], send_sem=send_sems.at[w, r - 1],
                    recv_sem=recv_sems.at[w, r - 1], device_id=(px, py, pc), device_id_type=pl.DeviceIdType.MESH)
                recv = pltpu.make_async_remote_copy(
                    src_ref=slab(w, pidx), dst_ref=outs[w].at[pidx], send_sem=send_sems.at[w, r - 1],
                    recv_sem=recv_sems.at[w, r - 1], device_id=(px, py, pc), device_id_type=pl.DeviceIdType.MESH)
                send.start()
                remote.append((send, recv))
        for send, recv in remote:
            recv.wait_recv()
        for send, recv in remote:
            send.wait_send()
        for cp in local:
            cp.wait()

    return pl.pallas_call(
        body, in_specs=[hbm] * n, out_specs=[hbm] * n, out_shape=out_shape,
        scratch_shapes=[pltpu.SemaphoreType.DMA((n, n_peer)), pltpu.SemaphoreType.DMA((n, n_peer)),
                        pltpu.SemaphoreType.DMA((n,))],
        name=name, compiler_params=pltpu.CompilerParams(has_side_effects=True))(*arrays)


def _row_tile(r, mult, cap):
    best = None
    for t in range(mult, min(r, cap) + 1, mult):
        if r % t == 0:
            best = t
    return best if best is not None else r


def adamw(name, parts, w, m, v):
    r, c = w.shape
    tr = _row_tile(r, 16, 256)
    spec = pl.BlockSpec((tr, c), lambda i: (i, 0))

    def body(p_ref, w_ref, m_ref, v_ref, g_ref, d_ref, m2_ref, v2_ref):
        g = p_ref[0].astype(F32)
        for dev in range(1, 8):
            g = g + p_ref[dev].astype(F32)
        m2 = ADAM_B1 * m_ref[...] + (1.0 - ADAM_B1) * g
        v2 = ADAM_B2 * v_ref[...] + (1.0 - ADAM_B2) * jnp.square(g)
        m_hat = m2 / (1.0 - ADAM_B1 ** ADAM_STEP)
        v_hat = v2 / (1.0 - ADAM_B2 ** ADAM_STEP)
        g_ref[...] = g
        d_ref[...] = -ADAM_LR * (m_hat / (jnp.sqrt(v_hat) + ADAM_EPS) + ADAM_WD * w_ref[...])
        m2_ref[...] = m2
        v2_ref[...] = v2

    return pl.pallas_call(
        body, grid=(r // tr,), in_specs=[pl.BlockSpec((8, tr, c), lambda i: (0, i, 0)), spec, spec, spec],
        out_specs=[spec] * 4, out_shape=[jax.ShapeDtypeStruct((r, c), F32)] * 4, name=name, compiler_params=_cparams())(parts, w, m, v)


PACK_ROWS = 256


def _pack(arrs):
    flat = jnp.concatenate([a.reshape(-1).astype(F32) for a in arrs])
    quantum = PACK_ROWS * LANES
    total = -(-flat.shape[0] // quantum) * quantum
    return jnp.pad(flat, (0, total - flat.shape[0])).reshape(-1, LANES)


def _unpack(packed, shapes):
    flat = packed.reshape(-1)
    out, off = [], 0
    for s in shapes:
        n = int(np.prod(s))
        out.append(flat[off:off + n].reshape(s))
        off += n
    return out


def _pad_lanes(vec):
    return jnp.pad(vec.astype(F32), (0, LANES - vec.shape[0]))[None, :]


def make_step(cfg):
    I, BC, H, RW, SW, LW, F, D = (cfg.ssd_inner, cfg.ssd_bc, cfg.ssd_heads, cfg.ret_w, cfg.sb_w, cfg.lru_width,
                                  cfg.ffn_dim, cfg.d_model)
    ND = cfg.n_dev
    q_off = 2 * I + 2 * BC
    dt_off = q_off + 4 * RW
    dt_w = (-dt_off) % 1024 or 1024

    def gather_cols(g):
        return jnp.transpose(g, (1, 0, 2)).reshape(g.shape[1], -1)

    def scatter_cols(full):
        r, c = full.shape
        return jnp.transpose(full.reshape(r, ND, c // ND), (1, 0, 2))

    def step(p, m_in, v_in, x, loss_target):
        nb = x.shape[0]
        rows = nb * cfg.tp

        small_shapes = [p[n].shape for n in SMALL_SHARDED]
        gathered = exchange("gather_weights", [p[n].astype(BF16) for n in BIG] + [_pack([p[n] for n in SMALL_SHARDED])],
                            [True] * (len(BIG) + 1))
        wfull = {}
        for n, g in zip(BIG, gathered):
            wfull[n] = gather_cols(g) if n in BIG_COL else g.reshape(-1, g.shape[-1])
        small_parts = [_unpack(gathered[-1][d], small_shapes) for d in range(ND)]
        sfull = {n: jnp.concatenate([small_parts[d][k] for d in range(ND)], axis=1) for k, n in enumerate(SMALL_SHARDED)}

        w0 = wfull['l0_w_in']
        w0a = jnp.concatenate([w0[:, :q_off], w0[:, q_off + H:], jnp.pad(w0[:, q_off:q_off + H], ((0, 0), (0, dt_w - H)))], axis=1)

        sel_hd = (jnp.arange(LANES)[:, None] == (jnp.arange(I)[None, :] // cfg.ssd_head_dim)).astype(F32)
        sel_lane = (jnp.arange(LANES)[:, None] == (jnp.arange(H * LANES)[None, :] // LANES)).astype(F32)
        rconst = _ret_consts(cfg)
        row2 = lambda vec: vec.astype(F32)[None, :]

        meta = jnp.broadcast_to(sfull['meta_tokens'][None], (nb, cfg.n_meta, D))
        h0 = jnp.concatenate([jnp.zeros((nb, cfg.pad, D), F32), meta, x], axis=1).reshape(rows, D)

        def norm_fwd(name, h, g):
            return stage_fwd(name, cfg, _norm_fn, [Row(h, D)], [Par(row2(g))], [(D, BF16)])[0]

        def norm_bwd(name, h, g, dhn, dh_next):
            fn = lambda valid, hv, gv: (_norm_fn(valid, hv, gv)[0], hv)
            (dh,), (dg,) = stage_bwd(name, cfg, fn, [Row(h, D)], [Par(row2(g))], [[Row(dhn, D)], [Row(dh_next, D)]], [F32])
            return dh, dg[0]

        hn0 = norm_fwd("l0_mix_norm", h0, p['l0_mix_norm'])
        u0 = matmul("l0_in_proj", hn0, w0a)
        cw, cb = sfull['l0_ssd_conv_w'], p['l0_ssd_conv_b']
        ssd_pre = _make_ssd_pre(cfg)
        ssd_post = _make_ssd_post(cfg)
        pre_rows = [Row(u0, I, 1, tail=True), Row(u0, BC, (2 * I) // BC, tail=True), Row(u0, BC, (2 * I) // BC + 1, tail=True),
                    Row(u0, LANES, dt_off // LANES)]
        pre_pars = [Par(cw[:, :I]), Par(cw[:, I:I + BC]), Par(cw[:, I + BC:]), Par(row2(cb[:I])), Par(row2(cb[I:I + BC])),
                    Par(row2(cb[I + BC:])), Par(_pad_lanes(p['l0_ssd_dt_bias'])), Par(_pad_lanes(p['l0_ssd_a_log'])),
                    Par(sel_hd, diff=False), Par(sel_lane, diff=False)]
        xs, bm, cm, xdt, adt_b = stage_fwd("ssd_pre", cfg, ssd_pre, pre_rows, pre_pars,
                                           [(I, F32), (BC, F32), (BC, F32), (I, F32), (H * LANES, F32)])
        y_raw = ssd_scan_fwd(cfg, xdt, bm, cm, adt_b)
        post_rows = [Row(y_raw, I), Row(xs, I), Row(u0, I, 0)]
        post_pars = [Par(_pad_lanes(p['l0_ssd_d'])), Par(row2(p['l0_ssd_norm'])), Par(sel_hd, diff=False)]
        (y_ssd,) = stage_fwd("ssd_post", cfg, ssd_post, post_rows, post_pars, [(I, BF16)])
        ret_g = row2(p['l0_ret_norm'])
        y_ret = ret_fwd(cfg, u0, q_off, ret_g, rconst)
        ycat0 = jnp.concatenate([y_ssd, y_ret], axis=1)
        h1 = matmul("l0_out_proj", ycat0, wfull['l0_w_out'], resid=h0, cfg=cfg)

        def ffn_fwd(tag, h, norm_g, w_in, conv_w, conv_b, w_out):
            hn = norm_fwd(tag + "_ffn_norm", h, norm_g)
            u = matmul(tag + "_ffn_in", hn, w_in)
            rws = [Row(u, F, 0, tail=True), Row(u, F, 1, tail=True)]
            prs = [Par(conv_w[:, :F]), Par(conv_w[:, F:]), Par(row2(conv_b[:F])), Par(row2(conv_b[F:]))]
            (act,) = stage_fwd(tag + "_ffn_act", cfg, _ffn_act_fn, rws, prs, [(F, BF16)])
            h_out = matmul(tag + "_ffn_out", act, w_out, resid=h, cfg=cfg)
            return h_out, (hn, u, rws, prs, act)

        h2, ffn0_saved = ffn_fwd("l0", h1, p['l0_ffn_norm'], wfull['l0_ffn_w_in'], sfull['l0_ffn_conv_w'], p['l0_ffn_conv_b'],
                                 wfull['l0_ffn_w_out'])

        hn2 = norm_fwd("l1_mix_norm", h2, p['l1_mix_norm'])
        u1 = matmul("l1_in_proj", hn2, wfull['l1_w_in'])
        y_sb, sb_tot = sb_fwd(cfg, u1, 0)
        lru_pre = _make_lru_pre(cfg)
        gate_blk = (3 * SW) // LW
        lpre_rows = [Row(u1, LW, gate_blk + 1, tail=True)]
        lpre_pars = [Par(sfull['l1_lru_conv_w']), Par(row2(p['l1_lru_conv_b'])), Par(p['l1_lru_wa']), Par(row2(p['l1_lru_ba'])),
                     Par(p['l1_lru_wx']), Par(row2(p['l1_lru_bx'])), Par(row2(p['l1_lru_lambda']))]
        lru_a, lru_b = stage_fwd("lru_pre", cfg, lru_pre, lpre_rows, lpre_pars, [(LW, F32), (LW, F32)])
        lru_h = lru_scan_fwd(cfg, lru_a, lru_b)
        lpost_rows = [Row(lru_h, LW), Row(u1, LW, gate_blk)]
        (y_lru,) = stage_fwd("lru_post", cfg, _lru_post_fn, lpost_rows, [], [(LW, BF16)])
        ycat1 = jnp.concatenate([y_sb, y_lru], axis=1)
        h3 = matmul("l1_out_proj", ycat1, wfull['l1_w_out'], resid=h2, cfg=cfg)
        h4, ffn1_saved = ffn_fwd("l1", h3, p['l1_ffn_norm'], wfull['l1_ffn_w_in'], sfull['l1_ffn_conv_w'], p['l1_ffn_conv_b'],
                                 wfull['l1_ffn_w_out'])

        loss_part, dh4, d_final = final_loss(cfg, h4, loss_target.reshape(nb * cfg.seq, D), row2(p['final_norm']))
        loss = lax.psum(loss_part[0, 0], ("x", "y", "c"))
        gfull, grep = {}, {'final_norm': d_final[0]}

        def ffn_bwd(tag, h, norm_g, w_in, w_out, saved, dh_out):
            hn, u, rws, prs, act = saved
            dact = matmul(tag + "_ffn_out_dx", dh_out, w_out, tb=True, out_dtype=BF16)
            gfull[tag + '_ffn_w_out'] = matmul(tag + "_ffn_out_dw", act, dh_out, ta=True, out_dtype=BF16, tm_cap=512, tn_cap=512, tk_cap=4352)
            (dug, duu), (dwg, dwu, dbg, dbu) = stage_bwd(tag + "_ffn_act_bwd", cfg, _ffn_act_fn, rws, prs, [[Row(dact, F)]], [BF16, BF16])
            du = jnp.concatenate([dug, duu], axis=1)
            gfull[tag + '_ffn_conv_w'] = jnp.concatenate([dwg, dwu], axis=1)
            grep[tag + '_ffn_conv_b'] = jnp.concatenate([dbg, dbu], axis=1)[0]
            dhn = matmul(tag + "_ffn_in_dx", du, w_in, tb=True)
            gfull[tag + '_ffn_w_in'] = matmul(tag + "_ffn_in_dw", hn, du, ta=True, out_dtype=BF16, tm_cap=512, tn_cap=512, tk_cap=4352)
            dh, grep[tag + '_ffn_norm'] = norm_bwd(tag + "_ffn_norm_bwd", h, norm_g, dhn, dh_out)
            return dh

        dh3 = ffn_bwd("l1", h3, p['l1_ffn_norm'], wfull['l1_ffn_w_in'], wfull['l1_ffn_w_out'], ffn1_saved, dh4)

        dycat1 = matmul("l1_out_dx", dh3, wfull['l1_w_out'], tb=True, out_dtype=BF16)
        gfull['l1_w_out'] = matmul("l1_out_dw", ycat1, dh3, ta=True, out_dtype=BF16, tm_cap=512, tn_cap=512, tk_cap=4352)
        (d_lru_h, d_gate), _ = stage_bwd("lru_post_bwd", cfg, _lru_post_fn, lpost_rows, [], [[Row(dycat1, LW, SW // LW)]], [F32, BF16])
        d_lru_a, d_lru_b = lru_scan_bwd(cfg, lru_a, lru_h, d_lru_h)
        (d_xr,), lgr = stage_bwd("lru_pre_bwd", cfg, lru_pre, lpre_rows, lpre_pars, [[Row(d_lru_a, LW)], [Row(d_lru_b, LW)]], [BF16])
        gfull['l1_lru_conv_w'] = lgr[0]
        grep.update({'l1_lru_conv_b': lgr[1][0], 'l1_lru_wa': lgr[2], 'l1_lru_ba': lgr[3][0], 'l1_lru_wx': lgr[4],
                     'l1_lru_bx': lgr[5][0], 'l1_lru_lambda': lgr[6][0]})
        d_q, d_k, d_v = sb_bwd(cfg, u1, 0, sb_tot, dycat1, 0)
        du1 = jnp.concatenate([d_q, d_k, d_v, d_gate, d_xr], axis=1)
        dhn2 = matmul("l1_in_dx", du1, wfull['l1_w_in'], tb=True)
        gfull['l1_w_in'] = matmul("l1_in_dw", hn2, du1, ta=True, out_dtype=BF16, tm_cap=512, tn_cap=512, tk_cap=4352)
        dh2, grep['l1_mix_norm'] = norm_bwd("l1_mix_norm_bwd", h2, p['l1_mix_norm'], dhn2, dh3)

        dh1 = ffn_bwd("l0", h1, p['l0_ffn_norm'], wfull['l0_ffn_w_in'], wfull['l0_ffn_w_out'], ffn0_saved, dh2)

        dycat0 = matmul("l0_out_dx", dh1, wfull['l0_w_out'], tb=True, out_dtype=BF16)
        gfull['l0_w_out'] = matmul("l0_out_dw", ycat0, dh1, ta=True, out_dtype=BF16, tm_cap=512, tn_cap=512, tk_cap=4352)
        (d_yraw, d_xs1, d_z), (d_dskip, d_ssdnorm) = stage_bwd("ssd_post_bwd", cfg, ssd_post, post_rows, post_pars,
                                                               [[Row(dycat0, I, 0)]], [F32, F32, BF16])
        d_xdt, d_bm, d_cm, d_adt = ssd_scan_bwd(cfg, xdt, bm, cm, adt_b, d_yraw)
        (d_xs, d_b, d_c, d_dt), sgr = stage_bwd("ssd_pre_bwd", cfg, ssd_pre, pre_rows, pre_pars,
                                                 [[Row(d_xs1, I)], [Row(d_bm, BC)], [Row(d_cm, BC)], [Row(d_xdt, I)], [Row(d_adt, H * LANES)]],
                                                 [BF16] * 4)
        gfull['l0_ssd_conv_w'] = jnp.concatenate(sgr[0:3], axis=1)
        grep.update({'l0_ssd_conv_b': jnp.concatenate(sgr[3:6], axis=1)[0], 'l0_ssd_dt_bias': sgr[6][0, :H],
                     'l0_ssd_a_log': sgr[7][0, :H], 'l0_ssd_d': d_dskip[0, :H], 'l0_ssd_norm': d_ssdnorm[0]})
        (d_rq, d_rk, d_rv, d_rg), d_retnorm = ret_bwd(cfg, u0, q_off, ret_g, rconst, dycat0, I)
        grep['l0_ret_norm'] = d_retnorm[0]
        du0 = jnp.concatenate([d_z, d_xs, d_b, d_c, d_rq, d_rk, d_rv, d_rg, d_dt, jnp.zeros((rows, dt_w - LANES), BF16)], axis=1)
        dhn0 = matmul("l0_in_dx", du0, w0a, tb=True)
        dw0a = matmul("l0_in_dw", hn0, du0, ta=True, out_dtype=BF16, tm_cap=512, tn_cap=512, tk_cap=4352)
        gfull['l0_w_in'] = jnp.concatenate([dw0a[:, :q_off], dw0a[:, dt_off:dt_off + H], dw0a[:, q_off:dt_off]], axis=1)
        dh0, grep['l0_mix_norm'] = norm_bwd("l0_mix_norm_bwd", h0, p['l0_mix_norm'], dhn0, dh1)

        dh0 = dh0.reshape(nb, cfg.tp, D)
        grad_x = dh0[:, cfg.chunk:, :]
        gfull['meta_tokens'] = jnp.sum(dh0[:, cfg.pad:cfg.chunk, :], axis=0)

        big_send = [scatter_cols(gfull[n]) if n in BIG_COL else gfull[n].reshape(ND, -1, gfull[n].shape[-1]) for n in BIG]
        small_send = jnp.stack([_pack([scatter_cols(gfull[n])[d] for n in SMALL_SHARDED]) for d in range(ND)])
        rep_send = _pack([grep[n] for n in REPLICATED])
        got = exchange("exchange_grads", big_send + [small_send, rep_send], [False] * (len(BIG) + 1) + [True])

        grad, delta, new_m, new_v = {}, {}, {}, {}
        for n, parts in zip(BIG, got):
            grad[n], delta[n], new_m[n], new_v[n] = adamw("adamw_" + n, parts, p[n], m_in[n], v_in[n])
        for names, parts in ((SMALL_SHARDED, got[len(BIG)]), (REPLICATED, got[len(BIG) + 1])):
            shapes = [p[n].shape for n in names]
            res = adamw("adamw_small_sharded" if names is SMALL_SHARDED else "adamw_replicated", parts,
                        _pack([p[n] for n in names]), _pack([m_in[n] for n in names]), _pack([v_in[n] for n in names]))
            for dst, packed in zip((grad, delta, new_m, new_v), res):
                for n, a in zip(names, _unpack(packed, shapes)):
                    dst[n] = a
        return (loss, grad_x, *[grad[n] for n in WEIGHTS], *[delta[n] for n in WEIGHTS], *[new_m[n] for n in WEIGHTS],
                *[new_v[n] for n in WEIGHTS])

    return step


_STEP = make_step(CFG)


def kernel(x, meta_tokens, l0_mix_norm, l0_w_in, l0_ssd_conv_w, l0_ssd_conv_b, l0_ssd_dt_bias, l0_ssd_a_log, l0_ssd_d, l0_ssd_norm, l0_ret_norm, l0_w_out, l0_ffn_norm, l0_ffn_w_in, l0_ffn_conv_w, l0_ffn_conv_b, l0_ffn_w_out, l1_mix_norm, l1_w_in, l1_lru_conv_w, l1_lru_conv_b, l1_lru_wa, l1_lru_ba, l1_lru_wx, l1_lru_bx, l1_lru_lambda, l1_w_out, l1_ffn_norm, l1_ffn_w_in, l1_ffn_conv_w, l1_ffn_conv_b, l1_ffn_w_out, final_norm, loss_target, m_meta_tokens, m_l0_mix_norm, m_l0_w_in, m_l0_ssd_conv_w, m_l0_ssd_conv_b, m_l0_ssd_dt_bias, m_l0_ssd_a_log, m_l0_ssd_d, m_l0_ssd_norm, m_l0_ret_norm, m_l0_w_out, m_l0_ffn_norm, m_l0_ffn_w_in, m_l0_ffn_conv_w, m_l0_ffn_conv_b, m_l0_ffn_w_out, m_l1_mix_norm, m_l1_w_in, m_l1_lru_conv_w, m_l1_lru_conv_b, m_l1_lru_wa, m_l1_lru_ba, m_l1_lru_wx, m_l1_lru_bx, m_l1_lru_lambda, m_l1_w_out, m_l1_ffn_norm, m_l1_ffn_w_in, m_l1_ffn_conv_w, m_l1_ffn_conv_b, m_l1_ffn_w_out, m_final_norm, v_meta_tokens, v_l0_mix_norm, v_l0_w_in, v_l0_ssd_conv_w, v_l0_ssd_conv_b, v_l0_ssd_dt_bias, v_l0_ssd_a_log, v_l0_ssd_d, v_l0_ssd_norm, v_l0_ret_norm, v_l0_w_out, v_l0_ffn_norm, v_l0_ffn_w_in, v_l0_ffn_conv_w, v_l0_ffn_conv_b, v_l0_ffn_w_out, v_l1_mix_norm, v_l1_w_in, v_l1_lru_conv_w, v_l1_lru_conv_b, v_l1_lru_wa, v_l1_lru_ba, v_l1_lru_wx, v_l1_lru_bx, v_l1_lru_lambda, v_l1_w_out, v_l1_ffn_norm, v_l1_ffn_w_in, v_l1_ffn_conv_w, v_l1_ffn_conv_b, v_l1_ffn_w_out, v_final_norm):
    args = locals()
    p = {n: args[n] for n in WEIGHTS}
    m_in = {n: args["m_" + n] for n in WEIGHTS}
    v_in = {n: args["v_" + n] for n in WEIGHTS}
    return _STEP(p, m_in, v_in, x, loss_target)
```

```python
import functools
from typing import NamedTuple

import numpy as np
import jax
import jax.numpy as jnp
from jax import lax
from jax.experimental import pallas as pl
from jax.experimental.pallas import tpu as pltpu

F32 = jnp.float32
BF16 = jnp.bfloat16
EPS = 1e-6
LRU_C = 8.0
NEG = -1e30
SUBLANES = 8
LANES = 128
VMEM_LIMIT = 56 * 1024 * 1024

ADAM_LR = 0.001
ADAM_B1 = 0.9
ADAM_B2 = 0.999
ADAM_EPS = 1e-08
ADAM_WD = 0.01
ADAM_STEP = 10

WEIGHTS = ['meta_tokens', 'l0_mix_norm', 'l0_w_in', 'l0_ssd_conv_w', 'l0_ssd_conv_b', 'l0_ssd_dt_bias', 'l0_ssd_a_log',
           'l0_ssd_d', 'l0_ssd_norm', 'l0_ret_norm', 'l0_w_out', 'l0_ffn_norm', 'l0_ffn_w_in', 'l0_ffn_conv_w',
           'l0_ffn_conv_b', 'l0_ffn_w_out', 'l1_mix_norm', 'l1_w_in', 'l1_lru_conv_w', 'l1_lru_conv_b', 'l1_lru_wa',
           'l1_lru_ba', 'l1_lru_wx', 'l1_lru_bx', 'l1_lru_lambda', 'l1_w_out', 'l1_ffn_norm', 'l1_ffn_w_in',
           'l1_ffn_conv_w', 'l1_ffn_conv_b', 'l1_ffn_w_out', 'final_norm']
BIG_COL = ['l0_w_in', 'l0_ffn_w_in', 'l1_w_in', 'l1_ffn_w_in']
BIG_ROW = ['l0_w_out', 'l0_ffn_w_out', 'l1_w_out', 'l1_ffn_w_out']
BIG = BIG_COL + BIG_ROW
SMALL_SHARDED = ['meta_tokens', 'l0_ssd_conv_w', 'l0_ffn_conv_w', 'l1_lru_conv_w', 'l1_ffn_conv_w']
REPLICATED = [w for w in WEIGHTS if w not in BIG and w not in SMALL_SHARDED]


class Config(NamedTuple):
    d_model: int = 1024
    seq: int = 2048
    n_meta: int = 16
    chunk: int = 128
    ssd_heads: int = 16
    ssd_head_dim: int = 64
    ssd_groups: int = 4
    ssd_state: int = 128
    ret_heads: int = 4
    ret_dim: int = 256
    sb_heads: int = 16
    sb_head_dim: int = 64
    lru_width: int = 1024
    lru_blocks: int = 8
    ffn_dim: int = 2816
    n_dev: int = 8

    @property
    def t(self):
        return self.n_meta + self.seq

    @property
    def pad(self):
        return (-self.t) % self.chunk

    @property
    def tp(self):
        return self.t + self.pad

    @property
    def nc(self):
        return self.tp // self.chunk

    @property
    def ssd_inner(self):
        return self.ssd_heads * self.ssd_head_dim

    @property
    def ssd_gw(self):
        return self.ssd_inner // self.ssd_groups

    @property
    def ssd_bc(self):
        return self.ssd_groups * self.ssd_state

    @property
    def ret_w(self):
        return self.ret_heads * self.ret_dim

    @property
    def sb_w(self):
        return self.sb_heads * self.sb_head_dim

    @property
    def mix0_segs(self):
        return (self.ssd_inner, self.ssd_inner + 2 * self.ssd_bc, self.ssd_heads, self.ret_w, self.ret_w, self.ret_w, self.ret_w)


CFG = Config()


def _dg(a, b, ca, cb):
    return lax.dot_general(a, b, (((ca,), (cb,)), ((), ())), preferred_element_type=F32)


def _cot_a(b, ca, cb, g):
    return _dg(g, b, 1, 1 - cb) if ca == 1 else _dg(b, g, 1 - cb, 1)


def _cot_b(a, ca, cb, g):
    return _dg(a, g, 1 - ca, 0) if cb == 0 else _dg(g, a, 0, 1 - ca)


@functools.partial(jax.custom_vjp, nondiff_argnums=(2, 3))
def _bdot(a, b, ca, cb):
    return _dg(a.astype(BF16), b.astype(BF16), ca, cb)


def _bdot_fwd(a, b, ca, cb):
    return _bdot(a, b, ca, cb), (a, b)


def _bdot_bwd(ca, cb, res, g):
    a, b = res
    gb = g.astype(BF16)
    return _cot_a(b.astype(BF16), ca, cb, gb).astype(a.dtype), _cot_b(a.astype(BF16), ca, cb, gb).astype(b.dtype)


_bdot.defvjp(_bdot_fwd, _bdot_bwd)


def _dot(a, b):
    return _bdot(a, b, 1, 0)


def _dot_nt(a, b):
    return _bdot(a, b, 1, 1)


def _dot_tn(a, b):
    return _bdot(a, b, 0, 0)


def _split3(a):
    hi = a.astype(BF16)
    r1 = a - hi.astype(F32)
    mid = r1.astype(BF16)
    return hi, mid, (r1 - mid.astype(F32)).astype(BF16)


@functools.partial(jax.custom_vjp, nondiff_argnums=(2, 3, 4))
def _edot(a, b, ca, cb, const):
    if const == 'b':
        bb = b.astype(BF16)
        return sum(_dg(p, bb, ca, cb) for p in _split3(a))
    ab = a.astype(BF16)
    return sum(_dg(ab, p, ca, cb) for p in _split3(b))


def _edot_fwd(a, b, ca, cb, const):
    return _edot(a, b, ca, cb, const), (a, b)


def _edot_bwd(ca, cb, const, res, g):
    a, b = res
    if const == 'b':
        bb = b.astype(BF16)
        return sum(_cot_a(bb, ca, cb, p) for p in _split3(g)), jnp.zeros_like(b)
    ab = a.astype(BF16)
    return jnp.zeros_like(a), sum(_cot_b(ab, ca, cb, p) for p in _split3(g))


_edot.defvjp(_edot_fwd, _edot_bwd)


def _dot_exact01(a, sel):
    return _edot(a, sel, 1, 0, 'b')


def _silu(x):
    return x * jax.nn.sigmoid(x)


def _softplus(x):
    return jnp.maximum(x, 0.0) + jnp.log1p(jnp.exp(-jnp.abs(x)))


def _neg_expm1(x):
    series = -x * (1.0 + x * 0.5 * (1.0 + x / 3.0 * (1.0 + x * 0.25)))
    return jnp.where(x > -0.01, series, 1.0 - jnp.exp(x))


def _causal_conv(x, tail, w, b):
    taps, rows = w.shape[0], x.shape[0]
    xx = jnp.concatenate([tail, x], axis=0)
    y = b
    for k in range(taps):
        off = SUBLANES - (taps - 1 - k)
        y = y + w[k:k + 1, :] * xx[off:off + rows, :]
    return y


def _valid_chunk(cfg, chunk_idx):
    first = (chunk_idx % cfg.nc) == 0
    rid = lax.broadcasted_iota(jnp.int32, (cfg.chunk, 1), 0)
    return jnp.where(jnp.logical_and(first, rid < cfg.pad), 0.0, 1.0).astype(F32)


def _cparams(**kw):
    return pltpu.CompilerParams(vmem_limit_bytes=VMEM_LIMIT, **kw)


class Row(NamedTuple):
    arr: jax.Array
    width: int
    blk: int = 0
    tail: bool = False
    diff: bool = True


class Par(NamedTuple):
    arr: jax.Array
    diff: bool = True


def _row_specs(cfg, rows, order):
    tr = cfg.chunk
    specs, args = [], []
    for r in rows:
        specs.append(pl.BlockSpec((tr, r.width), functools.partial(lambda i, b, o: (o(i), b), b=r.blk, o=order)))
        args.append(r.arr)
        if r.tail:
            per = tr // SUBLANES
            specs.append(pl.BlockSpec((SUBLANES, r.width),
                                      functools.partial(lambda i, b, o: (jnp.maximum(o(i) * per - 1, 0), b), b=r.blk, o=order)))
            args.append(r.arr)
    return specs, args


def _par_specs(pars):
    specs = [pl.BlockSpec(p.arr.shape, functools.partial(lambda i, nd: (0,) * nd, nd=p.arr.ndim)) for p in pars]
    return specs, [p.arr for p in pars]


def stage_fwd(name, cfg, fn, rows, pars, outs):
    tr = cfg.chunk
    n_rows = rows[0].arr.shape[0]
    rspecs, rargs = _row_specs(cfg, rows, lambda i: i)
    pspecs, pargs = _par_specs(pars)
    n_in = len(rargs) + len(pargs)

    def body(*refs):
        valid = _valid_chunk(cfg, pl.program_id(0))
        vals = [r[...].astype(F32) for r in refs[:n_in]]
        res = fn(valid, *vals)
        for o, v in zip(refs[n_in:], res):
            o[...] = v.astype(o.dtype)

    return pl.pallas_call(
        body, grid=(n_rows // tr,), in_specs=rspecs + pspecs,
        out_specs=[pl.BlockSpec((tr, w), lambda i: (i, 0)) for w, _ in outs],
        out_shape=[jax.ShapeDtypeStruct((n_rows, w), dt) for w, dt in outs],
        name=name, compiler_params=_cparams())(*rargs, *pargs)


def stage_bwd(name, cfg, fn, rows, pars, douts, drow_dtypes):
    tr = cfg.chunk
    n_rows = rows[0].arr.shape[0]
    n_blk = n_rows // tr
    order = lambda i: n_blk - 1 - i
    rspecs, rargs = _row_specs(cfg, rows, order)
    pspecs, pargs = _par_specs(pars)
    pieces = [p for out in douts for p in out]
    dspecs, dargs = _row_specs(cfg, pieces, order)
    n_r, n_p, n_d = len(rargs), len(pargs), len(dargs)
    diff_rows = [r for r in rows if r.diff]
    diff_pars = [p for p in pars if p.diff]
    tails = [r for r in diff_rows if r.tail]

    def body(*refs):
        in_refs = refs[:n_r + n_p]
        d_refs = refs[n_r + n_p:n_r + n_p + n_d]
        o_refs = refs[n_r + n_p + n_d:]
        drow_refs = o_refs[:len(diff_rows)]
        dpar_refs = o_refs[len(diff_rows):len(diff_rows) + len(diff_pars)]
        carry_refs = o_refs[len(diff_rows) + len(diff_pars):]
        step = pl.program_id(0)
        valid = _valid_chunk(cfg, order(step))
        vals = [r[...].astype(F32) for r in in_refs]
        slots, pos = [], 0
        for r in rows:
            if r.diff:
                slots.append(pos)
                if r.tail:
                    slots.append(pos + 1)
            pos += 2 if r.tail else 1
        for p in pars:
            if p.diff:
                slots.append(pos)
            pos += 1

        def g(*dv):
            full = list(vals)
            for s, v in zip(slots, dv):
                full[s] = v
            return tuple(fn(valid, *full))

        _, vjp = jax.vjp(g, *[vals[s] for s in slots])
        cts, k = [], 0
        for out in douts:
            parts = [d_refs[k + j][...].astype(F32) for j in range(len(out))]
            k += len(out)
            cts.append(parts[0] if len(parts) == 1 else jnp.concatenate(parts, axis=1))
        grads = list(vjp(tuple(cts)))

        @pl.when(step == 0)
        def _():
            for c in carry_refs:
                c[...] = jnp.zeros_like(c)
            for d in dpar_refs:
                d[...] = jnp.zeros_like(d)

        gi, ci = 0, 0
        for r, o in zip(diff_rows, drow_refs):
            dx = grads[gi]
            gi += 1
            if r.tail:
                dtail = grads[gi]
                gi += 1
                c = carry_refs[ci]
                ci += 1
                dx = dx + jnp.concatenate([jnp.zeros((tr - SUBLANES, r.width), F32), c[...]], axis=0)
                c[...] = dtail
            o[...] = dx.astype(o.dtype)
        for d in dpar_refs:
            d[...] += grads[gi]
            gi += 1

    out_specs = [pl.BlockSpec((tr, r.width), lambda i: (order(i), 0)) for r in diff_rows]
    out_shape = [jax.ShapeDtypeStruct((n_rows, r.width), dt) for r, dt in zip(diff_rows, drow_dtypes)]
    for p in diff_pars:
        out_specs.append(pl.BlockSpec(p.arr.shape, functools.partial(lambda i, nd: (0,) * nd, nd=p.arr.ndim)))
        out_shape.append(jax.ShapeDtypeStruct(p.arr.shape, F32))
    res = pl.pallas_call(
        body, grid=(n_blk,), in_specs=rspecs + pspecs + dspecs, out_specs=out_specs, out_shape=out_shape,
        scratch_shapes=[pltpu.VMEM((SUBLANES, r.width), F32) for r in tails],
        name=name, compiler_params=_cparams(dimension_semantics=("arbitrary",)))(*rargs, *pargs, *dargs)
    return list(res[:len(diff_rows)]), list(res[len(diff_rows):])


def _tile(n, cap):
    if n % LANES:
        return n
    q = n // LANES
    best = 1
    for k in range(1, q + 1):
        if q % k == 0 and k * LANES <= cap:
            best = k
    return best * LANES


def _mm_vmem(tm, tn, tk, a_bytes, b_bytes, out_bytes, resid, nk):
    total = 2 * (tm * tk * a_bytes + tk * tn * b_bytes + tm * tn * out_bytes) + tm * tn * 4
    total += 2 * tm * tn * 4 if resid else 0
    total += (tm * tk * 2 if a_bytes == 4 else 0) + (tk * tn * 2 if b_bytes == 4 else 0)
    return total + (tm * tn * 4 if nk > 1 else 0)


def matmul(name, a, b, *, ta=False, tb=False, out_dtype=F32, resid=None, cfg=None, tm_cap=2176, tn_cap=1024, tk_cap=2816):
    m, k = (a.shape[1], a.shape[0]) if ta else a.shape
    n = b.shape[0] if tb else b.shape[1]
    tm, tn, tk = _tile(m, tm_cap), _tile(n, tn_cap), _tile(k, tk_cap)
    budget = VMEM_LIMIT - 8 * 1024 * 1024
    sizes = (a.dtype.itemsize, b.dtype.itemsize, jnp.dtype(out_dtype).itemsize, resid is not None)
    while _mm_vmem(tm, tn, tk, *sizes, k // tk) > budget:
        if tm > 256:
            tm = _tile(m, tm - 1)
        elif tk > 512:
            tk = _tile(k, tk - 1)
        else:
            tn = _tile(n, tn - 1)
    nk = k // tk
    a_spec = pl.BlockSpec((tk, tm), lambda i, j, l: (l, i)) if ta else pl.BlockSpec((tm, tk), lambda i, j, l: (i, l))
    b_spec = pl.BlockSpec((tn, tk), lambda i, j, l: (j, l)) if tb else pl.BlockSpec((tk, tn), lambda i, j, l: (l, j))
    dims = (((0 if ta else 1,), (1 if tb else 0,)), ((), ()))
    in_specs, args = [a_spec, b_spec], [a, b]
    if resid is not None:
        in_specs.append(pl.BlockSpec((tm, tn), lambda i, j, l: (i, j)))
        args.append(resid)

    def body(*refs):
        a_ref, b_ref = refs[0], refs[1]
        o_ref = refs[3] if resid is not None else refs[2]
        part = lax.dot_general(a_ref[...].astype(BF16), b_ref[...].astype(BF16), dims, preferred_element_type=F32)

        def finish(res):
            if resid is not None:
                row = pl.program_id(0) * tm + lax.broadcasted_iota(jnp.int32, (tm, 1), 0)
                inpad = jnp.zeros((tm, 1), jnp.bool_)
                for s in range(m // cfg.tp):
                    inpad = jnp.logical_or(inpad, jnp.logical_and(row >= s * cfg.tp, row < s * cfg.tp + cfg.pad))
                res = refs[2][...] + jnp.where(inpad, 0.0, res)
            o_ref[...] = res.astype(o_ref.dtype)

        if nk == 1:
            finish(part)
            return
        acc = refs[-1]
        l = pl.program_id(2)

        @pl.when(l == 0)
        def _():
            acc[...] = part

        @pl.when(jnp.logical_and(l > 0, l < nk - 1))
        def _():
            acc[...] += part

        @pl.when(l == nk - 1)
        def _():
            finish(acc[...] + part)

    return pl.pallas_call(
        body, grid=(m // tm, n // tn, nk), in_specs=in_specs, out_specs=pl.BlockSpec((tm, tn), lambda i, j, l: (i, j)),
        out_shape=jax.ShapeDtypeStruct((m, n), out_dtype), scratch_shapes=[pltpu.VMEM((tm, tn), F32)] if nk > 1 else [],
        name=name, compiler_params=_cparams(dimension_semantics=("parallel", "parallel", "arbitrary")))(*args)


def _norm_fn(valid, h, g):
    y = h * lax.rsqrt(jnp.mean(h * h, axis=-1, keepdims=True) + EPS)
    return (valid * (y * g),)


def _make_ssd_pre(cfg):
    nh = cfg.ssd_heads

    def fn(valid, xs, xs_t, bm, bm_t, cm, cm_t, dtr, w_xs, w_b, w_c, b_xs, b_b, b_c, dt_bias, a_log, sel_hd, sel_lane):
        xs = valid * _silu(_causal_conv(xs, xs_t, w_xs, b_xs))
        bm = valid * _silu(_causal_conv(bm, bm_t, w_b, b_b))
        cm = valid * _silu(_causal_conv(cm, cm_t, w_c, b_c))
        lane = lax.broadcasted_iota(jnp.int32, (1, LANES), 1)
        head = jnp.where(lane < nh, 1.0, 0.0).astype(F32)
        dt = valid * head * _softplus(dtr + dt_bias)
        adt = dt * (-jnp.exp(a_log))
        x = xs * _dot_exact01(dt, sel_hd)
        adt_b = _dot_exact01(adt, sel_lane)
        return xs, bm, cm, x, adt_b

    return fn


def _make_ssd_post(cfg):
    gw = cfg.ssd_gw

    def fn(valid, y_raw, xs, z, d_skip, norm_g, sel_hd):
        d_rep = _dot_exact01(jnp.broadcast_to(d_skip, (SUBLANES, LANES)), sel_hd)[0:1]
        y = (y_raw + xs * d_rep) * _silu(z)
        parts = []
        for g in range(cfg.ssd_groups):
            yg = y[:, g * gw:(g + 1) * gw]
            parts.append(yg * lax.rsqrt(jnp.mean(yg * yg, axis=-1, keepdims=True) + EPS))
        return (jnp.concatenate(parts, axis=1) * norm_g,)

    return fn


def _ffn_act_fn(valid, ug, ug_t, uu, uu_t, w_g, w_u, b_g, b_u):
    return (valid * _silu(_causal_conv(ug, ug_t, w_g, b_g)) * _causal_conv(uu, uu_t, w_u, b_u),)


def _make_lru_pre(cfg):
    nb = cfg.lru_blocks
    bw = cfg.lru_width // nb

    def fn(valid, xr, xr_t, w_conv, b_conv, wa, ba, wx, bx, lam):
        xc = _causal_conv(xr, xr_t, w_conv, b_conv)
        a_parts, b_parts = [], []
        for n in range(nb):
            sl = slice(n * bw, (n + 1) * bw)
            xn = xc[:, sl]
            r = jax.nn.sigmoid(_dot(xn, wa[n]) + ba[:, sl])
            i = jax.nn.sigmoid(_dot(xn, wx[n]) + bx[:, sl])
            log_a = -LRU_C * r * _softplus(-lam[:, sl])
            a_parts.append(jnp.exp(log_a))
            b_parts.append(valid * jnp.sqrt(jnp.maximum(_neg_expm1(2.0 * log_a), 0.0)) * (i * xn))
        return jnp.concatenate(a_parts, axis=1), jnp.concatenate(b_parts, axis=1)

    return fn


def _lru_post_fn(valid, hs, gate):
    return (hs * jax.nn.gelu(gate),)


def _tri_consts(n):
    r = lax.broadcasted_iota(jnp.int32, (n, n), 0)
    c = lax.broadcasted_iota(jnp.int32, (n, n), 1)
    return r, c


def _ssd_chunk(x, bm, cm, a, s, tril, lower):
    rows = x.shape[0]
    heads = a.shape[1] // LANES
    p = x.shape[1] // heads
    cb = _dot_nt(cm, bm)
    ys, ss = [], []
    for e in range(heads):
        ae = a[:, e * LANES:(e + 1) * LANES]
        cs = _edot(tril, ae, 1, 0, 'a')
        lmat = jnp.exp(jnp.where(lower, cs - cs.T, NEG))
        xe = x[:, e * p:(e + 1) * p]
        se = s[e * p:(e + 1) * p, :]
        tot = cs[rows - 1:rows, :]
        y_diag = _dot(cb * lmat, xe)
        st = _dot_tn(xe * jnp.exp(tot - cs)[:, :p], bm)
        y_off = _dot_nt(cm, se) * jnp.exp(cs)[:, :p]
        ys.append(y_diag + y_off)
        ss.append(jnp.exp(tot[:, :1]) * se + st)
    return jnp.concatenate(ys, axis=1), jnp.concatenate(ss, axis=0)


def _ssd_specs(cfg, x, bm, cm, adt_b):
    gw, ns = cfg.ssd_gw, cfg.ssd_state
    hpg = cfg.ssd_heads // cfg.ssd_groups
    specs = [pl.BlockSpec((cfg.tp, gw), lambda b, g: (b, g)),
             pl.BlockSpec((cfg.tp, ns), lambda b, g: (b, g)),
             pl.BlockSpec((cfg.tp, ns), lambda b, g: (b, g)),
             pl.BlockSpec((cfg.tp, hpg * LANES), lambda b, g: (b, g))]
    return specs, [x, bm, cm, adt_b]


def ssd_scan_fwd(cfg, x, bm, cm, adt_b, ride=None):
    nb = x.shape[0] // cfg.tp
    L, nc = cfg.chunk, cfg.nc
    specs, args = _ssd_specs(cfg, x, bm, cm, adt_b)

    def body(x_ref, b_ref, c_ref, a_ref, y_ref, s_ref):
        r, c = _tri_consts(L)
        lower = r >= c
        tril = lower.astype(F32)
        s_ref[...] = jnp.zeros_like(s_ref)

        def step(ci, carry):
            rows = pl.ds(pl.multiple_of(ci * L, L), L)
            y, s_new = _ssd_chunk(x_ref[rows, :], b_ref[rows, :], c_ref[rows, :], a_ref[rows, :], s_ref[...], tril, lower)
            y_ref[rows, :] = y
            s_ref[...] = s_new
            return carry

        lax.fori_loop(0, nc, step, 0)

    outs, landed = call_with_ride(
        body, name="ssd_scan_fwd", grid=(nb, cfg.ssd_groups), in_specs=specs, args=args, out_specs=[specs[0]],
        out_shape=[jax.ShapeDtypeStruct(x.shape, F32)], scratch_shapes=[pltpu.VMEM((cfg.ssd_gw, cfg.ssd_state), F32)], ride=ride)
    return outs[0], landed


def ssd_scan_bwd(cfg, x, bm, cm, adt_b, dy, ride=None):
    nb = x.shape[0] // cfg.tp
    L, nc = cfg.chunk, cfg.nc
    specs, args = _ssd_specs(cfg, x, bm, cm, adt_b)

    def body(x_ref, b_ref, c_ref, a_ref, dy_ref, dx_ref, db_ref, dc_ref, da_ref, s_all, ds_ref):
        r, c = _tri_consts(L)
        lower = r >= c
        tril = lower.astype(F32)
        chunk = functools.partial(_ssd_chunk, tril=tril, lower=lower)
        ds_ref[...] = jnp.zeros_like(ds_ref)

        def fwd(ci, carry):
            rows = pl.ds(pl.multiple_of(ci * L, L), L)
            s_all[ci] = ds_ref[...]
            _, s_new = chunk(x_ref[rows, :], b_ref[rows, :], c_ref[rows, :], a_ref[rows, :], ds_ref[...])
            ds_ref[...] = s_new
            return carry

        lax.fori_loop(0, nc, fwd, 0)
        ds_ref[...] = jnp.zeros_like(ds_ref)

        def bwd(k, carry):
            ci = nc - 1 - k
            rows = pl.ds(pl.multiple_of(ci * L, L), L)
            _, vjp = jax.vjp(chunk, x_ref[rows, :], b_ref[rows, :], c_ref[rows, :], a_ref[rows, :], s_all[ci])
            dx, db, dc, da, ds = vjp((dy_ref[rows, :], ds_ref[...]))
            dx_ref[rows, :] = dx
            db_ref[rows, :] = db
            dc_ref[rows, :] = dc
            da_ref[rows, :] = da
            ds_ref[...] = ds
            return carry

        lax.fori_loop(0, nc, bwd, 0)

    return call_with_ride(
        body, name="ssd_scan_bwd", grid=(nb, cfg.ssd_groups), in_specs=specs + [specs[0]], args=args + [dy], out_specs=specs,
        out_shape=[jax.ShapeDtypeStruct(t.shape, F32) for t in (x, bm, cm, adt_b)],
        scratch_shapes=[pltpu.VMEM((nc, cfg.ssd_gw, cfg.ssd_state), F32), pltpu.VMEM((cfg.ssd_gw, cfg.ssd_state), F32)], ride=ride)


def _ret_chunk(q, k, v, g, norm_g, r_prev, cos, sin, valid, decay, zeta, xi, cdec, scale):
    half = q.shape[1] // 2

    def rot(t):
        t1, t2 = t[:, :half], t[:, half:]
        return jnp.concatenate([t1 * cos - t2 * sin, t1 * sin + t2 * cos], axis=1)

    qr = valid * rot(q)
    kr = valid * rot(k) * scale
    v = valid * v
    inner = _dot(_dot_nt(qr, kr) * decay, v)
    kv = _dot_tn(kr * zeta, v)
    cross = _dot(qr, r_prev) * xi
    o = inner + cross
    o = o - jnp.mean(o, axis=-1, keepdims=True)
    o = o * lax.rsqrt(jnp.mean(o * o, axis=-1, keepdims=True) + EPS)
    return _silu(g) * (o * norm_g), cdec * r_prev + kv


def _ret_consts(cfg):
    f32 = jnp.float32
    log_gamma = jnp.log1p(-jnp.exp2(-5.0 - jnp.arange(cfg.ret_heads, dtype=f32)))
    idx = jnp.arange(cfg.chunk, dtype=f32)
    diff = idx[:, None] - idx[None, :]
    decay = jnp.where(diff >= 0, jnp.exp(log_gamma[:, None, None] * jnp.maximum(diff, 0.0)), 0.0)
    zeta = jnp.exp(log_gamma[:, None] * (cfg.chunk - 1 - idx)[None, :])[..., None]
    xi = jnp.exp(log_gamma[:, None] * (idx + 1.0)[None, :])[..., None]
    cdec = jnp.exp(cfg.chunk * log_gamma)[:, None, None]
    half = cfg.ret_dim // 2
    inv_freq = 1.0 / (10000.0 ** (jnp.arange(half, dtype=f32) / (half - 1)))
    pos = jnp.arange(cfg.tp, dtype=f32) - cfg.pad
    ang = pos[:, None] * inv_freq[None, :]
    return dict(decay=decay, zeta=zeta, xi=xi, cdec=cdec, cos=jnp.cos(ang), sin=jnp.sin(ang))


def _ret_specs(cfg, u, col0, norm_g, consts):
    dk, nh, L = cfg.ret_dim, cfg.ret_heads, cfg.chunk
    base = col0 // dk
    seg = lambda s: pl.BlockSpec((cfg.tp, dk), functools.partial(lambda h, b, s: (b, base + s * nh + h), s=s))
    half = dk // 2
    specs = [seg(0), seg(1), seg(2), seg(3),
             pl.BlockSpec((1, dk), lambda h, b: (0, h)),
             pl.BlockSpec((cfg.tp, half), lambda h, b: (0, 0)),
             pl.BlockSpec((cfg.tp, half), lambda h, b: (0, 0)),
             pl.BlockSpec((None, L, L), lambda h, b: (h, 0, 0)),
             pl.BlockSpec((None, L, 1), lambda h, b: (h, 0, 0)),
             pl.BlockSpec((None, L, 1), lambda h, b: (h, 0, 0)),
             pl.BlockSpec((None, 1, 1), lambda h, b: (h, 0, 0))]
    args = [u, u, u, u, norm_g, consts["cos"], consts["sin"], consts["decay"], consts["zeta"], consts["xi"], consts["cdec"]]
    return specs, args


def _ret_chunk_at(cfg, refs, ci):
    L = cfg.chunk
    cos_ref, sin_ref, decay_ref, zeta_ref, xi_ref, cdec_ref = refs
    rows = pl.ds(pl.multiple_of(ci * L, L), L)
    fn = functools.partial(_ret_chunk, cos=cos_ref[rows, :], sin=sin_ref[rows, :], valid=_valid_chunk(cfg, ci),
                           decay=decay_ref[...], zeta=zeta_ref[...], xi=xi_ref[...], cdec=cdec_ref[...],
                           scale=cfg.ret_dim ** -0.5)
    return fn, rows


def ret_fwd(cfg, u, col0, norm_g, consts):
    nb = u.shape[0] // cfg.tp
    dk, nc = cfg.ret_dim, cfg.nc
    specs, args = _ret_specs(cfg, u, col0, norm_g, consts)

    def body(q_ref, k_ref, v_ref, g_ref, ng_ref, *rest):
        y_ref, r_ref = rest[-2], rest[-1]
        r_ref[...] = jnp.zeros_like(r_ref)

        def step(ci, carry):
            fn, rows = _ret_chunk_at(cfg, rest[:6], ci)
            y, r_new = fn(q_ref[rows, :], k_ref[rows, :], v_ref[rows, :], g_ref[rows, :], ng_ref[...], r_ref[...])
            y_ref[rows, :] = y.astype(y_ref.dtype)
            r_ref[...] = r_new
            return carry

        lax.fori_loop(0, nc, step, 0)

    return pl.pallas_call(
        body, grid=(cfg.ret_heads, nb), in_specs=specs, out_specs=pl.BlockSpec((cfg.tp, dk), lambda h, b: (b, h)),
        out_shape=jax.ShapeDtypeStruct((u.shape[0], cfg.ret_w), BF16), scratch_shapes=[pltpu.VMEM((dk, dk), F32)],
        name="ret_fwd", compiler_params=_cparams())(*args)


def ret_bwd(cfg, u, col0, norm_g, consts, dy, dy_col0):
    nb = u.shape[0] // cfg.tp
    dk, nc, nh = cfg.ret_dim, cfg.nc, cfg.ret_heads
    specs, args = _ret_specs(cfg, u, col0, norm_g, consts)
    dy_base = dy_col0 // dk
    specs.append(pl.BlockSpec((cfg.tp, dk), lambda h, b: (b, dy_base + h)))
    seg_out = lambda s: pl.BlockSpec((cfg.tp, dk), functools.partial(lambda h, b, s: (b, s * nh + h), s=s))

    def body(q_ref, k_ref, v_ref, g_ref, ng_ref, *rest):
        consts_refs, dy_ref = rest[:6], rest[6]
        dq_ref, dk_ref, dv_ref, dg_ref, dng_ref, r_all, dr_ref = rest[7:]
        dr_ref[...] = jnp.zeros_like(dr_ref)

        def fwd(ci, carry):
            fn, rows = _ret_chunk_at(cfg, consts_refs, ci)
            r_all[ci] = dr_ref[...]
            _, r_new = fn(q_ref[rows, :], k_ref[rows, :], v_ref[rows, :], g_ref[rows, :], ng_ref[...], dr_ref[...])
            dr_ref[...] = r_new
            return carry

        lax.fori_loop(0, nc, fwd, 0)
        dr_ref[...] = jnp.zeros_like(dr_ref)

        @pl.when(pl.program_id(1) == 0)
        def _():
            dng_ref[...] = jnp.zeros_like(dng_ref)

        def bwd(kk, carry):
            ci = nc - 1 - kk
            fn, rows = _ret_chunk_at(cfg, consts_refs, ci)
            _, vjp = jax.vjp(fn, q_ref[rows, :], k_ref[rows, :], v_ref[rows, :], g_ref[rows, :], ng_ref[...], r_all[ci])
            dq, dkk, dv, dg, dng, dr = vjp((dy_ref[rows, :].astype(F32), dr_ref[...]))
            dq_ref[rows, :] = dq.astype(dq_ref.dtype)
            dk_ref[rows, :] = dkk.astype(dk_ref.dtype)
            dv_ref[rows, :] = dv.astype(dv_ref.dtype)
            dg_ref[rows, :] = dg.astype(dg_ref.dtype)
            dng_ref[...] += dng
            dr_ref[...] = dr
            return carry

        lax.fori_loop(0, nc, bwd, 0)

    seg_shape = jax.ShapeDtypeStruct((u.shape[0], cfg.ret_w), BF16)
    res = pl.pallas_call(
        body, grid=(nh, nb), in_specs=specs,
        out_specs=[pl.BlockSpec((cfg.tp, dk), lambda h, b: (b, h))] * 4 + [pl.BlockSpec((1, dk), lambda h, b: (0, h))],
        out_shape=[seg_shape] * 4 + [jax.ShapeDtypeStruct((1, cfg.ret_w), F32)],
        scratch_shapes=[pltpu.VMEM((nc, dk, dk), F32), pltpu.VMEM((dk, dk), F32)],
        name="ret_bwd", compiler_params=_cparams(dimension_semantics=("arbitrary", "arbitrary")))(*args, dy)
    return res[:4], res[4]


def _dot2(a, tri):
    hi = a.astype(BF16)
    lo = (a - hi.astype(F32)).astype(BF16)
    return _dg(hi, tri, 1, 0) + _dg(lo, tri, 1, 0)


def _sb_group(cfg):
    return max(g for g in (4, 2, 1) if g <= cfg.nc)


def _sb_specs(cfg, u, col0):
    base = col0 // LANES
    per = cfg.sb_w // LANES
    seg = lambda s: pl.BlockSpec((cfg.tp, LANES), functools.partial(lambda b, p, s: (b, base + s * per + p), s=s))
    return [seg(0), seg(1), seg(2)], [u, u, u]


def _sb_scores(cfg, qh, ks, mask):
    z = _dg(qh, ks, 1, 1) * (cfg.sb_head_dim ** -0.5)
    sp = _softplus(z)
    return z, sp, jnp.where(mask, -sp, 0.0)


def _sb_running(x, tri, run, order):
    L = x.shape[0]
    parts = [None] * (x.shape[1] // L)
    for b in order:
        blk = x[:, b * L:(b + 1) * L]
        parts[b] = _dot2(blk, tri) + run
        run = run + jnp.sum(blk, axis=1, keepdims=True)
    return jnp.concatenate(parts, axis=1) if len(parts) > 1 else parts[0], run


def sb_fwd(cfg, u, col0, ride=None):
    nb = u.shape[0] // cfg.tp
    L, nc = cfg.chunk, cfg.nc
    hd = cfg.sb_head_dim
    G = _sb_group(cfg)
    specs, args = _sb_specs(cfg, u, col0)

    def body(q_ref, k_ref, v_ref, y_ref, tot_ref):
        tr, tc = _tri_consts(L)
        after_tri = (tr > tc).astype(BF16)
        rid = lax.broadcasted_iota(jnp.int32, (L, G * L), 0)
        cid = lax.broadcasted_iota(jnp.int32, (L, G * L), 1)
        lane = lax.broadcasted_iota(jnp.int32, (1, LANES), 1)
        mine = [jnp.logical_and(lane >= hh * hd, lane < (hh + 1) * hd) for hh in range(2)]

        def qloop(i, carry):
            qrows = pl.ds(pl.multiple_of(i * L, L), L)
            q = q_ref[qrows, :]
            qhs = [jnp.where(mine[hh], q, 0.0).astype(BF16) for hh in range(2)]

            def gloop(gg, c):
                accs, runs = c
                hi = i - gg * G
                start = jnp.maximum(hi - (G - 1), 0)
                krows = pl.ds(pl.multiple_of(start * L, L), G * L)
                ks = k_ref[krows, :].astype(BF16)
                vs = v_ref[krows, :].astype(BF16)
                kpos = start * L + cid
                mask = jnp.logical_and(jnp.logical_and(kpos < i * L + rid, kpos >= cfg.pad), kpos < (hi + 1) * L)
                new_accs, new_runs = [], []
                for hh in range(2):
                    z, sp, l1m = _sb_scores(cfg, qhs[hh], ks, mask)
                    after, run = _sb_running(l1m, after_tri, runs[hh], range(G - 1, -1, -1))
                    w = jnp.where(mask, jnp.exp(z - sp + after), 0.0)
                    new_accs.append(accs[hh] + _dg(w.astype(BF16), vs, 1, 0))
                    new_runs.append(run)
                return tuple(new_accs), tuple(new_runs)

            zero_acc, zero_run = jnp.zeros((L, LANES), F32), jnp.zeros((L, 1), F32)
            accs, runs = lax.fori_loop(0, (i + G) // G, gloop, ((zero_acc, zero_acc), (zero_run, zero_run)))
            for hh in range(2):
                tot_ref[0, hh, qrows, :] = runs[hh]
            y_ref[qrows, :] = jnp.where(lane < hd, accs[0], accs[1]).astype(y_ref.dtype)
            return carry

        lax.fori_loop(0, nc, qloop, 0)

    (y, tot), landed = call_with_ride(
        body, name="sb_fwd", grid=(nb, cfg.sb_w // LANES), in_specs=specs, args=args,
        out_specs=[pl.BlockSpec((cfg.tp, LANES), lambda b, p: (b, p)),
                   pl.BlockSpec((1, 2, cfg.tp, 1), lambda b, p: (b, p, 0, 0))],
        out_shape=[jax.ShapeDtypeStruct((u.shape[0], cfg.sb_w), BF16),
                   jax.ShapeDtypeStruct((nb, cfg.sb_heads, cfg.tp, 1), F32)], ride=ride)
    return y, tot, landed


def sb_bwd(cfg, u, col0, tot, dy, dy_col0, ride=None):
    nb = u.shape[0] // cfg.tp
    L, nc = cfg.chunk, cfg.nc
    hd = cfg.sb_head_dim
    scale = hd ** -0.5
    specs, args = _sb_specs(cfg, u, col0)
    dy_base = dy_col0 // LANES
    specs += [pl.BlockSpec((1, 2, cfg.tp, 1), lambda b, p: (b, p, 0, 0)),
              pl.BlockSpec((cfg.tp, LANES), lambda b, p: (b, dy_base + p))]

    G = _sb_group(cfg)

    def body(q_ref, k_ref, v_ref, tot_ref, dy_ref, dq_ref, dk_ref, dv_ref, dk_acc, dv_acc):
        tr, tc = _tri_consts(L)
        upto_tri = (tr <= tc).astype(BF16)
        before_tri = (tr < tc).astype(BF16)
        rid = lax.broadcasted_iota(jnp.int32, (L, G * L), 0)
        cid = lax.broadcasted_iota(jnp.int32, (L, G * L), 1)
        lane = lax.broadcasted_iota(jnp.int32, (1, LANES), 1)
        mine = [jnp.logical_and(lane >= hh * hd, lane < (hh + 1) * hd) for hh in range(2)]
        dk_acc[...] = jnp.zeros_like(dk_acc)
        dv_acc[...] = jnp.zeros_like(dv_acc)

        def qloop(i, carry):
            qrows = pl.ds(pl.multiple_of(i * L, L), L)
            q = q_ref[qrows, :]
            dy = dy_ref[qrows, :].astype(F32)
            qhs = [jnp.where(mine[hh], q, 0.0).astype(BF16) for hh in range(2)]
            dyhs = [jnp.where(mine[hh], dy, 0.0).astype(BF16) for hh in range(2)]
            tots = [tot_ref[0, hh, qrows, :] for hh in range(2)]

            def gloop(g, c):
                dq, run_ls, run_as = c
                lo = g * G
                start = jnp.minimum(lo, nc - G)
                krows = pl.ds(pl.multiple_of(start * L, L), G * L)
                k = k_ref[krows, :]
                ks = k.astype(BF16)
                vs = v_ref[krows, :].astype(BF16)
                kpos = start * L + cid
                mask = jnp.logical_and(jnp.logical_and(kpos < i * L + rid, kpos >= cfg.pad), kpos >= lo * L)
                new_ls, new_as = [], []
                dv_add = jnp.zeros((G * L, LANES), F32)
                dk_add = jnp.zeros((G * L, LANES), F32)
                for hh in range(2):
                    z, sp, l1m = _sb_scores(cfg, qhs[hh], ks, mask)
                    upto, run_l = _sb_running(l1m, upto_tri, run_ls[hh], range(G))
                    w = jnp.where(mask, jnp.exp(z - sp + (tots[hh] - upto)), 0.0)
                    da = w * _dg(dyhs[hh], vs, 1, 1)
                    d_l1m, run_a = _sb_running(da, before_tri, run_as[hh], range(G))
                    sg = jnp.exp(z - sp)
                    dz = (jnp.where(mask, da * (1.0 - sg) - d_l1m * sg, 0.0) * scale).astype(BF16)
                    dv_add = dv_add + _dg(w.astype(BF16), dyhs[hh], 0, 0)
                    dk_add = dk_add + _dg(dz, qhs[hh], 0, 0)
                    dq = dq + _dg(dz, jnp.where(mine[hh], k, 0.0).astype(BF16), 1, 0)
                    new_ls.append(run_l)
                    new_as.append(run_a)
                dv_acc[krows, :] += dv_add
                dk_acc[krows, :] += dk_add
                return dq, tuple(new_ls), tuple(new_as)

            zero = jnp.zeros((L, 1), F32)
            dq, _, _ = lax.fori_loop(0, (i + G) // G, gloop, (jnp.zeros((L, LANES), F32), (zero, zero), (zero, zero)))
            dq_ref[qrows, :] = dq.astype(dq_ref.dtype)
            return carry

        lax.fori_loop(0, nc, qloop, 0)
        dk_ref[...] = dk_acc[...].astype(dk_ref.dtype)
        dv_ref[...] = dv_acc[...].astype(dv_ref.dtype)

    out_spec = pl.BlockSpec((cfg.tp, LANES), lambda b, p: (b, p))
    seg_shape = jax.ShapeDtypeStruct((u.shape[0], cfg.sb_w), BF16)
    return call_with_ride(
        body, name="sb_bwd", grid=(nb, cfg.sb_w // LANES), in_specs=specs, args=args + [tot, dy], out_specs=[out_spec] * 3,
        out_shape=[seg_shape] * 3, scratch_shapes=[pltpu.VMEM((cfg.tp, LANES), F32), pltpu.VMEM((cfg.tp, LANES), F32)], ride=ride)


def _lru_cols(cfg):
    return _tile(cfg.lru_width, 256)


def lru_scan_fwd(cfg, a, b):
    nb = a.shape[0] // cfg.tp
    cw = _lru_cols(cfg)
    spec = pl.BlockSpec((cfg.tp, cw), lambda s, c: (s, c))

    def body(a_ref, b_ref, h_ref):
        def step(i, h):
            rows = pl.ds(pl.multiple_of(i * SUBLANES, SUBLANES), SUBLANES)
            at, bt = a_ref[rows, :], b_ref[rows, :]
            outs = []
            for r in range(SUBLANES):
                h = at[r:r + 1, :] * h + bt[r:r + 1, :]
                outs.append(h)
            h_ref[rows, :] = jnp.concatenate(outs, axis=0)
            return h

        lax.fori_loop(0, cfg.tp // SUBLANES, step, jnp.zeros((1, cw), F32))

    return pl.pallas_call(
        body, grid=(nb, cfg.lru_width // cw), in_specs=[spec, spec], out_specs=spec,
        out_shape=jax.ShapeDtypeStruct(a.shape, F32), name="lru_scan_fwd", compiler_params=_cparams())(a, b)


def lru_scan_bwd(cfg, a, h, dh):
    nb = a.shape[0] // cfg.tp
    cw = _lru_cols(cfg)
    nt = cfg.tp // SUBLANES
    spec = pl.BlockSpec((cfg.tp, cw), lambda s, c: (s, c))

    def body(a_ref, h_ref, dh_ref, da_ref, db_ref):
        def step(k, c):
            i = nt - 1 - k
            rows = pl.ds(pl.multiple_of(i * SUBLANES, SUBLANES), SUBLANES)
            prev = pl.ds(pl.multiple_of(jnp.maximum(i - 1, 0) * SUBLANES, SUBLANES), SUBLANES)
            at, ht, dht = a_ref[rows, :], h_ref[rows, :], dh_ref[rows, :]
            h_before = jnp.where(i > 0, h_ref[prev, :][SUBLANES - 1:SUBLANES, :], 0.0)
            das, dbs = [None] * SUBLANES, [None] * SUBLANES
            for r in range(SUBLANES - 1, -1, -1):
                g = dht[r:r + 1, :] + c
                dbs[r] = g
                das[r] = g * (ht[r - 1:r, :] if r > 0 else h_before)
                c = at[r:r + 1, :] * g
            da_ref[rows, :] = jnp.concatenate(das, axis=0)
            db_ref[rows, :] = jnp.concatenate(dbs, axis=0)
            return c

        lax.fori_loop(0, nt, step, jnp.zeros((1, cw), F32))

    return pl.pallas_call(
        body, grid=(nb, cfg.lru_width // cw), in_specs=[spec] * 3, out_specs=[spec] * 2,
        out_shape=[jax.ShapeDtypeStruct(a.shape, F32)] * 2, name="lru_scan_bwd", compiler_params=_cparams())(a, h, dh)


def final_loss(cfg, h, target, norm_g):
    assert cfg.seq % cfg.chunk == 0 and cfg.n_meta + cfg.pad == cfg.chunk
    L, nc, d = cfg.chunk, cfg.nc, cfg.d_model
    per_seq = cfg.seq // L

    def tgt_map(i):
        return ((i // nc) * per_seq + jnp.maximum(i % nc - 1, 0), 0)

    def body(h_ref, t_ref, g_ref, loss_ref, dh_ref, dg_ref):
        i = pl.program_id(0)
        real = jnp.where(i % nc == 0, 0.0, 1.0)
        tgt = t_ref[...]

        def loss_fn(hv, g):
            y = hv * lax.rsqrt(jnp.mean(hv * hv, axis=-1, keepdims=True) + EPS) * g
            return 0.5 * real * jnp.sum(jnp.mean(jnp.square(y - tgt), axis=-1))

        val, (dh, dg) = jax.value_and_grad(loss_fn, argnums=(0, 1))(h_ref[...], g_ref[...])

        @pl.when(i == 0)
        def _():
            loss_ref[...] = jnp.zeros_like(loss_ref)
            dg_ref[...] = jnp.zeros_like(dg_ref)

        loss_ref[...] += jnp.broadcast_to(val, loss_ref.shape)
        dg_ref[...] += dg
        dh_ref[...] = dh

    return pl.pallas_call(
        body, grid=(h.shape[0] // L,),
        in_specs=[pl.BlockSpec((L, d), lambda i: (i, 0)), pl.BlockSpec((L, d), tgt_map), pl.BlockSpec((1, d), lambda i: (0, 0))],
        out_specs=[pl.BlockSpec((SUBLANES, LANES), lambda i: (0, 0)), pl.BlockSpec((L, d), lambda i: (i, 0)),
                   pl.BlockSpec((1, d), lambda i: (0, 0))],
        out_shape=[jax.ShapeDtypeStruct((SUBLANES, LANES), F32), jax.ShapeDtypeStruct(h.shape, F32),
                   jax.ShapeDtypeStruct((1, d), F32)],
        name="final_loss", compiler_params=_cparams(dimension_semantics=("arbitrary",)))(h, target, norm_g)


N_PEER = 7


class Ride(NamedTuple):
    arrays: list
    same: list


def _ride_copies(ride_in, ride_out, same, sems, sending):
    send_sems, recv_sems, local_sems = sems
    n = len(ride_in)
    x, y, c = lax.axis_index("x"), lax.axis_index("y"), lax.axis_index("c")
    me = 4 * x + 2 * y + c

    def slab(w, dest):
        return ride_in[w] if same[w] else ride_in[w].at[dest]

    local = [pltpu.make_async_copy(slab(w, me), ride_out[w].at[me], local_sems.at[w]) for w in range(n)]
    remote = []
    for r in range(1, N_PEER + 1):
        peer = (1 - x if r & 4 else x, 1 - y if r & 2 else y, 1 - c if r & 1 else c)
        pidx = 4 * peer[0] + 2 * peer[1] + peer[2]
        for w in range(n):
            remote.append(pltpu.make_async_remote_copy(
                src_ref=slab(w, pidx), dst_ref=ride_out[w].at[me if sending else pidx], send_sem=send_sems.at[w, r - 1],
                recv_sem=recv_sems.at[w, r - 1], device_id=peer, device_id_type=pl.DeviceIdType.MESH))
    return local, remote


def _ride_start(ride_in, ride_out, same, sems):
    local, remote = _ride_copies(ride_in, ride_out, same, sems, True)
    for cp in local + remote:
        cp.start()


def _ride_wait(ride_in, ride_out, same, sems):
    local, remote = _ride_copies(ride_in, ride_out, same, sems, False)
    for cp in remote:
        cp.wait_recv()
    for cp in remote:
        cp.wait_send()
    for cp in local:
        cp.wait()


def call_with_ride(body, *, name, grid, in_specs, args, out_specs, out_shape, scratch_shapes=(), ride=None, **cparams):
    out_specs, out_shape, scratch_shapes = list(out_specs), list(out_shape), list(scratch_shapes)
    if ride is None:
        res = pl.pallas_call(body, grid=grid, in_specs=list(in_specs), out_specs=out_specs, out_shape=out_shape,
                             scratch_shapes=scratch_shapes, name=name, compiler_params=_cparams(**cparams))(*args)
        return list(res), []
    arrays, same = ride
    n, n_in, n_out, n_scr = len(arrays), len(args), len(out_shape), len(scratch_shapes)
    hbm = pl.BlockSpec(memory_space=pl.ANY)
    land_shape = [jax.ShapeDtypeStruct((8,) + (a.shape if s else a.shape[1:]), a.dtype) for a, s in zip(arrays, same)]

    def wrapped(*refs):
        ins, ride_in = refs[:n_in], refs[n_in:n_in + n]
        outs, ride_out = refs[n_in + n:n_in + n + n_out], refs[n_in + n + n_out:n_in + 2 * n + n_out]
        scr, sems = refs[n_in + 2 * n + n_out:n_in + 2 * n + n_out + n_scr], refs[n_in + 2 * n + n_out + n_scr:]
        first, last = True, True
        for axis, size in enumerate(grid):
            first = jnp.logical_and(first, pl.program_id(axis) == 0)
            last = jnp.logical_and(last, pl.program_id(axis) == size - 1)
        if grid:
            pl.when(first)(lambda: _ride_start(ride_in, ride_out, same, sems))
        else:
            _ride_start(ride_in, ride_out, same, sems)
        body(*ins, *outs, *scr)
        if grid:
            pl.when(last)(lambda: _ride_wait(ride_in, ride_out, same, sems))
        else:
            _ride_wait(ride_in, ride_out, same, sems)

    sems = [pltpu.SemaphoreType.DMA((n, N_PEER)), pltpu.SemaphoreType.DMA((n, N_PEER)), pltpu.SemaphoreType.DMA((n,))]
    cparams["dimension_semantics"] = ("arbitrary",) * len(grid)
    res = pl.pallas_call(wrapped, grid=grid, in_specs=list(in_specs) + [hbm] * n, out_specs=out_specs + [hbm] * n,
                         out_shape=out_shape + land_shape, scratch_shapes=scratch_shapes + sems, name=name,
                         compiler_params=_cparams(**cparams))(*args, *arrays)
    return list(res[:n_out]), list(res[n_out:])


def exchange(name, ride):
    return call_with_ride(lambda: None, name=name, grid=(), in_specs=[], args=[], out_specs=[], out_shape=[], ride=ride)[1]


def _row_tile(r, mult, cap):
    best = None
    for t in range(mult, min(r, cap) + 1, mult):
        if r % t == 0:
            best = t
    return best if best is not None else r


def adamw(name, parts, w, m, v):
    r, c = w.shape
    tr = _row_tile(r, 16, 256)
    spec = pl.BlockSpec((tr, c), lambda i: (i, 0))

    def body(p_ref, w_ref, m_ref, v_ref, g_ref, d_ref, m2_ref, v2_ref):
        g = p_ref[0].astype(F32)
        for dev in range(1, 8):
            g = g + p_ref[dev].astype(F32)
        m2 = ADAM_B1 * m_ref[...] + (1.0 - ADAM_B1) * g
        v2 = ADAM_B2 * v_ref[...] + (1.0 - ADAM_B2) * jnp.square(g)
        m_hat = m2 / (1.0 - ADAM_B1 ** ADAM_STEP)
        v_hat = v2 / (1.0 - ADAM_B2 ** ADAM_STEP)
        g_ref[...] = g
        d_ref[...] = -ADAM_LR * (m_hat / (jnp.sqrt(v_hat) + ADAM_EPS) + ADAM_WD * w_ref[...])
        m2_ref[...] = m2
        v2_ref[...] = v2

    return pl.pallas_call(
        body, grid=(r // tr,), in_specs=[pl.BlockSpec((8, tr, c), lambda i: (0, i, 0)), spec, spec, spec],
        out_specs=[spec] * 4, out_shape=[jax.ShapeDtypeStruct((r, c), F32)] * 4, name=name, compiler_params=_cparams())(parts, w, m, v)


PACK_ROWS = 256


def _pack(arrs):
    flat = jnp.concatenate([a.reshape(-1).astype(F32) for a in arrs])
    quantum = PACK_ROWS * LANES
    total = -(-flat.shape[0] // quantum) * quantum
    return jnp.pad(flat, (0, total - flat.shape[0])).reshape(-1, LANES)


def _unpack(packed, shapes):
    flat = packed.reshape(-1)
    out, off = [], 0
    for s in shapes:
        n = int(np.prod(s))
        out.append(flat[off:off + n].reshape(s))
        off += n
    return out


def _pad_lanes(vec):
    return jnp.pad(vec.astype(F32), (0, LANES - vec.shape[0]))[None, :]


def make_step(cfg):
    I, BC, H, RW, SW, LW, F, D = (cfg.ssd_inner, cfg.ssd_bc, cfg.ssd_heads, cfg.ret_w, cfg.sb_w, cfg.lru_width,
                                  cfg.ffn_dim, cfg.d_model)
    ND = cfg.n_dev
    q_off = 2 * I + 2 * BC
    dt_off = q_off + 4 * RW
    dt_w = (-dt_off) % 1024 or 1024

    def gather_cols(g):
        return jnp.transpose(g, (1, 0, 2)).reshape(g.shape[1], -1)

    def scatter_cols(full):
        r, c = full.shape
        return jnp.transpose(full.reshape(r, ND, c // ND), (1, 0, 2))

    def step(p, m_in, v_in, x, loss_target):
        nb = x.shape[0]
        rows = nb * cfg.tp

        small_shapes = [p[n].shape for n in SMALL_SHARDED]
        wfull = {}

        def gather_ride(names, extra=()):
            own = [p[n].astype(BF16) for n in names] + list(extra)
            return Ride(own, [True] * len(own))

        def gather_finish(names, landed):
            for n, g in zip(names, landed):
                wfull[n] = gather_cols(g) if n in BIG_COL else g.reshape(-1, g.shape[-1])

        first_names = ['l0_w_in', 'l0_w_out']
        landed = exchange("gather_l0_mixer", gather_ride(first_names, [_pack([p[n] for n in SMALL_SHARDED])]))
        gather_finish(first_names, landed)
        small_parts = [_unpack(landed[-1][d], small_shapes) for d in range(ND)]
        sfull = {n: jnp.concatenate([small_parts[d][k] for d in range(ND)], axis=1) for k, n in enumerate(SMALL_SHARDED)}

        w0 = wfull['l0_w_in']
        w0a = jnp.concatenate([w0[:, :q_off], w0[:, q_off + H:], jnp.pad(w0[:, q_off:q_off + H], ((0, 0), (0, dt_w - H)))], axis=1)

        sel_hd = (jnp.arange(LANES)[:, None] == (jnp.arange(I)[None, :] // cfg.ssd_head_dim)).astype(F32)
        sel_lane = (jnp.arange(LANES)[:, None] == (jnp.arange(H * LANES)[None, :] // LANES)).astype(F32)
        rconst = _ret_consts(cfg)
        row2 = lambda vec: vec.astype(F32)[None, :]

        meta = jnp.broadcast_to(sfull['meta_tokens'][None], (nb, cfg.n_meta, D))
        h0 = jnp.concatenate([jnp.zeros((nb, cfg.pad, D), F32), meta, x], axis=1).reshape(rows, D)

        def norm_fwd(name, h, g):
            return stage_fwd(name, cfg, _norm_fn, [Row(h, D)], [Par(row2(g))], [(D, BF16)])[0]

        def norm_bwd(name, h, g, dhn, dh_next):
            fn = lambda valid, hv, gv: (_norm_fn(valid, hv, gv)[0], hv)
            (dh,), (dg,) = stage_bwd(name, cfg, fn, [Row(h, D)], [Par(row2(g))], [[Row(dhn, D)], [Row(dh_next, D)]], [F32])
            return dh, dg[0]

        hn0 = norm_fwd("l0_mix_norm", h0, p['l0_mix_norm'])
        u0 = matmul("l0_in_proj", hn0, w0a)
        cw, cb = sfull['l0_ssd_conv_w'], p['l0_ssd_conv_b']
        ssd_pre = _make_ssd_pre(cfg)
        ssd_post = _make_ssd_post(cfg)
        pre_rows = [Row(u0, I, 1, tail=True), Row(u0, BC, (2 * I) // BC, tail=True), Row(u0, BC, (2 * I) // BC + 1, tail=True),
                    Row(u0, LANES, dt_off // LANES)]
        pre_pars = [Par(cw[:, :I]), Par(cw[:, I:I + BC]), Par(cw[:, I + BC:]), Par(row2(cb[:I])), Par(row2(cb[I:I + BC])),
                    Par(row2(cb[I + BC:])), Par(_pad_lanes(p['l0_ssd_dt_bias'])), Par(_pad_lanes(p['l0_ssd_a_log'])),
                    Par(sel_hd, diff=False), Par(sel_lane, diff=False)]
        xs, bm, cm, xdt, adt_b = stage_fwd("ssd_pre", cfg, ssd_pre, pre_rows, pre_pars,
                                           [(I, F32), (BC, F32), (BC, F32), (I, F32), (H * LANES, F32)])
        mid_names = ['l0_ffn_w_in', 'l0_ffn_w_out', 'l1_w_in']
        y_raw, landed = ssd_scan_fwd(cfg, xdt, bm, cm, adt_b, ride=gather_ride(mid_names))
        gather_finish(mid_names, landed)
        post_rows = [Row(y_raw, I), Row(xs, I), Row(u0, I, 0)]
        post_pars = [Par(_pad_lanes(p['l0_ssd_d'])), Par(row2(p['l0_ssd_norm'])), Par(sel_hd, diff=False)]
        (y_ssd,) = stage_fwd("ssd_post", cfg, ssd_post, post_rows, post_pars, [(I, BF16)])
        ret_g = row2(p['l0_ret_norm'])
        y_ret = ret_fwd(cfg, u0, q_off, ret_g, rconst)
        ycat0 = jnp.concatenate([y_ssd, y_ret], axis=1)
        h1 = matmul("l0_out_proj", ycat0, wfull['l0_w_out'], resid=h0, cfg=cfg)

        def ffn_fwd(tag, h, norm_g, w_in, conv_w, conv_b, w_out):
            hn = norm_fwd(tag + "_ffn_norm", h, norm_g)
            u = matmul(tag + "_ffn_in", hn, w_in)
            rws = [Row(u, F, 0, tail=True), Row(u, F, 1, tail=True)]
            prs = [Par(conv_w[:, :F]), Par(conv_w[:, F:]), Par(row2(conv_b[:F])), Par(row2(conv_b[F:]))]
            (act,) = stage_fwd(tag + "_ffn_act", cfg, _ffn_act_fn, rws, prs, [(F, BF16)])
            h_out = matmul(tag + "_ffn_out", act, w_out, resid=h, cfg=cfg)
            return h_out, (hn, u, rws, prs, act)

        h2, ffn0_saved = ffn_fwd("l0", h1, p['l0_ffn_norm'], wfull['l0_ffn_w_in'], sfull['l0_ffn_conv_w'], p['l0_ffn_conv_b'],
                                 wfull['l0_ffn_w_out'])

        hn2 = norm_fwd("l1_mix_norm", h2, p['l1_mix_norm'])
        u1 = matmul("l1_in_proj", hn2, wfull['l1_w_in'])
        late_names = ['l1_w_out', 'l1_ffn_w_in', 'l1_ffn_w_out']
        y_sb, sb_tot, landed = sb_fwd(cfg, u1, 0, ride=gather_ride(late_names))
        gather_finish(late_names, landed)
        lru_pre = _make_lru_pre(cfg)
        gate_blk = (3 * SW) // LW
        lpre_rows = [Row(u1, LW, gate_blk + 1, tail=True)]
        lpre_pars = [Par(sfull['l1_lru_conv_w']), Par(row2(p['l1_lru_conv_b'])), Par(p['l1_lru_wa']), Par(row2(p['l1_lru_ba'])),
                     Par(p['l1_lru_wx']), Par(row2(p['l1_lru_bx'])), Par(row2(p['l1_lru_lambda']))]
        lru_a, lru_b = stage_fwd("lru_pre", cfg, lru_pre, lpre_rows, lpre_pars, [(LW, F32), (LW, F32)])
        lru_h = lru_scan_fwd(cfg, lru_a, lru_b)
        lpost_rows = [Row(lru_h, LW), Row(u1, LW, gate_blk)]
        (y_lru,) = stage_fwd("lru_post", cfg, _lru_post_fn, lpost_rows, [], [(LW, BF16)])
        ycat1 = jnp.concatenate([y_sb, y_lru], axis=1)
        h3 = matmul("l1_out_proj", ycat1, wfull['l1_w_out'], resid=h2, cfg=cfg)
        h4, ffn1_saved = ffn_fwd("l1", h3, p['l1_ffn_norm'], wfull['l1_ffn_w_in'], sfull['l1_ffn_conv_w'], p['l1_ffn_conv_b'],
                                 wfull['l1_ffn_w_out'])

        loss_part, dh4, d_final = final_loss(cfg, h4, loss_target.reshape(nb * cfg.seq, D), row2(p['final_norm']))
        loss = lax.psum(loss_part[0, 0], ("x", "y", "c"))
        gfull, grep = {}, {'final_norm': d_final[0]}

        def ffn_bwd(tag, h, norm_g, w_in, w_out, saved, dh_out):
            hn, u, rws, prs, act = saved
            dact = matmul(tag + "_ffn_out_dx", dh_out, w_out, tb=True, out_dtype=BF16)
            gfull[tag + '_ffn_w_out'] = matmul(tag + "_ffn_out_dw", act, dh_out, ta=True, out_dtype=BF16, tm_cap=512, tn_cap=512, tk_cap=4352)
            (dug, duu), (dwg, dwu, dbg, dbu) = stage_bwd(tag + "_ffn_act_bwd", cfg, _ffn_act_fn, rws, prs, [[Row(dact, F)]], [BF16, BF16])
            du = jnp.concatenate([dug, duu], axis=1)
            gfull[tag + '_ffn_conv_w'] = jnp.concatenate([dwg, dwu], axis=1)
            grep[tag + '_ffn_conv_b'] = jnp.concatenate([dbg, dbu], axis=1)[0]
            dhn = matmul(tag + "_ffn_in_dx", du, w_in, tb=True)
            gfull[tag + '_ffn_w_in'] = matmul(tag + "_ffn_in_dw", hn, du, ta=True, out_dtype=BF16, tm_cap=512, tn_cap=512, tk_cap=4352)
            dh, grep[tag + '_ffn_norm'] = norm_bwd(tag + "_ffn_norm_bwd", h, norm_g, dhn, dh_out)
            return dh

        parts_of = {}

        def grads_ride(names, extra=(), extra_same=()):
            sends = [scatter_cols(gfull[n]) if n in BIG_COL else gfull[n].reshape(ND, -1, gfull[n].shape[-1]) for n in names]
            return Ride(sends + list(extra), [False] * len(names) + list(extra_same))

        dh3 = ffn_bwd("l1", h3, p['l1_ffn_norm'], wfull['l1_ffn_w_in'], wfull['l1_ffn_w_out'], ffn1_saved, dh4)

        dycat1 = matmul("l1_out_dx", dh3, wfull['l1_w_out'], tb=True, out_dtype=BF16)
        gfull['l1_w_out'] = matmul("l1_out_dw", ycat1, dh3, ta=True, out_dtype=BF16, tm_cap=512, tn_cap=512, tk_cap=4352)
        (d_lru_h, d_gate), _ = stage_bwd("lru_post_bwd", cfg, _lru_post_fn, lpost_rows, [], [[Row(dycat1, LW, SW // LW)]], [F32, BF16])
        d_lru_a, d_lru_b = lru_scan_bwd(cfg, lru_a, lru_h, d_lru_h)
        (d_xr,), lgr = stage_bwd("lru_pre_bwd", cfg, lru_pre, lpre_rows, lpre_pars, [[Row(d_lru_a, LW)], [Row(d_lru_b, LW)]], [BF16])
        gfull['l1_lru_conv_w'] = lgr[0]
        grep.update({'l1_lru_conv_b': lgr[1][0], 'l1_lru_wa': lgr[2], 'l1_lru_ba': lgr[3][0], 'l1_lru_wx': lgr[4],
                     'l1_lru_bx': lgr[5][0], 'l1_lru_lambda': lgr[6][0]})
        names = ['l1_ffn_w_out', 'l1_ffn_w_in', 'l1_w_out']
        (d_q, d_k, d_v), landed = sb_bwd(cfg, u1, 0, sb_tot, dycat1, 0, ride=grads_ride(names))
        parts_of.update(zip(names, landed))
        du1 = jnp.concatenate([d_q, d_k, d_v, d_gate, d_xr], axis=1)
        dhn2 = matmul("l1_in_dx", du1, wfull['l1_w_in'], tb=True)
        gfull['l1_w_in'] = matmul("l1_in_dw", hn2, du1, ta=True, out_dtype=BF16, tm_cap=512, tn_cap=512, tk_cap=4352)
        dh2, grep['l1_mix_norm'] = norm_bwd("l1_mix_norm_bwd", h2, p['l1_mix_norm'], dhn2, dh3)

        dh1 = ffn_bwd("l0", h1, p['l0_ffn_norm'], wfull['l0_ffn_w_in'], wfull['l0_ffn_w_out'], ffn0_saved, dh2)

        dycat0 = matmul("l0_out_dx", dh1, wfull['l0_w_out'], tb=True, out_dtype=BF16)
        gfull['l0_w_out'] = matmul("l0_out_dw", ycat0, dh1, ta=True, out_dtype=BF16, tm_cap=512, tn_cap=512, tk_cap=4352)
        (d_yraw, d_xs1, d_z), (d_dskip, d_ssdnorm) = stage_bwd("ssd_post_bwd", cfg, ssd_post, post_rows, post_pars,
                                                               [[Row(dycat0, I, 0)]], [F32, F32, BF16])
        names = ['l1_w_in', 'l0_ffn_w_out', 'l0_ffn_w_in', 'l0_w_out']
        (d_xdt, d_bm, d_cm, d_adt), landed = ssd_scan_bwd(cfg, xdt, bm, cm, adt_b, d_yraw, ride=grads_ride(names))
        parts_of.update(zip(names, landed))
        (d_xs, d_b, d_c, d_dt), sgr = stage_bwd("ssd_pre_bwd", cfg, ssd_pre, pre_rows, pre_pars,
                                                 [[Row(d_xs1, I)], [Row(d_bm, BC)], [Row(d_cm, BC)], [Row(d_xdt, I)], [Row(d_adt, H * LANES)]],
                                                 [BF16] * 4)
        gfull['l0_ssd_conv_w'] = jnp.concatenate(sgr[0:3], axis=1)
        grep.update({'l0_ssd_conv_b': jnp.concatenate(sgr[3:6], axis=1)[0], 'l0_ssd_dt_bias': sgr[6][0, :H],
                     'l0_ssd_a_log': sgr[7][0, :H], 'l0_ssd_d': d_dskip[0, :H], 'l0_ssd_norm': d_ssdnorm[0]})
        (d_rq, d_rk, d_rv, d_rg), d_retnorm = ret_bwd(cfg, u0, q_off, ret_g, rconst, dycat0, I)
        grep['l0_ret_norm'] = d_retnorm[0]
        du0 = jnp.concatenate([d_z, d_xs, d_b, d_c, d_rq, d_rk, d_rv, d_rg, d_dt, jnp.zeros((rows, dt_w - LANES), BF16)], axis=1)
        dhn0 = matmul("l0_in_dx", du0, w0a, tb=True)
        dw0a = matmul("l0_in_dw", hn0, du0, ta=True, out_dtype=BF16, tm_cap=512, tn_cap=512, tk_cap=4352)
        gfull['l0_w_in'] = jnp.concatenate([dw0a[:, :q_off], dw0a[:, dt_off:dt_off + H], dw0a[:, q_off:dt_off]], axis=1)
        dh0, grep['l0_mix_norm'] = norm_bwd("l0_mix_norm_bwd", h0, p['l0_mix_norm'], dhn0, dh1)

        dh0 = dh0.reshape(nb, cfg.tp, D)
        grad_x = dh0[:, cfg.chunk:, :]
        gfull['meta_tokens'] = jnp.sum(dh0[:, cfg.pad:cfg.chunk, :], axis=0)

        small_send = jnp.stack([_pack([scatter_cols(gfull[n])[d] for n in SMALL_SHARDED]) for d in range(ND)])
        rep_send = _pack([grep[n] for n in REPLICATED])
        landed = exchange("exchange_last_grads", grads_ride(['l0_w_in'], [small_send, rep_send], [False, True]))
        parts_of.update(zip(['l0_w_in', 'small', 'replicated'], landed))

        grad, delta, new_m, new_v = {}, {}, {}, {}
        for n in BIG:
            grad[n], delta[n], new_m[n], new_v[n] = adamw("adamw_" + n, parts_of[n], p[n], m_in[n], v_in[n])
        for names, parts in ((SMALL_SHARDED, parts_of['small']), (REPLICATED, parts_of['replicated'])):
            shapes = [p[n].shape for n in names]
            res = adamw("adamw_small_sharded" if names is SMALL_SHARDED else "adamw_replicated", parts,
                        _pack([p[n] for n in names]), _pack([m_in[n] for n in names]), _pack([v_in[n] for n in names]))
            for dst, packed in zip((grad, delta, new_m, new_v), res):
                for n, a in zip(names, _unpack(packed, shapes)):
                    dst[n] = a
        return (loss, grad_x, *[grad[n] for n in WEIGHTS], *[delta[n] for n in WEIGHTS], *[new_m[n] for n in WEIGHTS],
                *[new_v[n] for n in WEIGHTS])

    return step


_STEP = make_step(CFG)


def kernel(x, meta_tokens, l0_mix_norm, l0_w_in, l0_ssd_conv_w, l0_ssd_conv_b, l0_ssd_dt_bias, l0_ssd_a_log, l0_ssd_d, l0_ssd_norm, l0_ret_norm, l0_w_out, l0_ffn_norm, l0_ffn_w_in, l0_ffn_conv_w, l0_ffn_conv_b, l0_ffn_w_out, l1_mix_norm, l1_w_in, l1_lru_conv_w, l1_lru_conv_b, l1_lru_wa, l1_lru_ba, l1_lru_wx, l1_lru_bx, l1_lru_lambda, l1_w_out, l1_ffn_norm, l1_ffn_w_in, l1_ffn_conv_w, l1_ffn_conv_b, l1_ffn_w_out, final_norm, loss_target, m_meta_tokens, m_l0_mix_norm, m_l0_w_in, m_l0_ssd_conv_w, m_l0_ssd_conv_b, m_l0_ssd_dt_bias, m_l0_ssd_a_log, m_l0_ssd_d, m_l0_ssd_norm, m_l0_ret_norm, m_l0_w_out, m_l0_ffn_norm, m_l0_ffn_w_in, m_l0_ffn_conv_w, m_l0_ffn_conv_b, m_l0_ffn_w_out, m_l1_mix_norm, m_l1_w_in, m_l1_lru_conv_w, m_l1_lru_conv_b, m_l1_lru_wa, m_l1_lru_ba, m_l1_lru_wx, m_l1_lru_bx, m_l1_lru_lambda, m_l1_w_out, m_l1_ffn_norm, m_l1_ffn_w_in, m_l1_ffn_conv_w, m_l1_ffn_conv_b, m_l1_ffn_w_out, m_final_norm, v_meta_tokens, v_l0_mix_norm, v_l0_w_in, v_l0_ssd_conv_w, v_l0_ssd_conv_b, v_l0_ssd_dt_bias, v_l0_ssd_a_log, v_l0_ssd_d, v_l0_ssd_norm, v_l0_ret_norm, v_l0_w_out, v_l0_ffn_norm, v_l0_ffn_w_in, v_l0_ffn_conv_w, v_l0_ffn_conv_b, v_l0_ffn_w_out, v_l1_mix_norm, v_l1_w_in, v_l1_lru_conv_w, v_l1_lru_conv_b, v_l1_lru_wa, v_l1_lru_ba, v_l1_lru_wx, v_l1_lru_bx, v_l1_lru_lambda, v_l1_w_out, v_l1_ffn_norm, v_l1_ffn_w_in, v_l1_ffn_conv_w, v_l1_ffn_conv_b, v_l1_ffn_w_out, v_final_norm):
    args = locals()
    p = {n: args[n] for n in WEIGHTS}
    m_in = {n: args["m_" + n] for n in WEIGHTS}
    v_in = {n: args["v_" + n] for n in WEIGHTS}
    return _STEP(p, m_in, v_in, x, loss_target)
```

```python
import functools
from typing import NamedTuple

import numpy as np
import jax
import jax.numpy as jnp
from jax import lax
from jax.experimental import pallas as pl
from jax.experimental.pallas import tpu as pltpu

F32 = jnp.float32
BF16 = jnp.bfloat16
EPS = 1e-6
LRU_C = 8.0
NEG = -1e30
SUBLANES = 8
LANES = 128
VMEM_LIMIT = 56 * 1024 * 1024
CHUNK_UNROLL = 4

ADAM_LR = 0.001
ADAM_B1 = 0.9
ADAM_B2 = 0.999
ADAM_EPS = 1e-08
ADAM_WD = 0.01
ADAM_STEP = 10

WEIGHTS = ['meta_tokens', 'l0_mix_norm', 'l0_w_in', 'l0_ssd_conv_w', 'l0_ssd_conv_b', 'l0_ssd_dt_bias', 'l0_ssd_a_log',
           'l0_ssd_d', 'l0_ssd_norm', 'l0_ret_norm', 'l0_w_out', 'l0_ffn_norm', 'l0_ffn_w_in', 'l0_ffn_conv_w',
           'l0_ffn_conv_b', 'l0_ffn_w_out', 'l1_mix_norm', 'l1_w_in', 'l1_lru_conv_w', 'l1_lru_conv_b', 'l1_lru_wa',
           'l1_lru_ba', 'l1_lru_wx', 'l1_lru_bx', 'l1_lru_lambda', 'l1_w_out', 'l1_ffn_norm', 'l1_ffn_w_in',
           'l1_ffn_conv_w', 'l1_ffn_conv_b', 'l1_ffn_w_out', 'final_norm']
BIG_COL = ['l0_w_in', 'l0_ffn_w_in', 'l1_w_in', 'l1_ffn_w_in']
BIG_ROW = ['l0_w_out', 'l0_ffn_w_out', 'l1_w_out', 'l1_ffn_w_out']
BIG = BIG_COL + BIG_ROW
SMALL_SHARDED = ['meta_tokens', 'l0_ssd_conv_w', 'l0_ffn_conv_w', 'l1_lru_conv_w', 'l1_ffn_conv_w']
REPLICATED = [w for w in WEIGHTS if w not in BIG and w not in SMALL_SHARDED]
REP_EARLY = ['final_norm', 'l1_ffn_norm', 'l1_ffn_conv_b', 'l1_lru_conv_b', 'l1_lru_wa', 'l1_lru_ba', 'l1_lru_wx',
             'l1_lru_bx', 'l1_lru_lambda']
REP_LATE = [w for w in REPLICATED if w not in REP_EARLY]


class Config(NamedTuple):
    d_model: int = 1024
    seq: int = 2048
    n_meta: int = 16
    chunk: int = 128
    ssd_heads: int = 16
    ssd_head_dim: int = 64
    ssd_groups: int = 4
    ssd_state: int = 128
    ret_heads: int = 4
    ret_dim: int = 256
    sb_heads: int = 16
    sb_head_dim: int = 64
    lru_width: int = 1024
    lru_blocks: int = 8
    ffn_dim: int = 2816
    n_dev: int = 8

    @property
    def t(self):
        return self.n_meta + self.seq

    @property
    def pad(self):
        return (-self.t) % self.chunk

    @property
    def tp(self):
        return self.t + self.pad

    @property
    def nc(self):
        return self.tp // self.chunk

    @property
    def ssd_inner(self):
        return self.ssd_heads * self.ssd_head_dim

    @property
    def ssd_gw(self):
        return self.ssd_inner // self.ssd_groups

    @property
    def ssd_bc(self):
        return self.ssd_groups * self.ssd_state

    @property
    def ret_w(self):
        return self.ret_heads * self.ret_dim

    @property
    def sb_w(self):
        return self.sb_heads * self.sb_head_dim

    @property
    def mix0_segs(self):
        return (self.ssd_inner, self.ssd_inner + 2 * self.ssd_bc, self.ssd_heads, self.ret_w, self.ret_w, self.ret_w, self.ret_w)


CFG = Config()


def _dg(a, b, ca, cb):
    return lax.dot_general(a, b, (((ca,), (cb,)), ((), ())), preferred_element_type=F32)


def _cot_a(b, ca, cb, g):
    return _dg(g, b, 1, 1 - cb) if ca == 1 else _dg(b, g, 1 - cb, 1)


def _cot_b(a, ca, cb, g):
    return _dg(a, g, 1 - ca, 0) if cb == 0 else _dg(g, a, 0, 1 - ca)


@functools.partial(jax.custom_vjp, nondiff_argnums=(2, 3))
def _bdot(a, b, ca, cb):
    return _dg(a.astype(BF16), b.astype(BF16), ca, cb)


def _bdot_fwd(a, b, ca, cb):
    return _bdot(a, b, ca, cb), (a, b)


def _bdot_bwd(ca, cb, res, g):
    a, b = res
    gb = g.astype(BF16)
    return _cot_a(b.astype(BF16), ca, cb, gb).astype(a.dtype), _cot_b(a.astype(BF16), ca, cb, gb).astype(b.dtype)


_bdot.defvjp(_bdot_fwd, _bdot_bwd)


def _dot(a, b):
    return _bdot(a, b, 1, 0)


def _dot_nt(a, b):
    return _bdot(a, b, 1, 1)


def _dot_tn(a, b):
    return _bdot(a, b, 0, 0)


def _split3(a):
    hi = a.astype(BF16)
    r1 = a - hi.astype(F32)
    mid = r1.astype(BF16)
    return hi, mid, (r1 - mid.astype(F32)).astype(BF16)


@functools.partial(jax.custom_vjp, nondiff_argnums=(2, 3, 4))
def _edot(a, b, ca, cb, const):
    if const == 'b':
        bb = b.astype(BF16)
        return sum(_dg(p, bb, ca, cb) for p in _split3(a))
    ab = a.astype(BF16)
    return sum(_dg(ab, p, ca, cb) for p in _split3(b))


def _edot_fwd(a, b, ca, cb, const):
    return _edot(a, b, ca, cb, const), (a, b)


def _edot_bwd(ca, cb, const, res, g):
    a, b = res
    if const == 'b':
        bb = b.astype(BF16)
        return sum(_cot_a(bb, ca, cb, p) for p in _split3(g)), jnp.zeros_like(b)
    ab = a.astype(BF16)
    return jnp.zeros_like(a), sum(_cot_b(ab, ca, cb, p) for p in _split3(g))


_edot.defvjp(_edot_fwd, _edot_bwd)


def _dot_exact01(a, sel):
    return _edot(a, sel, 1, 0, 'b')


def _silu(x):
    return x * jax.nn.sigmoid(x)


def _softplus(x):
    return jnp.maximum(x, 0.0) + jnp.log1p(jnp.exp(-jnp.abs(x)))


def _neg_expm1(x):
    series = -x * (1.0 + x * 0.5 * (1.0 + x / 3.0 * (1.0 + x * 0.25)))
    return jnp.where(x > -0.01, series, 1.0 - jnp.exp(x))


def _causal_conv(x, tail, w, b):
    taps, rows = w.shape[0], x.shape[0]
    xx = jnp.concatenate([tail, x], axis=0)
    y = b
    for k in range(taps):
        off = SUBLANES - (taps - 1 - k)
        y = y + w[k:k + 1, :] * xx[off:off + rows, :]
    return y


def _valid_chunk(cfg, chunk_idx):
    first = (chunk_idx % cfg.nc) == 0
    rid = lax.broadcasted_iota(jnp.int32, (cfg.chunk, 1), 0)
    return jnp.where(jnp.logical_and(first, rid < cfg.pad), 0.0, 1.0).astype(F32)


def _cparams(**kw):
    return pltpu.CompilerParams(vmem_limit_bytes=VMEM_LIMIT, **kw)


class Row(NamedTuple):
    arr: jax.Array
    width: int
    blk: int = 0
    tail: bool = False
    diff: bool = True


class Par(NamedTuple):
    arr: jax.Array
    diff: bool = True


def _row_specs(cfg, rows, order):
    tr = cfg.chunk
    specs, args = [], []
    for r in rows:
        specs.append(pl.BlockSpec((tr, r.width), functools.partial(lambda i, b, o: (o(i), b), b=r.blk, o=order)))
        args.append(r.arr)
        if r.tail:
            per = tr // SUBLANES
            specs.append(pl.BlockSpec((SUBLANES, r.width),
                                      functools.partial(lambda i, b, o: (jnp.maximum(o(i) * per - 1, 0), b), b=r.blk, o=order)))
            args.append(r.arr)
    return specs, args


def _par_specs(pars):
    specs = [pl.BlockSpec(p.arr.shape, functools.partial(lambda i, nd: (0,) * nd, nd=p.arr.ndim)) for p in pars]
    return specs, [p.arr for p in pars]


def stage_fwd(name, cfg, fn, rows, pars, outs):
    tr = cfg.chunk
    n_rows = rows[0].arr.shape[0]
    rspecs, rargs = _row_specs(cfg, rows, lambda i: i)
    pspecs, pargs = _par_specs(pars)
    n_in = len(rargs) + len(pargs)

    def body(*refs):
        valid = _valid_chunk(cfg, pl.program_id(0))
        vals = [r[...].astype(F32) for r in refs[:n_in]]
        res = fn(valid, *vals)
        for o, v in zip(refs[n_in:], res):
            o[...] = v.astype(o.dtype)

    return pl.pallas_call(
        body, grid=(n_rows // tr,), in_specs=rspecs + pspecs,
        out_specs=[pl.BlockSpec((tr, w), lambda i: (i, 0)) for w, _ in outs],
        out_shape=[jax.ShapeDtypeStruct((n_rows, w), dt) for w, dt in outs],
        name=name, compiler_params=_cparams())(*rargs, *pargs)


def stage_bwd(name, cfg, fn, rows, pars, douts, drow_dtypes):
    tr = cfg.chunk
    n_rows = rows[0].arr.shape[0]
    n_blk = n_rows // tr
    order = lambda i: n_blk - 1 - i
    rspecs, rargs = _row_specs(cfg, rows, order)
    pspecs, pargs = _par_specs(pars)
    pieces = [p for out in douts for p in out]
    dspecs, dargs = _row_specs(cfg, pieces, order)
    n_r, n_p, n_d = len(rargs), len(pargs), len(dargs)
    diff_rows = [r for r in rows if r.diff]
    diff_pars = [p for p in pars if p.diff]
    tails = [r for r in diff_rows if r.tail]

    def body(*refs):
        in_refs = refs[:n_r + n_p]
        d_refs = refs[n_r + n_p:n_r + n_p + n_d]
        o_refs = refs[n_r + n_p + n_d:]
        drow_refs = o_refs[:len(diff_rows)]
        dpar_refs = o_refs[len(diff_rows):len(diff_rows) + len(diff_pars)]
        carry_refs = o_refs[len(diff_rows) + len(diff_pars):]
        step = pl.program_id(0)
        valid = _valid_chunk(cfg, order(step))
        vals = [r[...].astype(F32) for r in in_refs]
        slots, pos = [], 0
        for r in rows:
            if r.diff:
                slots.append(pos)
                if r.tail:
                    slots.append(pos + 1)
            pos += 2 if r.tail else 1
        for p in pars:
            if p.diff:
                slots.append(pos)
            pos += 1

        def g(*dv):
            full = list(vals)
            for s, v in zip(slots, dv):
                full[s] = v
            return tuple(fn(valid, *full))

        _, vjp = jax.vjp(g, *[vals[s] for s in slots])
        cts, k = [], 0
        for out in douts:
            parts = [d_refs[k + j][...].astype(F32) for j in range(len(out))]
            k += len(out)
            cts.append(parts[0] if len(parts) == 1 else jnp.concatenate(parts, axis=1))
        grads = list(vjp(tuple(cts)))

        @pl.when(step == 0)
        def _():
            for c in carry_refs:
                c[...] = jnp.zeros_like(c)
            for d in dpar_refs:
                d[...] = jnp.zeros_like(d)

        gi, ci = 0, 0
        for r, o in zip(diff_rows, drow_refs):
            dx = grads[gi]
            gi += 1
            if r.tail:
                dtail = grads[gi]
                gi += 1
                c = carry_refs[ci]
                ci += 1
                dx = dx + jnp.concatenate([jnp.zeros((tr - SUBLANES, r.width), F32), c[...]], axis=0)
                c[...] = dtail
            o[...] = dx.astype(o.dtype)
        for d in dpar_refs:
            d[...] += grads[gi]
            gi += 1

    out_specs = [pl.BlockSpec((tr, r.width), lambda i: (order(i), 0)) for r in diff_rows]
    out_shape = [jax.ShapeDtypeStruct((n_rows, r.width), dt) for r, dt in zip(diff_rows, drow_dtypes)]
    for p in diff_pars:
        out_specs.append(pl.BlockSpec(p.arr.shape, functools.partial(lambda i, nd: (0,) * nd, nd=p.arr.ndim)))
        out_shape.append(jax.ShapeDtypeStruct(p.arr.shape, F32))
    res = pl.pallas_call(
        body, grid=(n_blk,), in_specs=rspecs + pspecs + dspecs, out_specs=out_specs, out_shape=out_shape,
        scratch_shapes=[pltpu.VMEM((SUBLANES, r.width), F32) for r in tails],
        name=name, compiler_params=_cparams(dimension_semantics=("arbitrary",)))(*rargs, *pargs, *dargs)
    return list(res[:len(diff_rows)]), list(res[len(diff_rows):])


def _tile(n, cap):
    if n % LANES:
        return n
    q = n // LANES
    best = 1
    for k in range(1, q + 1):
        if q % k == 0 and k * LANES <= cap:
            best = k
    return best * LANES


def _mm_vmem(tm, tn, tk, a_bytes, b_bytes, out_bytes, resid, nk):
    total = 2 * (tm * tk * a_bytes + tk * tn * b_bytes + tm * tn * out_bytes) + tm * tn * 4
    total += 2 * tm * tn * 4 if resid else 0
    total += (tm * tk * 2 if a_bytes == 4 else 0) + (tk * tn * 2 if b_bytes == 4 else 0)
    return total + (tm * tn * 4 if nk > 1 else 0)


def matmul(name, a, b, *, ta=False, tb=False, out_dtype=F32, resid=None, cfg=None, tm_cap=2176, tn_cap=1024, tk_cap=2816,
           ride=None):
    m, k = (a.shape[1], a.shape[0]) if ta else a.shape
    n = b.shape[0] if tb else b.shape[1]
    tm, tn, tk = _tile(m, tm_cap), _tile(n, tn_cap), _tile(k, tk_cap)
    budget = VMEM_LIMIT - 8 * 1024 * 1024
    sizes = (a.dtype.itemsize, b.dtype.itemsize, jnp.dtype(out_dtype).itemsize, resid is not None)
    while _mm_vmem(tm, tn, tk, *sizes, k // tk) > budget:
        if tm > 256:
            tm = _tile(m, tm - 1)
        elif tk > 512:
            tk = _tile(k, tk - 1)
        else:
            tn = _tile(n, tn - 1)
    nk = k // tk
    a_spec = pl.BlockSpec((tk, tm), lambda i, j, l: (l, i)) if ta else pl.BlockSpec((tm, tk), lambda i, j, l: (i, l))
    b_spec = pl.BlockSpec((tn, tk), lambda i, j, l: (j, l)) if tb else pl.BlockSpec((tk, tn), lambda i, j, l: (l, j))
    dims = (((0 if ta else 1,), (1 if tb else 0,)), ((), ()))
    in_specs, args = [a_spec, b_spec], [a, b]
    if resid is not None:
        in_specs.append(pl.BlockSpec((tm, tn), lambda i, j, l: (i, j)))
        args.append(resid)

    def body(*refs):
        a_ref, b_ref = refs[0], refs[1]
        o_ref = refs[3] if resid is not None else refs[2]
        part = lax.dot_general(a_ref[...].astype(BF16), b_ref[...].astype(BF16), dims, preferred_element_type=F32)

        def finish(res):
            if resid is not None:
                row = pl.program_id(0) * tm + lax.broadcasted_iota(jnp.int32, (tm, 1), 0)
                inpad = jnp.zeros((tm, 1), jnp.bool_)
                for s in range(m // cfg.tp):
                    inpad = jnp.logical_or(inpad, jnp.logical_and(row >= s * cfg.tp, row < s * cfg.tp + cfg.pad))
                res = refs[2][...] + jnp.where(inpad, 0.0, res)
            o_ref[...] = res.astype(o_ref.dtype)

        if nk == 1:
            finish(part)
            return
        acc = refs[-1]
        l = pl.program_id(2)

        @pl.when(l == 0)
        def _():
            acc[...] = part

        @pl.when(jnp.logical_and(l > 0, l < nk - 1))
        def _():
            acc[...] += part

        @pl.when(l == nk - 1)
        def _():
            finish(acc[...] + part)

    outs, landed = call_with_ride(
        body, name=name, grid=(m // tm, n // tn, nk), in_specs=in_specs, args=args,
        out_specs=[pl.BlockSpec((tm, tn), lambda i, j, l: (i, j))], out_shape=[jax.ShapeDtypeStruct((m, n), out_dtype)],
        scratch_shapes=[pltpu.VMEM((tm, tn), F32)] if nk > 1 else [], ride=ride,
        dimension_semantics=("parallel", "parallel", "arbitrary"))
    return (outs[0], landed) if ride is not None else outs[0]


def _norm_fn(valid, h, g):
    y = h * lax.rsqrt(jnp.mean(h * h, axis=-1, keepdims=True) + EPS)
    return (valid * (y * g),)


def _make_ssd_pre(cfg):
    nh = cfg.ssd_heads

    def fn(valid, xs, xs_t, bm, bm_t, cm, cm_t, dtr, w_xs, w_b, w_c, b_xs, b_b, b_c, dt_bias, a_log, sel_hd, sel_lane):
        xs = valid * _silu(_causal_conv(xs, xs_t, w_xs, b_xs))
        bm = valid * _silu(_causal_conv(bm, bm_t, w_b, b_b))
        cm = valid * _silu(_causal_conv(cm, cm_t, w_c, b_c))
        lane = lax.broadcasted_iota(jnp.int32, (1, LANES), 1)
        head = jnp.where(lane < nh, 1.0, 0.0).astype(F32)
        dt = valid * head * _softplus(dtr + dt_bias)
        adt = dt * (-jnp.exp(a_log))
        x = xs * _dot_exact01(dt, sel_hd)
        adt_b = _dot_exact01(adt, sel_lane)
        return xs, bm, cm, x, adt_b

    return fn


def _make_ssd_post(cfg):
    gw = cfg.ssd_gw

    def fn(valid, y_raw, xs, z, d_skip, norm_g, sel_hd):
        d_rep = _dot_exact01(jnp.broadcast_to(d_skip, (SUBLANES, LANES)), sel_hd)[0:1]
        y = (y_raw + xs * d_rep) * _silu(z)
        parts = []
        for g in range(cfg.ssd_groups):
            yg = y[:, g * gw:(g + 1) * gw]
            parts.append(yg * lax.rsqrt(jnp.mean(yg * yg, axis=-1, keepdims=True) + EPS))
        return (jnp.concatenate(parts, axis=1) * norm_g,)

    return fn


def _ffn_act_fn(valid, ug, ug_t, uu, uu_t, w_g, w_u, b_g, b_u):
    return (valid * _silu(_causal_conv(ug, ug_t, w_g, b_g)) * _causal_conv(uu, uu_t, w_u, b_u),)


def _make_lru_pre(cfg):
    nb = cfg.lru_blocks
    bw = cfg.lru_width // nb

    def fn(valid, xr, xr_t, w_conv, b_conv, wa, ba, wx, bx, lam):
        xc = _causal_conv(xr, xr_t, w_conv, b_conv)
        a_parts, b_parts = [], []
        for n in range(nb):
            sl = slice(n * bw, (n + 1) * bw)
            xn = xc[:, sl]
            r = jax.nn.sigmoid(_dot(xn, wa[n]) + ba[:, sl])
            i = jax.nn.sigmoid(_dot(xn, wx[n]) + bx[:, sl])
            log_a = -LRU_C * r * _softplus(-lam[:, sl])
            a_parts.append(jnp.exp(log_a))
            b_parts.append(valid * jnp.sqrt(jnp.maximum(_neg_expm1(2.0 * log_a), 0.0)) * (i * xn))
        return jnp.concatenate(a_parts, axis=1), jnp.concatenate(b_parts, axis=1)

    return fn


def _lru_post_fn(valid, hs, gate):
    return (hs * jax.nn.gelu(gate),)


def _tri_consts(n):
    r = lax.broadcasted_iota(jnp.int32, (n, n), 0)
    c = lax.broadcasted_iota(jnp.int32, (n, n), 1)
    return r, c


def _ssd_chunk(x, bm, cm, a, s, tril, lower):
    rows = x.shape[0]
    heads = a.shape[1] // LANES
    p = x.shape[1] // heads
    cb = _dot_nt(cm, bm)
    ys, ss = [], []
    for e in range(heads):
        ae = a[:, e * LANES:(e + 1) * LANES]
        cs = _edot(tril, ae, 1, 0, 'a')
        lmat = jnp.exp(jnp.where(lower, cs - cs.T, NEG))
        xe = x[:, e * p:(e + 1) * p]
        se = s[e * p:(e + 1) * p, :]
        tot = cs[rows - 1:rows, :]
        y_diag = _dot(cb * lmat, xe)
        st = _dot_tn(xe * jnp.exp(tot - cs)[:, :p], bm)
        y_off = _dot_nt(cm, se) * jnp.exp(cs)[:, :p]
        ys.append(y_diag + y_off)
        ss.append(jnp.exp(tot[:, :1]) * se + st)
    return jnp.concatenate(ys, axis=1), jnp.concatenate(ss, axis=0)


def _ssd_chunk_state(x, bm, a, s, tril):
    rows = x.shape[0]
    heads = a.shape[1] // LANES
    p = x.shape[1] // heads
    ss = []
    for e in range(heads):
        cs = _edot(tril, a[:, e * LANES:(e + 1) * LANES], 1, 0, 'a')
        tot = cs[rows - 1:rows, :]
        st = _dot_tn(x[:, e * p:(e + 1) * p] * jnp.exp(tot - cs)[:, :p], bm)
        ss.append(jnp.exp(tot[:, :1]) * s[e * p:(e + 1) * p, :] + st)
    return jnp.concatenate(ss, axis=0)


def _ssd_specs(cfg, x, bm, cm, adt_b):
    gw, ns = cfg.ssd_gw, cfg.ssd_state
    hpg = cfg.ssd_heads // cfg.ssd_groups
    specs = [pl.BlockSpec((cfg.tp, gw), lambda b, g: (b, g)),
             pl.BlockSpec((cfg.tp, ns), lambda b, g: (b, g)),
             pl.BlockSpec((cfg.tp, ns), lambda b, g: (b, g)),
             pl.BlockSpec((cfg.tp, hpg * LANES), lambda b, g: (b, g))]
    return specs, [x, bm, cm, adt_b]


def ssd_scan_fwd(cfg, x, bm, cm, adt_b, ride=None):
    nb = x.shape[0] // cfg.tp
    L, nc = cfg.chunk, cfg.nc
    specs, args = _ssd_specs(cfg, x, bm, cm, adt_b)

    def body(x_ref, b_ref, c_ref, a_ref, y_ref, s_ref):
        r, c = _tri_consts(L)
        lower = r >= c
        tril = lower.astype(F32)
        s_ref[...] = jnp.zeros_like(s_ref)

        def step(ci, carry):
            rows = pl.ds(pl.multiple_of(ci * L, L), L)
            y, s_new = _ssd_chunk(x_ref[rows, :], b_ref[rows, :], c_ref[rows, :], a_ref[rows, :], s_ref[...], tril, lower)
            y_ref[rows, :] = y
            s_ref[...] = s_new
            return carry

        lax.fori_loop(0, nc, step, 0, unroll=CHUNK_UNROLL)

    outs, landed = call_with_ride(
        body, name="ssd_scan_fwd", grid=(nb, cfg.ssd_groups), in_specs=specs, args=args, out_specs=[specs[0]],
        out_shape=[jax.ShapeDtypeStruct(x.shape, F32)], scratch_shapes=[pltpu.VMEM((cfg.ssd_gw, cfg.ssd_state), F32)], ride=ride)
    return outs[0], landed


def ssd_scan_bwd(cfg, x, bm, cm, adt_b, dy, ride=None):
    nb = x.shape[0] // cfg.tp
    L, nc = cfg.chunk, cfg.nc
    specs, args = _ssd_specs(cfg, x, bm, cm, adt_b)

    def body(x_ref, b_ref, c_ref, a_ref, dy_ref, dx_ref, db_ref, dc_ref, da_ref, s_all, ds_ref):
        r, c = _tri_consts(L)
        lower = r >= c
        tril = lower.astype(F32)
        chunk = functools.partial(_ssd_chunk, tril=tril, lower=lower)
        ds_ref[...] = jnp.zeros_like(ds_ref)

        def fwd(ci, carry):
            rows = pl.ds(pl.multiple_of(ci * L, L), L)
            s_all[ci] = ds_ref[...]
            ds_ref[...] = _ssd_chunk_state(x_ref[rows, :], b_ref[rows, :], a_ref[rows, :], ds_ref[...], tril)
            return carry

        lax.fori_loop(0, nc, fwd, 0, unroll=CHUNK_UNROLL)
        ds_ref[...] = jnp.zeros_like(ds_ref)

        def bwd(k, carry):
            ci = nc - 1 - k
            rows = pl.ds(pl.multiple_of(ci * L, L), L)
            _, vjp = jax.vjp(chunk, x_ref[rows, :], b_ref[rows, :], c_ref[rows, :], a_ref[rows, :], s_all[ci])
            dx, db, dc, da, ds = vjp((dy_ref[rows, :], ds_ref[...]))
            dx_ref[rows, :] = dx
            db_ref[rows, :] = db
            dc_ref[rows, :] = dc
            da_ref[rows, :] = da
            ds_ref[...] = ds
            return carry

        lax.fori_loop(0, nc, bwd, 0, unroll=CHUNK_UNROLL)

    return call_with_ride(
        body, name="ssd_scan_bwd", grid=(nb, cfg.ssd_groups), in_specs=specs + [specs[0]], args=args + [dy], out_specs=specs,
        out_shape=[jax.ShapeDtypeStruct(t.shape, F32) for t in (x, bm, cm, adt_b)],
        scratch_shapes=[pltpu.VMEM((nc, cfg.ssd_gw, cfg.ssd_state), F32), pltpu.VMEM((cfg.ssd_gw, cfg.ssd_state), F32)], ride=ride)


def _ret_chunk(q, k, v, g, norm_g, r_prev, cos, sin, valid, decay, zeta, xi, cdec, scale):
    half = q.shape[1] // 2

    def rot(t):
        t1, t2 = t[:, :half], t[:, half:]
        return jnp.concatenate([t1 * cos - t2 * sin, t1 * sin + t2 * cos], axis=1)

    qr = valid * rot(q)
    kr = valid * rot(k) * scale
    v = valid * v
    inner = _dot(_dot_nt(qr, kr) * decay, v)
    kv = _dot_tn(kr * zeta, v)
    cross = _dot(qr, r_prev) * xi
    o = inner + cross
    o = o - jnp.mean(o, axis=-1, keepdims=True)
    o = o * lax.rsqrt(jnp.mean(o * o, axis=-1, keepdims=True) + EPS)
    return _silu(g) * (o * norm_g), cdec * r_prev + kv


def _ret_consts(cfg):
    f32 = jnp.float32
    log_gamma = jnp.log1p(-jnp.exp2(-5.0 - jnp.arange(cfg.ret_heads, dtype=f32)))
    idx = jnp.arange(cfg.chunk, dtype=f32)
    diff = idx[:, None] - idx[None, :]
    decay = jnp.where(diff >= 0, jnp.exp(log_gamma[:, None, None] * jnp.maximum(diff, 0.0)), 0.0)
    zeta = jnp.exp(log_gamma[:, None] * (cfg.chunk - 1 - idx)[None, :])[..., None]
    xi = jnp.exp(log_gamma[:, None] * (idx + 1.0)[None, :])[..., None]
    cdec = jnp.exp(cfg.chunk * log_gamma)[:, None, None]
    half = cfg.ret_dim // 2
    inv_freq = 1.0 / (10000.0 ** (jnp.arange(half, dtype=f32) / (half - 1)))
    pos = jnp.arange(cfg.tp, dtype=f32) - cfg.pad
    ang = pos[:, None] * inv_freq[None, :]
    return dict(decay=decay, zeta=zeta, xi=xi, cdec=cdec, cos=jnp.cos(ang), sin=jnp.sin(ang))


def _ret_specs(cfg, u, col0, norm_g, consts):
    dk, nh, L = cfg.ret_dim, cfg.ret_heads, cfg.chunk
    base = col0 // dk
    seg = lambda s: pl.BlockSpec((cfg.tp, dk), functools.partial(lambda h, b, s: (b, base + s * nh + h), s=s))
    half = dk // 2
    specs = [seg(0), seg(1), seg(2), seg(3),
             pl.BlockSpec((1, dk), lambda h, b: (0, h)),
             pl.BlockSpec((cfg.tp, half), lambda h, b: (0, 0)),
             pl.BlockSpec((cfg.tp, half), lambda h, b: (0, 0)),
             pl.BlockSpec((None, L, L), lambda h, b: (h, 0, 0)),
             pl.BlockSpec((None, L, 1), lambda h, b: (h, 0, 0)),
             pl.BlockSpec((None, L, 1), lambda h, b: (h, 0, 0)),
             pl.BlockSpec((None, 1, 1), lambda h, b: (h, 0, 0))]
    args = [u, u, u, u, norm_g, consts["cos"], consts["sin"], consts["decay"], consts["zeta"], consts["xi"], consts["cdec"]]
    return specs, args


def _ret_chunk_at(cfg, refs, ci):
    L = cfg.chunk
    cos_ref, sin_ref, decay_ref, zeta_ref, xi_ref, cdec_ref = refs
    rows = pl.ds(pl.multiple_of(ci * L, L), L)
    fn = functools.partial(_ret_chunk, cos=cos_ref[rows, :], sin=sin_ref[rows, :], valid=_valid_chunk(cfg, ci),
                           decay=decay_ref[...], zeta=zeta_ref[...], xi=xi_ref[...], cdec=cdec_ref[...],
                           scale=cfg.ret_dim ** -0.5)
    return fn, rows


def ret_fwd(cfg, u, col0, norm_g, consts):
    nb = u.shape[0] // cfg.tp
    dk, nc = cfg.ret_dim, cfg.nc
    specs, args = _ret_specs(cfg, u, col0, norm_g, consts)

    def body(q_ref, k_ref, v_ref, g_ref, ng_ref, *rest):
        y_ref, r_ref = rest[-2], rest[-1]
        r_ref[...] = jnp.zeros_like(r_ref)

        def step(ci, carry):
            fn, rows = _ret_chunk_at(cfg, rest[:6], ci)
            y, r_new = fn(q_ref[rows, :], k_ref[rows, :], v_ref[rows, :], g_ref[rows, :], ng_ref[...], r_ref[...])
            y_ref[rows, :] = y.astype(y_ref.dtype)
            r_ref[...] = r_new
            return carry

        lax.fori_loop(0, nc, step, 0, unroll=CHUNK_UNROLL)

    return pl.pallas_call(
        body, grid=(cfg.ret_heads, nb), in_specs=specs, out_specs=pl.BlockSpec((cfg.tp, dk), lambda h, b: (b, h)),
        out_shape=jax.ShapeDtypeStruct((u.shape[0], cfg.ret_w), BF16), scratch_shapes=[pltpu.VMEM((dk, dk), F32)],
        name="ret_fwd", compiler_params=_cparams())(*args)


def ret_bwd(cfg, u, col0, norm_g, consts, dy, dy_col0):
    nb = u.shape[0] // cfg.tp
    dk, nc, nh = cfg.ret_dim, cfg.nc, cfg.ret_heads
    specs, args = _ret_specs(cfg, u, col0, norm_g, consts)
    dy_base = dy_col0 // dk
    specs.append(pl.BlockSpec((cfg.tp, dk), lambda h, b: (b, dy_base + h)))
    seg_out = lambda s: pl.BlockSpec((cfg.tp, dk), functools.partial(lambda h, b, s: (b, s * nh + h), s=s))

    def body(q_ref, k_ref, v_ref, g_ref, ng_ref, *rest):
        consts_refs, dy_ref = rest[:6], rest[6]
        dq_ref, dk_ref, dv_ref, dg_ref, dng_ref, r_all, dr_ref = rest[7:]
        dr_ref[...] = jnp.zeros_like(dr_ref)

        def fwd(ci, carry):
            fn, rows = _ret_chunk_at(cfg, consts_refs, ci)
            r_all[ci] = dr_ref[...]
            _, r_new = fn(q_ref[rows, :], k_ref[rows, :], v_ref[rows, :], g_ref[rows, :], ng_ref[...], dr_ref[...])
            dr_ref[...] = r_new
            return carry

        lax.fori_loop(0, nc, fwd, 0, unroll=CHUNK_UNROLL)
        dr_ref[...] = jnp.zeros_like(dr_ref)

        @pl.when(pl.program_id(1) == 0)
        def _():
            dng_ref[...] = jnp.zeros_like(dng_ref)

        def bwd(kk, carry):
            ci = nc - 1 - kk
            fn, rows = _ret_chunk_at(cfg, consts_refs, ci)
            _, vjp = jax.vjp(fn, q_ref[rows, :], k_ref[rows, :], v_ref[rows, :], g_ref[rows, :], ng_ref[...], r_all[ci])
            dq, dkk, dv, dg, dng, dr = vjp((dy_ref[rows, :].astype(F32), dr_ref[...]))
            dq_ref[rows, :] = dq.astype(dq_ref.dtype)
            dk_ref[rows, :] = dkk.astype(dk_ref.dtype)
            dv_ref[rows, :] = dv.astype(dv_ref.dtype)
            dg_ref[rows, :] = dg.astype(dg_ref.dtype)
            dng_ref[...] += dng
            dr_ref[...] = dr
            return carry

        lax.fori_loop(0, nc, bwd, 0, unroll=CHUNK_UNROLL)

    seg_shape = jax.ShapeDtypeStruct((u.shape[0], cfg.ret_w), BF16)
    res = pl.pallas_call(
        body, grid=(nh, nb), in_specs=specs,
        out_specs=[pl.BlockSpec((cfg.tp, dk), lambda h, b: (b, h))] * 4 + [pl.BlockSpec((1, dk), lambda h, b: (0, h))],
        out_shape=[seg_shape] * 4 + [jax.ShapeDtypeStruct((1, cfg.ret_w), F32)],
        scratch_shapes=[pltpu.VMEM((nc, dk, dk), F32), pltpu.VMEM((dk, dk), F32)],
        name="ret_bwd", compiler_params=_cparams(dimension_semantics=("arbitrary", "arbitrary")))(*args, dy)
    return res[:4], res[4]


def _dot2(a, tri):
    hi = a.astype(BF16)
    lo = (a - hi.astype(F32)).astype(BF16)
    return _dg(hi, tri, 1, 0) + _dg(lo, tri, 1, 0)


def _sb_group(cfg):
    return max(g for g in (4, 2, 1) if g <= cfg.nc)


def _sb_specs(cfg, u, col0):
    base = col0 // LANES
    per = cfg.sb_w // LANES
    seg = lambda s: pl.BlockSpec((cfg.tp, LANES), functools.partial(lambda b, p, s: (b, base + s * per + p), s=s))
    return [seg(0), seg(1), seg(2)], [u, u, u]


def _sb_scores(cfg, qh, ks, mask):
    z = _dg(qh, ks, 1, 1) * (cfg.sb_head_dim ** -0.5)
    sp = jnp.maximum(z, 0.0) + jnp.log(1.0 + jnp.exp(-jnp.abs(z)))
    return z, sp, jnp.where(mask, -sp, 0.0)


def _sb_running(x, tri, run, order):
    L = x.shape[0]
    parts = [None] * (x.shape[1] // L)
    for b in order:
        blk = x[:, b * L:(b + 1) * L]
        parts[b] = _dot2(blk, tri) + run
        run = run + jnp.sum(blk, axis=1, keepdims=True)
    return jnp.concatenate(parts, axis=1) if len(parts) > 1 else parts[0], run


def sb_fwd(cfg, u, col0, ride=None):
    nb = u.shape[0] // cfg.tp
    L, nc = cfg.chunk, cfg.nc
    hd = cfg.sb_head_dim
    G = _sb_group(cfg)
    specs, args = _sb_specs(cfg, u, col0)

    def body(q_ref, k_ref, v_ref, y_ref, tot_ref):
        tr, tc = _tri_consts(L)
        after_tri = (tr > tc).astype(BF16)
        rid = lax.broadcasted_iota(jnp.int32, (L, G * L), 0)
        cid = lax.broadcasted_iota(jnp.int32, (L, G * L), 1)
        lane = lax.broadcasted_iota(jnp.int32, (1, LANES), 1)
        mine = [jnp.logical_and(lane >= hh * hd, lane < (hh + 1) * hd) for hh in range(2)]

        def qloop(i, carry):
            qrows = pl.ds(pl.multiple_of(i * L, L), L)
            q = q_ref[qrows, :]
            qhs = [jnp.where(mine[hh], q, 0.0).astype(BF16) for hh in range(2)]

            def gloop(gg, c):
                accs, runs = c
                hi = i - gg * G
                start = jnp.maximum(hi - (G - 1), 0)
                krows = pl.ds(pl.multiple_of(start * L, L), G * L)
                ks = k_ref[krows, :].astype(BF16)
                vs = v_ref[krows, :].astype(BF16)
                kpos = start * L + cid
                mask = jnp.logical_and(jnp.logical_and(kpos < i * L + rid, kpos >= cfg.pad), kpos < (hi + 1) * L)
                new_accs, new_runs = [], []
                for hh in range(2):
                    z, sp, l1m = _sb_scores(cfg, qhs[hh], ks, mask)
                    after, run = _sb_running(l1m, after_tri, runs[hh], range(G - 1, -1, -1))
                    w = jnp.where(mask, jnp.exp(z - sp + after), 0.0)
                    new_accs.append(accs[hh] + _dg(w.astype(BF16), vs, 1, 0))
                    new_runs.append(run)
                return tuple(new_accs), tuple(new_runs)

            zero_acc, zero_run = jnp.zeros((L, LANES), F32), jnp.zeros((L, 1), F32)
            accs, runs = lax.fori_loop(0, (i + G) // G, gloop, ((zero_acc, zero_acc), (zero_run, zero_run)))
            for hh in range(2):
                tot_ref[0, hh, qrows, :] = runs[hh]
            y_ref[qrows, :] = jnp.where(lane < hd, accs[0], accs[1]).astype(y_ref.dtype)
            return carry

        lax.fori_loop(0, nc, qloop, 0)

    (y, tot), landed = call_with_ride(
        body, name="sb_fwd", grid=(nb, cfg.sb_w // LANES), in_specs=specs, args=args,
        out_specs=[pl.BlockSpec((cfg.tp, LANES), lambda b, p: (b, p)),
                   pl.BlockSpec((1, 2, cfg.tp, 1), lambda b, p: (b, p, 0, 0))],
        out_shape=[jax.ShapeDtypeStruct((u.shape[0], cfg.sb_w), BF16),
                   jax.ShapeDtypeStruct((nb, cfg.sb_heads, cfg.tp, 1), F32)], ride=ride)
    return y, tot, landed


def sb_bwd(cfg, u, col0, tot, dy, dy_col0, ride=None):
    nb = u.shape[0] // cfg.tp
    L, nc = cfg.chunk, cfg.nc
    hd = cfg.sb_head_dim
    scale = hd ** -0.5
    specs, args = _sb_specs(cfg, u, col0)
    dy_base = dy_col0 // LANES
    specs += [pl.BlockSpec((1, 2, cfg.tp, 1), lambda b, p: (b, p, 0, 0)),
              pl.BlockSpec((cfg.tp, LANES), lambda b, p: (b, dy_base + p))]

    G = _sb_group(cfg)

    def body(q_ref, k_ref, v_ref, tot_ref, dy_ref, dq_ref, dk_ref, dv_ref, dk_acc, dv_acc):
        tr, tc = _tri_consts(L)
        upto_tri = (tr <= tc).astype(BF16)
        before_tri = (tr < tc).astype(BF16)
        rid = lax.broadcasted_iota(jnp.int32, (L, G * L), 0)
        cid = lax.broadcasted_iota(jnp.int32, (L, G * L), 1)
        lane = lax.broadcasted_iota(jnp.int32, (1, LANES), 1)
        mine = [jnp.logical_and(lane >= hh * hd, lane < (hh + 1) * hd) for hh in range(2)]
        dk_acc[...] = jnp.zeros_like(dk_acc)
        dv_acc[...] = jnp.zeros_like(dv_acc)

        def qloop(i, carry):
            qrows = pl.ds(pl.multiple_of(i * L, L), L)
            q = q_ref[qrows, :]
            dy = dy_ref[qrows, :].astype(F32)
            qhs = [jnp.where(mine[hh], q, 0.0).astype(BF16) for hh in range(2)]
            dyhs = [jnp.where(mine[hh], dy, 0.0).astype(BF16) for hh in range(2)]
            tots = [tot_ref[0, hh, qrows, :] for hh in range(2)]

            def gloop(g, c):
                dq, run_ls, run_as = c
                lo = g * G
                start = jnp.minimum(lo, nc - G)
                krows = pl.ds(pl.multiple_of(start * L, L), G * L)
                k = k_ref[krows, :]
                ks = k.astype(BF16)
                vs = v_ref[krows, :].astype(BF16)
                kpos = start * L + cid
                mask = jnp.logical_and(jnp.logical_and(kpos < i * L + rid, kpos >= cfg.pad), kpos >= lo * L)
                new_ls, new_as = [], []
                dv_add = jnp.zeros((G * L, LANES), F32)
                dk_add = jnp.zeros((G * L, LANES), F32)
                for hh in range(2):
                    z, sp, l1m = _sb_scores(cfg, qhs[hh], ks, mask)
                    upto, run_l = _sb_running(l1m, upto_tri, run_ls[hh], range(G))
                    w = jnp.where(mask, jnp.exp(z - sp + (tots[hh] - upto)), 0.0)
                    da = w * _dg(dyhs[hh], vs, 1, 1)
                    d_l1m, run_a = _sb_running(da, before_tri, run_as[hh], range(G))
                    sg = jnp.exp(z - sp)
                    dz = (jnp.where(mask, da * (1.0 - sg) - d_l1m * sg, 0.0) * scale).astype(BF16)
                    dv_add = dv_add + _dg(w.astype(BF16), dyhs[hh], 0, 0)
                    dk_add = dk_add + _dg(dz, qhs[hh], 0, 0)
                    dq = dq + _dg(dz, jnp.where(mine[hh], k, 0.0).astype(BF16), 1, 0)
                    new_ls.append(run_l)
                    new_as.append(run_a)
                dv_acc[krows, :] += dv_add
                dk_acc[krows, :] += dk_add
                return dq, tuple(new_ls), tuple(new_as)

            zero = jnp.zeros((L, 1), F32)
            dq, _, _ = lax.fori_loop(0, (i + G) // G, gloop, (jnp.zeros((L, LANES), F32), (zero, zero), (zero, zero)))
            dq_ref[qrows, :] = dq.astype(dq_ref.dtype)
            return carry

        lax.fori_loop(0, nc, qloop, 0)
        dk_ref[...] = dk_acc[...].astype(dk_ref.dtype)
        dv_ref[...] = dv_acc[...].astype(dv_ref.dtype)

    out_spec = pl.BlockSpec((cfg.tp, LANES), lambda b, p: (b, p))
    seg_shape = jax.ShapeDtypeStruct((u.shape[0], cfg.sb_w), BF16)
    return call_with_ride(
        body, name="sb_bwd", grid=(nb, cfg.sb_w // LANES), in_specs=specs, args=args + [tot, dy], out_specs=[out_spec] * 3,
        out_shape=[seg_shape] * 3, scratch_shapes=[pltpu.VMEM((cfg.tp, LANES), F32), pltpu.VMEM((cfg.tp, LANES), F32)], ride=ride)


def _lru_cols(cfg):
    return _tile(cfg.lru_width, 256)


def lru_scan_fwd(cfg, a, b):
    nb = a.shape[0] // cfg.tp
    cw = _lru_cols(cfg)
    spec = pl.BlockSpec((cfg.tp, cw), lambda s, c: (s, c))

    def body(a_ref, b_ref, h_ref):
        def step(i, h):
            rows = pl.ds(pl.multiple_of(i * SUBLANES, SUBLANES), SUBLANES)
            at, bt = a_ref[rows, :], b_ref[rows, :]
            outs = []
            for r in range(SUBLANES):
                h = at[r:r + 1, :] * h + bt[r:r + 1, :]
                outs.append(h)
            h_ref[rows, :] = jnp.concatenate(outs, axis=0)
            return h

        lax.fori_loop(0, cfg.tp // SUBLANES, step, jnp.zeros((1, cw), F32))

    return pl.pallas_call(
        body, grid=(nb, cfg.lru_width // cw), in_specs=[spec, spec], out_specs=spec,
        out_shape=jax.ShapeDtypeStruct(a.shape, F32), name="lru_scan_fwd", compiler_params=_cparams())(a, b)


def lru_scan_bwd(cfg, a, h, dh):
    nb = a.shape[0] // cfg.tp
    cw = _lru_cols(cfg)
    nt = cfg.tp // SUBLANES
    spec = pl.BlockSpec((cfg.tp, cw), lambda s, c: (s, c))

    def body(a_ref, h_ref, dh_ref, da_ref, db_ref):
        def step(k, c):
            i = nt - 1 - k
            rows = pl.ds(pl.multiple_of(i * SUBLANES, SUBLANES), SUBLANES)
            prev = pl.ds(pl.multiple_of(jnp.maximum(i - 1, 0) * SUBLANES, SUBLANES), SUBLANES)
            at, ht, dht = a_ref[rows, :], h_ref[rows, :], dh_ref[rows, :]
            h_before = jnp.where(i > 0, h_ref[prev, :][SUBLANES - 1:SUBLANES, :], 0.0)
            das, dbs = [None] * SUBLANES, [None] * SUBLANES
            for r in range(SUBLANES - 1, -1, -1):
                g = dht[r:r + 1, :] + c
                dbs[r] = g
                das[r] = g * (ht[r - 1:r, :] if r > 0 else h_before)
                c = at[r:r + 1, :] * g
            da_ref[rows, :] = jnp.concatenate(das, axis=0)
            db_ref[rows, :] = jnp.concatenate(dbs, axis=0)
            return c

        lax.fori_loop(0, nt, step, jnp.zeros((1, cw), F32))

    return pl.pallas_call(
        body, grid=(nb, cfg.lru_width // cw), in_specs=[spec] * 3, out_specs=[spec] * 2,
        out_shape=[jax.ShapeDtypeStruct(a.shape, F32)] * 2, name="lru_scan_bwd", compiler_params=_cparams())(a, h, dh)


def final_loss(cfg, h, target, norm_g):
    assert cfg.seq % cfg.chunk == 0 and cfg.n_meta + cfg.pad == cfg.chunk
    L, nc, d = cfg.chunk, cfg.nc, cfg.d_model
    per_seq = cfg.seq // L

    def tgt_map(i):
        return ((i // nc) * per_seq + jnp.maximum(i % nc - 1, 0), 0)

    def body(h_ref, t_ref, g_ref, loss_ref, dh_ref, dg_ref):
        i = pl.program_id(0)
        real = jnp.where(i % nc == 0, 0.0, 1.0)
        tgt = t_ref[...]

        def loss_fn(hv, g):
            y = hv * lax.rsqrt(jnp.mean(hv * hv, axis=-1, keepdims=True) + EPS) * g
            return 0.5 * real * jnp.sum(jnp.mean(jnp.square(y - tgt), axis=-1))

        val, (dh, dg) = jax.value_and_grad(loss_fn, argnums=(0, 1))(h_ref[...], g_ref[...])

        @pl.when(i == 0)
        def _():
            loss_ref[...] = jnp.zeros_like(loss_ref)
            dg_ref[...] = jnp.zeros_like(dg_ref)

        loss_ref[...] += jnp.broadcast_to(val, loss_ref.shape)
        dg_ref[...] += dg
        dh_ref[...] = dh

    return pl.pallas_call(
        body, grid=(h.shape[0] // L,),
        in_specs=[pl.BlockSpec((L, d), lambda i: (i, 0)), pl.BlockSpec((L, d), tgt_map), pl.BlockSpec((1, d), lambda i: (0, 0))],
        out_specs=[pl.BlockSpec((SUBLANES, LANES), lambda i: (0, 0)), pl.BlockSpec((L, d), lambda i: (i, 0)),
                   pl.BlockSpec((1, d), lambda i: (0, 0))],
        out_shape=[jax.ShapeDtypeStruct((SUBLANES, LANES), F32), jax.ShapeDtypeStruct(h.shape, F32),
                   jax.ShapeDtypeStruct((1, d), F32)],
        name="final_loss", compiler_params=_cparams(dimension_semantics=("arbitrary",)))(h, target, norm_g)


N_PEER = 7


class Ride(NamedTuple):
    arrays: list
    same: list


def _ride_copies(ride_in, ride_out, same, sems, sending):
    send_sems, recv_sems, local_sems = sems
    n = len(ride_in)
    x, y, c = lax.axis_index("x"), lax.axis_index("y"), lax.axis_index("c")
    me = 4 * x + 2 * y + c

    def slab(w, dest):
        return ride_in[w] if same[w] else ride_in[w].at[dest]

    local = [pltpu.make_async_copy(slab(w, me), ride_out[w].at[me], local_sems.at[w]) for w in range(n)]
    remote = []
    for r in range(1, N_PEER + 1):
        peer = (1 - x if r & 4 else x, 1 - y if r & 2 else y, 1 - c if r & 1 else c)
        pidx = 4 * peer[0] + 2 * peer[1] + peer[2]
        for w in range(n):
            remote.append(pltpu.make_async_remote_copy(
                src_ref=slab(w, pidx), dst_ref=ride_out[w].at[me if sending else pidx], send_sem=send_sems.at[w, r - 1],
                recv_sem=recv_sems.at[w, r - 1], device_id=peer, device_id_type=pl.DeviceIdType.MESH))
    return local, remote


def _ride_start(ride_in, ride_out, same, sems):
    local, remote = _ride_copies(ride_in, ride_out, same, sems, True)
    for cp in local + remote:
        cp.start()


def _ride_wait(ride_in, ride_out, same, sems):
    local, remote = _ride_copies(ride_in, ride_out, same, sems, False)
    for cp in remote:
        cp.wait_recv()
    for cp in remote:
        cp.wait_send()
    for cp in local:
        cp.wait()


def call_with_ride(body, *, name, grid, in_specs, args, out_specs, out_shape, scratch_shapes=(), ride=None, **cparams):
    out_specs, out_shape, scratch_shapes = list(out_specs), list(out_shape), list(scratch_shapes)
    if ride is None:
        res = pl.pallas_call(body, grid=grid, in_specs=list(in_specs), out_specs=out_specs, out_shape=out_shape,
                             scratch_shapes=scratch_shapes, name=name, compiler_params=_cparams(**cparams))(*args)
        return list(res), []
    arrays, same = ride
    n, n_in, n_out, n_scr = len(arrays), len(args), len(out_shape), len(scratch_shapes)
    hbm = pl.BlockSpec(memory_space=pl.ANY)
    land_shape = [jax.ShapeDtypeStruct((8,) + (a.shape if s else a.shape[1:]), a.dtype) for a, s in zip(arrays, same)]

    def wrapped(*refs):
        ins, ride_in = refs[:n_in], refs[n_in:n_in + n]
        outs, ride_out = refs[n_in + n:n_in + n + n_out], refs[n_in + n + n_out:n_in + 2 * n + n_out]
        scr, sems = refs[n_in + 2 * n + n_out:n_in + 2 * n + n_out + n_scr], refs[n_in + 2 * n + n_out + n_scr:]
        first, last = True, True
        for axis, size in enumerate(grid):
            first = jnp.logical_and(first, pl.program_id(axis) == 0)
            last = jnp.logical_and(last, pl.program_id(axis) == size - 1)
        if grid:
            pl.when(first)(lambda: _ride_start(ride_in, ride_out, same, sems))
        else:
            _ride_start(ride_in, ride_out, same, sems)
        body(*ins, *outs, *scr)
        if grid:
            pl.when(last)(lambda: _ride_wait(ride_in, ride_out, same, sems))
        else:
            _ride_wait(ride_in, ride_out, same, sems)

    sems = [pltpu.SemaphoreType.DMA((n, N_PEER)), pltpu.SemaphoreType.DMA((n, N_PEER)), pltpu.SemaphoreType.DMA((n,))]
    cparams["dimension_semantics"] = ("arbitrary",) * len(grid)
    res = pl.pallas_call(wrapped, grid=grid, in_specs=list(in_specs) + [hbm] * n, out_specs=out_specs + [hbm] * n,
                         out_shape=out_shape + land_shape, scratch_shapes=scratch_shapes + sems, name=name,
                         compiler_params=_cparams(**cparams))(*args, *arrays)
    return list(res[:n_out]), list(res[n_out:])


def exchange(name, ride):
    return call_with_ride(lambda: None, name=name, grid=(), in_specs=[], args=[], out_specs=[], out_shape=[], ride=ride)[1]


def _row_tile(r, mult, cap):
    best = None
    for t in range(mult, min(r, cap) + 1, mult):
        if r % t == 0:
            best = t
    return best if best is not None else r


def adamw(name, parts, w, m, v):
    r, c = w.shape
    tr = _row_tile(r, 16, 256)
    spec = pl.BlockSpec((tr, c), lambda i: (i, 0))

    def body(p_ref, w_ref, m_ref, v_ref, g_ref, d_ref, m2_ref, v2_ref):
        g = p_ref[0].astype(F32)
        for dev in range(1, 8):
            g = g + p_ref[dev].astype(F32)
        m2 = ADAM_B1 * m_ref[...] + (1.0 - ADAM_B1) * g
        v2 = ADAM_B2 * v_ref[...] + (1.0 - ADAM_B2) * jnp.square(g)
        m_hat = m2 / (1.0 - ADAM_B1 ** ADAM_STEP)
        v_hat = v2 / (1.0 - ADAM_B2 ** ADAM_STEP)
        g_ref[...] = g
        d_ref[...] = -ADAM_LR * (m_hat / (jnp.sqrt(v_hat) + ADAM_EPS) + ADAM_WD * w_ref[...])
        m2_ref[...] = m2
        v2_ref[...] = v2

    return pl.pallas_call(
        body, grid=(r // tr,), in_specs=[pl.BlockSpec((8, tr, c), lambda i: (0, i, 0)), spec, spec, spec],
        out_specs=[spec] * 4, out_shape=[jax.ShapeDtypeStruct((r, c), F32)] * 4, name=name, compiler_params=_cparams())(parts, w, m, v)


PACK_ROWS = 256


def _pack(arrs):
    flat = jnp.concatenate([a.reshape(-1).astype(F32) for a in arrs])
    quantum = PACK_ROWS * LANES
    total = -(-flat.shape[0] // quantum) * quantum
    return jnp.pad(flat, (0, total - flat.shape[0])).reshape(-1, LANES)


def _unpack(packed, shapes):
    flat = packed.reshape(-1)
    out, off = [], 0
    for s in shapes:
        n = int(np.prod(s))
        out.append(flat[off:off + n].reshape(s))
        off += n
    return out


def _pad_lanes(vec):
    return jnp.pad(vec.astype(F32), (0, LANES - vec.shape[0]))[None, :]


def make_step(cfg):
    I, BC, H, RW, SW, LW, F, D = (cfg.ssd_inner, cfg.ssd_bc, cfg.ssd_heads, cfg.ret_w, cfg.sb_w, cfg.lru_width,
                                  cfg.ffn_dim, cfg.d_model)
    ND = cfg.n_dev
    q_off = 2 * I + 2 * BC
    dt_off = q_off + 4 * RW
    dt_w = (-dt_off) % 1024 or 1024

    def gather_cols(g):
        return jnp.transpose(g, (1, 0, 2)).reshape(g.shape[1], -1)

    def scatter_cols(full):
        r, c = full.shape
        return jnp.transpose(full.reshape(r, ND, c // ND), (1, 0, 2))

    def step(p, m_in, v_in, x, loss_target):
        nb = x.shape[0]
        rows = nb * cfg.tp

        small_shapes = [p[n].shape for n in SMALL_SHARDED]
        wfull = {}

        def gather_ride(names, extra=()):
            own = [p[n].astype(BF16) for n in names] + list(extra)
            return Ride(own, [True] * len(own))

        def gather_finish(names, landed):
            for n, g in zip(names, landed):
                wfull[n] = gather_cols(g) if n in BIG_COL else g.reshape(-1, g.shape[-1])

        first_names = ['l0_w_in', 'l0_w_out']
        landed = exchange("gather_l0_mixer", gather_ride(first_names, [_pack([p[n] for n in SMALL_SHARDED])]))
        gather_finish(first_names, landed)
        small_parts = [_unpack(landed[-1][d], small_shapes) for d in range(ND)]
        sfull = {n: jnp.concatenate([small_parts[d][k] for d in range(ND)], axis=1) for k, n in enumerate(SMALL_SHARDED)}

        w0 = wfull['l0_w_in']
        w0a = jnp.concatenate([w0[:, :q_off], w0[:, q_off + H:], jnp.pad(w0[:, q_off:q_off + H], ((0, 0), (0, dt_w - H)))], axis=1)

        sel_hd = (jnp.arange(LANES)[:, None] == (jnp.arange(I)[None, :] // cfg.ssd_head_dim)).astype(F32)
        sel_lane = (jnp.arange(LANES)[:, None] == (jnp.arange(H * LANES)[None, :] // LANES)).astype(F32)
        rconst = _ret_consts(cfg)
        row2 = lambda vec: vec.astype(F32)[None, :]

        meta = jnp.broadcast_to(sfull['meta_tokens'][None], (nb, cfg.n_meta, D))
        h0 = jnp.concatenate([jnp.zeros((nb, cfg.pad, D), F32), meta, x], axis=1).reshape(rows, D)

        def norm_fwd(name, h, g):
            return stage_fwd(name, cfg, _norm_fn, [Row(h, D)], [Par(row2(g))], [(D, BF16)])[0]

        def norm_bwd(name, h, g, dhn, dh_next):
            fn = lambda valid, hv, gv: (_norm_fn(valid, hv, gv)[0], hv)
            (dh,), (dg,) = stage_bwd(name, cfg, fn, [Row(h, D)], [Par(row2(g))], [[Row(dhn, D)], [Row(dh_next, D)]], [F32])
            return dh, dg[0]

        hn0 = norm_fwd("l0_mix_norm", h0, p['l0_mix_norm'])
        u0 = matmul("l0_in_proj", hn0, w0a)
        cw, cb = sfull['l0_ssd_conv_w'], p['l0_ssd_conv_b']
        ssd_pre = _make_ssd_pre(cfg)
        ssd_post = _make_ssd_post(cfg)
        pre_rows = [Row(u0, I, 1, tail=True), Row(u0, BC, (2 * I) // BC, tail=True), Row(u0, BC, (2 * I) // BC + 1, tail=True),
                    Row(u0, LANES, dt_off // LANES)]
        pre_pars = [Par(cw[:, :I]), Par(cw[:, I:I + BC]), Par(cw[:, I + BC:]), Par(row2(cb[:I])), Par(row2(cb[I:I + BC])),
                    Par(row2(cb[I + BC:])), Par(_pad_lanes(p['l0_ssd_dt_bias'])), Par(_pad_lanes(p['l0_ssd_a_log'])),
                    Par(sel_hd, diff=False), Par(sel_lane, diff=False)]
        xs, bm, cm, xdt, adt_b = stage_fwd("ssd_pre", cfg, ssd_pre, pre_rows, pre_pars,
                                           [(I, F32), (BC, F32), (BC, F32), (I, F32), (H * LANES, F32)])
        mid_names = ['l0_ffn_w_in', 'l0_ffn_w_out', 'l1_w_in']
        y_raw, landed = ssd_scan_fwd(cfg, xdt, bm, cm, adt_b, ride=gather_ride(mid_names))
        gather_finish(mid_names, landed)
        post_rows = [Row(y_raw, I), Row(xs, I), Row(u0, I, 0)]
        post_pars = [Par(_pad_lanes(p['l0_ssd_d'])), Par(row2(p['l0_ssd_norm'])), Par(sel_hd, diff=False)]
        (y_ssd,) = stage_fwd("ssd_post", cfg, ssd_post, post_rows, post_pars, [(I, BF16)])
        ret_g = row2(p['l0_ret_norm'])
        y_ret = ret_fwd(cfg, u0, q_off, ret_g, rconst)
        ycat0 = jnp.concatenate([y_ssd, y_ret], axis=1)
        h1 = matmul("l0_out_proj", ycat0, wfull['l0_w_out'], resid=h0, cfg=cfg)

        def ffn_fwd(tag, h, norm_g, w_in, conv_w, conv_b, w_out):
            hn = norm_fwd(tag + "_ffn_norm", h, norm_g)
            u = matmul(tag + "_ffn_in", hn, w_in)
            rws = [Row(u, F, 0, tail=True), Row(u, F, 1, tail=True)]
            prs = [Par(conv_w[:, :F]), Par(conv_w[:, F:]), Par(row2(conv_b[:F])), Par(row2(conv_b[F:]))]
            (act,) = stage_fwd(tag + "_ffn_act", cfg, _ffn_act_fn, rws, prs, [(F, BF16)])
            h_out = matmul(tag + "_ffn_out", act, w_out, resid=h, cfg=cfg)
            return h_out, (hn, u, rws, prs, act)

        h2, ffn0_saved = ffn_fwd("l0", h1, p['l0_ffn_norm'], wfull['l0_ffn_w_in'], sfull['l0_ffn_conv_w'], p['l0_ffn_conv_b'],
                                 wfull['l0_ffn_w_out'])

        hn2 = norm_fwd("l1_mix_norm", h2, p['l1_mix_norm'])
        u1 = matmul("l1_in_proj", hn2, wfull['l1_w_in'])
        late_names = ['l1_w_out', 'l1_ffn_w_in', 'l1_ffn_w_out']
        y_sb, sb_tot, landed = sb_fwd(cfg, u1, 0, ride=gather_ride(late_names))
        gather_finish(late_names, landed)
        lru_pre = _make_lru_pre(cfg)
        gate_blk = (3 * SW) // LW
        lpre_rows = [Row(u1, LW, gate_blk + 1, tail=True)]
        lpre_pars = [Par(sfull['l1_lru_conv_w']), Par(row2(p['l1_lru_conv_b'])), Par(p['l1_lru_wa']), Par(row2(p['l1_lru_ba'])),
                     Par(p['l1_lru_wx']), Par(row2(p['l1_lru_bx'])), Par(row2(p['l1_lru_lambda']))]
        lru_a, lru_b = stage_fwd("lru_pre", cfg, lru_pre, lpre_rows, lpre_pars, [(LW, F32), (LW, F32)])
        lru_h = lru_scan_fwd(cfg, lru_a, lru_b)
        lpost_rows = [Row(lru_h, LW), Row(u1, LW, gate_blk)]
        (y_lru,) = stage_fwd("lru_post", cfg, _lru_post_fn, lpost_rows, [], [(LW, BF16)])
        ycat1 = jnp.concatenate([y_sb, y_lru], axis=1)
        h3 = matmul("l1_out_proj", ycat1, wfull['l1_w_out'], resid=h2, cfg=cfg)
        h4, ffn1_saved = ffn_fwd("l1", h3, p['l1_ffn_norm'], wfull['l1_ffn_w_in'], sfull['l1_ffn_conv_w'], p['l1_ffn_conv_b'],
                                 wfull['l1_ffn_w_out'])

        loss_part, dh4, d_final = final_loss(cfg, h4, loss_target.reshape(nb * cfg.seq, D), row2(p['final_norm']))
        loss = lax.psum(loss_part[0, 0], ("x", "y", "c"))
        gfull, grep = {}, {'final_norm': d_final[0]}

        def ffn_bwd(tag, h, norm_g, w_in, w_out, saved, dh_out):
            hn, u, rws, prs, act = saved
            dact = matmul(tag + "_ffn_out_dx", dh_out, w_out, tb=True, out_dtype=BF16)
            gfull[tag + '_ffn_w_out'] = matmul(tag + "_ffn_out_dw", act, dh_out, ta=True, out_dtype=BF16, tm_cap=512, tn_cap=512, tk_cap=4352)
            (dug, duu), (dwg, dwu, dbg, dbu) = stage_bwd(tag + "_ffn_act_bwd", cfg, _ffn_act_fn, rws, prs, [[Row(dact, F)]], [BF16, BF16])
            du = jnp.concatenate([dug, duu], axis=1)
            gfull[tag + '_ffn_conv_w'] = jnp.concatenate([dwg, dwu], axis=1)
            grep[tag + '_ffn_conv_b'] = jnp.concatenate([dbg, dbu], axis=1)[0]
            dhn = matmul(tag + "_ffn_in_dx", du, w_in, tb=True)
            gfull[tag + '_ffn_w_in'] = matmul(tag + "_ffn_in_dw", hn, du, ta=True, out_dtype=BF16, tm_cap=512, tn_cap=512, tk_cap=4352)
            dh, grep[tag + '_ffn_norm'] = norm_bwd(tag + "_ffn_norm_bwd", h, norm_g, dhn, dh_out)
            return dh

        parts_of = {}

        def grads_ride(names, extra=(), extra_same=()):
            sends = [scatter_cols(gfull[n]) if n in BIG_COL else gfull[n].reshape(ND, -1, gfull[n].shape[-1]) for n in names]
            return Ride(sends + list(extra), [False] * len(names) + list(extra_same))

        dh3 = ffn_bwd("l1", h3, p['l1_ffn_norm'], wfull['l1_ffn_w_in'], wfull['l1_ffn_w_out'], ffn1_saved, dh4)

        dycat1 = matmul("l1_out_dx", dh3, wfull['l1_w_out'], tb=True, out_dtype=BF16)
        gfull['l1_w_out'] = matmul("l1_out_dw", ycat1, dh3, ta=True, out_dtype=BF16, tm_cap=512, tn_cap=512, tk_cap=4352)
        (d_lru_h, d_gate), _ = stage_bwd("lru_post_bwd", cfg, _lru_post_fn, lpost_rows, [], [[Row(dycat1, LW, SW // LW)]], [F32, BF16])
        d_lru_a, d_lru_b = lru_scan_bwd(cfg, lru_a, lru_h, d_lru_h)
        (d_xr,), lgr = stage_bwd("lru_pre_bwd", cfg, lru_pre, lpre_rows, lpre_pars, [[Row(d_lru_a, LW)], [Row(d_lru_b, LW)]], [BF16])
        gfull['l1_lru_conv_w'] = lgr[0]
        grep.update({'l1_lru_conv_b': lgr[1][0], 'l1_lru_wa': lgr[2], 'l1_lru_ba': lgr[3][0], 'l1_lru_wx': lgr[4],
                     'l1_lru_bx': lgr[5][0], 'l1_lru_lambda': lgr[6][0]})
        names = ['l1_ffn_w_out', 'l1_ffn_w_in', 'l1_w_out']
        early = _pack([grep[n] for n in REP_EARLY])
        (d_q, d_k, d_v), landed = sb_bwd(cfg, u1, 0, sb_tot, dycat1, 0, ride=grads_ride(names, [early], [True]))
        parts_of.update(zip(names + ['rep_early'], landed))
        du1 = jnp.concatenate([d_q, d_k, d_v, d_gate, d_xr], axis=1)
        dhn2 = matmul("l1_in_dx", du1, wfull['l1_w_in'], tb=True)
        gfull['l1_w_in'] = matmul("l1_in_dw", hn2, du1, ta=True, out_dtype=BF16, tm_cap=512, tn_cap=512, tk_cap=4352)
        dh2, grep['l1_mix_norm'] = norm_bwd("l1_mix_norm_bwd", h2, p['l1_mix_norm'], dhn2, dh3)

        dh1 = ffn_bwd("l0", h1, p['l0_ffn_norm'], wfull['l0_ffn_w_in'], wfull['l0_ffn_w_out'], ffn0_saved, dh2)

        dycat0 = matmul("l0_out_dx", dh1, wfull['l0_w_out'], tb=True, out_dtype=BF16)
        gfull['l0_w_out'] = matmul("l0_out_dw", ycat0, dh1, ta=True, out_dtype=BF16, tm_cap=512, tn_cap=512, tk_cap=4352)
        (d_yraw, d_xs1, d_z), (d_dskip, d_ssdnorm) = stage_bwd("ssd_post_bwd", cfg, ssd_post, post_rows, post_pars,
                                                               [[Row(dycat0, I, 0)]], [F32, F32, BF16])
        names = ['l1_w_in', 'l0_ffn_w_out', 'l0_ffn_w_in', 'l0_w_out']
        (d_xdt, d_bm, d_cm, d_adt), landed = ssd_scan_bwd(cfg, xdt, bm, cm, adt_b, d_yraw, ride=grads_ride(names))
        parts_of.update(zip(names, landed))
        (d_xs, d_b, d_c, d_dt), sgr = stage_bwd("ssd_pre_bwd", cfg, ssd_pre, pre_rows, pre_pars,
                                                 [[Row(d_xs1, I)], [Row(d_bm, BC)], [Row(d_cm, BC)], [Row(d_xdt, I)], [Row(d_adt, H * LANES)]],
                                                 [BF16] * 4)
        gfull['l0_ssd_conv_w'] = jnp.concatenate(sgr[0:3], axis=1)
        grep.update({'l0_ssd_conv_b': jnp.concatenate(sgr[3:6], axis=1)[0], 'l0_ssd_dt_bias': sgr[6][0, :H],
                     'l0_ssd_a_log': sgr[7][0, :H], 'l0_ssd_d': d_dskip[0, :H], 'l0_ssd_norm': d_ssdnorm[0]})
        (d_rq, d_rk, d_rv, d_rg), d_retnorm = ret_bwd(cfg, u0, q_off, ret_g, rconst, dycat0, I)
        grep['l0_ret_norm'] = d_retnorm[0]
        du0 = jnp.concatenate([d_z, d_xs, d_b, d_c, d_rq, d_rk, d_rv, d_rg, d_dt, jnp.zeros((rows, dt_w - LANES), BF16)], axis=1)
        dw0a = matmul("l0_in_dw", hn0, du0, ta=True, out_dtype=BF16, tm_cap=512, tn_cap=512, tk_cap=4352)
        gfull['l0_w_in'] = jnp.concatenate([dw0a[:, :q_off], dw0a[:, dt_off:dt_off + H], dw0a[:, q_off:dt_off]], axis=1)
        dhn0, landed = matmul("l0_in_dx", du0, w0a, tb=True, ride=grads_ride(['l0_w_in']))
        parts_of['l0_w_in'] = landed[0]
        dh0, grep['l0_mix_norm'] = norm_bwd("l0_mix_norm_bwd", h0, p['l0_mix_norm'], dhn0, dh1)

        dh0 = dh0.reshape(nb, cfg.tp, D)
        grad_x = dh0[:, cfg.chunk:, :]
        gfull['meta_tokens'] = jnp.sum(dh0[:, cfg.pad:cfg.chunk, :], axis=0)

        small_send = jnp.stack([_pack([scatter_cols(gfull[n])[d] for n in SMALL_SHARDED]) for d in range(ND)])
        late = _pack([grep[n] for n in REP_LATE])
        landed = exchange("exchange_last_grads", Ride([small_send, late], [False, True]))
        parts_of.update(zip(['small', 'rep_late'], landed))

        grad, delta, new_m, new_v = {}, {}, {}, {}
        for n in BIG:
            grad[n], delta[n], new_m[n], new_v[n] = adamw("adamw_" + n, parts_of[n], p[n], m_in[n], v_in[n])
        for names, tag in ((SMALL_SHARDED, 'small'), (REP_EARLY, 'rep_early'), (REP_LATE, 'rep_late')):
            shapes = [p[n].shape for n in names]
            res = adamw("adamw_" + tag, parts_of[tag], _pack([p[n] for n in names]), _pack([m_in[n] for n in names]),
                        _pack([v_in[n] for n in names]))
            for dst, packed in zip((grad, delta, new_m, new_v), res):
                for n, a in zip(names, _unpack(packed, shapes)):
                    dst[n] = a
        return (loss, grad_x, *[grad[n] for n in WEIGHTS], *[delta[n] for n in WEIGHTS], *[new_m[n] for n in WEIGHTS],
                *[new_v[n] for n in WEIGHTS])

    return step


_STEP = make_step(CFG)


def kernel(x, meta_tokens, l0_mix_norm, l0_w_in, l0_ssd_conv_w, l0_ssd_conv_b, l0_ssd_dt_bias, l0_ssd_a_log, l0_ssd_d, l0_ssd_norm, l0_ret_norm, l0_w_out, l0_ffn_norm, l0_ffn_w_in, l0_ffn_conv_w, l0_ffn_conv_b, l0_ffn_w_out, l1_mix_norm, l1_w_in, l1_lru_conv_w, l1_lru_conv_b, l1_lru_wa, l1_lru_ba, l1_lru_wx, l1_lru_bx, l1_lru_lambda, l1_w_out, l1_ffn_norm, l1_ffn_w_in, l1_ffn_conv_w, l1_ffn_conv_b, l1_ffn_w_out, final_norm, loss_target, m_meta_tokens, m_l0_mix_norm, m_l0_w_in, m_l0_ssd_conv_w, m_l0_ssd_conv_b, m_l0_ssd_dt_bias, m_l0_ssd_a_log, m_l0_ssd_d, m_l0_ssd_norm, m_l0_ret_norm, m_l0_w_out, m_l0_ffn_norm, m_l0_ffn_w_in, m_l0_ffn_conv_w, m_l0_ffn_conv_b, m_l0_ffn_w_out, m_l1_mix_norm, m_l1_w_in, m_l1_lru_conv_w, m_l1_lru_conv_b, m_l1_lru_wa, m_l1_lru_ba, m_l1_lru_wx, m_l1_lru_bx, m_l1_lru_lambda, m_l1_w_out, m_l1_ffn_norm, m_l1_ffn_w_in, m_l1_ffn_conv_w, m_l1_ffn_conv_b, m_l1_ffn_w_out, m_final_norm, v_meta_tokens, v_l0_mix_norm, v_l0_w_in, v_l0_ssd_conv_w, v_l0_ssd_conv_b, v_l0_ssd_dt_bias, v_l0_ssd_a_log, v_l0_ssd_d, v_l0_ssd_norm, v_l0_ret_norm, v_l0_w_out, v_l0_ffn_norm, v_l0_ffn_w_in, v_l0_ffn_conv_w, v_l0_ffn_conv_b, v_l0_ffn_w_out, v_l1_mix_norm, v_l1_w_in, v_l1_lru_conv_w, v_l1_lru_conv_b, v_l1_lru_wa, v_l1_lru_ba, v_l1_lru_wx, v_l1_lru_bx, v_l1_lru_lambda, v_l1_w_out, v_l1_ffn_norm, v_l1_ffn_w_in, v_l1_ffn_conv_w, v_l1_ffn_conv_b, v_l1_ffn_w_out, v_final_norm):
    args = locals()
    p = {n: args[n] for n in WEIGHTS}
    m_in = {n: args["m_" + n] for n in WEIGHTS}
    v_in = {n: args["v_" + n] for n in WEIGHTS}
    return _STEP(p, m_in, v_in, x, loss_target)
```

```python
import functools
from typing import NamedTuple

import numpy as np
import jax
import jax.numpy as jnp
from jax import lax
from jax.experimental import pallas as pl
from jax.experimental.pallas import tpu as pltpu

F32 = jnp.float32
BF16 = jnp.bfloat16
EPS = 1e-6
LRU_C = 8.0
NEG = -1e30
SUBLANES = 8
LANES = 128
VMEM_LIMIT = 56 * 1024 * 1024
CHUNK_UNROLL = 4

ADAM_LR = 0.001
ADAM_B1 = 0.9
ADAM_B2 = 0.999
ADAM_EPS = 1e-08
ADAM_WD = 0.01
ADAM_STEP = 10

WEIGHTS = ['meta_tokens', 'l0_mix_norm', 'l0_w_in', 'l0_ssd_conv_w', 'l0_ssd_conv_b', 'l0_ssd_dt_bias', 'l0_ssd_a_log',
           'l0_ssd_d', 'l0_ssd_norm', 'l0_ret_norm', 'l0_w_out', 'l0_ffn_norm', 'l0_ffn_w_in', 'l0_ffn_conv_w',
           'l0_ffn_conv_b', 'l0_ffn_w_out', 'l1_mix_norm', 'l1_w_in', 'l1_lru_conv_w', 'l1_lru_conv_b', 'l1_lru_wa',
           'l1_lru_ba', 'l1_lru_wx', 'l1_lru_bx', 'l1_lru_lambda', 'l1_w_out', 'l1_ffn_norm', 'l1_ffn_w_in',
           'l1_ffn_conv_w', 'l1_ffn_conv_b', 'l1_ffn_w_out', 'final_norm']
BIG_COL = ['l0_w_in', 'l0_ffn_w_in', 'l1_w_in', 'l1_ffn_w_in']
BIG_ROW = ['l0_w_out', 'l0_ffn_w_out', 'l1_w_out', 'l1_ffn_w_out']
BIG = BIG_COL + BIG_ROW
SMALL_SHARDED = ['meta_tokens', 'l0_ssd_conv_w', 'l0_ffn_conv_w', 'l1_lru_conv_w', 'l1_ffn_conv_w']
REPLICATED = [w for w in WEIGHTS if w not in BIG and w not in SMALL_SHARDED]
REP_EARLY = ['final_norm', 'l1_ffn_norm', 'l1_ffn_conv_b', 'l1_lru_conv_b', 'l1_lru_wa', 'l1_lru_ba', 'l1_lru_wx',
             'l1_lru_bx', 'l1_lru_lambda']
REP_LATE = [w for w in REPLICATED if w not in REP_EARLY]


class Config(NamedTuple):
    d_model: int = 1024
    seq: int = 2048
    n_meta: int = 16
    chunk: int = 128
    ssd_heads: int = 16
    ssd_head_dim: int = 64
    ssd_groups: int = 4
    ssd_state: int = 128
    ret_heads: int = 4
    ret_dim: int = 256
    sb_heads: int = 16
    sb_head_dim: int = 64
    lru_width: int = 1024
    lru_blocks: int = 8
    ffn_dim: int = 2816
    n_dev: int = 8

    @property
    def t(self):
        return self.n_meta + self.seq

    @property
    def pad(self):
        return (-self.t) % self.chunk

    @property
    def tp(self):
        return self.t + self.pad

    @property
    def nc(self):
        return self.tp // self.chunk

    @property
    def ssd_inner(self):
        return self.ssd_heads * self.ssd_head_dim

    @property
    def ssd_gw(self):
        return self.ssd_inner // self.ssd_groups

    @property
    def ssd_bc(self):
        return self.ssd_groups * self.ssd_state

    @property
    def ret_w(self):
        return self.ret_heads * self.ret_dim

    @property
    def sb_w(self):
        return self.sb_heads * self.sb_head_dim

    @property
    def mix0_segs(self):
        return (self.ssd_inner, self.ssd_inner + 2 * self.ssd_bc, self.ssd_heads, self.ret_w, self.ret_w, self.ret_w, self.ret_w)


CFG = Config()


def _dg(a, b, ca, cb):
    return lax.dot_general(a, b, (((ca,), (cb,)), ((), ())), preferred_element_type=F32)


def _cot_a(b, ca, cb, g):
    return _dg(g, b, 1, 1 - cb) if ca == 1 else _dg(b, g, 1 - cb, 1)


def _cot_b(a, ca, cb, g):
    return _dg(a, g, 1 - ca, 0) if cb == 0 else _dg(g, a, 0, 1 - ca)


@functools.partial(jax.custom_vjp, nondiff_argnums=(2, 3))
def _bdot(a, b, ca, cb):
    return _dg(a.astype(BF16), b.astype(BF16), ca, cb)


def _bdot_fwd(a, b, ca, cb):
    return _bdot(a, b, ca, cb), (a, b)


def _bdot_bwd(ca, cb, res, g):
    a, b = res
    gb = g.astype(BF16)
    return _cot_a(b.astype(BF16), ca, cb, gb).astype(a.dtype), _cot_b(a.astype(BF16), ca, cb, gb).astype(b.dtype)


_bdot.defvjp(_bdot_fwd, _bdot_bwd)


def _dot(a, b):
    return _bdot(a, b, 1, 0)


def _dot_nt(a, b):
    return _bdot(a, b, 1, 1)


def _dot_tn(a, b):
    return _bdot(a, b, 0, 0)


def _split3(a):
    hi = a.astype(BF16)
    r1 = a - hi.astype(F32)
    mid = r1.astype(BF16)
    return hi, mid, (r1 - mid.astype(F32)).astype(BF16)


@functools.partial(jax.custom_vjp, nondiff_argnums=(2, 3, 4))
def _edot(a, b, ca, cb, const):
    if const == 'b':
        bb = b.astype(BF16)
        return sum(_dg(p, bb, ca, cb) for p in _split3(a))
    ab = a.astype(BF16)
    return sum(_dg(ab, p, ca, cb) for p in _split3(b))


def _edot_fwd(a, b, ca, cb, const):
    return _edot(a, b, ca, cb, const), (a, b)


def _edot_bwd(ca, cb, const, res, g):
    a, b = res
    if const == 'b':
        bb = b.astype(BF16)
        return sum(_cot_a(bb, ca, cb, p) for p in _split3(g)), jnp.zeros_like(b)
    ab = a.astype(BF16)
    return jnp.zeros_like(a), sum(_cot_b(ab, ca, cb, p) for p in _split3(g))


_edot.defvjp(_edot_fwd, _edot_bwd)


def _dot_exact01(a, sel):
    return _edot(a, sel, 1, 0, 'b')


def _silu(x):
    return x * jax.nn.sigmoid(x)


def _softplus(x):
    return jnp.maximum(x, 0.0) + jnp.log1p(jnp.exp(-jnp.abs(x)))


def _neg_expm1(x):
    series = -x * (1.0 + x * 0.5 * (1.0 + x / 3.0 * (1.0 + x * 0.25)))
    return jnp.where(x > -0.01, series, 1.0 - jnp.exp(x))


def _causal_conv(x, tail, w, b):
    taps, rows = w.shape[0], x.shape[0]
    xx = jnp.concatenate([tail, x], axis=0)
    y = b
    for k in range(taps):
        off = SUBLANES - (taps - 1 - k)
        y = y + w[k:k + 1, :] * xx[off:off + rows, :]
    return y


def _valid_chunk(cfg, chunk_idx):
    first = (chunk_idx % cfg.nc) == 0
    rid = lax.broadcasted_iota(jnp.int32, (cfg.chunk, 1), 0)
    return jnp.where(jnp.logical_and(first, rid < cfg.pad), 0.0, 1.0).astype(F32)


def _cparams(**kw):
    return pltpu.CompilerParams(vmem_limit_bytes=VMEM_LIMIT, **kw)


class Row(NamedTuple):
    arr: jax.Array
    width: int
    blk: int = 0
    tail: bool = False
    diff: bool = True


class Par(NamedTuple):
    arr: jax.Array
    diff: bool = True


def _row_specs(cfg, rows, order):
    tr = cfg.chunk
    specs, args = [], []
    for r in rows:
        specs.append(pl.BlockSpec((tr, r.width), functools.partial(lambda i, b, o: (o(i), b), b=r.blk, o=order)))
        args.append(r.arr)
        if r.tail:
            per = tr // SUBLANES
            specs.append(pl.BlockSpec((SUBLANES, r.width),
                                      functools.partial(lambda i, b, o: (jnp.maximum(o(i) * per - 1, 0), b), b=r.blk, o=order)))
            args.append(r.arr)
    return specs, args


def _par_specs(pars):
    specs = [pl.BlockSpec(p.arr.shape, functools.partial(lambda i, nd: (0,) * nd, nd=p.arr.ndim)) for p in pars]
    return specs, [p.arr for p in pars]


def stage_fwd(name, cfg, fn, rows, pars, outs):
    tr = cfg.chunk
    n_rows = rows[0].arr.shape[0]
    rspecs, rargs = _row_specs(cfg, rows, lambda i: i)
    pspecs, pargs = _par_specs(pars)
    n_in = len(rargs) + len(pargs)

    def body(*refs):
        valid = _valid_chunk(cfg, pl.program_id(0))
        vals = [r[...].astype(F32) for r in refs[:n_in]]
        res = fn(valid, *vals)
        for o, v in zip(refs[n_in:], res):
            o[...] = v.astype(o.dtype)

    return pl.pallas_call(
        body, grid=(n_rows // tr,), in_specs=rspecs + pspecs,
        out_specs=[pl.BlockSpec((tr, w), lambda i: (i, 0)) for w, _ in outs],
        out_shape=[jax.ShapeDtypeStruct((n_rows, w), dt) for w, dt in outs],
        name=name, compiler_params=_cparams())(*rargs, *pargs)


def stage_bwd(name, cfg, fn, rows, pars, douts, drow_dtypes):
    tr = cfg.chunk
    n_rows = rows[0].arr.shape[0]
    n_blk = n_rows // tr
    order = lambda i: n_blk - 1 - i
    rspecs, rargs = _row_specs(cfg, rows, order)
    pspecs, pargs = _par_specs(pars)
    pieces = [p for out in douts for p in out]
    dspecs, dargs = _row_specs(cfg, pieces, order)
    n_r, n_p, n_d = len(rargs), len(pargs), len(dargs)
    diff_rows = [r for r in rows if r.diff]
    diff_pars = [p for p in pars if p.diff]
    tails = [r for r in diff_rows if r.tail]

    def body(*refs):
        in_refs = refs[:n_r + n_p]
        d_refs = refs[n_r + n_p:n_r + n_p + n_d]
        o_refs = refs[n_r + n_p + n_d:]
        drow_refs = o_refs[:len(diff_rows)]
        dpar_refs = o_refs[len(diff_rows):len(diff_rows) + len(diff_pars)]
        carry_refs = o_refs[len(diff_rows) + len(diff_pars):]
        step = pl.program_id(0)
        valid = _valid_chunk(cfg, order(step))
        vals = [r[...].astype(F32) for r in in_refs]
        slots, pos = [], 0
        for r in rows:
            if r.diff:
                slots.append(pos)
                if r.tail:
                    slots.append(pos + 1)
            pos += 2 if r.tail else 1
        for p in pars:
            if p.diff:
                slots.append(pos)
            pos += 1

        def g(*dv):
            full = list(vals)
            for s, v in zip(slots, dv):
                full[s] = v
            return tuple(fn(valid, *full))

        _, vjp = jax.vjp(g, *[vals[s] for s in slots])
        cts, k = [], 0
        for out in douts:
            parts = [d_refs[k + j][...].astype(F32) for j in range(len(out))]
            k += len(out)
            cts.append(parts[0] if len(parts) == 1 else jnp.concatenate(parts, axis=1))
        grads = list(vjp(tuple(cts)))

        @pl.when(step == 0)
        def _():
            for c in carry_refs:
                c[...] = jnp.zeros_like(c)
            for d in dpar_refs:
                d[...] = jnp.zeros_like(d)

        gi, ci = 0, 0
        for r, o in zip(diff_rows, drow_refs):
            dx = grads[gi]
            gi += 1
            if r.tail:
                dtail = grads[gi]
                gi += 1
                c = carry_refs[ci]
                ci += 1
                dx = dx + jnp.concatenate([jnp.zeros((tr - SUBLANES, r.width), F32), c[...]], axis=0)
                c[...] = dtail
            o[...] = dx.astype(o.dtype)
        for d in dpar_refs:
            d[...] += grads[gi]
            gi += 1

    out_specs = [pl.BlockSpec((tr, r.width), lambda i: (order(i), 0)) for r in diff_rows]
    out_shape = [jax.ShapeDtypeStruct((n_rows, r.width), dt) for r, dt in zip(diff_rows, drow_dtypes)]
    for p in diff_pars:
        out_specs.append(pl.BlockSpec(p.arr.shape, functools.partial(lambda i, nd: (0,) * nd, nd=p.arr.ndim)))
        out_shape.append(jax.ShapeDtypeStruct(p.arr.shape, F32))
    res = pl.pallas_call(
        body, grid=(n_blk,), in_specs=rspecs + pspecs + dspecs, out_specs=out_specs, out_shape=out_shape,
        scratch_shapes=[pltpu.VMEM((SUBLANES, r.width), F32) for r in tails],
        name=name, compiler_params=_cparams(dimension_semantics=("arbitrary",)))(*rargs, *pargs, *dargs)
    return list(res[:len(diff_rows)]), list(res[len(diff_rows):])


def _tile(n, cap):
    if n % LANES:
        return n
    q = n // LANES
    best = 1
    for k in range(1, q + 1):
        if q % k == 0 and k * LANES <= cap:
            best = k
    return best * LANES


def _mm_vmem(tm, tn, tk, a_bytes, b_bytes, out_bytes, resid, nk):
    total = 2 * (tm * tk * a_bytes + tk * tn * b_bytes + tm * tn * out_bytes) + tm * tn * 4
    total += 2 * tm * tn * 4 if resid else 0
    total += (tm * tk * 2 if a_bytes == 4 else 0) + (tk * tn * 2 if b_bytes == 4 else 0)
    return total + (tm * tn * 4 if nk > 1 else 0)


def matmul(name, a, b, *, ta=False, tb=False, out_dtype=F32, resid=None, cfg=None, tm_cap=2176, tn_cap=1024, tk_cap=2816,
           ride=None):
    m, k = (a.shape[1], a.shape[0]) if ta else a.shape
    n = b.shape[0] if tb else b.shape[1]
    tm, tn, tk = _tile(m, tm_cap), _tile(n, tn_cap), _tile(k, tk_cap)
    budget = VMEM_LIMIT - 8 * 1024 * 1024
    sizes = (a.dtype.itemsize, b.dtype.itemsize, jnp.dtype(out_dtype).itemsize, resid is not None)
    while _mm_vmem(tm, tn, tk, *sizes, k // tk) > budget:
        if tm > 256:
            tm = _tile(m, tm - 1)
        elif tk > 512:
            tk = _tile(k, tk - 1)
        else:
            tn = _tile(n, tn - 1)
    nk = k // tk
    a_spec = pl.BlockSpec((tk, tm), lambda i, j, l: (l, i)) if ta else pl.BlockSpec((tm, tk), lambda i, j, l: (i, l))
    b_spec = pl.BlockSpec((tn, tk), lambda i, j, l: (j, l)) if tb else pl.BlockSpec((tk, tn), lambda i, j, l: (l, j))
    dims = (((0 if ta else 1,), (1 if tb else 0,)), ((), ()))
    in_specs, args = [a_spec, b_spec], [a, b]
    if resid is not None:
        in_specs.append(pl.BlockSpec((tm, tn), lambda i, j, l: (i, j)))
        args.append(resid)

    def body(*refs):
        a_ref, b_ref = refs[0], refs[1]
        o_ref = refs[3] if resid is not None else refs[2]
        part = lax.dot_general(a_ref[...].astype(BF16), b_ref[...].astype(BF16), dims, preferred_element_type=F32)

        def finish(res):
            if resid is not None:
                row = pl.program_id(0) * tm + lax.broadcasted_iota(jnp.int32, (tm, 1), 0)
                inpad = jnp.zeros((tm, 1), jnp.bool_)
                for s in range(m // cfg.tp):
                    inpad = jnp.logical_or(inpad, jnp.logical_and(row >= s * cfg.tp, row < s * cfg.tp + cfg.pad))
                res = refs[2][...] + jnp.where(inpad, 0.0, res)
            o_ref[...] = res.astype(o_ref.dtype)

        if nk == 1:
            finish(part)
            return
        acc = refs[-1]
        l = pl.program_id(2)

        @pl.when(l == 0)
        def _():
            acc[...] = part

        @pl.when(jnp.logical_and(l > 0, l < nk - 1))
        def _():
            acc[...] += part

        @pl.when(l == nk - 1)
        def _():
            finish(acc[...] + part)

    outs, landed = call_with_ride(
        body, name=name, grid=(m // tm, n // tn, nk), in_specs=in_specs, args=args,
        out_specs=[pl.BlockSpec((tm, tn), lambda i, j, l: (i, j))], out_shape=[jax.ShapeDtypeStruct((m, n), out_dtype)],
        scratch_shapes=[pltpu.VMEM((tm, tn), F32)] if nk > 1 else [], ride=ride,
        dimension_semantics=("parallel", "parallel", "arbitrary"))
    return (outs[0], landed) if ride is not None else outs[0]


def _norm_fn(valid, h, g):
    y = h * lax.rsqrt(jnp.mean(h * h, axis=-1, keepdims=True) + EPS)
    return (valid * (y * g),)


def _make_ssd_pre(cfg):
    nh = cfg.ssd_heads

    def fn(valid, xs, xs_t, bm, bm_t, cm, cm_t, dtr, w_xs, w_b, w_c, b_xs, b_b, b_c, dt_bias, a_log, sel_hd, sel_lane):
        xs = valid * _silu(_causal_conv(xs, xs_t, w_xs, b_xs))
        bm = valid * _silu(_causal_conv(bm, bm_t, w_b, b_b))
        cm = valid * _silu(_causal_conv(cm, cm_t, w_c, b_c))
        lane = lax.broadcasted_iota(jnp.int32, (1, LANES), 1)
        head = jnp.where(lane < nh, 1.0, 0.0).astype(F32)
        dt = valid * head * _softplus(dtr + dt_bias)
        adt = dt * (-jnp.exp(a_log))
        x = xs * _dot_exact01(dt, sel_hd)
        adt_b = _dot_exact01(adt, sel_lane)
        return xs, bm, cm, x, adt_b

    return fn


def _make_ssd_post(cfg):
    gw = cfg.ssd_gw

    def fn(valid, y_raw, xs, z, d_skip, norm_g, sel_hd):
        d_rep = _dot_exact01(jnp.broadcast_to(d_skip, (SUBLANES, LANES)), sel_hd)[0:1]
        y = (y_raw + xs * d_rep) * _silu(z)
        parts = []
        for g in range(cfg.ssd_groups):
            yg = y[:, g * gw:(g + 1) * gw]
            parts.append(yg * lax.rsqrt(jnp.mean(yg * yg, axis=-1, keepdims=True) + EPS))
        return (jnp.concatenate(parts, axis=1) * norm_g,)

    return fn


def _ffn_act_fn(valid, ug, ug_t, uu, uu_t, w_g, w_u, b_g, b_u):
    return (valid * _silu(_causal_conv(ug, ug_t, w_g, b_g)) * _causal_conv(uu, uu_t, w_u, b_u),)


def _make_lru_pre(cfg):
    nb = cfg.lru_blocks
    bw = cfg.lru_width // nb

    def fn(valid, xr, xr_t, w_conv, b_conv, wa, ba, wx, bx, lam):
        xc = _causal_conv(xr, xr_t, w_conv, b_conv)
        a_parts, b_parts = [], []
        for n in range(nb):
            sl = slice(n * bw, (n + 1) * bw)
            xn = xc[:, sl]
            r = jax.nn.sigmoid(_dot(xn, wa[n]) + ba[:, sl])
            i = jax.nn.sigmoid(_dot(xn, wx[n]) + bx[:, sl])
            log_a = -LRU_C * r * _softplus(-lam[:, sl])
            a_parts.append(jnp.exp(log_a))
            b_parts.append(valid * jnp.sqrt(jnp.maximum(_neg_expm1(2.0 * log_a), 0.0)) * (i * xn))
        return jnp.concatenate(a_parts, axis=1), jnp.concatenate(b_parts, axis=1)

    return fn


def _lru_post_fn(valid, hs, gate):
    return (hs * jax.nn.gelu(gate),)


def _tri_consts(n):
    r = lax.broadcasted_iota(jnp.int32, (n, n), 0)
    c = lax.broadcasted_iota(jnp.int32, (n, n), 1)
    return r, c


def _ssd_chunk(x, bm, cm, a, s, tril, lower):
    rows = x.shape[0]
    heads = a.shape[1] // LANES
    p = x.shape[1] // heads
    cb = _dot_nt(cm, bm)
    ys, ss = [], []
    for e in range(heads):
        ae = a[:, e * LANES:(e + 1) * LANES]
        cs = _edot(tril, ae, 1, 0, 'a')
        lmat = jnp.exp(jnp.where(lower, cs - cs.T, NEG))
        xe = x[:, e * p:(e + 1) * p]
        se = s[e * p:(e + 1) * p, :]
        tot = cs[rows - 1:rows, :]
        y_diag = _dot(cb * lmat, xe)
        st = _dot_tn(xe * jnp.exp(tot - cs)[:, :p], bm)
        y_off = _dot_nt(cm, se) * jnp.exp(cs)[:, :p]
        ys.append(y_diag + y_off)
        ss.append(jnp.exp(tot[:, :1]) * se + st)
    return jnp.concatenate(ys, axis=1), jnp.concatenate(ss, axis=0)


def _ssd_chunk_state(x, bm, a, s, tril):
    rows = x.shape[0]
    heads = a.shape[1] // LANES
    p = x.shape[1] // heads
    ss = []
    for e in range(heads):
        cs = _edot(tril, a[:, e * LANES:(e + 1) * LANES], 1, 0, 'a')
        tot = cs[rows - 1:rows, :]
        st = _dot_tn(x[:, e * p:(e + 1) * p] * jnp.exp(tot - cs)[:, :p], bm)
        ss.append(jnp.exp(tot[:, :1]) * s[e * p:(e + 1) * p, :] + st)
    return jnp.concatenate(ss, axis=0)


def _ssd_specs(cfg, x, bm, cm, adt_b):
    gw, ns = cfg.ssd_gw, cfg.ssd_state
    hpg = cfg.ssd_heads // cfg.ssd_groups
    specs = [pl.BlockSpec((cfg.tp, gw), lambda b, g: (b, g)),
             pl.BlockSpec((cfg.tp, ns), lambda b, g: (b, g)),
             pl.BlockSpec((cfg.tp, ns), lambda b, g: (b, g)),
             pl.BlockSpec((cfg.tp, hpg * LANES), lambda b, g: (b, g))]
    return specs, [x, bm, cm, adt_b]


def ssd_scan_fwd(cfg, x, bm, cm, adt_b, ride=None):
    nb = x.shape[0] // cfg.tp
    L, nc = cfg.chunk, cfg.nc
    specs, args = _ssd_specs(cfg, x, bm, cm, adt_b)

    def body(x_ref, b_ref, c_ref, a_ref, y_ref, s_ref):
        r, c = _tri_consts(L)
        lower = r >= c
        tril = lower.astype(F32)
        s_ref[...] = jnp.zeros_like(s_ref)

        def step(ci, carry):
            rows = pl.ds(pl.multiple_of(ci * L, L), L)
            y, s_new = _ssd_chunk(x_ref[rows, :], b_ref[rows, :], c_ref[rows, :], a_ref[rows, :], s_ref[...], tril, lower)
            y_ref[rows, :] = y
            s_ref[...] = s_new
            return carry

        lax.fori_loop(0, nc, step, 0, unroll=CHUNK_UNROLL)

    outs, landed = call_with_ride(
        body, name="ssd_scan_fwd", grid=(nb, cfg.ssd_groups), in_specs=specs, args=args, out_specs=[specs[0]],
        out_shape=[jax.ShapeDtypeStruct(x.shape, F32)], scratch_shapes=[pltpu.VMEM((cfg.ssd_gw, cfg.ssd_state), F32)], ride=ride)
    return outs[0], landed


def ssd_scan_bwd(cfg, x, bm, cm, adt_b, dy, ride=None):
    nb = x.shape[0] // cfg.tp
    L, nc = cfg.chunk, cfg.nc
    specs, args = _ssd_specs(cfg, x, bm, cm, adt_b)

    def body(x_ref, b_ref, c_ref, a_ref, dy_ref, dx_ref, db_ref, dc_ref, da_ref, s_all, ds_ref):
        r, c = _tri_consts(L)
        lower = r >= c
        tril = lower.astype(F32)
        chunk = functools.partial(_ssd_chunk, tril=tril, lower=lower)
        ds_ref[...] = jnp.zeros_like(ds_ref)

        def fwd(ci, carry):
            rows = pl.ds(pl.multiple_of(ci * L, L), L)
            s_all[ci] = ds_ref[...]
            ds_ref[...] = _ssd_chunk_state(x_ref[rows, :], b_ref[rows, :], a_ref[rows, :], ds_ref[...], tril)
            return carry

        lax.fori_loop(0, nc, fwd, 0, unroll=CHUNK_UNROLL)
        ds_ref[...] = jnp.zeros_like(ds_ref)

        def bwd(k, carry):
            ci = nc - 1 - k
            rows = pl.ds(pl.multiple_of(ci * L, L), L)
            _, vjp = jax.vjp(chunk, x_ref[rows, :], b_ref[rows, :], c_ref[rows, :], a_ref[rows, :], s_all[ci])
            dx, db, dc, da, ds = vjp((dy_ref[rows, :], ds_ref[...]))
            dx_ref[rows, :] = dx
            db_ref[rows, :] = db
            dc_ref[rows, :] = dc
            da_ref[rows, :] = da
            ds_ref[...] = ds
            return carry

        lax.fori_loop(0, nc, bwd, 0, unroll=CHUNK_UNROLL)

    return call_with_ride(
        body, name="ssd_scan_bwd", grid=(nb, cfg.ssd_groups), in_specs=specs + [specs[0]], args=args + [dy], out_specs=specs,
        out_shape=[jax.ShapeDtypeStruct(t.shape, F32) for t in (x, bm, cm, adt_b)],
        scratch_shapes=[pltpu.VMEM((nc, cfg.ssd_gw, cfg.ssd_state), F32), pltpu.VMEM((cfg.ssd_gw, cfg.ssd_state), F32)], ride=ride)


def _ret_chunk(q, k, v, g, norm_g, r_prev, cos, sin, valid, decay, zeta, xi, cdec, scale):
    half = q.shape[1] // 2

    def rot(t):
        t1, t2 = t[:, :half], t[:, half:]
        return jnp.concatenate([t1 * cos - t2 * sin, t1 * sin + t2 * cos], axis=1)

    qr = valid * rot(q)
    kr = valid * rot(k) * scale
    v = valid * v
    inner = _dot(_dot_nt(qr, kr) * decay, v)
    kv = _dot_tn(kr * zeta, v)
    cross = _dot(qr, r_prev) * xi
    o = inner + cross
    o = o - jnp.mean(o, axis=-1, keepdims=True)
    o = o * lax.rsqrt(jnp.mean(o * o, axis=-1, keepdims=True) + EPS)
    return _silu(g) * (o * norm_g), cdec * r_prev + kv


def _ret_consts(cfg):
    f32 = jnp.float32
    log_gamma = jnp.log1p(-jnp.exp2(-5.0 - jnp.arange(cfg.ret_heads, dtype=f32)))
    idx = jnp.arange(cfg.chunk, dtype=f32)
    diff = idx[:, None] - idx[None, :]
    decay = jnp.where(diff >= 0, jnp.exp(log_gamma[:, None, None] * jnp.maximum(diff, 0.0)), 0.0)
    zeta = jnp.exp(log_gamma[:, None] * (cfg.chunk - 1 - idx)[None, :])[..., None]
    xi = jnp.exp(log_gamma[:, None] * (idx + 1.0)[None, :])[..., None]
    cdec = jnp.exp(cfg.chunk * log_gamma)[:, None, None]
    half = cfg.ret_dim // 2
    inv_freq = 1.0 / (10000.0 ** (jnp.arange(half, dtype=f32) / (half - 1)))
    pos = jnp.arange(cfg.tp, dtype=f32) - cfg.pad
    ang = pos[:, None] * inv_freq[None, :]
    return dict(decay=decay, zeta=zeta, xi=xi, cdec=cdec, cos=jnp.cos(ang), sin=jnp.sin(ang))


def _ret_specs(cfg, u, col0, norm_g, consts):
    dk, nh, L = cfg.ret_dim, cfg.ret_heads, cfg.chunk
    base = col0 // dk
    seg = lambda s: pl.BlockSpec((cfg.tp, dk), functools.partial(lambda h, b, s: (b, base + s * nh + h), s=s))
    half = dk // 2
    specs = [seg(0), seg(1), seg(2), seg(3),
             pl.BlockSpec((1, dk), lambda h, b: (0, h)),
             pl.BlockSpec((cfg.tp, half), lambda h, b: (0, 0)),
             pl.BlockSpec((cfg.tp, half), lambda h, b: (0, 0)),
             pl.BlockSpec((None, L, L), lambda h, b: (h, 0, 0)),
             pl.BlockSpec((None, L, 1), lambda h, b: (h, 0, 0)),
             pl.BlockSpec((None, L, 1), lambda h, b: (h, 0, 0)),
             pl.BlockSpec((None, 1, 1), lambda h, b: (h, 0, 0))]
    args = [u, u, u, u, norm_g, consts["cos"], consts["sin"], consts["decay"], consts["zeta"], consts["xi"], consts["cdec"]]
    return specs, args


def _ret_chunk_at(cfg, refs, ci):
    L = cfg.chunk
    cos_ref, sin_ref, decay_ref, zeta_ref, xi_ref, cdec_ref = refs
    rows = pl.ds(pl.multiple_of(ci * L, L), L)
    fn = functools.partial(_ret_chunk, cos=cos_ref[rows, :], sin=sin_ref[rows, :], valid=_valid_chunk(cfg, ci),
                           decay=decay_ref[...], zeta=zeta_ref[...], xi=xi_ref[...], cdec=cdec_ref[...],
                           scale=cfg.ret_dim ** -0.5)
    return fn, rows


def ret_fwd(cfg, u, col0, norm_g, consts, ride=None):
    nb = u.shape[0] // cfg.tp
    dk, nc = cfg.ret_dim, cfg.nc
    specs, args = _ret_specs(cfg, u, col0, norm_g, consts)

    def body(q_ref, k_ref, v_ref, g_ref, ng_ref, *rest):
        y_ref, r_ref = rest[-2], rest[-1]
        r_ref[...] = jnp.zeros_like(r_ref)

        def step(ci, carry):
            fn, rows = _ret_chunk_at(cfg, rest[:6], ci)
            y, r_new = fn(q_ref[rows, :], k_ref[rows, :], v_ref[rows, :], g_ref[rows, :], ng_ref[...], r_ref[...])
            y_ref[rows, :] = y.astype(y_ref.dtype)
            r_ref[...] = r_new
            return carry

        lax.fori_loop(0, nc, step, 0, unroll=CHUNK_UNROLL)

    outs, landed = call_with_ride(
        body, name="ret_fwd", grid=(cfg.ret_heads, nb), in_specs=specs, args=args,
        out_specs=[pl.BlockSpec((cfg.tp, dk), lambda h, b: (b, h))],
        out_shape=[jax.ShapeDtypeStruct((u.shape[0], cfg.ret_w), BF16)], scratch_shapes=[pltpu.VMEM((dk, dk), F32)], ride=ride)
    return outs[0], landed


def ret_bwd(cfg, u, col0, norm_g, consts, dy, dy_col0):
    nb = u.shape[0] // cfg.tp
    dk, nc, nh = cfg.ret_dim, cfg.nc, cfg.ret_heads
    specs, args = _ret_specs(cfg, u, col0, norm_g, consts)
    dy_base = dy_col0 // dk
    specs.append(pl.BlockSpec((cfg.tp, dk), lambda h, b: (b, dy_base + h)))
    seg_out = lambda s: pl.BlockSpec((cfg.tp, dk), functools.partial(lambda h, b, s: (b, s * nh + h), s=s))

    def body(q_ref, k_ref, v_ref, g_ref, ng_ref, *rest):
        consts_refs, dy_ref = rest[:6], rest[6]
        dq_ref, dk_ref, dv_ref, dg_ref, dng_ref, r_all, dr_ref = rest[7:]
        dr_ref[...] = jnp.zeros_like(dr_ref)

        def fwd(ci, carry):
            fn, rows = _ret_chunk_at(cfg, consts_refs, ci)
            r_all[ci] = dr_ref[...]
            _, r_new = fn(q_ref[rows, :], k_ref[rows, :], v_ref[rows, :], g_ref[rows, :], ng_ref[...], dr_ref[...])
            dr_ref[...] = r_new
            return carry

        lax.fori_loop(0, nc, fwd, 0, unroll=CHUNK_UNROLL)
        dr_ref[...] = jnp.zeros_like(dr_ref)

        @pl.when(pl.program_id(1) == 0)
        def _():
            dng_ref[...] = jnp.zeros_like(dng_ref)

        def bwd(kk, carry):
            ci = nc - 1 - kk
            fn, rows = _ret_chunk_at(cfg, consts_refs, ci)
            _, vjp = jax.vjp(fn, q_ref[rows, :], k_ref[rows, :], v_ref[rows, :], g_ref[rows, :], ng_ref[...], r_all[ci])
            dq, dkk, dv, dg, dng, dr = vjp((dy_ref[rows, :].astype(F32), dr_ref[...]))
            dq_ref[rows, :] = dq.astype(dq_ref.dtype)
            dk_ref[rows, :] = dkk.astype(dk_ref.dtype)
            dv_ref[rows, :] = dv.astype(dv_ref.dtype)
            dg_ref[rows, :] = dg.astype(dg_ref.dtype)
            dng_ref[...] += dng
            dr_ref[...] = dr
            return carry

        lax.fori_loop(0, nc, bwd, 0, unroll=CHUNK_UNROLL)

    seg_shape = jax.ShapeDtypeStruct((u.shape[0], cfg.ret_w), BF16)
    res = pl.pallas_call(
        body, grid=(nh, nb), in_specs=specs,
        out_specs=[pl.BlockSpec((cfg.tp, dk), lambda h, b: (b, h))] * 4 + [pl.BlockSpec((1, dk), lambda h, b: (0, h))],
        out_shape=[seg_shape] * 4 + [jax.ShapeDtypeStruct((1, cfg.ret_w), F32)],
        scratch_shapes=[pltpu.VMEM((nc, dk, dk), F32), pltpu.VMEM((dk, dk), F32)],
        name="ret_bwd", compiler_params=_cparams(dimension_semantics=("arbitrary", "arbitrary")))(*args, dy)
    return res[:4], res[4]


def _dot2(a, tri):
    hi = a.astype(BF16)
    lo = (a - hi.astype(F32)).astype(BF16)
    return _dg(hi, tri, 1, 0) + _dg(lo, tri, 1, 0)


def _sb_group(cfg):
    return max(g for g in (4, 2, 1) if g <= cfg.nc)


def _sb_specs(cfg, u, col0):
    base = col0 // LANES
    per = cfg.sb_w // LANES
    seg = lambda s: pl.BlockSpec((cfg.tp, LANES), functools.partial(lambda b, p, s: (b, base + s * per + p), s=s))
    return [seg(0), seg(1), seg(2)], [u, u, u]


def _sb_scores(cfg, qh, ks, mask):
    z = _dg(qh, ks, 1, 1) * (cfg.sb_head_dim ** -0.5)
    sp = jnp.maximum(z, 0.0) + jnp.log(1.0 + jnp.exp(-jnp.abs(z)))
    return z, sp, jnp.where(mask, -sp, 0.0)


def _sb_running(x, tri, run, order):
    L = x.shape[0]
    parts = [None] * (x.shape[1] // L)
    for b in order:
        blk = x[:, b * L:(b + 1) * L]
        parts[b] = _dot2(blk, tri) + run
        run = run + jnp.sum(blk, axis=1, keepdims=True)
    return jnp.concatenate(parts, axis=1) if len(parts) > 1 else parts[0], run


def sb_fwd(cfg, u, col0, ride=None):
    nb = u.shape[0] // cfg.tp
    L, nc = cfg.chunk, cfg.nc
    hd = cfg.sb_head_dim
    G = _sb_group(cfg)
    specs, args = _sb_specs(cfg, u, col0)

    def body(q_ref, k_ref, v_ref, y_ref, tot_ref):
        tr, tc = _tri_consts(L)
        after_tri = (tr > tc).astype(BF16)
        rid = lax.broadcasted_iota(jnp.int32, (L, G * L), 0)
        cid = lax.broadcasted_iota(jnp.int32, (L, G * L), 1)
        lane = lax.broadcasted_iota(jnp.int32, (1, LANES), 1)
        mine = [jnp.logical_and(lane >= hh * hd, lane < (hh + 1) * hd) for hh in range(2)]

        def qloop(i, carry):
            qrows = pl.ds(pl.multiple_of(i * L, L), L)
            q = q_ref[qrows, :]
            qhs = [jnp.where(mine[hh], q, 0.0).astype(BF16) for hh in range(2)]

            def gloop(gg, c):
                accs, runs = c
                hi = i - gg * G
                start = jnp.maximum(hi - (G - 1), 0)
                krows = pl.ds(pl.multiple_of(start * L, L), G * L)
                ks = k_ref[krows, :].astype(BF16)
                vs = v_ref[krows, :].astype(BF16)
                kpos = start * L + cid
                mask = jnp.logical_and(jnp.logical_and(kpos < i * L + rid, kpos >= cfg.pad), kpos < (hi + 1) * L)
                new_accs, new_runs = [], []
                for hh in range(2):
                    z, sp, l1m = _sb_scores(cfg, qhs[hh], ks, mask)
                    after, run = _sb_running(l1m, after_tri, runs[hh], range(G - 1, -1, -1))
                    w = jnp.where(mask, jnp.exp(z - sp + after), 0.0)
                    new_accs.append(accs[hh] + _dg(w.astype(BF16), vs, 1, 0))
                    new_runs.append(run)
                return tuple(new_accs), tuple(new_runs)

            zero_acc, zero_run = jnp.zeros((L, LANES), F32), jnp.zeros((L, 1), F32)
            accs, runs = lax.fori_loop(0, (i + G) // G, gloop, ((zero_acc, zero_acc), (zero_run, zero_run)))
            for hh in range(2):
                tot_ref[0, hh, qrows, :] = runs[hh]
            y_ref[qrows, :] = jnp.where(lane < hd, accs[0], accs[1]).astype(y_ref.dtype)
            return carry

        lax.fori_loop(0, nc, qloop, 0)

    (y, tot), landed = call_with_ride(
        body, name="sb_fwd", grid=(nb, cfg.sb_w // LANES), in_specs=specs, args=args,
        out_specs=[pl.BlockSpec((cfg.tp, LANES), lambda b, p: (b, p)),
                   pl.BlockSpec((1, 2, cfg.tp, 1), lambda b, p: (b, p, 0, 0))],
        out_shape=[jax.ShapeDtypeStruct((u.shape[0], cfg.sb_w), BF16),
                   jax.ShapeDtypeStruct((nb, cfg.sb_heads, cfg.tp, 1), F32)], ride=ride)
    return y, tot, landed


def sb_bwd(cfg, u, col0, tot, dy, dy_col0, ride=None):
    nb = u.shape[0] // cfg.tp
    L, nc = cfg.chunk, cfg.nc
    hd = cfg.sb_head_dim
    scale = hd ** -0.5
    specs, args = _sb_specs(cfg, u, col0)
    dy_base = dy_col0 // LANES
    specs += [pl.BlockSpec((1, 2, cfg.tp, 1), lambda b, p: (b, p, 0, 0)),
              pl.BlockSpec((cfg.tp, LANES), lambda b, p: (b, dy_base + p))]

    G = _sb_group(cfg)

    def body(q_ref, k_ref, v_ref, tot_ref, dy_ref, dq_ref, dk_ref, dv_ref, dk_acc, dv_acc):
        tr, tc = _tri_consts(L)
        upto_tri = (tr <= tc).astype(BF16)
        before_tri = (tr < tc).astype(BF16)
        rid = lax.broadcasted_iota(jnp.int32, (L, G * L), 0)
        cid = lax.broadcasted_iota(jnp.int32, (L, G * L), 1)
        lane = lax.broadcasted_iota(jnp.int32, (1, LANES), 1)
        mine = [jnp.logical_and(lane >= hh * hd, lane < (hh + 1) * hd) for hh in range(2)]
        dk_acc[...] = jnp.zeros_like(dk_acc)
        dv_acc[...] = jnp.zeros_like(dv_acc)

        def qloop(i, carry):
            qrows = pl.ds(pl.multiple_of(i * L, L), L)
            q = q_ref[qrows, :]
            dy = dy_ref[qrows, :].astype(F32)
            qhs = [jnp.where(mine[hh], q, 0.0).astype(BF16) for hh in range(2)]
            dyhs = [jnp.where(mine[hh], dy, 0.0).astype(BF16) for hh in range(2)]
            tots = [tot_ref[0, hh, qrows, :] for hh in range(2)]

            def gloop(g, c):
                dq, run_ls, run_as = c
                lo = g * G
                start = jnp.minimum(lo, nc - G)
                krows = pl.ds(pl.multiple_of(start * L, L), G * L)
                k = k_ref[krows, :]
                ks = k.astype(BF16)
                vs = v_ref[krows, :].astype(BF16)
                kpos = start * L + cid
                mask = jnp.logical_and(jnp.logical_and(kpos < i * L + rid, kpos >= cfg.pad), kpos >= lo * L)
                new_ls, new_as = [], []
                dv_add = jnp.zeros((G * L, LANES), F32)
                dk_add = jnp.zeros((G * L, LANES), F32)
                for hh in range(2):
                    z, sp, l1m = _sb_scores(cfg, qhs[hh], ks, mask)
                    upto, run_l = _sb_running(l1m, upto_tri, run_ls[hh], range(G))
                    w = jnp.where(mask, jnp.exp(z - sp + (tots[hh] - upto)), 0.0)
                    da = w * _dg(dyhs[hh], vs, 1, 1)
                    d_l1m, run_a = _sb_running(da, before_tri, run_as[hh], range(G))
                    sg = jnp.exp(z - sp)
                    dz = (jnp.where(mask, da * (1.0 - sg) - d_l1m * sg, 0.0) * scale).astype(BF16)
                    dv_add = dv_add + _dg(w.astype(BF16), dyhs[hh], 0, 0)
                    dk_add = dk_add + _dg(dz, qhs[hh], 0, 0)
                    dq = dq + _dg(dz, jnp.where(mine[hh], k, 0.0).astype(BF16), 1, 0)
                    new_ls.append(run_l)
                    new_as.append(run_a)
                dv_acc[krows, :] += dv_add
                dk_acc[krows, :] += dk_add
                return dq, tuple(new_ls), tuple(new_as)

            zero = jnp.zeros((L, 1), F32)
            dq, _, _ = lax.fori_loop(0, (i + G) // G, gloop, (jnp.zeros((L, LANES), F32), (zero, zero), (zero, zero)))
            dq_ref[qrows, :] = dq.astype(dq_ref.dtype)
            return carry

        lax.fori_loop(0, nc, qloop, 0)
        dk_ref[...] = dk_acc[...].astype(dk_ref.dtype)
        dv_ref[...] = dv_acc[...].astype(dv_ref.dtype)

    out_spec = pl.BlockSpec((cfg.tp, LANES), lambda b, p: (b, p))
    seg_shape = jax.ShapeDtypeStruct((u.shape[0], cfg.sb_w), BF16)
    return call_with_ride(
        body, name="sb_bwd", grid=(nb, cfg.sb_w // LANES), in_specs=specs, args=args + [tot, dy], out_specs=[out_spec] * 3,
        out_shape=[seg_shape] * 3, scratch_shapes=[pltpu.VMEM((cfg.tp, LANES), F32), pltpu.VMEM((cfg.tp, LANES), F32)], ride=ride)


def _lru_cols(cfg):
    return _tile(cfg.lru_width, 256)


def lru_scan_fwd(cfg, a, b):
    nb = a.shape[0] // cfg.tp
    cw = _lru_cols(cfg)
    spec = pl.BlockSpec((cfg.tp, cw), lambda s, c: (s, c))

    def body(a_ref, b_ref, h_ref):
        def step(i, h):
            rows = pl.ds(pl.multiple_of(i * SUBLANES, SUBLANES), SUBLANES)
            at, bt = a_ref[rows, :], b_ref[rows, :]
            outs = []
            for r in range(SUBLANES):
                h = at[r:r + 1, :] * h + bt[r:r + 1, :]
                outs.append(h)
            h_ref[rows, :] = jnp.concatenate(outs, axis=0)
            return h

        lax.fori_loop(0, cfg.tp // SUBLANES, step, jnp.zeros((1, cw), F32))

    return pl.pallas_call(
        body, grid=(nb, cfg.lru_width // cw), in_specs=[spec, spec], out_specs=spec,
        out_shape=jax.ShapeDtypeStruct(a.shape, F32), name="lru_scan_fwd", compiler_params=_cparams())(a, b)


def lru_scan_bwd(cfg, a, h, dh):
    nb = a.shape[0] // cfg.tp
    cw = _lru_cols(cfg)
    nt = cfg.tp // SUBLANES
    spec = pl.BlockSpec((cfg.tp, cw), lambda s, c: (s, c))

    def body(a_ref, h_ref, dh_ref, da_ref, db_ref):
        def step(k, c):
            i = nt - 1 - k
            rows = pl.ds(pl.multiple_of(i * SUBLANES, SUBLANES), SUBLANES)
            prev = pl.ds(pl.multiple_of(jnp.maximum(i - 1, 0) * SUBLANES, SUBLANES), SUBLANES)
            at, ht, dht = a_ref[rows, :], h_ref[rows, :], dh_ref[rows, :]
            h_before = jnp.where(i > 0, h_ref[prev, :][SUBLANES - 1:SUBLANES, :], 0.0)
            das, dbs = [None] * SUBLANES, [None] * SUBLANES
            for r in range(SUBLANES - 1, -1, -1):
                g = dht[r:r + 1, :] + c
                dbs[r] = g
                das[r] = g * (ht[r - 1:r, :] if r > 0 else h_before)
                c = at[r:r + 1, :] * g
            da_ref[rows, :] = jnp.concatenate(das, axis=0)
            db_ref[rows, :] = jnp.concatenate(dbs, axis=0)
            return c

        lax.fori_loop(0, nt, step, jnp.zeros((1, cw), F32))

    return pl.pallas_call(
        body, grid=(nb, cfg.lru_width // cw), in_specs=[spec] * 3, out_specs=[spec] * 2,
        out_shape=[jax.ShapeDtypeStruct(a.shape, F32)] * 2, name="lru_scan_bwd", compiler_params=_cparams())(a, h, dh)


def final_loss(cfg, h, target, norm_g):
    assert cfg.seq % cfg.chunk == 0 and cfg.n_meta + cfg.pad == cfg.chunk
    L, nc, d = cfg.chunk, cfg.nc, cfg.d_model
    per_seq = cfg.seq // L

    def tgt_map(i):
        return ((i // nc) * per_seq + jnp.maximum(i % nc - 1, 0), 0)

    def body(h_ref, t_ref, g_ref, loss_ref, dh_ref, dg_ref):
        i = pl.program_id(0)
        real = jnp.where(i % nc == 0, 0.0, 1.0)
        tgt = t_ref[...]

        def loss_fn(hv, g):
            y = hv * lax.rsqrt(jnp.mean(hv * hv, axis=-1, keepdims=True) + EPS) * g
            return 0.5 * real * jnp.sum(jnp.mean(jnp.square(y - tgt), axis=-1))

        val, (dh, dg) = jax.value_and_grad(loss_fn, argnums=(0, 1))(h_ref[...], g_ref[...])

        @pl.when(i == 0)
        def _():
            loss_ref[...] = jnp.zeros_like(loss_ref)
            dg_ref[...] = jnp.zeros_like(dg_ref)

        loss_ref[...] += jnp.broadcast_to(val, loss_ref.shape)
        dg_ref[...] += dg
        dh_ref[...] = dh

    return pl.pallas_call(
        body, grid=(h.shape[0] // L,),
        in_specs=[pl.BlockSpec((L, d), lambda i: (i, 0)), pl.BlockSpec((L, d), tgt_map), pl.BlockSpec((1, d), lambda i: (0, 0))],
        out_specs=[pl.BlockSpec((SUBLANES, LANES), lambda i: (0, 0)), pl.BlockSpec((L, d), lambda i: (i, 0)),
                   pl.BlockSpec((1, d), lambda i: (0, 0))],
        out_shape=[jax.ShapeDtypeStruct((SUBLANES, LANES), F32), jax.ShapeDtypeStruct(h.shape, F32),
                   jax.ShapeDtypeStruct((1, d), F32)],
        name="final_loss", compiler_params=_cparams(dimension_semantics=("arbitrary",)))(h, target, norm_g)


N_PEER = 7


class Ride(NamedTuple):
    arrays: list
    same: list


def _ride_copies(ride_in, ride_out, same, sems, sending):
    send_sems, recv_sems, local_sems = sems
    n = len(ride_in)
    x, y, c = lax.axis_index("x"), lax.axis_index("y"), lax.axis_index("c")
    me = 4 * x + 2 * y + c

    def slab(w, dest):
        return ride_in[w] if same[w] else ride_in[w].at[dest]

    local = [pltpu.make_async_copy(slab(w, me), ride_out[w].at[me], local_sems.at[w]) for w in range(n)]
    remote = []
    for r in range(1, N_PEER + 1):
        peer = (1 - x if r & 4 else x, 1 - y if r & 2 else y, 1 - c if r & 1 else c)
        pidx = 4 * peer[0] + 2 * peer[1] + peer[2]
        for w in range(n):
            remote.append(pltpu.make_async_remote_copy(
                src_ref=slab(w, pidx), dst_ref=ride_out[w].at[me if sending else pidx], send_sem=send_sems.at[w, r - 1],
                recv_sem=recv_sems.at[w, r - 1], device_id=peer, device_id_type=pl.DeviceIdType.MESH))
    return local, remote


def _ride_start(ride_in, ride_out, same, sems):
    local, remote = _ride_copies(ride_in, ride_out, same, sems, True)
    for cp in local + remote:
        cp.start()


def _ride_wait(ride_in, ride_out, same, sems):
    local, remote = _ride_copies(ride_in, ride_out, same, sems, False)
    for cp in remote:
        cp.wait_recv()
    for cp in remote:
        cp.wait_send()
    for cp in local:
        cp.wait()


def call_with_ride(body, *, name, grid, in_specs, args, out_specs, out_shape, scratch_shapes=(), ride=None, **cparams):
    out_specs, out_shape, scratch_shapes = list(out_specs), list(out_shape), list(scratch_shapes)
    if ride is None:
        res = pl.pallas_call(body, grid=grid, in_specs=list(in_specs), out_specs=out_specs, out_shape=out_shape,
                             scratch_shapes=scratch_shapes, name=name, compiler_params=_cparams(**cparams))(*args)
        return list(res), []
    arrays, same = ride
    n, n_in, n_out, n_scr = len(arrays), len(args), len(out_shape), len(scratch_shapes)
    hbm = pl.BlockSpec(memory_space=pl.ANY)
    land_shape = [jax.ShapeDtypeStruct((8,) + (a.shape if s else a.shape[1:]), a.dtype) for a, s in zip(arrays, same)]

    def wrapped(*refs):
        ins, ride_in = refs[:n_in], refs[n_in:n_in + n]
        outs, ride_out = refs[n_in + n:n_in + n + n_out], refs[n_in + n + n_out:n_in + 2 * n + n_out]
        scr, sems = refs[n_in + 2 * n + n_out:n_in + 2 * n + n_out + n_scr], refs[n_in + 2 * n + n_out + n_scr:]
        first, last = True, True
        for axis, size in enumerate(grid):
            first = jnp.logical_and(first, pl.program_id(axis) == 0)
            last = jnp.logical_and(last, pl.program_id(axis) == size - 1)
        if grid:
            pl.when(first)(lambda: _ride_start(ride_in, ride_out, same, sems))
        else:
            _ride_start(ride_in, ride_out, same, sems)
        body(*ins, *outs, *scr)
        if grid:
            pl.when(last)(lambda: _ride_wait(ride_in, ride_out, same, sems))
        else:
            _ride_wait(ride_in, ride_out, same, sems)

    sems = [pltpu.SemaphoreType.DMA((n, N_PEER)), pltpu.SemaphoreType.DMA((n, N_PEER)), pltpu.SemaphoreType.DMA((n,))]
    cparams["dimension_semantics"] = ("arbitrary",) * len(grid)
    res = pl.pallas_call(wrapped, grid=grid, in_specs=list(in_specs) + [hbm] * n, out_specs=out_specs + [hbm] * n,
                         out_shape=out_shape + land_shape, scratch_shapes=scratch_shapes + sems, name=name,
                         compiler_params=_cparams(**cparams))(*args, *arrays)
    return list(res[:n_out]), list(res[n_out:])


def exchange(name, ride):
    return call_with_ride(lambda: None, name=name, grid=(), in_specs=[], args=[], out_specs=[], out_shape=[], ride=ride)[1]


def _row_tile(r, mult, cap):
    best = None
    for t in range(mult, min(r, cap) + 1, mult):
        if r % t == 0:
            best = t
    return best if best is not None else r


def adamw(name, parts, w, m, v):
    r, c = w.shape
    tr = _row_tile(r, 16, 256)
    spec = pl.BlockSpec((tr, c), lambda i: (i, 0))

    def body(p_ref, w_ref, m_ref, v_ref, g_ref, d_ref, m2_ref, v2_ref):
        g = p_ref[0].astype(F32)
        for dev in range(1, 8):
            g = g + p_ref[dev].astype(F32)
        m2 = ADAM_B1 * m_ref[...] + (1.0 - ADAM_B1) * g
        v2 = ADAM_B2 * v_ref[...] + (1.0 - ADAM_B2) * jnp.square(g)
        m_hat = m2 / (1.0 - ADAM_B1 ** ADAM_STEP)
        v_hat = v2 / (1.0 - ADAM_B2 ** ADAM_STEP)
        g_ref[...] = g
        d_ref[...] = -ADAM_LR * (m_hat / (jnp.sqrt(v_hat) + ADAM_EPS) + ADAM_WD * w_ref[...])
        m2_ref[...] = m2
        v2_ref[...] = v2

    return pl.pallas_call(
        body, grid=(r // tr,), in_specs=[pl.BlockSpec((8, tr, c), lambda i: (0, i, 0)), spec, spec, spec],
        out_specs=[spec] * 4, out_shape=[jax.ShapeDtypeStruct((r, c), F32)] * 4, name=name, compiler_params=_cparams())(parts, w, m, v)


PACK_ROWS = 256


def _pack(arrs):
    flat = jnp.concatenate([a.reshape(-1).astype(F32) for a in arrs])
    quantum = PACK_ROWS * LANES
    total = -(-flat.shape[0] // quantum) * quantum
    return jnp.pad(flat, (0, total - flat.shape[0])).reshape(-1, LANES)


def _unpack(packed, shapes):
    flat = packed.reshape(-1)
    out, off = [], 0
    for s in shapes:
        n = int(np.prod(s))
        out.append(flat[off:off + n].reshape(s))
        off += n
    return out


def _pad_lanes(vec):
    return jnp.pad(vec.astype(F32), (0, LANES - vec.shape[0]))[None, :]


def make_step(cfg):
    I, BC, H, RW, SW, LW, F, D = (cfg.ssd_inner, cfg.ssd_bc, cfg.ssd_heads, cfg.ret_w, cfg.sb_w, cfg.lru_width,
                                  cfg.ffn_dim, cfg.d_model)
    ND = cfg.n_dev
    q_off = 2 * I + 2 * BC
    dt_off = q_off + 4 * RW
    dt_w = (-dt_off) % 1024 or 1024

    def gather_cols(g):
        return jnp.transpose(g, (1, 0, 2)).reshape(g.shape[1], -1)

    def scatter_cols(full):
        r, c = full.shape
        return jnp.transpose(full.reshape(r, ND, c // ND), (1, 0, 2))

    def step(p, m_in, v_in, x, loss_target):
        nb = x.shape[0]
        rows = nb * cfg.tp

        small_shapes = [p[n].shape for n in SMALL_SHARDED]
        wfull = {}

        def gather_ride(names, extra=()):
            own = [p[n].astype(BF16) for n in names] + list(extra)
            return Ride(own, [True] * len(own))

        def gather_finish(names, landed):
            for n, g in zip(names, landed):
                wfull[n] = gather_cols(g) if n in BIG_COL else g.reshape(-1, g.shape[-1])

        landed = exchange("gather_l0_mixer", gather_ride(['l0_w_in'], [_pack([p[n] for n in SMALL_SHARDED])]))
        gather_finish(['l0_w_in'], landed)
        small_parts = [_unpack(landed[-1][d], small_shapes) for d in range(ND)]
        sfull = {n: jnp.concatenate([small_parts[d][k] for d in range(ND)], axis=1) for k, n in enumerate(SMALL_SHARDED)}

        w0 = wfull['l0_w_in']
        w0a = jnp.concatenate([w0[:, :q_off], w0[:, q_off + H:], jnp.pad(w0[:, q_off:q_off + H], ((0, 0), (0, dt_w - H)))], axis=1)

        sel_hd = (jnp.arange(LANES)[:, None] == (jnp.arange(I)[None, :] // cfg.ssd_head_dim)).astype(F32)
        sel_lane = (jnp.arange(LANES)[:, None] == (jnp.arange(H * LANES)[None, :] // LANES)).astype(F32)
        rconst = _ret_consts(cfg)
        row2 = lambda vec: vec.astype(F32)[None, :]

        meta = jnp.broadcast_to(sfull['meta_tokens'][None], (nb, cfg.n_meta, D))
        h0 = jnp.concatenate([jnp.zeros((nb, cfg.pad, D), F32), meta, x], axis=1).reshape(rows, D)

        def norm_fwd(name, h, g):
            return stage_fwd(name, cfg, _norm_fn, [Row(h, D)], [Par(row2(g))], [(D, BF16)])[0]

        def norm_bwd(name, h, g, dhn, dh_next):
            fn = lambda valid, hv, gv: (_norm_fn(valid, hv, gv)[0], hv)
            (dh,), (dg,) = stage_bwd(name, cfg, fn, [Row(h, D)], [Par(row2(g))], [[Row(dhn, D)], [Row(dh_next, D)]], [F32])
            return dh, dg[0]

        hn0 = norm_fwd("l0_mix_norm", h0, p['l0_mix_norm'])
        u0, landed = matmul("l0_in_proj", hn0, w0a, ride=gather_ride(['l0_w_out']))
        gather_finish(['l0_w_out'], landed)
        cw, cb = sfull['l0_ssd_conv_w'], p['l0_ssd_conv_b']
        ssd_pre = _make_ssd_pre(cfg)
        ssd_post = _make_ssd_post(cfg)
        pre_rows = [Row(u0, I, 1, tail=True), Row(u0, BC, (2 * I) // BC, tail=True), Row(u0, BC, (2 * I) // BC + 1, tail=True),
                    Row(u0, LANES, dt_off // LANES)]
        pre_pars = [Par(cw[:, :I]), Par(cw[:, I:I + BC]), Par(cw[:, I + BC:]), Par(row2(cb[:I])), Par(row2(cb[I:I + BC])),
                    Par(row2(cb[I + BC:])), Par(_pad_lanes(p['l0_ssd_dt_bias'])), Par(_pad_lanes(p['l0_ssd_a_log'])),
                    Par(sel_hd, diff=False), Par(sel_lane, diff=False)]
        xs, bm, cm, xdt, adt_b = stage_fwd("ssd_pre", cfg, ssd_pre, pre_rows, pre_pars,
                                           [(I, F32), (BC, F32), (BC, F32), (I, F32), (H * LANES, F32)])
        y_raw, landed = ssd_scan_fwd(cfg, xdt, bm, cm, adt_b, ride=gather_ride(['l0_ffn_w_in']))
        gather_finish(['l0_ffn_w_in'], landed)
        post_rows = [Row(y_raw, I), Row(xs, I), Row(u0, I, 0)]
        post_pars = [Par(_pad_lanes(p['l0_ssd_d'])), Par(row2(p['l0_ssd_norm'])), Par(sel_hd, diff=False)]
        (y_ssd,) = stage_fwd("ssd_post", cfg, ssd_post, post_rows, post_pars, [(I, BF16)])
        ret_g = row2(p['l0_ret_norm'])
        y_ret, landed = ret_fwd(cfg, u0, q_off, ret_g, rconst, ride=gather_ride(['l0_ffn_w_out']))
        gather_finish(['l0_ffn_w_out'], landed)
        ycat0 = jnp.concatenate([y_ssd, y_ret], axis=1)
        h1 = matmul("l0_out_proj", ycat0, wfull['l0_w_out'], resid=h0, cfg=cfg)

        def ffn_fwd(tag, h, norm_g, w_in, conv_w, conv_b, w_out, gather_names=()):
            hn = norm_fwd(tag + "_ffn_norm", h, norm_g)
            if gather_names:
                u, landed = matmul(tag + "_ffn_in", hn, w_in, ride=gather_ride(list(gather_names)))
                gather_finish(gather_names, landed)
            else:
                u = matmul(tag + "_ffn_in", hn, w_in)
            rws = [Row(u, F, 0, tail=True), Row(u, F, 1, tail=True)]
            prs = [Par(conv_w[:, :F]), Par(conv_w[:, F:]), Par(row2(conv_b[:F])), Par(row2(conv_b[F:]))]
            (act,) = stage_fwd(tag + "_ffn_act", cfg, _ffn_act_fn, rws, prs, [(F, BF16)])
            h_out = matmul(tag + "_ffn_out", act, w_out, resid=h, cfg=cfg)
            return h_out, (hn, u, rws, prs, act)

        h2, ffn0_saved = ffn_fwd("l0", h1, p['l0_ffn_norm'], wfull['l0_ffn_w_in'], sfull['l0_ffn_conv_w'], p['l0_ffn_conv_b'],
                                 wfull['l0_ffn_w_out'], gather_names=['l1_w_in'])

        hn2 = norm_fwd("l1_mix_norm", h2, p['l1_mix_norm'])
        u1 = matmul("l1_in_proj", hn2, wfull['l1_w_in'])
        late_names = ['l1_w_out', 'l1_ffn_w_in', 'l1_ffn_w_out']
        y_sb, sb_tot, landed = sb_fwd(cfg, u1, 0, ride=gather_ride(late_names))
        gather_finish(late_names, landed)
        lru_pre = _make_lru_pre(cfg)
        gate_blk = (3 * SW) // LW
        lpre_rows = [Row(u1, LW, gate_blk + 1, tail=True)]
        lpre_pars = [Par(sfull['l1_lru_conv_w']), Par(row2(p['l1_lru_conv_b'])), Par(p['l1_lru_wa']), Par(row2(p['l1_lru_ba'])),
                     Par(p['l1_lru_wx']), Par(row2(p['l1_lru_bx'])), Par(row2(p['l1_lru_lambda']))]
        lru_a, lru_b = stage_fwd("lru_pre", cfg, lru_pre, lpre_rows, lpre_pars, [(LW, F32), (LW, F32)])
        lru_h = lru_scan_fwd(cfg, lru_a, lru_b)
        lpost_rows = [Row(lru_h, LW), Row(u1, LW, gate_blk)]
        (y_lru,) = stage_fwd("lru_post", cfg, _lru_post_fn, lpost_rows, [], [(LW, BF16)])
        ycat1 = jnp.concatenate([y_sb, y_lru], axis=1)
        h3 = matmul("l1_out_proj", ycat1, wfull['l1_w_out'], resid=h2, cfg=cfg)
        h4, ffn1_saved = ffn_fwd("l1", h3, p['l1_ffn_norm'], wfull['l1_ffn_w_in'], sfull['l1_ffn_conv_w'], p['l1_ffn_conv_b'],
                                 wfull['l1_ffn_w_out'])

        loss_part, dh4, d_final = final_loss(cfg, h4, loss_target.reshape(nb * cfg.seq, D), row2(p['final_norm']))
        loss = lax.psum(loss_part[0, 0], ("x", "y", "c"))
        gfull, grep = {}, {'final_norm': d_final[0]}

        def ffn_bwd(tag, h, norm_g, w_in, w_out, saved, dh_out, ride_names=()):
            hn, u, rws, prs, act = saved
            dact = matmul(tag + "_ffn_out_dx", dh_out, w_out, tb=True, out_dtype=BF16)
            gfull[tag + '_ffn_w_out'] = matmul(tag + "_ffn_out_dw", act, dh_out, ta=True, out_dtype=BF16, tm_cap=512, tn_cap=512, tk_cap=4352)
            (dug, duu), (dwg, dwu, dbg, dbu) = stage_bwd(tag + "_ffn_act_bwd", cfg, _ffn_act_fn, rws, prs, [[Row(dact, F)]], [BF16, BF16])
            du = jnp.concatenate([dug, duu], axis=1)
            gfull[tag + '_ffn_conv_w'] = jnp.concatenate([dwg, dwu], axis=1)
            grep[tag + '_ffn_conv_b'] = jnp.concatenate([dbg, dbu], axis=1)[0]
            if ride_names:
                dhn, landed = matmul(tag + "_ffn_in_dx", du, w_in, tb=True, ride=grads_ride(list(ride_names)))
                parts_of.update(zip(ride_names, landed))
            else:
                dhn = matmul(tag + "_ffn_in_dx", du, w_in, tb=True)
            gfull[tag + '_ffn_w_in'] = matmul(tag + "_ffn_in_dw", hn, du, ta=True, out_dtype=BF16, tm_cap=512, tn_cap=512, tk_cap=4352)
            dh, grep[tag + '_ffn_norm'] = norm_bwd(tag + "_ffn_norm_bwd", h, norm_g, dhn, dh_out)
            return dh

        parts_of = {}

        def grads_ride(names, extra=(), extra_same=()):
            sends = [scatter_cols(gfull[n]) if n in BIG_COL else gfull[n].reshape(ND, -1, gfull[n].shape[-1]) for n in names]
            return Ride(sends + list(extra), [False] * len(names) + list(extra_same))

        dh3 = ffn_bwd("l1", h3, p['l1_ffn_norm'], wfull['l1_ffn_w_in'], wfull['l1_ffn_w_out'], ffn1_saved, dh4)

        dycat1 = matmul("l1_out_dx", dh3, wfull['l1_w_out'], tb=True, out_dtype=BF16)
        gfull['l1_w_out'] = matmul("l1_out_dw", ycat1, dh3, ta=True, out_dtype=BF16, tm_cap=512, tn_cap=512, tk_cap=4352)
        (d_lru_h, d_gate), _ = stage_bwd("lru_post_bwd", cfg, _lru_post_fn, lpost_rows, [], [[Row(dycat1, LW, SW // LW)]], [F32, BF16])
        d_lru_a, d_lru_b = lru_scan_bwd(cfg, lru_a, lru_h, d_lru_h)
        (d_xr,), lgr = stage_bwd("lru_pre_bwd", cfg, lru_pre, lpre_rows, lpre_pars, [[Row(d_lru_a, LW)], [Row(d_lru_b, LW)]], [BF16])
        gfull['l1_lru_conv_w'] = lgr[0]
        grep.update({'l1_lru_conv_b': lgr[1][0], 'l1_lru_wa': lgr[2], 'l1_lru_ba': lgr[3][0], 'l1_lru_wx': lgr[4],
                     'l1_lru_bx': lgr[5][0], 'l1_lru_lambda': lgr[6][0]})
        names = ['l1_ffn_w_out', 'l1_ffn_w_in', 'l1_w_out']
        early = _pack([grep[n] for n in REP_EARLY])
        (d_q, d_k, d_v), landed = sb_bwd(cfg, u1, 0, sb_tot, dycat1, 0, ride=grads_ride(names, [early], [True]))
        parts_of.update(zip(names + ['rep_early'], landed))
        du1 = jnp.concatenate([d_q, d_k, d_v, d_gate, d_xr], axis=1)
        gfull['l1_w_in'] = matmul("l1_in_dw", hn2, du1, ta=True, out_dtype=BF16, tm_cap=512, tn_cap=512, tk_cap=4352)
        dhn2, landed = matmul("l1_in_dx", du1, wfull['l1_w_in'], tb=True, ride=grads_ride(['l1_w_in']))
        parts_of['l1_w_in'] = landed[0]
        dh2, grep['l1_mix_norm'] = norm_bwd("l1_mix_norm_bwd", h2, p['l1_mix_norm'], dhn2, dh3)

        dh1 = ffn_bwd("l0", h1, p['l0_ffn_norm'], wfull['l0_ffn_w_in'], wfull['l0_ffn_w_out'], ffn0_saved, dh2,
                      ride_names=['l0_ffn_w_out'])

        dycat0 = matmul("l0_out_dx", dh1, wfull['l0_w_out'], tb=True, out_dtype=BF16)
        gfull['l0_w_out'] = matmul("l0_out_dw", ycat0, dh1, ta=True, out_dtype=BF16, tm_cap=512, tn_cap=512, tk_cap=4352)
        (d_yraw, d_xs1, d_z), (d_dskip, d_ssdnorm) = stage_bwd("ssd_post_bwd", cfg, ssd_post, post_rows, post_pars,
                                                               [[Row(dycat0, I, 0)]], [F32, F32, BF16])
        names = ['l0_ffn_w_in', 'l0_w_out']
        (d_xdt, d_bm, d_cm, d_adt), landed = ssd_scan_bwd(cfg, xdt, bm, cm, adt_b, d_yraw, ride=grads_ride(names))
        parts_of.update(zip(names, landed))
        (d_xs, d_b, d_c, d_dt), sgr = stage_bwd("ssd_pre_bwd", cfg, ssd_pre, pre_rows, pre_pars,
                                                 [[Row(d_xs1, I)], [Row(d_bm, BC)], [Row(d_cm, BC)], [Row(d_xdt, I)], [Row(d_adt, H * LANES)]],
                                                 [BF16] * 4)
        gfull['l0_ssd_conv_w'] = jnp.concatenate(sgr[0:3], axis=1)
        grep.update({'l0_ssd_conv_b': jnp.concatenate(sgr[3:6], axis=1)[0], 'l0_ssd_dt_bias': sgr[6][0, :H],
                     'l0_ssd_a_log': sgr[7][0, :H], 'l0_ssd_d': d_dskip[0, :H], 'l0_ssd_norm': d_ssdnorm[0]})
        (d_rq, d_rk, d_rv, d_rg), d_retnorm = ret_bwd(cfg, u0, q_off, ret_g, rconst, dycat0, I)
        grep['l0_ret_norm'] = d_retnorm[0]
        du0 = jnp.concatenate([d_z, d_xs, d_b, d_c, d_rq, d_rk, d_rv, d_rg, d_dt, jnp.zeros((rows, dt_w - LANES), BF16)], axis=1)
        dw0a = matmul("l0_in_dw", hn0, du0, ta=True, out_dtype=BF16, tm_cap=512, tn_cap=512, tk_cap=4352)
        gfull['l0_w_in'] = jnp.concatenate([dw0a[:, :q_off], dw0a[:, dt_off:dt_off + H], dw0a[:, q_off:dt_off]], axis=1)
        dhn0, landed = matmul("l0_in_dx", du0, w0a, tb=True, ride=grads_ride(['l0_w_in']))
        parts_of['l0_w_in'] = landed[0]
        dh0, grep['l0_mix_norm'] = norm_bwd("l0_mix_norm_bwd", h0, p['l0_mix_norm'], dhn0, dh1)

        dh0 = dh0.reshape(nb, cfg.tp, D)
        grad_x = dh0[:, cfg.chunk:, :]
        gfull['meta_tokens'] = jnp.sum(dh0[:, cfg.pad:cfg.chunk, :], axis=0)

        flat = jnp.concatenate([scatter_cols(gfull[n].astype(F32)).reshape(ND, -1) for n in SMALL_SHARDED], axis=1)
        quantum = PACK_ROWS * LANES
        small_send = jnp.pad(flat, ((0, 0), (0, -flat.shape[1] % quantum))).reshape(ND, -1, LANES)
        late = _pack([grep[n] for n in REP_LATE])
        landed = exchange("exchange_last_grads", Ride([small_send, late], [False, True]))
        parts_of.update(zip(['small', 'rep_late'], landed))

        grad, delta, new_m, new_v = {}, {}, {}, {}
        for n in BIG:
            grad[n], delta[n], new_m[n], new_v[n] = adamw("adamw_" + n, parts_of[n], p[n], m_in[n], v_in[n])
        for names, tag in ((SMALL_SHARDED, 'small'), (REP_EARLY, 'rep_early'), (REP_LATE, 'rep_late')):
            shapes = [p[n].shape for n in names]
            res = adamw("adamw_" + tag, parts_of[tag], _pack([p[n] for n in names]), _pack([m_in[n] for n in names]),
                        _pack([v_in[n] for n in names]))
            for dst, packed in zip((grad, delta, new_m, new_v), res):
                for n, a in zip(names, _unpack(packed, shapes)):
                    dst[n] = a
        return (loss, grad_x, *[grad[n] for n in WEIGHTS], *[delta[n] for n in WEIGHTS], *[new_m[n] for n in WEIGHTS],
                *[new_v[n] for n in WEIGHTS])

    return step


_STEP = make_step(CFG)


def kernel(x, meta_tokens, l0_mix_norm, l0_w_in, l0_ssd_conv_w, l0_ssd_conv_b, l0_ssd_dt_bias, l0_ssd_a_log, l0_ssd_d, l0_ssd_norm, l0_ret_norm, l0_w_out, l0_ffn_norm, l0_ffn_w_in, l0_ffn_conv_w, l0_ffn_conv_b, l0_ffn_w_out, l1_mix_norm, l1_w_in, l1_lru_conv_w, l1_lru_conv_b, l1_lru_wa, l1_lru_ba, l1_lru_wx, l1_lru_bx, l1_lru_lambda, l1_w_out, l1_ffn_norm, l1_ffn_w_in, l1_ffn_conv_w, l1_ffn_conv_b, l1_ffn_w_out, final_norm, loss_target, m_meta_tokens, m_l0_mix_norm, m_l0_w_in, m_l0_ssd_conv_w, m_l0_ssd_conv_b, m_l0_ssd_dt_bias, m_l0_ssd_a_log, m_l0_ssd_d, m_l0_ssd_norm, m_l0_ret_norm, m_l0_w_out, m_l0_ffn_norm, m_l0_ffn_w_in, m_l0_ffn_conv_w, m_l0_ffn_conv_b, m_l0_ffn_w_out, m_l1_mix_norm, m_l1_w_in, m_l1_lru_conv_w, m_l1_lru_conv_b, m_l1_lru_wa, m_l1_lru_ba, m_l1_lru_wx, m_l1_lru_bx, m_l1_lru_lambda, m_l1_w_out, m_l1_ffn_norm, m_l1_ffn_w_in, m_l1_ffn_conv_w, m_l1_ffn_conv_b, m_l1_ffn_w_out, m_final_norm, v_meta_tokens, v_l0_mix_norm, v_l0_w_in, v_l0_ssd_conv_w, v_l0_ssd_conv_b, v_l0_ssd_dt_bias, v_l0_ssd_a_log, v_l0_ssd_d, v_l0_ssd_norm, v_l0_ret_norm, v_l0_w_out, v_l0_ffn_norm, v_l0_ffn_w_in, v_l0_ffn_conv_w, v_l0_ffn_conv_b, v_l0_ffn_w_out, v_l1_mix_norm, v_l1_w_in, v_l1_lru_conv_w, v_l1_lru_conv_b, v_l1_lru_wa, v_l1_lru_ba, v_l1_lru_wx, v_l1_lru_bx, v_l1_lru_lambda, v_l1_w_out, v_l1_ffn_norm, v_l1_ffn_w_in, v_l1_ffn_conv_w, v_l1_ffn_conv_b, v_l1_ffn_w_out, v_final_norm):
    args = locals()
    p = {n: args[n] for n in WEIGHTS}
    m_in = {n: args["m_" + n] for n in WEIGHTS}
    v_in = {n: args["v_" + n] for n in WEIGHTS}
    return _STEP(p, m_in, v_in, x, loss_target)
```

```python
import functools
from typing import NamedTuple

import numpy as np
import jax
import jax.numpy as jnp
from jax import lax
from jax.experimental import pallas as pl
from jax.experimental.pallas import tpu as pltpu

F32 = jnp.float32
BF16 = jnp.bfloat16
EPS = 1e-6
LRU_C = 8.0
NEG = -1e30
SUBLANES = 8
LANES = 128
VMEM_LIMIT = 56 * 1024 * 1024
CHUNK_UNROLL = 4

ADAM_LR = 0.001
ADAM_B1 = 0.9
ADAM_B2 = 0.999
ADAM_EPS = 1e-08
ADAM_WD = 0.01
ADAM_STEP = 10

WEIGHTS = ['meta_tokens', 'l0_mix_norm', 'l0_w_in', 'l0_ssd_conv_w', 'l0_ssd_conv_b', 'l0_ssd_dt_bias', 'l0_ssd_a_log',
           'l0_ssd_d', 'l0_ssd_norm', 'l0_ret_norm', 'l0_w_out', 'l0_ffn_norm', 'l0_ffn_w_in', 'l0_ffn_conv_w',
           'l0_ffn_conv_b', 'l0_ffn_w_out', 'l1_mix_norm', 'l1_w_in', 'l1_lru_conv_w', 'l1_lru_conv_b', 'l1_lru_wa',
           'l1_lru_ba', 'l1_lru_wx', 'l1_lru_bx', 'l1_lru_lambda', 'l1_w_out', 'l1_ffn_norm', 'l1_ffn_w_in',
           'l1_ffn_conv_w', 'l1_ffn_conv_b', 'l1_ffn_w_out', 'final_norm']
BIG_COL = ['l0_w_in', 'l0_ffn_w_in', 'l1_w_in', 'l1_ffn_w_in']
BIG_ROW = ['l0_w_out', 'l0_ffn_w_out', 'l1_w_out', 'l1_ffn_w_out']
BIG = BIG_COL + BIG_ROW
SMALL_SHARDED = ['meta_tokens', 'l0_ssd_conv_w', 'l0_ffn_conv_w', 'l1_lru_conv_w', 'l1_ffn_conv_w']
REPLICATED = [w for w in WEIGHTS if w not in BIG and w not in SMALL_SHARDED]
REP_EARLY = ['final_norm', 'l1_ffn_norm', 'l1_ffn_conv_b', 'l1_lru_conv_b', 'l1_lru_wa', 'l1_lru_ba', 'l1_lru_wx',
             'l1_lru_bx', 'l1_lru_lambda']
REP_LATE = [w for w in REPLICATED if w not in REP_EARLY]


class Config(NamedTuple):
    d_model: int = 1024
    seq: int = 2048
    n_meta: int = 16
    chunk: int = 128
    ssd_heads: int = 16
    ssd_head_dim: int = 64
    ssd_groups: int = 4
    ssd_state: int = 128
    ret_heads: int = 4
    ret_dim: int = 256
    sb_heads: int = 16
    sb_head_dim: int = 64
    lru_width: int = 1024
    lru_blocks: int = 8
    ffn_dim: int = 2816
    n_dev: int = 8

    @property
    def t(self):
        return self.n_meta + self.seq

    @property
    def pad(self):
        return (-self.t) % self.chunk

    @property
    def tp(self):
        return self.t + self.pad

    @property
    def nc(self):
        return self.tp // self.chunk

    @property
    def ssd_inner(self):
        return self.ssd_heads * self.ssd_head_dim

    @property
    def ssd_gw(self):
        return self.ssd_inner // self.ssd_groups

    @property
    def ssd_bc(self):
        return self.ssd_groups * self.ssd_state

    @property
    def ret_w(self):
        return self.ret_heads * self.ret_dim

    @property
    def sb_w(self):
        return self.sb_heads * self.sb_head_dim

    @property
    def mix0_segs(self):
        return (self.ssd_inner, self.ssd_inner + 2 * self.ssd_bc, self.ssd_heads, self.ret_w, self.ret_w, self.ret_w, self.ret_w)


CFG = Config()


def _dg(a, b, ca, cb):
    return lax.dot_general(a, b, (((ca,), (cb,)), ((), ())), preferred_element_type=F32)


def _cot_a(b, ca, cb, g):
    return _dg(g, b, 1, 1 - cb) if ca == 1 else _dg(b, g, 1 - cb, 1)


def _cot_b(a, ca, cb, g):
    return _dg(a, g, 1 - ca, 0) if cb == 0 else _dg(g, a, 0, 1 - ca)


@functools.partial(jax.custom_vjp, nondiff_argnums=(2, 3))
def _bdot(a, b, ca, cb):
    return _dg(a.astype(BF16), b.astype(BF16), ca, cb)


def _bdot_fwd(a, b, ca, cb):
    return _bdot(a, b, ca, cb), (a, b)


def _bdot_bwd(ca, cb, res, g):
    a, b = res
    gb = g.astype(BF16)
    return _cot_a(b.astype(BF16), ca, cb, gb).astype(a.dtype), _cot_b(a.astype(BF16), ca, cb, gb).astype(b.dtype)


_bdot.defvjp(_bdot_fwd, _bdot_bwd)


def _dot(a, b):
    return _bdot(a, b, 1, 0)


def _dot_nt(a, b):
    return _bdot(a, b, 1, 1)


def _dot_tn(a, b):
    return _bdot(a, b, 0, 0)


def _split3(a):
    hi = a.astype(BF16)
    r1 = a - hi.astype(F32)
    mid = r1.astype(BF16)
    return hi, mid, (r1 - mid.astype(F32)).astype(BF16)


@functools.partial(jax.custom_vjp, nondiff_argnums=(2, 3, 4))
def _edot(a, b, ca, cb, const):
    if const == 'b':
        bb = b.astype(BF16)
        return sum(_dg(p, bb, ca, cb) for p in _split3(a))
    ab = a.astype(BF16)
    return sum(_dg(ab, p, ca, cb) for p in _split3(b))


def _edot_fwd(a, b, ca, cb, const):
    return _edot(a, b, ca, cb, const), (a, b)


def _edot_bwd(ca, cb, const, res, g):
    a, b = res
    if const == 'b':
        bb = b.astype(BF16)
        return sum(_cot_a(bb, ca, cb, p) for p in _split3(g)), jnp.zeros_like(b)
    ab = a.astype(BF16)
    return jnp.zeros_like(a), sum(_cot_b(ab, ca, cb, p) for p in _split3(g))


_edot.defvjp(_edot_fwd, _edot_bwd)


def _dot_exact01(a, sel):
    return _edot(a, sel, 1, 0, 'b')


def _silu(x):
    return x * jax.nn.sigmoid(x)


def _softplus(x):
    return jnp.maximum(x, 0.0) + jnp.log1p(jnp.exp(-jnp.abs(x)))


def _neg_expm1(x):
    series = -x * (1.0 + x * 0.5 * (1.0 + x / 3.0 * (1.0 + x * 0.25)))
    return jnp.where(x > -0.01, series, 1.0 - jnp.exp(x))


def _causal_conv(x, tail, w, b):
    taps, rows = w.shape[0], x.shape[0]
    xx = jnp.concatenate([tail, x], axis=0)
    y = b
    for k in range(taps):
        off = SUBLANES - (taps - 1 - k)
        y = y + w[k:k + 1, :] * xx[off:off + rows, :]
    return y


def _valid_chunk(cfg, chunk_idx):
    first = (chunk_idx % cfg.nc) == 0
    rid = lax.broadcasted_iota(jnp.int32, (cfg.chunk, 1), 0)
    return jnp.where(jnp.logical_and(first, rid < cfg.pad), 0.0, 1.0).astype(F32)


def _cparams(**kw):
    return pltpu.CompilerParams(vmem_limit_bytes=VMEM_LIMIT, **kw)


class Row(NamedTuple):
    arr: jax.Array
    width: int
    blk: int = 0
    tail: bool = False
    diff: bool = True


class Par(NamedTuple):
    arr: jax.Array
    diff: bool = True


def _row_specs(tr, rows, order):
    specs, args = [], []
    for r in rows:
        specs.append(pl.BlockSpec((tr, r.width), functools.partial(lambda i, b, o: (o(i), b), b=r.blk, o=order)))
        args.append(r.arr)
        if r.tail:
            per = tr // SUBLANES
            specs.append(pl.BlockSpec((SUBLANES, r.width),
                                      functools.partial(lambda i, b, o: (jnp.maximum(o(i) * per - 1, 0), b), b=r.blk, o=order)))
            args.append(r.arr)
    return specs, args


def _par_specs(pars):
    specs = [pl.BlockSpec(p.arr.shape, functools.partial(lambda i, nd: (0,) * nd, nd=p.arr.ndim)) for p in pars]
    return specs, [p.arr for p in pars]


def _valid_block(cfg, blk, tr, n_rows):
    row = blk * tr + lax.broadcasted_iota(jnp.int32, (tr, 1), 0)
    inpad = jnp.zeros((tr, 1), jnp.bool_)
    for s in range(n_rows // cfg.tp):
        inpad = jnp.logical_or(inpad, jnp.logical_and(row >= s * cfg.tp, row < s * cfg.tp + cfg.pad))
    return jnp.where(inpad, 0.0, 1.0).astype(F32)


def light_rows(n_rows):
    return n_rows // 8 if n_rows % (8 * 16) == 0 else None


def stage_fwd(name, cfg, fn, rows, pars, outs, tr=None, ride=None):
    tr = tr or cfg.chunk
    n_rows = rows[0].arr.shape[0]
    rspecs, rargs = _row_specs(tr, rows, lambda i: i)
    pspecs, pargs = _par_specs(pars)
    n_in = len(rargs) + len(pargs)

    def body(*refs):
        valid = _valid_block(cfg, pl.program_id(0), tr, n_rows)
        vals = [r[...].astype(F32) for r in refs[:n_in]]
        res = fn(valid, *vals)
        for o, v in zip(refs[n_in:], res):
            o[...] = v.astype(o.dtype)

    res, landed = call_with_ride(
        body, name=name, grid=(n_rows // tr,), in_specs=rspecs + pspecs, args=rargs + pargs,
        out_specs=[pl.BlockSpec((tr, w), lambda i: (i, 0)) for w, _ in outs],
        out_shape=[jax.ShapeDtypeStruct((n_rows, w), dt) for w, dt in outs], ride=ride)
    return (res, landed) if ride is not None else res


def stage_bwd(name, cfg, fn, rows, pars, douts, drow_dtypes, tr=None, ride=None):
    tr = tr or cfg.chunk
    n_rows = rows[0].arr.shape[0]
    n_blk = n_rows // tr
    order = lambda i: n_blk - 1 - i
    rspecs, rargs = _row_specs(tr, rows, order)
    pspecs, pargs = _par_specs(pars)
    pieces = [p for out in douts for p in out]
    dspecs, dargs = _row_specs(tr, pieces, order)
    n_r, n_p, n_d = len(rargs), len(pargs), len(dargs)
    diff_rows = [r for r in rows if r.diff]
    diff_pars = [p for p in pars if p.diff]
    tails = [r for r in diff_rows if r.tail]

    def body(*refs):
        in_refs = refs[:n_r + n_p]
        d_refs = refs[n_r + n_p:n_r + n_p + n_d]
        o_refs = refs[n_r + n_p + n_d:]
        drow_refs = o_refs[:len(diff_rows)]
        dpar_refs = o_refs[len(diff_rows):len(diff_rows) + len(diff_pars)]
        carry_refs = o_refs[len(diff_rows) + len(diff_pars):]
        step = pl.program_id(0)
        valid = _valid_block(cfg, order(step), tr, n_rows)
        vals = [r[...].astype(F32) for r in in_refs]
        slots, pos = [], 0
        for r in rows:
            if r.diff:
                slots.append(pos)
                if r.tail:
                    slots.append(pos + 1)
            pos += 2 if r.tail else 1
        for p in pars:
            if p.diff:
                slots.append(pos)
            pos += 1

        def g(*dv):
            full = list(vals)
            for s, v in zip(slots, dv):
                full[s] = v
            return tuple(fn(valid, *full))

        _, vjp = jax.vjp(g, *[vals[s] for s in slots])
        cts, k = [], 0
        for out in douts:
            parts = [d_refs[k + j][...].astype(F32) for j in range(len(out))]
            k += len(out)
            cts.append(parts[0] if len(parts) == 1 else jnp.concatenate(parts, axis=1))
        grads = list(vjp(tuple(cts)))

        @pl.when(step == 0)
        def _():
            for c in carry_refs:
                c[...] = jnp.zeros_like(c)
            for d in dpar_refs:
                d[...] = jnp.zeros_like(d)

        gi, ci = 0, 0
        for r, o in zip(diff_rows, drow_refs):
            dx = grads[gi]
            gi += 1
            if r.tail:
                dtail = grads[gi]
                gi += 1
                c = carry_refs[ci]
                ci += 1
                dx = dx + jnp.concatenate([jnp.zeros((tr - SUBLANES, r.width), F32), c[...]], axis=0)
                c[...] = dtail
            o[...] = dx.astype(o.dtype)
        for d in dpar_refs:
            d[...] += grads[gi]
            gi += 1

    out_specs = [pl.BlockSpec((tr, r.width), lambda i: (order(i), 0)) for r in diff_rows]
    out_shape = [jax.ShapeDtypeStruct((n_rows, r.width), dt) for r, dt in zip(diff_rows, drow_dtypes)]
    for p in diff_pars:
        out_specs.append(pl.BlockSpec(p.arr.shape, functools.partial(lambda i, nd: (0,) * nd, nd=p.arr.ndim)))
        out_shape.append(jax.ShapeDtypeStruct(p.arr.shape, F32))
    res, landed = call_with_ride(
        body, name=name, grid=(n_blk,), in_specs=rspecs + pspecs + dspecs, args=rargs + pargs + dargs, out_specs=out_specs,
        out_shape=out_shape, scratch_shapes=[pltpu.VMEM((SUBLANES, r.width), F32) for r in tails], ride=ride,
        dimension_semantics=("arbitrary",))
    grads = (list(res[:len(diff_rows)]), list(res[len(diff_rows):]))
    return grads + (landed,) if ride is not None else grads


def _tile(n, cap):
    if n % LANES:
        return n
    q = n // LANES
    best = 1
    for k in range(1, q + 1):
        if q % k == 0 and k * LANES <= cap:
            best = k
    return best * LANES


def _mm_vmem(tm, tn, tk, a_bytes, b_bytes, out_bytes, resid, nk):
    total = 2 * (tm * tk * a_bytes + tk * tn * b_bytes + tm * tn * out_bytes) + tm * tn * 4
    total += 2 * tm * tn * 4 if resid else 0
    total += (tm * tk * 2 if a_bytes == 4 else 0) + (tk * tn * 2 if b_bytes == 4 else 0)
    return total + (tm * tn * 4 if nk > 1 else 0)


def matmul(name, a, b, *, ta=False, tb=False, out_dtype=F32, resid=None, cfg=None, tm_cap=2176, tn_cap=1024, tk_cap=2816,
           ride=None):
    m, k = (a.shape[1], a.shape[0]) if ta else a.shape
    n = b.shape[0] if tb else b.shape[1]
    tm, tn, tk = _tile(m, tm_cap), _tile(n, tn_cap), _tile(k, tk_cap)
    budget = VMEM_LIMIT - 8 * 1024 * 1024
    sizes = (a.dtype.itemsize, b.dtype.itemsize, jnp.dtype(out_dtype).itemsize, resid is not None)
    while _mm_vmem(tm, tn, tk, *sizes, k // tk) > budget:
        if tm > 256:
            tm = _tile(m, tm - 1)
        elif tk > 512:
            tk = _tile(k, tk - 1)
        else:
            tn = _tile(n, tn - 1)
    nk = k // tk
    a_spec = pl.BlockSpec((tk, tm), lambda i, j, l: (l, i)) if ta else pl.BlockSpec((tm, tk), lambda i, j, l: (i, l))
    b_spec = pl.BlockSpec((tn, tk), lambda i, j, l: (j, l)) if tb else pl.BlockSpec((tk, tn), lambda i, j, l: (l, j))
    dims = (((0 if ta else 1,), (1 if tb else 0,)), ((), ()))
    in_specs, args = [a_spec, b_spec], [a, b]
    if resid is not None:
        in_specs.append(pl.BlockSpec((tm, tn), lambda i, j, l: (i, j)))
        args.append(resid)

    def body(*refs):
        a_ref, b_ref = refs[0], refs[1]
        o_ref = refs[3] if resid is not None else refs[2]
        part = lax.dot_general(a_ref[...].astype(BF16), b_ref[...].astype(BF16), dims, preferred_element_type=F32)

        def finish(res):
            if resid is not None:
                row = pl.program_id(0) * tm + lax.broadcasted_iota(jnp.int32, (tm, 1), 0)
                inpad = jnp.zeros((tm, 1), jnp.bool_)
                for s in range(m // cfg.tp):
                    inpad = jnp.logical_or(inpad, jnp.logical_and(row >= s * cfg.tp, row < s * cfg.tp + cfg.pad))
                res = refs[2][...] + jnp.where(inpad, 0.0, res)
            o_ref[...] = res.astype(o_ref.dtype)

        if nk == 1:
            finish(part)
            return
        acc = refs[-1]
        l = pl.program_id(2)

        @pl.when(l == 0)
        def _():
            acc[...] = part

        @pl.when(jnp.logical_and(l > 0, l < nk - 1))
        def _():
            acc[...] += part

        @pl.when(l == nk - 1)
        def _():
            finish(acc[...] + part)

    outs, landed = call_with_ride(
        body, name=name, grid=(m // tm, n // tn, nk), in_specs=in_specs, args=args,
        out_specs=[pl.BlockSpec((tm, tn), lambda i, j, l: (i, j))], out_shape=[jax.ShapeDtypeStruct((m, n), out_dtype)],
        scratch_shapes=[pltpu.VMEM((tm, tn), F32)] if nk > 1 else [], ride=ride,
        dimension_semantics=("parallel", "parallel", "arbitrary"))
    return (outs[0], landed) if ride is not None else outs[0]


def _norm_fn(valid, h, g):
    y = h * lax.rsqrt(jnp.mean(h * h, axis=-1, keepdims=True) + EPS)
    return (valid * (y * g),)


def _make_ssd_pre(cfg):
    nh = cfg.ssd_heads

    def fn(valid, xs, xs_t, bm, bm_t, cm, cm_t, dtr, w_xs, w_b, w_c, b_xs, b_b, b_c, dt_bias, a_log, sel_hd, sel_lane):
        xs = valid * _silu(_causal_conv(xs, xs_t, w_xs, b_xs))
        bm = valid * _silu(_causal_conv(bm, bm_t, w_b, b_b))
        cm = valid * _silu(_causal_conv(cm, cm_t, w_c, b_c))
        lane = lax.broadcasted_iota(jnp.int32, (1, LANES), 1)
        head = jnp.where(lane < nh, 1.0, 0.0).astype(F32)
        dt = valid * head * _softplus(dtr + dt_bias)
        adt = dt * (-jnp.exp(a_log))
        x = xs * _dot_exact01(dt, sel_hd)
        adt_b = _dot_exact01(adt, sel_lane)
        return xs, bm, cm, x, adt_b

    return fn


def _make_ssd_post(cfg):
    gw = cfg.ssd_gw

    def fn(valid, y_raw, xs, z, d_skip, norm_g, sel_hd):
        d_rep = _dot_exact01(jnp.broadcast_to(d_skip, (SUBLANES, LANES)), sel_hd)[0:1]
        y = (y_raw + xs * d_rep) * _silu(z)
        parts = []
        for g in range(cfg.ssd_groups):
            yg = y[:, g * gw:(g + 1) * gw]
            parts.append(yg * lax.rsqrt(jnp.mean(yg * yg, axis=-1, keepdims=True) + EPS))
        return (jnp.concatenate(parts, axis=1) * norm_g,)

    return fn


def _ffn_act_fn(valid, ug, ug_t, uu, uu_t, w_g, w_u, b_g, b_u):
    return (valid * _silu(_causal_conv(ug, ug_t, w_g, b_g)) * _causal_conv(uu, uu_t, w_u, b_u),)


def _make_lru_pre(cfg):
    nb = cfg.lru_blocks
    bw = cfg.lru_width // nb

    def fn(valid, xr, xr_t, w_conv, b_conv, wa, ba, wx, bx, lam):
        xc = _causal_conv(xr, xr_t, w_conv, b_conv)
        a_parts, b_parts = [], []
        for n in range(nb):
            sl = slice(n * bw, (n + 1) * bw)
            xn = xc[:, sl]
            r = jax.nn.sigmoid(_dot(xn, wa[n]) + ba[:, sl])
            i = jax.nn.sigmoid(_dot(xn, wx[n]) + bx[:, sl])
            log_a = -LRU_C * r * _softplus(-lam[:, sl])
            a_parts.append(jnp.exp(log_a))
            b_parts.append(valid * jnp.sqrt(jnp.maximum(_neg_expm1(2.0 * log_a), 0.0)) * (i * xn))
        return jnp.concatenate(a_parts, axis=1), jnp.concatenate(b_parts, axis=1)

    return fn


def _lru_post_fn(valid, hs, gate):
    return (hs * jax.nn.gelu(gate),)


def _tri_consts(n):
    r = lax.broadcasted_iota(jnp.int32, (n, n), 0)
    c = lax.broadcasted_iota(jnp.int32, (n, n), 1)
    return r, c


def _ssd_chunk(x, bm, cm, a, s, tril, lower):
    rows = x.shape[0]
    heads = a.shape[1] // LANES
    p = x.shape[1] // heads
    cb = _dot_nt(cm, bm)
    ys, ss = [], []
    for e in range(heads):
        ae = a[:, e * LANES:(e + 1) * LANES]
        cs = _edot(tril, ae, 1, 0, 'a')
        lmat = jnp.exp(jnp.where(lower, cs - cs.T, NEG))
        xe = x[:, e * p:(e + 1) * p]
        se = s[e * p:(e + 1) * p, :]
        tot = cs[rows - 1:rows, :]
        y_diag = _dot(cb * lmat, xe)
        st = _dot_tn(xe * jnp.exp(tot - cs)[:, :p], bm)
        y_off = _dot_nt(cm, se) * jnp.exp(cs)[:, :p]
        ys.append(y_diag + y_off)
        ss.append(jnp.exp(tot[:, :1]) * se + st)
    return jnp.concatenate(ys, axis=1), jnp.concatenate(ss, axis=0)


def _ssd_chunk_state(x, bm, a, s, tril):
    rows = x.shape[0]
    heads = a.shape[1] // LANES
    p = x.shape[1] // heads
    ss = []
    for e in range(heads):
        cs = _edot(tril, a[:, e * LANES:(e + 1) * LANES], 1, 0, 'a')
        tot = cs[rows - 1:rows, :]
        st = _dot_tn(x[:, e * p:(e + 1) * p] * jnp.exp(tot - cs)[:, :p], bm)
        ss.append(jnp.exp(tot[:, :1]) * s[e * p:(e + 1) * p, :] + st)
    return jnp.concatenate(ss, axis=0)


def _ssd_specs(cfg, x, bm, cm, adt_b):
    gw, ns = cfg.ssd_gw, cfg.ssd_state
    hpg = cfg.ssd_heads // cfg.ssd_groups
    specs = [pl.BlockSpec((cfg.tp, gw), lambda b, g: (b, g)),
             pl.BlockSpec((cfg.tp, ns), lambda b, g: (b, g)),
             pl.BlockSpec((cfg.tp, ns), lambda b, g: (b, g)),
             pl.BlockSpec((cfg.tp, hpg * LANES), lambda b, g: (b, g))]
    return specs, [x, bm, cm, adt_b]


def ssd_scan_fwd(cfg, x, bm, cm, adt_b, ride=None):
    nb = x.shape[0] // cfg.tp
    L, nc = cfg.chunk, cfg.nc
    specs, args = _ssd_specs(cfg, x, bm, cm, adt_b)

    def body(x_ref, b_ref, c_ref, a_ref, y_ref, s_ref):
        r, c = _tri_consts(L)
        lower = r >= c
        tril = lower.astype(F32)
        s_ref[...] = jnp.zeros_like(s_ref)

        def step(ci, carry):
            rows = pl.ds(pl.multiple_of(ci * L, L), L)
            y, s_new = _ssd_chunk(x_ref[rows, :], b_ref[rows, :], c_ref[rows, :], a_ref[rows, :], s_ref[...], tril, lower)
            y_ref[rows, :] = y
            s_ref[...] = s_new
            return carry

        lax.fori_loop(0, nc, step, 0, unroll=CHUNK_UNROLL)

    outs, landed = call_with_ride(
        body, name="ssd_scan_fwd", grid=(nb, cfg.ssd_groups), in_specs=specs, args=args, out_specs=[specs[0]],
        out_shape=[jax.ShapeDtypeStruct(x.shape, F32)], scratch_shapes=[pltpu.VMEM((cfg.ssd_gw, cfg.ssd_state), F32)], ride=ride)
    return outs[0], landed


def ssd_scan_bwd(cfg, x, bm, cm, adt_b, dy, ride=None):
    nb = x.shape[0] // cfg.tp
    L, nc = cfg.chunk, cfg.nc
    specs, args = _ssd_specs(cfg, x, bm, cm, adt_b)

    def body(x_ref, b_ref, c_ref, a_ref, dy_ref, dx_ref, db_ref, dc_ref, da_ref, s_all, ds_ref):
        r, c = _tri_consts(L)
        lower = r >= c
        tril = lower.astype(F32)
        chunk = functools.partial(_ssd_chunk, tril=tril, lower=lower)
        ds_ref[...] = jnp.zeros_like(ds_ref)

        def fwd(ci, carry):
            rows = pl.ds(pl.multiple_of(ci * L, L), L)
            s_all[ci] = ds_ref[...]
            ds_ref[...] = _ssd_chunk_state(x_ref[rows, :], b_ref[rows, :], a_ref[rows, :], ds_ref[...], tril)
            return carry

        lax.fori_loop(0, nc, fwd, 0, unroll=CHUNK_UNROLL)
        ds_ref[...] = jnp.zeros_like(ds_ref)

        def bwd(k, carry):
            ci = nc - 1 - k
            rows = pl.ds(pl.multiple_of(ci * L, L), L)
            _, vjp = jax.vjp(chunk, x_ref[rows, :], b_ref[rows, :], c_ref[rows, :], a_ref[rows, :], s_all[ci])
            dx, db, dc, da, ds = vjp((dy_ref[rows, :], ds_ref[...]))
            dx_ref[rows, :] = dx
            db_ref[rows, :] = db
            dc_ref[rows, :] = dc
            da_ref[rows, :] = da
            ds_ref[...] = ds
            return carry

        lax.fori_loop(0, nc, bwd, 0, unroll=CHUNK_UNROLL)

    return call_with_ride(
        body, name="ssd_scan_bwd", grid=(nb, cfg.ssd_groups), in_specs=specs + [specs[0]], args=args + [dy], out_specs=specs,
        out_shape=[jax.ShapeDtypeStruct(t.shape, F32) for t in (x, bm, cm, adt_b)],
        scratch_shapes=[pltpu.VMEM((nc, cfg.ssd_gw, cfg.ssd_state), F32), pltpu.VMEM((cfg.ssd_gw, cfg.ssd_state), F32)], ride=ride)


def _ret_chunk(q, k, v, g, norm_g, r_prev, cos, sin, valid, decay, zeta, xi, cdec, scale):
    half = q.shape[1] // 2

    def rot(t):
        t1, t2 = t[:, :half], t[:, half:]
        return jnp.concatenate([t1 * cos - t2 * sin, t1 * sin + t2 * cos], axis=1)

    qr = valid * rot(q)
    kr = valid * rot(k) * scale
    v = valid * v
    inner = _dot(_dot_nt(qr, kr) * decay, v)
    kv = _dot_tn(kr * zeta, v)
    cross = _dot(qr, r_prev) * xi
    o = inner + cross
    o = o - jnp.mean(o, axis=-1, keepdims=True)
    o = o * lax.rsqrt(jnp.mean(o * o, axis=-1, keepdims=True) + EPS)
    return _silu(g) * (o * norm_g), cdec * r_prev + kv


def _ret_consts(cfg):
    f32 = jnp.float32
    log_gamma = jnp.log1p(-jnp.exp2(-5.0 - jnp.arange(cfg.ret_heads, dtype=f32)))
    idx = jnp.arange(cfg.chunk, dtype=f32)
    diff = idx[:, None] - idx[None, :]
    decay = jnp.where(diff >= 0, jnp.exp(log_gamma[:, None, None] * jnp.maximum(diff, 0.0)), 0.0)
    zeta = jnp.exp(log_gamma[:, None] * (cfg.chunk - 1 - idx)[None, :])[..., None]
    xi = jnp.exp(log_gamma[:, None] * (idx + 1.0)[None, :])[..., None]
    cdec = jnp.exp(cfg.chunk * log_gamma)[:, None, None]
    half = cfg.ret_dim // 2
    inv_freq = 1.0 / (10000.0 ** (jnp.arange(half, dtype=f32) / (half - 1)))
    pos = jnp.arange(cfg.tp, dtype=f32) - cfg.pad
    ang = pos[:, None] * inv_freq[None, :]
    return dict(decay=decay, zeta=zeta, xi=xi, cdec=cdec, cos=jnp.cos(ang), sin=jnp.sin(ang))


def _ret_specs(cfg, u, col0, norm_g, consts):
    dk, nh, L = cfg.ret_dim, cfg.ret_heads, cfg.chunk
    base = col0 // dk
    seg = lambda s: pl.BlockSpec((cfg.tp, dk), functools.partial(lambda h, b, s: (b, base + s * nh + h), s=s))
    half = dk // 2
    specs = [seg(0), seg(1), seg(2), seg(3),
             pl.BlockSpec((1, dk), lambda h, b: (0, h)),
             pl.BlockSpec((cfg.tp, half), lambda h, b: (0, 0)),
             pl.BlockSpec((cfg.tp, half), lambda h, b: (0, 0)),
             pl.BlockSpec((None, L, L), lambda h, b: (h, 0, 0)),
             pl.BlockSpec((None, L, 1), lambda h, b: (h, 0, 0)),
             pl.BlockSpec((None, L, 1), lambda h, b: (h, 0, 0)),
             pl.BlockSpec((None, 1, 1), lambda h, b: (h, 0, 0))]
    args = [u, u, u, u, norm_g, consts["cos"], consts["sin"], consts["decay"], consts["zeta"], consts["xi"], consts["cdec"]]
    return specs, args


def _ret_chunk_at(cfg, refs, ci):
    L = cfg.chunk
    cos_ref, sin_ref, decay_ref, zeta_ref, xi_ref, cdec_ref = refs
    rows = pl.ds(pl.multiple_of(ci * L, L), L)
    fn = functools.partial(_ret_chunk, cos=cos_ref[rows, :], sin=sin_ref[rows, :], valid=_valid_chunk(cfg, ci),
                           decay=decay_ref[...], zeta=zeta_ref[...], xi=xi_ref[...], cdec=cdec_ref[...],
                           scale=cfg.ret_dim ** -0.5)
    return fn, rows


def ret_fwd(cfg, u, col0, norm_g, consts, ride=None):
    nb = u.shape[0] // cfg.tp
    dk, nc = cfg.ret_dim, cfg.nc
    specs, args = _ret_specs(cfg, u, col0, norm_g, consts)

    def body(q_ref, k_ref, v_ref, g_ref, ng_ref, *rest):
        y_ref, r_ref = rest[-2], rest[-1]
        r_ref[...] = jnp.zeros_like(r_ref)

        def step(ci, carry):
            fn, rows = _ret_chunk_at(cfg, rest[:6], ci)
            y, r_new = fn(q_ref[rows, :], k_ref[rows, :], v_ref[rows, :], g_ref[rows, :], ng_ref[...], r_ref[...])
            y_ref[rows, :] = y.astype(y_ref.dtype)
            r_ref[...] = r_new
            return carry

        lax.fori_loop(0, nc, step, 0, unroll=CHUNK_UNROLL)

    outs, landed = call_with_ride(
        body, name="ret_fwd", grid=(cfg.ret_heads, nb), in_specs=specs, args=args,
        out_specs=[pl.BlockSpec((cfg.tp, dk), lambda h, b: (b, h))],
        out_shape=[jax.ShapeDtypeStruct((u.shape[0], cfg.ret_w), BF16)], scratch_shapes=[pltpu.VMEM((dk, dk), F32)], ride=ride)
    return outs[0], landed


def ret_bwd(cfg, u, col0, norm_g, consts, dy, dy_col0):
    nb = u.shape[0] // cfg.tp
    dk, nc, nh = cfg.ret_dim, cfg.nc, cfg.ret_heads
    specs, args = _ret_specs(cfg, u, col0, norm_g, consts)
    dy_base = dy_col0 // dk
    specs.append(pl.BlockSpec((cfg.tp, dk), lambda h, b: (b, dy_base + h)))
    seg_out = lambda s: pl.BlockSpec((cfg.tp, dk), functools.partial(lambda h, b, s: (b, s * nh + h), s=s))

    def body(q_ref, k_ref, v_ref, g_ref, ng_ref, *rest):
        consts_refs, dy_ref = rest[:6], rest[6]
        dq_ref, dk_ref, dv_ref, dg_ref, dng_ref, r_all, dr_ref = rest[7:]
        dr_ref[...] = jnp.zeros_like(dr_ref)

        def fwd(ci, carry):
            fn, rows = _ret_chunk_at(cfg, consts_refs, ci)
            r_all[ci] = dr_ref[...]
            _, r_new = fn(q_ref[rows, :], k_ref[rows, :], v_ref[rows, :], g_ref[rows, :], ng_ref[...], dr_ref[...])
            dr_ref[...] = r_new
            return carry

        lax.fori_loop(0, nc, fwd, 0, unroll=CHUNK_UNROLL)
        dr_ref[...] = jnp.zeros_like(dr_ref)

        @pl.when(pl.program_id(1) == 0)
        def _():
            dng_ref[...] = jnp.zeros_like(dng_ref)

        def bwd(kk, carry):
            ci = nc - 1 - kk
            fn, rows = _ret_chunk_at(cfg, consts_refs, ci)
            _, vjp = jax.vjp(fn, q_ref[rows, :], k_ref[rows, :], v_ref[rows, :], g_ref[rows, :], ng_ref[...], r_all[ci])
            dq, dkk, dv, dg, dng, dr = vjp((dy_ref[rows, :].astype(F32), dr_ref[...]))
            dq_ref[rows, :] = dq.astype(dq_ref.dtype)
            dk_ref[rows, :] = dkk.astype(dk_ref.dtype)
            dv_ref[rows, :] = dv.astype(dv_ref.dtype)
            dg_ref[rows, :] = dg.astype(dg_ref.dtype)
            dng_ref[...] += dng
            dr_ref[...] = dr
            return carry

        lax.fori_loop(0, nc, bwd, 0, unroll=CHUNK_UNROLL)

    seg_shape = jax.ShapeDtypeStruct((u.shape[0], cfg.ret_w), BF16)
    res = pl.pallas_call(
        body, grid=(nh, nb), in_specs=specs,
        out_specs=[pl.BlockSpec((cfg.tp, dk), lambda h, b: (b, h))] * 4 + [pl.BlockSpec((1, dk), lambda h, b: (0, h))],
        out_shape=[seg_shape] * 4 + [jax.ShapeDtypeStruct((1, cfg.ret_w), F32)],
        scratch_shapes=[pltpu.VMEM((nc, dk, dk), F32), pltpu.VMEM((dk, dk), F32)],
        name="ret_bwd", compiler_params=_cparams(dimension_semantics=("arbitrary", "arbitrary")))(*args, dy)
    return res[:4], res[4]


def _dot2(a, tri):
    hi = a.astype(BF16)
    lo = (a - hi.astype(F32)).astype(BF16)
    return _dg(hi, tri, 1, 0) + _dg(lo, tri, 1, 0)


def _sb_group(cfg):
    return max(g for g in (4, 2, 1) if g <= cfg.nc)


def _sb_specs(cfg, u, col0):
    base = col0 // LANES
    per = cfg.sb_w // LANES
    seg = lambda s: pl.BlockSpec((cfg.tp, LANES), functools.partial(lambda b, p, s: (b, base + s * per + p), s=s))
    return [seg(0), seg(1), seg(2)], [u, u, u]


def _sb_scores(cfg, qh, ks, mask):
    z = _dg(qh, ks, 1, 1)
    sp = jnp.maximum(z, 0.0) + jnp.log(1.0 + jnp.exp(-jnp.abs(z)))
    return z, sp, jnp.where(mask, -sp, 0.0)


def _sb_running(x, tri, run, order):
    L = x.shape[0]
    parts = [None] * (x.shape[1] // L)
    for b in order:
        blk = x[:, b * L:(b + 1) * L]
        parts[b] = _dot2(blk, tri) + run
        run = run + jnp.sum(blk, axis=1, keepdims=True)
    return jnp.concatenate(parts, axis=1) if len(parts) > 1 else parts[0], run


def sb_fwd(cfg, u, col0, ride=None):
    nb = u.shape[0] // cfg.tp
    L, nc = cfg.chunk, cfg.nc
    hd = cfg.sb_head_dim
    scale = hd ** -0.5
    G = _sb_group(cfg)
    specs, args = _sb_specs(cfg, u, col0)

    def body(q_ref, k_ref, v_ref, y_ref, tot_ref):
        tr, tc = _tri_consts(L)
        after_tri = (tr > tc).astype(BF16)
        rid = lax.broadcasted_iota(jnp.int32, (L, G * L), 0)
        cid = lax.broadcasted_iota(jnp.int32, (L, G * L), 1)
        lane = lax.broadcasted_iota(jnp.int32, (1, LANES), 1)
        mine = [jnp.logical_and(lane >= hh * hd, lane < (hh + 1) * hd) for hh in range(2)]

        def qloop(i, carry):
            qrows = pl.ds(pl.multiple_of(i * L, L), L)
            q = q_ref[qrows, :]
            qhs = [jnp.where(mine[hh], q * scale, 0.0).astype(BF16) for hh in range(2)]

            def gloop(gg, c):
                accs, runs = c
                hi = i - gg * G
                start = jnp.maximum(hi - (G - 1), 0)
                krows = pl.ds(pl.multiple_of(start * L, L), G * L)
                ks = k_ref[krows, :].astype(BF16)
                vs = v_ref[krows, :].astype(BF16)
                kpos = start * L + cid
                mask = jnp.logical_and(jnp.logical_and(kpos < i * L + rid, kpos >= cfg.pad), kpos < (hi + 1) * L)
                new_accs, new_runs = [], []
                for hh in range(2):
                    z, sp, l1m = _sb_scores(cfg, qhs[hh], ks, mask)
                    after, run = _sb_running(l1m, after_tri, runs[hh], range(G - 1, -1, -1))
                    w = jnp.where(mask, jnp.exp(z - sp + after), 0.0)
                    new_accs.append(accs[hh] + _dg(w.astype(BF16), vs, 1, 0))
                    new_runs.append(run)
                return tuple(new_accs), tuple(new_runs)

            zero_acc, zero_run = jnp.zeros((L, LANES), F32), jnp.zeros((L, 1), F32)
            accs, runs = lax.fori_loop(0, (i + G) // G, gloop, ((zero_acc, zero_acc), (zero_run, zero_run)))
            for hh in range(2):
                tot_ref[0, hh, qrows, :] = runs[hh]
            y_ref[qrows, :] = jnp.where(lane < hd, accs[0], accs[1]).astype(y_ref.dtype)
            return carry

        lax.fori_loop(0, nc, qloop, 0)

    (y, tot), landed = call_with_ride(
        body, name="sb_fwd", grid=(nb, cfg.sb_w // LANES), in_specs=specs, args=args,
        out_specs=[pl.BlockSpec((cfg.tp, LANES), lambda b, p: (b, p)),
                   pl.BlockSpec((1, 2, cfg.tp, 1), lambda b, p: (b, p, 0, 0))],
        out_shape=[jax.ShapeDtypeStruct((u.shape[0], cfg.sb_w), BF16),
                   jax.ShapeDtypeStruct((nb, cfg.sb_heads, cfg.tp, 1), F32)], ride=ride)
    return y, tot, landed


def sb_bwd(cfg, u, col0, tot, dy, dy_col0, ride=None):
    nb = u.shape[0] // cfg.tp
    L, nc = cfg.chunk, cfg.nc
    hd = cfg.sb_head_dim
    scale = hd ** -0.5
    specs, args = _sb_specs(cfg, u, col0)
    dy_base = dy_col0 // LANES
    specs += [pl.BlockSpec((1, 2, cfg.tp, 1), lambda b, p: (b, p, 0, 0)),
              pl.BlockSpec((cfg.tp, LANES), lambda b, p: (b, dy_base + p))]

    G = _sb_group(cfg)

    def body(q_ref, k_ref, v_ref, tot_ref, dy_ref, dq_ref, dk_ref, dv_ref, dk_acc, dv_acc):
        tr, tc = _tri_consts(L)
        upto_tri = (tr <= tc).astype(BF16)
        before_tri = (tr < tc).astype(BF16)
        rid = lax.broadcasted_iota(jnp.int32, (L, G * L), 0)
        cid = lax.broadcasted_iota(jnp.int32, (L, G * L), 1)
        lane = lax.broadcasted_iota(jnp.int32, (1, LANES), 1)
        mine = [jnp.logical_and(lane >= hh * hd, lane < (hh + 1) * hd) for hh in range(2)]
        dk_acc[...] = jnp.zeros_like(dk_acc)
        dv_acc[...] = jnp.zeros_like(dv_acc)

        def qloop(i, carry):
            qrows = pl.ds(pl.multiple_of(i * L, L), L)
            q = q_ref[qrows, :]
            dy = dy_ref[qrows, :].astype(F32)
            qhs = [jnp.where(mine[hh], q * scale, 0.0).astype(BF16) for hh in range(2)]
            dyhs = [jnp.where(mine[hh], dy, 0.0).astype(BF16) for hh in range(2)]
            tots = [tot_ref[0, hh, qrows, :] for hh in range(2)]

            def gloop(g, c):
                dq, run_ls, run_as = c
                lo = g * G
                start = jnp.minimum(lo, nc - G)
                krows = pl.ds(pl.multiple_of(start * L, L), G * L)
                k = k_ref[krows, :]
                ks = k.astype(BF16)
                vs = v_ref[krows, :].astype(BF16)
                kpos = start * L + cid
                mask = jnp.logical_and(jnp.logical_and(kpos < i * L + rid, kpos >= cfg.pad), kpos >= lo * L)
                new_ls, new_as = [], []
                dv_add = jnp.zeros((G * L, LANES), F32)
                dk_add = jnp.zeros((G * L, LANES), F32)
                for hh in range(2):
                    z, sp, l1m = _sb_scores(cfg, qhs[hh], ks, mask)
                    upto, run_l = _sb_running(l1m, upto_tri, run_ls[hh], range(G))
                    w = jnp.where(mask, jnp.exp(z - sp + (tots[hh] - upto)), 0.0)
                    da = w * _dg(dyhs[hh], vs, 1, 1)
                    d_l1m, run_a = _sb_running(da, before_tri, run_as[hh], range(G))
                    sg = jnp.exp(z - sp)
                    dz = jnp.where(mask, da * (1.0 - sg) - d_l1m * sg, 0.0).astype(BF16)
                    dv_add = dv_add + _dg(w.astype(BF16), dyhs[hh], 0, 0)
                    dk_add = dk_add + _dg(dz, qhs[hh], 0, 0)
                    dq = dq + _dg(dz, jnp.where(mine[hh], k, 0.0).astype(BF16), 1, 0)
                    new_ls.append(run_l)
                    new_as.append(run_a)
                dv_acc[krows, :] += dv_add
                dk_acc[krows, :] += dk_add
                return dq, tuple(new_ls), tuple(new_as)

            zero = jnp.zeros((L, 1), F32)
            dq, _, _ = lax.fori_loop(0, (i + G) // G, gloop, (jnp.zeros((L, LANES), F32), (zero, zero), (zero, zero)))
            dq_ref[qrows, :] = (dq * scale).astype(dq_ref.dtype)
            return carry

        lax.fori_loop(0, nc, qloop, 0)
        dk_ref[...] = dk_acc[...].astype(dk_ref.dtype)
        dv_ref[...] = dv_acc[...].astype(dv_ref.dtype)

    out_spec = pl.BlockSpec((cfg.tp, LANES), lambda b, p: (b, p))
    seg_shape = jax.ShapeDtypeStruct((u.shape[0], cfg.sb_w), BF16)
    return call_with_ride(
        body, name="sb_bwd", grid=(nb, cfg.sb_w // LANES), in_specs=specs, args=args + [tot, dy], out_specs=[out_spec] * 3,
        out_shape=[seg_shape] * 3, scratch_shapes=[pltpu.VMEM((cfg.tp, LANES), F32), pltpu.VMEM((cfg.tp, LANES), F32)], ride=ride)


def _lru_cols(cfg):
    return _tile(cfg.lru_width, 256)


def lru_scan_fwd(cfg, a, b):
    nb = a.shape[0] // cfg.tp
    cw = _lru_cols(cfg)
    spec = pl.BlockSpec((cfg.tp, cw), lambda s, c: (s, c))

    def body(a_ref, b_ref, h_ref):
        def step(i, h):
            rows = pl.ds(pl.multiple_of(i * SUBLANES, SUBLANES), SUBLANES)
            at, bt = a_ref[rows, :], b_ref[rows, :]
            outs = []
            for r in range(SUBLANES):
                h = at[r:r + 1, :] * h + bt[r:r + 1, :]
                outs.append(h)
            h_ref[rows, :] = jnp.concatenate(outs, axis=0)
            return h

        lax.fori_loop(0, cfg.tp // SUBLANES, step, jnp.zeros((1, cw), F32))

    return pl.pallas_call(
        body, grid=(nb, cfg.lru_width // cw), in_specs=[spec, spec], out_specs=spec,
        out_shape=jax.ShapeDtypeStruct(a.shape, F32), name="lru_scan_fwd", compiler_params=_cparams())(a, b)


def lru_scan_bwd(cfg, a, h, dh):
    nb = a.shape[0] // cfg.tp
    cw = _lru_cols(cfg)
    nt = cfg.tp // SUBLANES
    spec = pl.BlockSpec((cfg.tp, cw), lambda s, c: (s, c))

    def body(a_ref, h_ref, dh_ref, da_ref, db_ref):
        def step(k, c):
            i = nt - 1 - k
            rows = pl.ds(pl.multiple_of(i * SUBLANES, SUBLANES), SUBLANES)
            prev = pl.ds(pl.multiple_of(jnp.maximum(i - 1, 0) * SUBLANES, SUBLANES), SUBLANES)
            at, ht, dht = a_ref[rows, :], h_ref[rows, :], dh_ref[rows, :]
            h_before = jnp.where(i > 0, h_ref[prev, :][SUBLANES - 1:SUBLANES, :], 0.0)
            das, dbs = [None] * SUBLANES, [None] * SUBLANES
            for r in range(SUBLANES - 1, -1, -1):
                g = dht[r:r + 1, :] + c
                dbs[r] = g
                das[r] = g * (ht[r - 1:r, :] if r > 0 else h_before)
                c = at[r:r + 1, :] * g
            da_ref[rows, :] = jnp.concatenate(das, axis=0)
            db_ref[rows, :] = jnp.concatenate(dbs, axis=0)
            return c

        lax.fori_loop(0, nt, step, jnp.zeros((1, cw), F32))

    return pl.pallas_call(
        body, grid=(nb, cfg.lru_width // cw), in_specs=[spec] * 3, out_specs=[spec] * 2,
        out_shape=[jax.ShapeDtypeStruct(a.shape, F32)] * 2, name="lru_scan_bwd", compiler_params=_cparams())(a, h, dh)


def final_loss(cfg, h, target, norm_g):
    assert cfg.seq % cfg.chunk == 0 and cfg.n_meta + cfg.pad == cfg.chunk
    L, nc, d = cfg.chunk, cfg.nc, cfg.d_model
    per_seq = cfg.seq // L

    def tgt_map(i):
        return ((i // nc) * per_seq + jnp.maximum(i % nc - 1, 0), 0)

    def body(h_ref, t_ref, g_ref, loss_ref, dh_ref, dg_ref):
        i = pl.program_id(0)
        real = jnp.where(i % nc == 0, 0.0, 1.0)
        tgt = t_ref[...]

        def loss_fn(hv, g):
            y = hv * lax.rsqrt(jnp.mean(hv * hv, axis=-1, keepdims=True) + EPS) * g
            return 0.5 * real * jnp.sum(jnp.mean(jnp.square(y - tgt), axis=-1))

        val, (dh, dg) = jax.value_and_grad(loss_fn, argnums=(0, 1))(h_ref[...], g_ref[...])

        @pl.when(i == 0)
        def _():
            loss_ref[...] = jnp.zeros_like(loss_ref)
            dg_ref[...] = jnp.zeros_like(dg_ref)

        loss_ref[...] += jnp.broadcast_to(val, loss_ref.shape)
        dg_ref[...] += dg
        dh_ref[...] = dh

    return pl.pallas_call(
        body, grid=(h.shape[0] // L,),
        in_specs=[pl.BlockSpec((L, d), lambda i: (i, 0)), pl.BlockSpec((L, d), tgt_map), pl.BlockSpec((1, d), lambda i: (0, 0))],
        out_specs=[pl.BlockSpec((SUBLANES, LANES), lambda i: (0, 0)), pl.BlockSpec((L, d), lambda i: (i, 0)),
                   pl.BlockSpec((1, d), lambda i: (0, 0))],
        out_shape=[jax.ShapeDtypeStruct((SUBLANES, LANES), F32), jax.ShapeDtypeStruct(h.shape, F32),
                   jax.ShapeDtypeStruct((1, d), F32)],
        name="final_loss", compiler_params=_cparams(dimension_semantics=("arbitrary",)))(h, target, norm_g)


N_PEER = 7


class Ride(NamedTuple):
    arrays: list
    same: list


def _ride_copies(ride_in, ride_out, same, sems, sending):
    send_sems, recv_sems, local_sems = sems
    n = len(ride_in)
    x, y, c = lax.axis_index("x"), lax.axis_index("y"), lax.axis_index("c")
    me = 4 * x + 2 * y + c

    def slab(w, dest):
        return ride_in[w] if same[w] else ride_in[w].at[dest]

    local = [pltpu.make_async_copy(slab(w, me), ride_out[w].at[me], local_sems.at[w]) for w in range(n)]
    remote = []
    for r in range(1, N_PEER + 1):
        peer = (1 - x if r & 4 else x, 1 - y if r & 2 else y, 1 - c if r & 1 else c)
        pidx = 4 * peer[0] + 2 * peer[1] + peer[2]
        for w in range(n):
            remote.append(pltpu.make_async_remote_copy(
                src_ref=slab(w, pidx), dst_ref=ride_out[w].at[me if sending else pidx], send_sem=send_sems.at[w, r - 1],
                recv_sem=recv_sems.at[w, r - 1], device_id=peer, device_id_type=pl.DeviceIdType.MESH))
    return local, remote


def _ride_start(ride_in, ride_out, same, sems):
    local, remote = _ride_copies(ride_in, ride_out, same, sems, True)
    for cp in local + remote:
        cp.start()


def _ride_wait(ride_in, ride_out, same, sems):
    local, remote = _ride_copies(ride_in, ride_out, same, sems, False)
    for cp in remote:
        cp.wait_recv()
    for cp in remote:
        cp.wait_send()
    for cp in local:
        cp.wait()


def call_with_ride(body, *, name, grid, in_specs, args, out_specs, out_shape, scratch_shapes=(), ride=None, **cparams):
    out_specs, out_shape, scratch_shapes = list(out_specs), list(out_shape), list(scratch_shapes)
    if ride is None:
        res = pl.pallas_call(body, grid=grid, in_specs=list(in_specs), out_specs=out_specs, out_shape=out_shape,
                             scratch_shapes=scratch_shapes, name=name, compiler_params=_cparams(**cparams))(*args)
        return list(res), []
    arrays, same = ride
    n, n_in, n_out, n_scr = len(arrays), len(args), len(out_shape), len(scratch_shapes)
    hbm = pl.BlockSpec(memory_space=pl.ANY)
    land_shape = [jax.ShapeDtypeStruct((8,) + (a.shape if s else a.shape[1:]), a.dtype) for a, s in zip(arrays, same)]

    def wrapped(*refs):
        ins, ride_in = refs[:n_in], refs[n_in:n_in + n]
        outs, ride_out = refs[n_in + n:n_in + n + n_out], refs[n_in + n + n_out:n_in + 2 * n + n_out]
        scr, sems = refs[n_in + 2 * n + n_out:n_in + 2 * n + n_out + n_scr], refs[n_in + 2 * n + n_out + n_scr:]
        first, last = True, True
        for axis, size in enumerate(grid):
            first = jnp.logical_and(first, pl.program_id(axis) == 0)
            last = jnp.logical_and(last, pl.program_id(axis) == size - 1)
        if grid:
            pl.when(first)(lambda: _ride_start(ride_in, ride_out, same, sems))
        else:
            _ride_start(ride_in, ride_out, same, sems)
        body(*ins, *outs, *scr)
        if grid:
            pl.when(last)(lambda: _ride_wait(ride_in, ride_out, same, sems))
        else:
            _ride_wait(ride_in, ride_out, same, sems)

    sems = [pltpu.SemaphoreType.DMA((n, N_PEER)), pltpu.SemaphoreType.DMA((n, N_PEER)), pltpu.SemaphoreType.DMA((n,))]
    cparams["dimension_semantics"] = ("arbitrary",) * len(grid)
    res = pl.pallas_call(wrapped, grid=grid, in_specs=list(in_specs) + [hbm] * n, out_specs=out_specs + [hbm] * n,
                         out_shape=out_shape + land_shape, scratch_shapes=scratch_shapes + sems, name=name,
                         compiler_params=_cparams(**cparams))(*args, *arrays)
    return list(res[:n_out]), list(res[n_out:])


def exchange(name, ride):
    return call_with_ride(lambda: None, name=name, grid=(), in_specs=[], args=[], out_specs=[], out_shape=[], ride=ride)[1]


def _row_tile(r, mult, cap):
    best = None
    for t in range(mult, min(r, cap) + 1, mult):
        if r % t == 0:
            best = t
    return best if best is not None else r


def adamw(name, parts, w, m, v):
    r, c = w.shape
    tr = _row_tile(r, 16, 256)
    spec = pl.BlockSpec((tr, c), lambda i: (i, 0))

    def body(p_ref, w_ref, m_ref, v_ref, g_ref, d_ref, m2_ref, v2_ref):
        g = p_ref[0].astype(F32)
        for dev in range(1, 8):
            g = g + p_ref[dev].astype(F32)
        m2 = ADAM_B1 * m_ref[...] + (1.0 - ADAM_B1) * g
        v2 = ADAM_B2 * v_ref[...] + (1.0 - ADAM_B2) * jnp.square(g)
        m_hat = m2 / (1.0 - ADAM_B1 ** ADAM_STEP)
        v_hat = v2 / (1.0 - ADAM_B2 ** ADAM_STEP)
        g_ref[...] = g
        d_ref[...] = -ADAM_LR * (m_hat / (jnp.sqrt(v_hat) + ADAM_EPS) + ADAM_WD * w_ref[...])
        m2_ref[...] = m2
        v2_ref[...] = v2

    return pl.pallas_call(
        body, grid=(r // tr,), in_specs=[pl.BlockSpec((8, tr, c), lambda i: (0, i, 0)), spec, spec, spec],
        out_specs=[spec] * 4, out_shape=[jax.ShapeDtypeStruct((r, c), F32)] * 4, name=name, compiler_params=_cparams())(parts, w, m, v)


PACK_ROWS = 256


def _pack(arrs):
    flat = jnp.concatenate([a.reshape(-1).astype(F32) for a in arrs])
    quantum = PACK_ROWS * LANES
    total = -(-flat.shape[0] // quantum) * quantum
    return jnp.pad(flat, (0, total - flat.shape[0])).reshape(-1, LANES)


def _unpack(packed, shapes):
    flat = packed.reshape(-1)
    out, off = [], 0
    for s in shapes:
        n = int(np.prod(s))
        out.append(flat[off:off + n].reshape(s))
        off += n
    return out


def _pad_lanes(vec):
    return jnp.pad(vec.astype(F32), (0, LANES - vec.shape[0]))[None, :]


def make_step(cfg):
    I, BC, H, RW, SW, LW, F, D = (cfg.ssd_inner, cfg.ssd_bc, cfg.ssd_heads, cfg.ret_w, cfg.sb_w, cfg.lru_width,
                                  cfg.ffn_dim, cfg.d_model)
    ND = cfg.n_dev
    q_off = 2 * I + 2 * BC
    dt_off = q_off + 4 * RW
    dt_w = (-dt_off) % 1024 or 1024

    def gather_cols(g):
        return jnp.transpose(g, (1, 0, 2)).reshape(g.shape[1], -1)

    def scatter_cols(full):
        r, c = full.shape
        return jnp.transpose(full.reshape(r, ND, c // ND), (1, 0, 2))

    def step(p, m_in, v_in, x, loss_target):
        nb = x.shape[0]
        rows = nb * cfg.tp

        small_shapes = [p[n].shape for n in SMALL_SHARDED]
        wfull = {}

        halves = {}

        def gather_ride(names, extra=()):
            own = []
            for n in names:
                if isinstance(n, tuple):
                    half = p[n[0]].shape[0] // 2
                    own.append(p[n[0]][n[1] * half:(n[1] + 1) * half].astype(BF16))
                else:
                    own.append(p[n].astype(BF16))
            own += list(extra)
            return Ride(own, [True] * len(own))

        def gather_finish(names, landed):
            for n, g in zip(names, landed):
                if isinstance(n, tuple):
                    halves.setdefault(n[0], {})[n[1]] = g
                    if len(halves[n[0]]) < 2:
                        continue
                    n, g = n[0], jnp.concatenate([halves[n[0]][0], halves[n[0]][1]], axis=1)
                wfull[n] = gather_cols(g) if n in BIG_COL else g.reshape(-1, g.shape[-1])

        landed = exchange("gather_l0_mixer", gather_ride(['l0_w_in'], [_pack([p[n] for n in SMALL_SHARDED])]))
        gather_finish(['l0_w_in'], landed)
        small_parts = [_unpack(landed[-1][d], small_shapes) for d in range(ND)]
        sfull = {n: jnp.concatenate([small_parts[d][k] for d in range(ND)], axis=1) for k, n in enumerate(SMALL_SHARDED)}

        w0 = wfull['l0_w_in']
        w0a = jnp.concatenate([w0[:, :q_off], w0[:, q_off + H:], jnp.pad(w0[:, q_off:q_off + H], ((0, 0), (0, dt_w - H)))], axis=1)

        sel_hd = (jnp.arange(LANES)[:, None] == (jnp.arange(I)[None, :] // cfg.ssd_head_dim)).astype(F32)
        sel_lane = (jnp.arange(LANES)[:, None] == (jnp.arange(H * LANES)[None, :] // LANES)).astype(F32)
        rconst = _ret_consts(cfg)
        row2 = lambda vec: vec.astype(F32)[None, :]

        meta = jnp.broadcast_to(sfull['meta_tokens'][None], (nb, cfg.n_meta, D))
        h0 = jnp.concatenate([jnp.zeros((nb, cfg.pad, D), F32), meta, x], axis=1).reshape(rows, D)

        light = light_rows(rows)

        def norm_fwd(name, h, g):
            return stage_fwd(name, cfg, _norm_fn, [Row(h, D)], [Par(row2(g))], [(D, BF16)], tr=light)[0]

        def norm_bwd(name, h, g, dhn, dh_next):
            fn = lambda valid, hv, gv: (_norm_fn(valid, hv, gv)[0], hv)
            (dh,), (dg,) = stage_bwd(name, cfg, fn, [Row(h, D)], [Par(row2(g))], [[Row(dhn, D)], [Row(dh_next, D)]], [F32],
                                     tr=light)
            return dh, dg[0]

        hn0 = norm_fwd("l0_mix_norm", h0, p['l0_mix_norm'])
        u0, landed = matmul("l0_in_proj", hn0, w0a, ride=gather_ride(['l0_w_out']))
        gather_finish(['l0_w_out'], landed)
        cw, cb = sfull['l0_ssd_conv_w'], p['l0_ssd_conv_b']
        ssd_pre = _make_ssd_pre(cfg)
        ssd_post = _make_ssd_post(cfg)
        pre_rows = [Row(u0, I, 1, tail=True), Row(u0, BC, (2 * I) // BC, tail=True), Row(u0, BC, (2 * I) // BC + 1, tail=True),
                    Row(u0, LANES, dt_off // LANES)]
        pre_pars = [Par(cw[:, :I]), Par(cw[:, I:I + BC]), Par(cw[:, I + BC:]), Par(row2(cb[:I])), Par(row2(cb[I:I + BC])),
                    Par(row2(cb[I + BC:])), Par(_pad_lanes(p['l0_ssd_dt_bias'])), Par(_pad_lanes(p['l0_ssd_a_log'])),
                    Par(sel_hd, diff=False), Par(sel_lane, diff=False)]
        (xs, bm, cm, xdt, adt_b), landed = stage_fwd("ssd_pre", cfg, ssd_pre, pre_rows, pre_pars,
                                                     [(I, F32), (BC, F32), (BC, F32), (I, F32), (H * LANES, F32)],
                                                     ride=gather_ride([('l0_ffn_w_in', 0)]))
        gather_finish([('l0_ffn_w_in', 0)], landed)
        y_raw, landed = ssd_scan_fwd(cfg, xdt, bm, cm, adt_b, ride=gather_ride([('l0_ffn_w_in', 1)]))
        gather_finish([('l0_ffn_w_in', 1)], landed)
        post_rows = [Row(y_raw, I), Row(xs, I), Row(u0, I, 0)]
        post_pars = [Par(_pad_lanes(p['l0_ssd_d'])), Par(row2(p['l0_ssd_norm'])), Par(sel_hd, diff=False)]
        (y_ssd,) = stage_fwd("ssd_post", cfg, ssd_post, post_rows, post_pars, [(I, BF16)], tr=light)
        ret_g = row2(p['l0_ret_norm'])
        y_ret, landed = ret_fwd(cfg, u0, q_off, ret_g, rconst, ride=gather_ride(['l0_ffn_w_out']))
        gather_finish(['l0_ffn_w_out'], landed)
        ycat0 = jnp.concatenate([y_ssd, y_ret], axis=1)
        h1 = matmul("l0_out_proj", ycat0, wfull['l0_w_out'], resid=h0, cfg=cfg)

        def ffn_fwd(tag, h, norm_g, w_in, conv_w, conv_b, w_out, gather_names=(), act_gather_names=()):
            hn = norm_fwd(tag + "_ffn_norm", h, norm_g)
            if gather_names:
                u, landed = matmul(tag + "_ffn_in", hn, w_in, ride=gather_ride(list(gather_names)))
                gather_finish(gather_names, landed)
            else:
                u = matmul(tag + "_ffn_in", hn, w_in)
            rws = [Row(u, F, 0, tail=True), Row(u, F, 1, tail=True)]
            prs = [Par(conv_w[:, :F]), Par(conv_w[:, F:]), Par(row2(conv_b[:F])), Par(row2(conv_b[F:]))]
            if act_gather_names:
                (act,), landed = stage_fwd(tag + "_ffn_act", cfg, _ffn_act_fn, rws, prs, [(F, BF16)],
                                           ride=gather_ride(list(act_gather_names)))
                gather_finish(act_gather_names, landed)
            else:
                (act,) = stage_fwd(tag + "_ffn_act", cfg, _ffn_act_fn, rws, prs, [(F, BF16)])
            h_out = matmul(tag + "_ffn_out", act, w_out, resid=h, cfg=cfg)
            return h_out, (hn, u, rws, prs, act)

        h2, ffn0_saved = ffn_fwd("l0", h1, p['l0_ffn_norm'], wfull['l0_ffn_w_in'], sfull['l0_ffn_conv_w'], p['l0_ffn_conv_b'],
                                 wfull['l0_ffn_w_out'], gather_names=[('l1_w_in', 0)], act_gather_names=[('l1_w_in', 1)])

        hn2 = norm_fwd("l1_mix_norm", h2, p['l1_mix_norm'])
        u1 = matmul("l1_in_proj", hn2, wfull['l1_w_in'])
        late_names = ['l1_w_out', 'l1_ffn_w_in', 'l1_ffn_w_out']
        y_sb, sb_tot, landed = sb_fwd(cfg, u1, 0, ride=gather_ride(late_names))
        gather_finish(late_names, landed)
        lru_pre = _make_lru_pre(cfg)
        gate_blk = (3 * SW) // LW
        lpre_rows = [Row(u1, LW, gate_blk + 1, tail=True)]
        lpre_pars = [Par(sfull['l1_lru_conv_w']), Par(row2(p['l1_lru_conv_b'])), Par(p['l1_lru_wa']), Par(row2(p['l1_lru_ba'])),
                     Par(p['l1_lru_wx']), Par(row2(p['l1_lru_bx'])), Par(row2(p['l1_lru_lambda']))]
        lru_a, lru_b = stage_fwd("lru_pre", cfg, lru_pre, lpre_rows, lpre_pars, [(LW, F32), (LW, F32)])
        lru_h = lru_scan_fwd(cfg, lru_a, lru_b)
        lpost_rows = [Row(lru_h, LW), Row(u1, LW, gate_blk)]
        (y_lru,) = stage_fwd("lru_post", cfg, _lru_post_fn, lpost_rows, [], [(LW, BF16)], tr=light)
        ycat1 = jnp.concatenate([y_sb, y_lru], axis=1)
        h3 = matmul("l1_out_proj", ycat1, wfull['l1_w_out'], resid=h2, cfg=cfg)
        h4, ffn1_saved = ffn_fwd("l1", h3, p['l1_ffn_norm'], wfull['l1_ffn_w_in'], sfull['l1_ffn_conv_w'], p['l1_ffn_conv_b'],
                                 wfull['l1_ffn_w_out'])

        loss_part, dh4, d_final = final_loss(cfg, h4, loss_target.reshape(nb * cfg.seq, D), row2(p['final_norm']))
        loss = lax.psum(loss_part[0, 0], ("x", "y", "c"))
        gfull, grep = {}, {'final_norm': d_final[0]}

        def ffn_bwd(tag, h, norm_g, w_in, w_out, saved, dh_out, ride_names=(), act_ride_names=()):
            hn, u, rws, prs, act = saved
            dact = matmul(tag + "_ffn_out_dx", dh_out, w_out, tb=True, out_dtype=BF16)
            gfull[tag + '_ffn_w_out'] = matmul(tag + "_ffn_out_dw", act, dh_out, ta=True, out_dtype=BF16, tm_cap=512, tn_cap=512, tk_cap=4352)
            if act_ride_names:
                (dug, duu), (dwg, dwu, dbg, dbu), landed = stage_bwd(tag + "_ffn_act_bwd", cfg, _ffn_act_fn, rws, prs,
                                                                     [[Row(dact, F)]], [BF16, BF16],
                                                                     ride=grads_ride(list(act_ride_names)))
                parts_of.update(zip(act_ride_names, landed))
            else:
                (dug, duu), (dwg, dwu, dbg, dbu) = stage_bwd(tag + "_ffn_act_bwd", cfg, _ffn_act_fn, rws, prs,
                                                             [[Row(dact, F)]], [BF16, BF16])
            du = jnp.concatenate([dug, duu], axis=1)
            gfull[tag + '_ffn_conv_w'] = jnp.concatenate([dwg, dwu], axis=1)
            grep[tag + '_ffn_conv_b'] = jnp.concatenate([dbg, dbu], axis=1)[0]
            if ride_names:
                dhn, landed = matmul(tag + "_ffn_in_dx", du, w_in, tb=True, ride=grads_ride(list(ride_names)))
                parts_of.update(zip(ride_names, landed))
            else:
                dhn = matmul(tag + "_ffn_in_dx", du, w_in, tb=True)
            gfull[tag + '_ffn_w_in'] = matmul(tag + "_ffn_in_dw", hn, du, ta=True, out_dtype=BF16, tm_cap=512, tn_cap=512, tk_cap=4352)
            dh, grep[tag + '_ffn_norm'] = norm_bwd(tag + "_ffn_norm_bwd", h, norm_g, dhn, dh_out)
            return dh

        parts_of = {}

        def grads_ride(names, extra=(), extra_same=()):
            sends = [scatter_cols(gfull[n]) if n in BIG_COL else gfull[n].reshape(ND, -1, gfull[n].shape[-1]) for n in names]
            return Ride(sends + list(extra), [False] * len(names) + list(extra_same))

        dh3 = ffn_bwd("l1", h3, p['l1_ffn_norm'], wfull['l1_ffn_w_in'], wfull['l1_ffn_w_out'], ffn1_saved, dh4)

        dycat1 = matmul("l1_out_dx", dh3, wfull['l1_w_out'], tb=True, out_dtype=BF16)
        gfull['l1_w_out'] = matmul("l1_out_dw", ycat1, dh3, ta=True, out_dtype=BF16, tm_cap=512, tn_cap=512, tk_cap=4352)
        (d_lru_h, d_gate), _ = stage_bwd("lru_post_bwd", cfg, _lru_post_fn, lpost_rows, [], [[Row(dycat1, LW, SW // LW)]], [F32, BF16],
                                         tr=light)
        d_lru_a, d_lru_b = lru_scan_bwd(cfg, lru_a, lru_h, d_lru_h)
        (d_xr,), lgr = stage_bwd("lru_pre_bwd", cfg, lru_pre, lpre_rows, lpre_pars, [[Row(d_lru_a, LW)], [Row(d_lru_b, LW)]], [BF16])
        gfull['l1_lru_conv_w'] = lgr[0]
        grep.update({'l1_lru_conv_b': lgr[1][0], 'l1_lru_wa': lgr[2], 'l1_lru_ba': lgr[3][0], 'l1_lru_wx': lgr[4],
                     'l1_lru_bx': lgr[5][0], 'l1_lru_lambda': lgr[6][0]})
        names = ['l1_ffn_w_out', 'l1_ffn_w_in', 'l1_w_out']
        early = _pack([grep[n] for n in REP_EARLY])
        (d_q, d_k, d_v), landed = sb_bwd(cfg, u1, 0, sb_tot, dycat1, 0, ride=grads_ride(names, [early], [True]))
        parts_of.update(zip(names + ['rep_early'], landed))
        du1 = jnp.concatenate([d_q, d_k, d_v, d_gate, d_xr], axis=1)
        gfull['l1_w_in'] = matmul("l1_in_dw", hn2, du1, ta=True, out_dtype=BF16, tm_cap=512, tn_cap=512, tk_cap=4352)
        dhn2 = matmul("l1_in_dx", du1, wfull['l1_w_in'], tb=True)
        dh2, grep['l1_mix_norm'] = norm_bwd("l1_mix_norm_bwd", h2, p['l1_mix_norm'], dhn2, dh3)

        dh1 = ffn_bwd("l0", h1, p['l0_ffn_norm'], wfull['l0_ffn_w_in'], wfull['l0_ffn_w_out'], ffn0_saved, dh2,
                      ride_names=['l0_ffn_w_out'], act_ride_names=['l1_w_in'])

        dycat0 = matmul("l0_out_dx", dh1, wfull['l0_w_out'], tb=True, out_dtype=BF16)
        gfull['l0_w_out'] = matmul("l0_out_dw", ycat0, dh1, ta=True, out_dtype=BF16, tm_cap=512, tn_cap=512, tk_cap=4352)
        (d_yraw, d_xs1, d_z), (d_dskip, d_ssdnorm) = stage_bwd("ssd_post_bwd", cfg, ssd_post, post_rows, post_pars,
                                                               [[Row(dycat0, I, 0)]], [F32, F32, BF16])
        names = ['l0_ffn_w_in', 'l0_w_out']
        (d_xdt, d_bm, d_cm, d_adt), landed = ssd_scan_bwd(cfg, xdt, bm, cm, adt_b, d_yraw, ride=grads_ride(names))
        parts_of.update(zip(names, landed))
        (d_xs, d_b, d_c, d_dt), sgr = stage_bwd("ssd_pre_bwd", cfg, ssd_pre, pre_rows, pre_pars,
                                                 [[Row(d_xs1, I)], [Row(d_bm, BC)], [Row(d_cm, BC)], [Row(d_xdt, I)], [Row(d_adt, H * LANES)]],
                                                 [BF16] * 4)
        gfull['l0_ssd_conv_w'] = jnp.concatenate(sgr[0:3], axis=1)
        grep.update({'l0_ssd_conv_b': jnp.concatenate(sgr[3:6], axis=1)[0], 'l0_ssd_dt_bias': sgr[6][0, :H],
                     'l0_ssd_a_log': sgr[7][0, :H], 'l0_ssd_d': d_dskip[0, :H], 'l0_ssd_norm': d_ssdnorm[0]})
        (d_rq, d_rk, d_rv, d_rg), d_retnorm = ret_bwd(cfg, u0, q_off, ret_g, rconst, dycat0, I)
        grep['l0_ret_norm'] = d_retnorm[0]
        du0 = jnp.concatenate([d_z, d_xs, d_b, d_c, d_rq, d_rk, d_rv, d_rg, d_dt, jnp.zeros((rows, dt_w - LANES), BF16)], axis=1)
        dw0a = matmul("l0_in_dw", hn0, du0, ta=True, out_dtype=BF16, tm_cap=512, tn_cap=512, tk_cap=4352)
        gfull['l0_w_in'] = jnp.concatenate([dw0a[:, :q_off], dw0a[:, dt_off:dt_off + H], dw0a[:, q_off:dt_off]], axis=1)
        dhn0, landed = matmul("l0_in_dx", du0, w0a, tb=True, ride=grads_ride(['l0_w_in']))
        parts_of['l0_w_in'] = landed[0]
        dh0, grep['l0_mix_norm'] = norm_bwd("l0_mix_norm_bwd", h0, p['l0_mix_norm'], dhn0, dh1)

        dh0 = dh0.reshape(nb, cfg.tp, D)
        grad_x = dh0[:, cfg.chunk:, :]
        gfull['meta_tokens'] = jnp.sum(dh0[:, cfg.pad:cfg.chunk, :], axis=0)

        flat = jnp.concatenate([scatter_cols(gfull[n].astype(F32)).reshape(ND, -1) for n in SMALL_SHARDED], axis=1)
        quantum = PACK_ROWS * LANES
        small_send = jnp.pad(flat, ((0, 0), (0, -flat.shape[1] % quantum))).reshape(ND, -1, LANES)
        late = _pack([grep[n] for n in REP_LATE])
        landed = exchange("exchange_last_grads", Ride([small_send, late], [False, True]))
        parts_of.update(zip(['small', 'rep_late'], landed))

        grad, delta, new_m, new_v = {}, {}, {}, {}
        for n in BIG:
            grad[n], delta[n], new_m[n], new_v[n] = adamw("adamw_" + n, parts_of[n], p[n], m_in[n], v_in[n])
        for names, tag in ((SMALL_SHARDED, 'small'), (REP_EARLY, 'rep_early'), (REP_LATE, 'rep_late')):
            shapes = [p[n].shape for n in names]
            res = adamw("adamw_" + tag, parts_of[tag], _pack([p[n] for n in names]), _pack([m_in[n] for n in names]),
                        _pack([v_in[n] for n in names]))
            for dst, packed in zip((grad, delta, new_m, new_v), res):
                for n, a in zip(names, _unpack(packed, shapes)):
                    dst[n] = a
        return (loss, grad_x, *[grad[n] for n in WEIGHTS], *[delta[n] for n in WEIGHTS], *[new_m[n] for n in WEIGHTS],
                *[new_v[n] for n in WEIGHTS])

    return step


_STEP = make_step(CFG)


def kernel(x, meta_tokens, l0_mix_norm, l0_w_in, l0_ssd_conv_w, l0_ssd_conv_b, l0_ssd_dt_bias, l0_ssd_a_log, l0_ssd_d, l0_ssd_norm, l0_ret_norm, l0_w_out, l0_ffn_norm, l0_ffn_w_in, l0_ffn_conv_w, l0_ffn_conv_b, l0_ffn_w_out, l1_mix_norm, l1_w_in, l1_lru_conv_w, l1_lru_conv_b, l1_lru_wa, l1_lru_ba, l1_lru_wx, l1_lru_bx, l1_lru_lambda, l1_w_out, l1_ffn_norm, l1_ffn_w_in, l1_ffn_conv_w, l1_ffn_conv_b, l1_ffn_w_out, final_norm, loss_target, m_meta_tokens, m_l0_mix_norm, m_l0_w_in, m_l0_ssd_conv_w, m_l0_ssd_conv_b, m_l0_ssd_dt_bias, m_l0_ssd_a_log, m_l0_ssd_d, m_l0_ssd_norm, m_l0_ret_norm, m_l0_w_out, m_l0_ffn_norm, m_l0_ffn_w_in, m_l0_ffn_conv_w, m_l0_ffn_conv_b, m_l0_ffn_w_out, m_l1_mix_norm, m_l1_w_in, m_l1_lru_conv_w, m_l1_lru_conv_b, m_l1_lru_wa, m_l1_lru_ba, m_l1_lru_wx, m_l1_lru_bx, m_l1_lru_lambda, m_l1_w_out, m_l1_ffn_norm, m_l1_ffn_w_in, m_l1_ffn_conv_w, m_l1_ffn_conv_b, m_l1_ffn_w_out, m_final_norm, v_meta_tokens, v_l0_mix_norm, v_l0_w_in, v_l0_ssd_conv_w, v_l0_ssd_conv_b, v_l0_ssd_dt_bias, v_l0_ssd_a_log, v_l0_ssd_d, v_l0_ssd_norm, v_l0_ret_norm, v_l0_w_out, v_l0_ffn_norm, v_l0_ffn_w_in, v_l0_ffn_conv_w, v_l0_ffn_conv_b, v_l0_ffn_w_out, v_l1_mix_norm, v_l1_w_in, v_l1_lru_conv_w, v_l1_lru_conv_b, v_l1_lru_wa, v_l1_lru_ba, v_l1_lru_wx, v_l1_lru_bx, v_l1_lru_lambda, v_l1_w_out, v_l1_ffn_norm, v_l1_ffn_w_in, v_l1_ffn_conv_w, v_l1_ffn_conv_b, v_l1_ffn_w_out, v_final_norm):
    args = locals()
    p = {n: args[n] for n in WEIGHTS}
    m_in = {n: args["m_" + n] for n in WEIGHTS}
    v_in = {n: args["v_" + n] for n in WEIGHTS}
    return _STEP(p, m_in, v_in, x, loss_target)
```

```python
import functools
from typing import NamedTuple

import numpy as np
import jax
import jax.numpy as jnp
from jax import lax
from jax.experimental import pallas as pl
from jax.experimental.pallas import tpu as pltpu

F32 = jnp.float32
BF16 = jnp.bfloat16
EPS = 1e-6
LRU_C = 8.0
NEG = -1e30
SUBLANES = 8
LANES = 128
VMEM_LIMIT = 56 * 1024 * 1024
CHUNK_UNROLL = 4

ADAM_LR = 0.001
ADAM_B1 = 0.9
ADAM_B2 = 0.999
ADAM_EPS = 1e-08
ADAM_WD = 0.01
ADAM_STEP = 10

WEIGHTS = ['meta_tokens', 'l0_mix_norm', 'l0_w_in', 'l0_ssd_conv_w', 'l0_ssd_conv_b', 'l0_ssd_dt_bias', 'l0_ssd_a_log',
           'l0_ssd_d', 'l0_ssd_norm', 'l0_ret_norm', 'l0_w_out', 'l0_ffn_norm', 'l0_ffn_w_in', 'l0_ffn_conv_w',
           'l0_ffn_conv_b', 'l0_ffn_w_out', 'l1_mix_norm', 'l1_w_in', 'l1_lru_conv_w', 'l1_lru_conv_b', 'l1_lru_wa',
           'l1_lru_ba', 'l1_lru_wx', 'l1_lru_bx', 'l1_lru_lambda', 'l1_w_out', 'l1_ffn_norm', 'l1_ffn_w_in',
           'l1_ffn_conv_w', 'l1_ffn_conv_b', 'l1_ffn_w_out', 'final_norm']
BIG_COL = ['l0_w_in', 'l0_ffn_w_in', 'l1_w_in', 'l1_ffn_w_in']
BIG_ROW = ['l0_w_out', 'l0_ffn_w_out', 'l1_w_out', 'l1_ffn_w_out']
BIG = BIG_COL + BIG_ROW
SMALL_SHARDED = ['meta_tokens', 'l0_ssd_conv_w', 'l0_ffn_conv_w', 'l1_lru_conv_w', 'l1_ffn_conv_w']
REPLICATED = [w for w in WEIGHTS if w not in BIG and w not in SMALL_SHARDED]
REP_EARLY = ['final_norm', 'l1_ffn_norm', 'l1_ffn_conv_b', 'l1_lru_conv_b', 'l1_lru_wa', 'l1_lru_ba', 'l1_lru_wx',
             'l1_lru_bx', 'l1_lru_lambda']
REP_LATE = [w for w in REPLICATED if w not in REP_EARLY]


class Config(NamedTuple):
    d_model: int = 1024
    seq: int = 2048
    n_meta: int = 16
    chunk: int = 128
    ssd_heads: int = 16
    ssd_head_dim: int = 64
    ssd_groups: int = 4
    ssd_state: int = 128
    ret_heads: int = 4
    ret_dim: int = 256
    sb_heads: int = 16
    sb_head_dim: int = 64
    lru_width: int = 1024
    lru_blocks: int = 8
    ffn_dim: int = 2816
    n_dev: int = 8

    @property
    def t(self):
        return self.n_meta + self.seq

    @property
    def pad(self):
        return (-self.t) % self.chunk

    @property
    def tp(self):
        return self.t + self.pad

    @property
    def nc(self):
        return self.tp // self.chunk

    @property
    def ssd_inner(self):
        return self.ssd_heads * self.ssd_head_dim

    @property
    def ssd_gw(self):
        return self.ssd_inner // self.ssd_groups

    @property
    def ssd_bc(self):
        return self.ssd_groups * self.ssd_state

    @property
    def ret_w(self):
        return self.ret_heads * self.ret_dim

    @property
    def sb_w(self):
        return self.sb_heads * self.sb_head_dim

    @property
    def mix0_segs(self):
        return (self.ssd_inner, self.ssd_inner + 2 * self.ssd_bc, self.ssd_heads, self.ret_w, self.ret_w, self.ret_w, self.ret_w)


CFG = Config()


def _dg(a, b, ca, cb):
    return lax.dot_general(a, b, (((ca,), (cb,)), ((), ())), preferred_element_type=F32)


def _cot_a(b, ca, cb, g):
    return _dg(g, b, 1, 1 - cb) if ca == 1 else _dg(b, g, 1 - cb, 1)


def _cot_b(a, ca, cb, g):
    return _dg(a, g, 1 - ca, 0) if cb == 0 else _dg(g, a, 0, 1 - ca)


@functools.partial(jax.custom_vjp, nondiff_argnums=(2, 3))
def _bdot(a, b, ca, cb):
    return _dg(a.astype(BF16), b.astype(BF16), ca, cb)


def _bdot_fwd(a, b, ca, cb):
    return _bdot(a, b, ca, cb), (a, b)


def _bdot_bwd(ca, cb, res, g):
    a, b = res
    gb = g.astype(BF16)
    return _cot_a(b.astype(BF16), ca, cb, gb).astype(a.dtype), _cot_b(a.astype(BF16), ca, cb, gb).astype(b.dtype)


_bdot.defvjp(_bdot_fwd, _bdot_bwd)


def _dot(a, b):
    return _bdot(a, b, 1, 0)


def _dot_nt(a, b):
    return _bdot(a, b, 1, 1)


def _dot_tn(a, b):
    return _bdot(a, b, 0, 0)


def _split3(a):
    hi = a.astype(BF16)
    r1 = a - hi.astype(F32)
    mid = r1.astype(BF16)
    return hi, mid, (r1 - mid.astype(F32)).astype(BF16)


@functools.partial(jax.custom_vjp, nondiff_argnums=(2, 3, 4))
def _edot(a, b, ca, cb, const):
    if const == 'b':
        bb = b.astype(BF16)
        return sum(_dg(p, bb, ca, cb) for p in _split3(a))
    ab = a.astype(BF16)
    return sum(_dg(ab, p, ca, cb) for p in _split3(b))


def _edot_fwd(a, b, ca, cb, const):
    return _edot(a, b, ca, cb, const), (a, b)


def _edot_bwd(ca, cb, const, res, g):
    a, b = res
    if const == 'b':
        bb = b.astype(BF16)
        return sum(_cot_a(bb, ca, cb, p) for p in _split3(g)), jnp.zeros_like(b)
    ab = a.astype(BF16)
    return jnp.zeros_like(a), sum(_cot_b(ab, ca, cb, p) for p in _split3(g))


_edot.defvjp(_edot_fwd, _edot_bwd)


def _dot_exact01(a, sel):
    return _edot(a, sel, 1, 0, 'b')


def _silu(x):
    return x * jax.nn.sigmoid(x)


def _softplus(x):
    return jnp.maximum(x, 0.0) + jnp.log1p(jnp.exp(-jnp.abs(x)))


def _neg_expm1(x):
    series = -x * (1.0 + x * 0.5 * (1.0 + x / 3.0 * (1.0 + x * 0.25)))
    return jnp.where(x > -0.01, series, 1.0 - jnp.exp(x))


def _causal_conv(x, tail, w, b):
    taps, rows = w.shape[0], x.shape[0]
    xx = jnp.concatenate([tail, x], axis=0)
    y = b
    for k in range(taps):
        off = SUBLANES - (taps - 1 - k)
        y = y + w[k:k + 1, :] * xx[off:off + rows, :]
    return y


def _valid_chunk(cfg, chunk_idx):
    first = (chunk_idx % cfg.nc) == 0
    rid = lax.broadcasted_iota(jnp.int32, (cfg.chunk, 1), 0)
    return jnp.where(jnp.logical_and(first, rid < cfg.pad), 0.0, 1.0).astype(F32)


def _cparams(**kw):
    return pltpu.CompilerParams(vmem_limit_bytes=VMEM_LIMIT, **kw)


class Row(NamedTuple):
    arr: jax.Array
    width: int
    blk: int = 0
    tail: bool = False
    diff: bool = True


class Par(NamedTuple):
    arr: jax.Array
    diff: bool = True


def _row_specs(tr, rows, order):
    specs, args = [], []
    for r in rows:
        specs.append(pl.BlockSpec((tr, r.width), functools.partial(lambda i, b, o: (o(i), b), b=r.blk, o=order)))
        args.append(r.arr)
        if r.tail:
            per = tr // SUBLANES
            specs.append(pl.BlockSpec((SUBLANES, r.width),
                                      functools.partial(lambda i, b, o: (jnp.maximum(o(i) * per - 1, 0), b), b=r.blk, o=order)))
            args.append(r.arr)
    return specs, args


def _par_specs(pars):
    specs = [pl.BlockSpec(p.arr.shape, functools.partial(lambda i, nd: (0,) * nd, nd=p.arr.ndim)) for p in pars]
    return specs, [p.arr for p in pars]


def _valid_block(cfg, blk, tr, n_rows):
    row = blk * tr + lax.broadcasted_iota(jnp.int32, (tr, 1), 0)
    inpad = jnp.zeros((tr, 1), jnp.bool_)
    for s in range(n_rows // cfg.tp):
        inpad = jnp.logical_or(inpad, jnp.logical_and(row >= s * cfg.tp, row < s * cfg.tp + cfg.pad))
    return jnp.where(inpad, 0.0, 1.0).astype(F32)


def light_rows(n_rows):
    return n_rows // 8 if n_rows % (8 * 16) == 0 else None


def stage_fwd(name, cfg, fn, rows, pars, outs, tr=None, ride=None):
    tr = tr or cfg.chunk
    n_rows = rows[0].arr.shape[0]
    rspecs, rargs = _row_specs(tr, rows, lambda i: i)
    pspecs, pargs = _par_specs(pars)
    n_in = len(rargs) + len(pargs)

    def body(*refs):
        valid = _valid_block(cfg, pl.program_id(0), tr, n_rows)
        vals = [r[...].astype(F32) for r in refs[:n_in]]
        res = fn(valid, *vals)
        for o, v in zip(refs[n_in:], res):
            o[...] = v.astype(o.dtype)

    res, landed = call_with_ride(
        body, name=name, grid=(n_rows // tr,), in_specs=rspecs + pspecs, args=rargs + pargs,
        out_specs=[pl.BlockSpec((tr, w), lambda i: (i, 0)) for w, _ in outs],
        out_shape=[jax.ShapeDtypeStruct((n_rows, w), dt) for w, dt in outs], ride=ride)
    return (res, landed) if ride is not None else res


def stage_bwd(name, cfg, fn, rows, pars, douts, drow_dtypes, tr=None, ride=None):
    tr = tr or cfg.chunk
    n_rows = rows[0].arr.shape[0]
    n_blk = n_rows // tr
    order = lambda i: n_blk - 1 - i
    rspecs, rargs = _row_specs(tr, rows, order)
    pspecs, pargs = _par_specs(pars)
    pieces = [p for out in douts for p in out]
    dspecs, dargs = _row_specs(tr, pieces, order)
    n_r, n_p, n_d = len(rargs), len(pargs), len(dargs)
    diff_rows = [r for r in rows if r.diff]
    diff_pars = [p for p in pars if p.diff]
    tails = [r for r in diff_rows if r.tail]

    def body(*refs):
        in_refs = refs[:n_r + n_p]
        d_refs = refs[n_r + n_p:n_r + n_p + n_d]
        o_refs = refs[n_r + n_p + n_d:]
        drow_refs = o_refs[:len(diff_rows)]
        dpar_refs = o_refs[len(diff_rows):len(diff_rows) + len(diff_pars)]
        carry_refs = o_refs[len(diff_rows) + len(diff_pars):]
        step = pl.program_id(0)
        valid = _valid_block(cfg, order(step), tr, n_rows)
        vals = [r[...].astype(F32) for r in in_refs]
        slots, pos = [], 0
        for r in rows:
            if r.diff:
                slots.append(pos)
                if r.tail:
                    slots.append(pos + 1)
            pos += 2 if r.tail else 1
        for p in pars:
            if p.diff:
                slots.append(pos)
            pos += 1

        def g(*dv):
            full = list(vals)
            for s, v in zip(slots, dv):
                full[s] = v
            return tuple(fn(valid, *full))

        _, vjp = jax.vjp(g, *[vals[s] for s in slots])
        cts, k = [], 0
        for out in douts:
            parts = [d_refs[k + j][...].astype(F32) for j in range(len(out))]
            k += len(out)
            cts.append(parts[0] if len(parts) == 1 else jnp.concatenate(parts, axis=1))
        grads = list(vjp(tuple(cts)))

        @pl.when(step == 0)
        def _():
            for c in carry_refs:
                c[...] = jnp.zeros_like(c)
            for d in dpar_refs:
                d[...] = jnp.zeros_like(d)

        gi, ci = 0, 0
        for r, o in zip(diff_rows, drow_refs):
            dx = grads[gi]
            gi += 1
            if r.tail:
                dtail = grads[gi]
                gi += 1
                c = carry_refs[ci]
                ci += 1
                dx = dx + jnp.concatenate([jnp.zeros((tr - SUBLANES, r.width), F32), c[...]], axis=0)
                c[...] = dtail
            o[...] = dx.astype(o.dtype)
        for d in dpar_refs:
            d[...] += grads[gi]
            gi += 1

    out_specs = [pl.BlockSpec((tr, r.width), lambda i: (order(i), 0)) for r in diff_rows]
    out_shape = [jax.ShapeDtypeStruct((n_rows, r.width), dt) for r, dt in zip(diff_rows, drow_dtypes)]
    for p in diff_pars:
        out_specs.append(pl.BlockSpec(p.arr.shape, functools.partial(lambda i, nd: (0,) * nd, nd=p.arr.ndim)))
        out_shape.append(jax.ShapeDtypeStruct(p.arr.shape, F32))
    res, landed = call_with_ride(
        body, name=name, grid=(n_blk,), in_specs=rspecs + pspecs + dspecs, args=rargs + pargs + dargs, out_specs=out_specs,
        out_shape=out_shape, scratch_shapes=[pltpu.VMEM((SUBLANES, r.width), F32) for r in tails], ride=ride,
        dimension_semantics=("arbitrary",))
    grads = (list(res[:len(diff_rows)]), list(res[len(diff_rows):]))
    return grads + (landed,) if ride is not None else grads


def _tile(n, cap):
    if n % LANES:
        return n
    q = n // LANES
    best = 1
    for k in range(1, q + 1):
        if q % k == 0 and k * LANES <= cap:
            best = k
    return best * LANES


def _mm_vmem(tm, tn, tk, a_bytes, b_bytes, out_bytes, resid, nk):
    total = 2 * (tm * tk * a_bytes + tk * tn * b_bytes + tm * tn * out_bytes) + tm * tn * 4
    total += 2 * tm * tn * 4 if resid else 0
    total += (tm * tk * 2 if a_bytes == 4 else 0) + (tk * tn * 2 if b_bytes == 4 else 0)
    return total + (tm * tn * 4 if nk > 1 else 0)


def matmul(name, a, b, *, ta=False, tb=False, out_dtype=F32, resid=None, cfg=None, tm_cap=2176, tn_cap=1024, tk_cap=2816,
           ride=None):
    m, k = (a.shape[1], a.shape[0]) if ta else a.shape
    n = b.shape[0] if tb else b.shape[1]
    tm, tn, tk = _tile(m, tm_cap), _tile(n, tn_cap), _tile(k, tk_cap)
    budget = VMEM_LIMIT - 8 * 1024 * 1024
    sizes = (a.dtype.itemsize, b.dtype.itemsize, jnp.dtype(out_dtype).itemsize, resid is not None)
    while _mm_vmem(tm, tn, tk, *sizes, k // tk) > budget:
        if tm > 256:
            tm = _tile(m, tm - 1)
        elif tk > 512:
            tk = _tile(k, tk - 1)
        else:
            tn = _tile(n, tn - 1)
    nk = k // tk
    a_spec = pl.BlockSpec((tk, tm), lambda i, j, l: (l, i)) if ta else pl.BlockSpec((tm, tk), lambda i, j, l: (i, l))
    b_spec = pl.BlockSpec((tn, tk), lambda i, j, l: (j, l)) if tb else pl.BlockSpec((tk, tn), lambda i, j, l: (l, j))
    dims = (((0 if ta else 1,), (1 if tb else 0,)), ((), ()))
    in_specs, args = [a_spec, b_spec], [a, b]
    if resid is not None:
        in_specs.append(pl.BlockSpec((tm, tn), lambda i, j, l: (i, j)))
        args.append(resid)

    def body(*refs):
        a_ref, b_ref = refs[0], refs[1]
        o_ref = refs[3] if resid is not None else refs[2]
        part = lax.dot_general(a_ref[...].astype(BF16), b_ref[...].astype(BF16), dims, preferred_element_type=F32)

        def finish(res):
            if resid is not None:
                row = pl.program_id(0) * tm + lax.broadcasted_iota(jnp.int32, (tm, 1), 0)
                inpad = jnp.zeros((tm, 1), jnp.bool_)
                for s in range(m // cfg.tp):
                    inpad = jnp.logical_or(inpad, jnp.logical_and(row >= s * cfg.tp, row < s * cfg.tp + cfg.pad))
                res = refs[2][...] + jnp.where(inpad, 0.0, res)
            o_ref[...] = res.astype(o_ref.dtype)

        if nk == 1:
            finish(part)
            return
        acc = refs[-1]
        l = pl.program_id(2)

        @pl.when(l == 0)
        def _():
            acc[...] = part

        @pl.when(jnp.logical_and(l > 0, l < nk - 1))
        def _():
            acc[...] += part

        @pl.when(l == nk - 1)
        def _():
            finish(acc[...] + part)

    outs, landed = call_with_ride(
        body, name=name, grid=(m // tm, n // tn, nk), in_specs=in_specs, args=args,
        out_specs=[pl.BlockSpec((tm, tn), lambda i, j, l: (i, j))], out_shape=[jax.ShapeDtypeStruct((m, n), out_dtype)],
        scratch_shapes=[pltpu.VMEM((tm, tn), F32)] if nk > 1 else [], ride=ride,
        dimension_semantics=("parallel", "parallel", "arbitrary"))
    return (outs[0], landed) if ride is not None else outs[0]


def _norm_fn(valid, h, g):
    y = h * lax.rsqrt(jnp.mean(h * h, axis=-1, keepdims=True) + EPS)
    return (valid * (y * g),)


def _make_ssd_pre(cfg):
    nh = cfg.ssd_heads

    def fn(valid, xs, xs_t, bm, bm_t, cm, cm_t, dtr, w_xs, w_b, w_c, b_xs, b_b, b_c, dt_bias, a_log, sel_hd, sel_lane):
        xs = valid * _silu(_causal_conv(xs, xs_t, w_xs, b_xs))
        bm = valid * _silu(_causal_conv(bm, bm_t, w_b, b_b))
        cm = valid * _silu(_causal_conv(cm, cm_t, w_c, b_c))
        lane = lax.broadcasted_iota(jnp.int32, (1, LANES), 1)
        head = jnp.where(lane < nh, 1.0, 0.0).astype(F32)
        dt = valid * head * _softplus(dtr + dt_bias)
        adt = dt * (-jnp.exp(a_log))
        x = xs * _dot_exact01(dt, sel_hd)
        adt_b = _dot_exact01(adt, sel_lane)
        return xs, bm, cm, x, adt_b

    return fn


def _make_ssd_post(cfg):
    gw = cfg.ssd_gw

    def fn(valid, y_raw, xs, z, d_skip, norm_g, sel_hd):
        d_rep = _dot_exact01(jnp.broadcast_to(d_skip, (SUBLANES, LANES)), sel_hd)[0:1]
        y = (y_raw + xs * d_rep) * _silu(z)
        parts = []
        for g in range(cfg.ssd_groups):
            yg = y[:, g * gw:(g + 1) * gw]
            parts.append(yg * lax.rsqrt(jnp.mean(yg * yg, axis=-1, keepdims=True) + EPS))
        return (jnp.concatenate(parts, axis=1) * norm_g,)

    return fn


def _ffn_act_fn(valid, ug, ug_t, uu, uu_t, w_g, w_u, b_g, b_u):
    return (valid * _silu(_causal_conv(ug, ug_t, w_g, b_g)) * _causal_conv(uu, uu_t, w_u, b_u),)


def _make_lru_pre(cfg):
    nb = cfg.lru_blocks
    bw = cfg.lru_width // nb

    def fn(valid, xr, xr_t, w_conv, b_conv, wa, ba, wx, bx, lam):
        xc = _causal_conv(xr, xr_t, w_conv, b_conv)
        a_parts, b_parts = [], []
        for n in range(nb):
            sl = slice(n * bw, (n + 1) * bw)
            xn = xc[:, sl]
            r = jax.nn.sigmoid(_dot(xn, wa[n]) + ba[:, sl])
            i = jax.nn.sigmoid(_dot(xn, wx[n]) + bx[:, sl])
            log_a = -LRU_C * r * _softplus(-lam[:, sl])
            a_parts.append(jnp.exp(log_a))
            b_parts.append(valid * jnp.sqrt(jnp.maximum(_neg_expm1(2.0 * log_a), 0.0)) * (i * xn))
        return jnp.concatenate(a_parts, axis=1), jnp.concatenate(b_parts, axis=1)

    return fn


def _lru_post_fn(valid, hs, gate):
    return (hs * jax.nn.gelu(gate),)


def _tri_consts(n):
    r = lax.broadcasted_iota(jnp.int32, (n, n), 0)
    c = lax.broadcasted_iota(jnp.int32, (n, n), 1)
    return r, c


def _ssd_chunk(x, bm, cm, a, s, tril, lower):
    rows = x.shape[0]
    heads = a.shape[1] // LANES
    p = x.shape[1] // heads
    cb = _dot_nt(cm, bm)
    ys, ss = [], []
    for e in range(heads):
        ae = a[:, e * LANES:(e + 1) * LANES]
        cs = _edot(tril, ae, 1, 0, 'a')
        lmat = jnp.exp(jnp.where(lower, cs - cs.T, NEG))
        xe = x[:, e * p:(e + 1) * p]
        se = s[e * p:(e + 1) * p, :]
        tot = cs[rows - 1:rows, :]
        y_diag = _dot(cb * lmat, xe)
        st = _dot_tn(xe * jnp.exp(tot - cs)[:, :p], bm)
        y_off = _dot_nt(cm, se) * jnp.exp(cs)[:, :p]
        ys.append(y_diag + y_off)
        ss.append(jnp.exp(tot[:, :1]) * se + st)
    return jnp.concatenate(ys, axis=1), jnp.concatenate(ss, axis=0)


def _ssd_specs(cfg, x, bm, cm, adt_b):
    gw, ns = cfg.ssd_gw, cfg.ssd_state
    hpg = cfg.ssd_heads // cfg.ssd_groups
    specs = [pl.BlockSpec((cfg.tp, gw), lambda b, g: (b, g)),
             pl.BlockSpec((cfg.tp, ns), lambda b, g: (b, g)),
             pl.BlockSpec((cfg.tp, ns), lambda b, g: (b, g)),
             pl.BlockSpec((cfg.tp, hpg * LANES), lambda b, g: (b, g))]
    return specs, [x, bm, cm, adt_b]


def ssd_scan_fwd(cfg, x, bm, cm, adt_b, ride=None):
    nb = x.shape[0] // cfg.tp
    L, nc = cfg.chunk, cfg.nc
    specs, args = _ssd_specs(cfg, x, bm, cm, adt_b)

    def body(x_ref, b_ref, c_ref, a_ref, y_ref, states_ref, s_ref):
        r, c = _tri_consts(L)
        lower = r >= c
        tril = lower.astype(F32)
        s_ref[...] = jnp.zeros_like(s_ref)

        def step(ci, carry):
            rows = pl.ds(pl.multiple_of(ci * L, L), L)
            states_ref[ci] = s_ref[...]
            y, s_new = _ssd_chunk(x_ref[rows, :], b_ref[rows, :], c_ref[rows, :], a_ref[rows, :], s_ref[...], tril, lower)
            y_ref[rows, :] = y
            s_ref[...] = s_new
            return carry

        lax.fori_loop(0, nc, step, 0, unroll=CHUNK_UNROLL)

    gw, ns = cfg.ssd_gw, cfg.ssd_state
    outs, landed = call_with_ride(
        body, name="ssd_scan_fwd", grid=(nb, cfg.ssd_groups), in_specs=specs, args=args,
        out_specs=[specs[0], pl.BlockSpec((None, None, nc, gw, ns), lambda b, g: (b, g, 0, 0, 0))],
        out_shape=[jax.ShapeDtypeStruct(x.shape, F32), jax.ShapeDtypeStruct((nb, cfg.ssd_groups, nc, gw, ns), F32)],
        scratch_shapes=[pltpu.VMEM((gw, ns), F32)], ride=ride)
    return outs[0], outs[1], landed


def ssd_scan_bwd(cfg, x, bm, cm, adt_b, states, dy, ride=None):
    nb = x.shape[0] // cfg.tp
    L, nc = cfg.chunk, cfg.nc
    specs, args = _ssd_specs(cfg, x, bm, cm, adt_b)
    state_spec = pl.BlockSpec((None, None, nc, cfg.ssd_gw, cfg.ssd_state), lambda b, g: (b, g, 0, 0, 0))

    def body(x_ref, b_ref, c_ref, a_ref, s_all, dy_ref, dx_ref, db_ref, dc_ref, da_ref, ds_ref):
        r, c = _tri_consts(L)
        lower = r >= c
        tril = lower.astype(F32)
        chunk = functools.partial(_ssd_chunk, tril=tril, lower=lower)
        ds_ref[...] = jnp.zeros_like(ds_ref)

        def bwd(k, carry):
            ci = nc - 1 - k
            rows = pl.ds(pl.multiple_of(ci * L, L), L)
            _, vjp = jax.vjp(chunk, x_ref[rows, :], b_ref[rows, :], c_ref[rows, :], a_ref[rows, :], s_all[ci])
            dx, db, dc, da, ds = vjp((dy_ref[rows, :], ds_ref[...]))
            dx_ref[rows, :] = dx
            db_ref[rows, :] = db
            dc_ref[rows, :] = dc
            da_ref[rows, :] = da
            ds_ref[...] = ds
            return carry

        lax.fori_loop(0, nc, bwd, 0, unroll=CHUNK_UNROLL)

    return call_with_ride(
        body, name="ssd_scan_bwd", grid=(nb, cfg.ssd_groups), in_specs=specs + [state_spec, specs[0]], args=args + [states, dy],
        out_specs=specs, out_shape=[jax.ShapeDtypeStruct(t.shape, F32) for t in (x, bm, cm, adt_b)],
        scratch_shapes=[pltpu.VMEM((cfg.ssd_gw, cfg.ssd_state), F32)], ride=ride)


def _ret_chunk(q, k, v, g, norm_g, r_prev, cos, sin, valid, decay, zeta, xi, cdec, scale):
    half = q.shape[1] // 2

    def rot(t):
        t1, t2 = t[:, :half], t[:, half:]
        return jnp.concatenate([t1 * cos - t2 * sin, t1 * sin + t2 * cos], axis=1)

    qr = valid * rot(q)
    kr = valid * rot(k) * scale
    v = valid * v
    inner = _dot(_dot_nt(qr, kr) * decay, v)
    kv = _dot_tn(kr * zeta, v)
    cross = _dot(qr, r_prev) * xi
    o = inner + cross
    o = o - jnp.mean(o, axis=-1, keepdims=True)
    o = o * lax.rsqrt(jnp.mean(o * o, axis=-1, keepdims=True) + EPS)
    return _silu(g) * (o * norm_g), cdec * r_prev + kv


def _ret_consts(cfg):
    f32 = jnp.float32
    log_gamma = jnp.log1p(-jnp.exp2(-5.0 - jnp.arange(cfg.ret_heads, dtype=f32)))
    idx = jnp.arange(cfg.chunk, dtype=f32)
    diff = idx[:, None] - idx[None, :]
    decay = jnp.where(diff >= 0, jnp.exp(log_gamma[:, None, None] * jnp.maximum(diff, 0.0)), 0.0)
    zeta = jnp.exp(log_gamma[:, None] * (cfg.chunk - 1 - idx)[None, :])[..., None]
    xi = jnp.exp(log_gamma[:, None] * (idx + 1.0)[None, :])[..., None]
    cdec = jnp.exp(cfg.chunk * log_gamma)[:, None, None]
    half = cfg.ret_dim // 2
    inv_freq = 1.0 / (10000.0 ** (jnp.arange(half, dtype=f32) / (half - 1)))
    pos = jnp.arange(cfg.tp, dtype=f32) - cfg.pad
    ang = pos[:, None] * inv_freq[None, :]
    return dict(decay=decay, zeta=zeta, xi=xi, cdec=cdec, cos=jnp.cos(ang), sin=jnp.sin(ang))


def _ret_specs(cfg, u, col0, norm_g, consts):
    dk, nh, L = cfg.ret_dim, cfg.ret_heads, cfg.chunk
    base = col0 // dk
    seg = lambda s: pl.BlockSpec((cfg.tp, dk), functools.partial(lambda h, b, s: (b, base + s * nh + h), s=s))
    half = dk // 2
    specs = [seg(0), seg(1), seg(2), seg(3),
             pl.BlockSpec((1, dk), lambda h, b: (0, h)),
             pl.BlockSpec((cfg.tp, half), lambda h, b: (0, 0)),
             pl.BlockSpec((cfg.tp, half), lambda h, b: (0, 0)),
             pl.BlockSpec((None, L, L), lambda h, b: (h, 0, 0)),
             pl.BlockSpec((None, L, 1), lambda h, b: (h, 0, 0)),
             pl.BlockSpec((None, L, 1), lambda h, b: (h, 0, 0)),
             pl.BlockSpec((None, 1, 1), lambda h, b: (h, 0, 0))]
    args = [u, u, u, u, norm_g, consts["cos"], consts["sin"], consts["decay"], consts["zeta"], consts["xi"], consts["cdec"]]
    return specs, args


def _ret_chunk_at(cfg, refs, ci):
    L = cfg.chunk
    cos_ref, sin_ref, decay_ref, zeta_ref, xi_ref, cdec_ref = refs
    rows = pl.ds(pl.multiple_of(ci * L, L), L)
    fn = functools.partial(_ret_chunk, cos=cos_ref[rows, :], sin=sin_ref[rows, :], valid=_valid_chunk(cfg, ci),
                           decay=decay_ref[...], zeta=zeta_ref[...], xi=xi_ref[...], cdec=cdec_ref[...],
                           scale=cfg.ret_dim ** -0.5)
    return fn, rows


def ret_fwd(cfg, u, col0, norm_g, consts, ride=None):
    nb = u.shape[0] // cfg.tp
    dk, nc = cfg.ret_dim, cfg.nc
    specs, args = _ret_specs(cfg, u, col0, norm_g, consts)

    def body(q_ref, k_ref, v_ref, g_ref, ng_ref, *rest):
        y_ref, r_ref = rest[-2], rest[-1]
        r_ref[...] = jnp.zeros_like(r_ref)

        def step(ci, carry):
            fn, rows = _ret_chunk_at(cfg, rest[:6], ci)
            y, r_new = fn(q_ref[rows, :], k_ref[rows, :], v_ref[rows, :], g_ref[rows, :], ng_ref[...], r_ref[...])
            y_ref[rows, :] = y.astype(y_ref.dtype)
            r_ref[...] = r_new
            return carry

        lax.fori_loop(0, nc, step, 0, unroll=CHUNK_UNROLL)

    outs, landed = call_with_ride(
        body, name="ret_fwd", grid=(cfg.ret_heads, nb), in_specs=specs, args=args,
        out_specs=[pl.BlockSpec((cfg.tp, dk), lambda h, b: (b, h))],
        out_shape=[jax.ShapeDtypeStruct((u.shape[0], cfg.ret_w), BF16)], scratch_shapes=[pltpu.VMEM((dk, dk), F32)], ride=ride)
    return outs[0], landed


def ret_bwd(cfg, u, col0, norm_g, consts, dy, dy_col0):
    nb = u.shape[0] // cfg.tp
    dk, nc, nh = cfg.ret_dim, cfg.nc, cfg.ret_heads
    specs, args = _ret_specs(cfg, u, col0, norm_g, consts)
    dy_base = dy_col0 // dk
    specs.append(pl.BlockSpec((cfg.tp, dk), lambda h, b: (b, dy_base + h)))
    seg_out = lambda s: pl.BlockSpec((cfg.tp, dk), functools.partial(lambda h, b, s: (b, s * nh + h), s=s))

    def body(q_ref, k_ref, v_ref, g_ref, ng_ref, *rest):
        consts_refs, dy_ref = rest[:6], rest[6]
        dq_ref, dk_ref, dv_ref, dg_ref, dng_ref, r_all, dr_ref = rest[7:]
        dr_ref[...] = jnp.zeros_like(dr_ref)

        def fwd(ci, carry):
            fn, rows = _ret_chunk_at(cfg, consts_refs, ci)
            r_all[ci] = dr_ref[...]
            _, r_new = fn(q_ref[rows, :], k_ref[rows, :], v_ref[rows, :], g_ref[rows, :], ng_ref[...], dr_ref[...])
            dr_ref[...] = r_new
            return carry

        lax.fori_loop(0, nc, fwd, 0, unroll=CHUNK_UNROLL)
        dr_ref[...] = jnp.zeros_like(dr_ref)

        @pl.when(pl.program_id(1) == 0)
        def _():
            dng_ref[...] = jnp.zeros_like(dng_ref)

        def bwd(kk, carry):
            ci = nc - 1 - kk
            fn, rows = _ret_chunk_at(cfg, consts_refs, ci)
            _, vjp = jax.vjp(fn, q_ref[rows, :], k_ref[rows, :], v_ref[rows, :], g_ref[rows, :], ng_ref[...], r_all[ci])
            dq, dkk, dv, dg, dng, dr = vjp((dy_ref[rows, :].astype(F32), dr_ref[...]))
            dq_ref[rows, :] = dq.astype(dq_ref.dtype)
            dk_ref[rows, :] = dkk.astype(dk_ref.dtype)
            dv_ref[rows, :] = dv.astype(dv_ref.dtype)
            dg_ref[rows, :] = dg.astype(dg_ref.dtype)
            dng_ref[...] += dng
            dr_ref[...] = dr
            return carry

        lax.fori_loop(0, nc, bwd, 0, unroll=CHUNK_UNROLL)

    seg_shape = jax.ShapeDtypeStruct((u.shape[0], cfg.ret_w), BF16)
    res = pl.pallas_call(
        body, grid=(nh, nb), in_specs=specs,
        out_specs=[pl.BlockSpec((cfg.tp, dk), lambda h, b: (b, h))] * 4 + [pl.BlockSpec((1, dk), lambda h, b: (0, h))],
        out_shape=[seg_shape] * 4 + [jax.ShapeDtypeStruct((1, cfg.ret_w), F32)],
        scratch_shapes=[pltpu.VMEM((nc, dk, dk), F32), pltpu.VMEM((dk, dk), F32)],
        name="ret_bwd", compiler_params=_cparams(dimension_semantics=("arbitrary", "arbitrary")))(*args, dy)
    return res[:4], res[4]


def _dot2(a, tri):
    hi = a.astype(BF16)
    lo = (a - hi.astype(F32)).astype(BF16)
    return _dg(hi, tri, 1, 0) + _dg(lo, tri, 1, 0)


def _sb_group(cfg):
    return max(g for g in (4, 2, 1) if g <= cfg.nc)


def _sb_specs(cfg, u, col0):
    base = col0 // LANES
    per = cfg.sb_w // LANES
    seg = lambda s: pl.BlockSpec((cfg.tp, LANES), functools.partial(lambda b, p, s: (b, base + s * per + p), s=s))
    return [seg(0), seg(1), seg(2)], [u, u, u]


def _sb_scores(cfg, qh, ks, mask):
    z = _dg(qh, ks, 1, 1)
    sp = jnp.maximum(z, 0.0) + jnp.log(1.0 + jnp.exp(-jnp.abs(z)))
    return z, sp, jnp.where(mask, -sp, 0.0)


def _sb_running(x, tri, run, order):
    L = x.shape[0]
    parts = [None] * (x.shape[1] // L)
    for b in order:
        blk = x[:, b * L:(b + 1) * L]
        parts[b] = _dot2(blk, tri) + run
        run = run + jnp.sum(blk, axis=1, keepdims=True)
    return jnp.concatenate(parts, axis=1) if len(parts) > 1 else parts[0], run


def sb_fwd(cfg, u, col0, ride=None):
    nb = u.shape[0] // cfg.tp
    L, nc = cfg.chunk, cfg.nc
    hd = cfg.sb_head_dim
    scale = hd ** -0.5
    G = _sb_group(cfg)
    specs, args = _sb_specs(cfg, u, col0)

    def body(q_ref, k_ref, v_ref, y_ref, tot_ref):
        tr, tc = _tri_consts(L)
        after_tri = (tr > tc).astype(BF16)
        rid = lax.broadcasted_iota(jnp.int32, (L, G * L), 0)
        cid = lax.broadcasted_iota(jnp.int32, (L, G * L), 1)
        lane = lax.broadcasted_iota(jnp.int32, (1, LANES), 1)
        mine = [jnp.logical_and(lane >= hh * hd, lane < (hh + 1) * hd) for hh in range(2)]

        def qloop(i, carry):
            qrows = pl.ds(pl.multiple_of(i * L, L), L)
            q = q_ref[qrows, :]
            qhs = [jnp.where(mine[hh], q * scale, 0.0).astype(BF16) for hh in range(2)]

            def gloop(gg, c):
                accs, runs = c
                hi = i - gg * G
                start = jnp.maximum(hi - (G - 1), 0)
                krows = pl.ds(pl.multiple_of(start * L, L), G * L)
                ks = k_ref[krows, :].astype(BF16)
                vs = v_ref[krows, :].astype(BF16)
                kpos = start * L + cid
                mask = jnp.logical_and(jnp.logical_and(kpos < i * L + rid, kpos >= cfg.pad), kpos < (hi + 1) * L)
                new_accs, new_runs = [], []
                for hh in range(2):
                    z, sp, l1m = _sb_scores(cfg, qhs[hh], ks, mask)
                    after, run = _sb_running(l1m, after_tri, runs[hh], range(G - 1, -1, -1))
                    w = jnp.where(mask, jnp.exp(z - sp + after), 0.0)
                    new_accs.append(accs[hh] + _dg(w.astype(BF16), vs, 1, 0))
                    new_runs.append(run)
                return tuple(new_accs), tuple(new_runs)

            zero_acc, zero_run = jnp.zeros((L, LANES), F32), jnp.zeros((L, 1), F32)
            accs, runs = lax.fori_loop(0, (i + G) // G, gloop, ((zero_acc, zero_acc), (zero_run, zero_run)))
            for hh in range(2):
                tot_ref[0, hh, qrows, :] = runs[hh]
            y_ref[qrows, :] = jnp.where(lane < hd, accs[0], accs[1]).astype(y_ref.dtype)
            return carry

        lax.fori_loop(0, nc, qloop, 0)

    (y, tot), landed = call_with_ride(
        body, name="sb_fwd", grid=(nb, cfg.sb_w // LANES), in_specs=specs, args=args,
        out_specs=[pl.BlockSpec((cfg.tp, LANES), lambda b, p: (b, p)),
                   pl.BlockSpec((1, 2, cfg.tp, 1), lambda b, p: (b, p, 0, 0))],
        out_shape=[jax.ShapeDtypeStruct((u.shape[0], cfg.sb_w), BF16),
                   jax.ShapeDtypeStruct((nb, cfg.sb_heads, cfg.tp, 1), F32)], ride=ride)
    return y, tot, landed


def sb_bwd(cfg, u, col0, tot, dy, dy_col0, ride=None):
    nb = u.shape[0] // cfg.tp
    L, nc = cfg.chunk, cfg.nc
    hd = cfg.sb_head_dim
    scale = hd ** -0.5
    specs, args = _sb_specs(cfg, u, col0)
    dy_base = dy_col0 // LANES
    specs += [pl.BlockSpec((1, 2, cfg.tp, 1), lambda b, p: (b, p, 0, 0)),
              pl.BlockSpec((cfg.tp, LANES), lambda b, p: (b, dy_base + p))]

    G = _sb_group(cfg)

    def body(q_ref, k_ref, v_ref, tot_ref, dy_ref, dq_ref, dk_ref, dv_ref, dk_acc, dv_acc):
        tr, tc = _tri_consts(L)
        upto_tri = (tr <= tc).astype(BF16)
        before_tri = (tr < tc).astype(BF16)
        rid = lax.broadcasted_iota(jnp.int32, (L, G * L), 0)
        cid = lax.broadcasted_iota(jnp.int32, (L, G * L), 1)
        lane = lax.broadcasted_iota(jnp.int32, (1, LANES), 1)
        mine = [jnp.logical_and(lane >= hh * hd, lane < (hh + 1) * hd) for hh in range(2)]
        dk_acc[...] = jnp.zeros_like(dk_acc)
        dv_acc[...] = jnp.zeros_like(dv_acc)

        def qloop(i, carry):
            qrows = pl.ds(pl.multiple_of(i * L, L), L)
            q = q_ref[qrows, :]
            dy = dy_ref[qrows, :].astype(F32)
            qhs = [jnp.where(mine[hh], q * scale, 0.0).astype(BF16) for hh in range(2)]
            dyhs = [jnp.where(mine[hh], dy, 0.0).astype(BF16) for hh in range(2)]
            tots = [tot_ref[0, hh, qrows, :] for hh in range(2)]

            def gloop(g, c):
                dq, run_ls, run_as = c
                lo = g * G
                start = jnp.minimum(lo, nc - G)
                krows = pl.ds(pl.multiple_of(start * L, L), G * L)
                k = k_ref[krows, :]
                ks = k.astype(BF16)
                vs = v_ref[krows, :].astype(BF16)
                kpos = start * L + cid
                mask = jnp.logical_and(jnp.logical_and(kpos < i * L + rid, kpos >= cfg.pad), kpos >= lo * L)
                new_ls, new_as = [], []
                dv_add = jnp.zeros((G * L, LANES), F32)
                dk_add = jnp.zeros((G * L, LANES), F32)
                for hh in range(2):
                    z, sp, l1m = _sb_scores(cfg, qhs[hh], ks, mask)
                    upto, run_l = _sb_running(l1m, upto_tri, run_ls[hh], range(G))
                    w = jnp.where(mask, jnp.exp(z - sp + (tots[hh] - upto)), 0.0)
                    da = w * _dg(dyhs[hh], vs, 1, 1)
                    d_l1m, run_a = _sb_running(da, before_tri, run_as[hh], range(G))
                    sg = jnp.exp(z - sp)
                    dz = jnp.where(mask, da * (1.0 - sg) - d_l1m * sg, 0.0).astype(BF16)
                    dv_add = dv_add + _dg(w.astype(BF16), dyhs[hh], 0, 0)
                    dk_add = dk_add + _dg(dz, qhs[hh], 0, 0)
                    dq = dq + _dg(dz, jnp.where(mine[hh], k, 0.0).astype(BF16), 1, 0)
                    new_ls.append(run_l)
                    new_as.append(run_a)
                dv_acc[krows, :] += dv_add
                dk_acc[krows, :] += dk_add
                return dq, tuple(new_ls), tuple(new_as)

            zero = jnp.zeros((L, 1), F32)
            dq, _, _ = lax.fori_loop(0, (i + G) // G, gloop, (jnp.zeros((L, LANES), F32), (zero, zero), (zero, zero)))
            dq_ref[qrows, :] = (dq * scale).astype(dq_ref.dtype)
            return carry

        lax.fori_loop(0, nc, qloop, 0)
        dk_ref[...] = dk_acc[...].astype(dk_ref.dtype)
        dv_ref[...] = dv_acc[...].astype(dv_ref.dtype)

    out_spec = pl.BlockSpec((cfg.tp, LANES), lambda b, p: (b, p))
    seg_shape = jax.ShapeDtypeStruct((u.shape[0], cfg.sb_w), BF16)
    return call_with_ride(
        body, name="sb_bwd", grid=(nb, cfg.sb_w // LANES), in_specs=specs, args=args + [tot, dy], out_specs=[out_spec] * 3,
        out_shape=[seg_shape] * 3, scratch_shapes=[pltpu.VMEM((cfg.tp, LANES), F32), pltpu.VMEM((cfg.tp, LANES), F32)], ride=ride)


def _lru_cols(cfg):
    return _tile(cfg.lru_width, 256)


def lru_scan_fwd(cfg, a, b):
    nb = a.shape[0] // cfg.tp
    cw = _lru_cols(cfg)
    spec = pl.BlockSpec((cfg.tp, cw), lambda s, c: (s, c))

    def body(a_ref, b_ref, h_ref):
        def step(i, h):
            rows = pl.ds(pl.multiple_of(i * SUBLANES, SUBLANES), SUBLANES)
            at, bt = a_ref[rows, :], b_ref[rows, :]
            outs = []
            for r in range(SUBLANES):
                h = at[r:r + 1, :] * h + bt[r:r + 1, :]
                outs.append(h)
            h_ref[rows, :] = jnp.concatenate(outs, axis=0)
            return h

        lax.fori_loop(0, cfg.tp // SUBLANES, step, jnp.zeros((1, cw), F32))

    return pl.pallas_call(
        body, grid=(nb, cfg.lru_width // cw), in_specs=[spec, spec], out_specs=spec,
        out_shape=jax.ShapeDtypeStruct(a.shape, F32), name="lru_scan_fwd", compiler_params=_cparams())(a, b)


def lru_scan_bwd(cfg, a, h, dh):
    nb = a.shape[0] // cfg.tp
    cw = _lru_cols(cfg)
    nt = cfg.tp // SUBLANES
    spec = pl.BlockSpec((cfg.tp, cw), lambda s, c: (s, c))

    def body(a_ref, h_ref, dh_ref, da_ref, db_ref):
        def step(k, c):
            i = nt - 1 - k
            rows = pl.ds(pl.multiple_of(i * SUBLANES, SUBLANES), SUBLANES)
            prev = pl.ds(pl.multiple_of(jnp.maximum(i - 1, 0) * SUBLANES, SUBLANES), SUBLANES)
            at, ht, dht = a_ref[rows, :], h_ref[rows, :], dh_ref[rows, :]
            h_before = jnp.where(i > 0, h_ref[prev, :][SUBLANES - 1:SUBLANES, :], 0.0)
            das, dbs = [None] * SUBLANES, [None] * SUBLANES
            for r in range(SUBLANES - 1, -1, -1):
                g = dht[r:r + 1, :] + c
                dbs[r] = g
                das[r] = g * (ht[r - 1:r, :] if r > 0 else h_before)
                c = at[r:r + 1, :] * g
            da_ref[rows, :] = jnp.concatenate(das, axis=0)
            db_ref[rows, :] = jnp.concatenate(dbs, axis=0)
            return c

        lax.fori_loop(0, nt, step, jnp.zeros((1, cw), F32))

    return pl.pallas_call(
        body, grid=(nb, cfg.lru_width // cw), in_specs=[spec] * 3, out_specs=[spec] * 2,
        out_shape=[jax.ShapeDtypeStruct(a.shape, F32)] * 2, name="lru_scan_bwd", compiler_params=_cparams())(a, h, dh)


def final_loss(cfg, h, target, norm_g):
    assert cfg.seq % cfg.chunk == 0 and cfg.n_meta + cfg.pad == cfg.chunk
    L, nc, d = cfg.chunk, cfg.nc, cfg.d_model
    per_seq = cfg.seq // L

    def tgt_map(i):
        return ((i // nc) * per_seq + jnp.maximum(i % nc - 1, 0), 0)

    def body(h_ref, t_ref, g_ref, loss_ref, dh_ref, dg_ref):
        i = pl.program_id(0)
        real = jnp.where(i % nc == 0, 0.0, 1.0)
        tgt = t_ref[...]

        def loss_fn(hv, g):
            y = hv * lax.rsqrt(jnp.mean(hv * hv, axis=-1, keepdims=True) + EPS) * g
            return 0.5 * real * jnp.sum(jnp.mean(jnp.square(y - tgt), axis=-1))

        val, (dh, dg) = jax.value_and_grad(loss_fn, argnums=(0, 1))(h_ref[...], g_ref[...])

        @pl.when(i == 0)
        def _():
            loss_ref[...] = jnp.zeros_like(loss_ref)
            dg_ref[...] = jnp.zeros_like(dg_ref)

        loss_ref[...] += jnp.broadcast_to(val, loss_ref.shape)
        dg_ref[...] += dg
        dh_ref[...] = dh

    return pl.pallas_call(
        body, grid=(h.shape[0] // L,),
        in_specs=[pl.BlockSpec((L, d), lambda i: (i, 0)), pl.BlockSpec((L, d), tgt_map), pl.BlockSpec((1, d), lambda i: (0, 0))],
        out_specs=[pl.BlockSpec((SUBLANES, LANES), lambda i: (0, 0)), pl.BlockSpec((L, d), lambda i: (i, 0)),
                   pl.BlockSpec((1, d), lambda i: (0, 0))],
        out_shape=[jax.ShapeDtypeStruct((SUBLANES, LANES), F32), jax.ShapeDtypeStruct(h.shape, F32),
                   jax.ShapeDtypeStruct((1, d), F32)],
        name="final_loss", compiler_params=_cparams(dimension_semantics=("arbitrary",)))(h, target, norm_g)


N_PEER = 7


class Ride(NamedTuple):
    arrays: list
    same: list


def _ride_copies(ride_in, ride_out, same, sems, sending):
    send_sems, recv_sems, local_sems = sems
    n = len(ride_in)
    x, y, c = lax.axis_index("x"), lax.axis_index("y"), lax.axis_index("c")
    me = 4 * x + 2 * y + c

    def slab(w, dest):
        return ride_in[w] if same[w] else ride_in[w].at[dest]

    local = [pltpu.make_async_copy(slab(w, me), ride_out[w].at[me], local_sems.at[w]) for w in range(n)]
    remote = []
    for r in range(1, N_PEER + 1):
        peer = (1 - x if r & 4 else x, 1 - y if r & 2 else y, 1 - c if r & 1 else c)
        pidx = 4 * peer[0] + 2 * peer[1] + peer[2]
        for w in range(n):
            remote.append(pltpu.make_async_remote_copy(
                src_ref=slab(w, pidx), dst_ref=ride_out[w].at[me if sending else pidx], send_sem=send_sems.at[w, r - 1],
                recv_sem=recv_sems.at[w, r - 1], device_id=peer, device_id_type=pl.DeviceIdType.MESH))
    return local, remote


def _ride_start(ride_in, ride_out, same, sems):
    local, remote = _ride_copies(ride_in, ride_out, same, sems, True)
    for cp in local + remote:
        cp.start()


def _ride_wait(ride_in, ride_out, same, sems):
    local, remote = _ride_copies(ride_in, ride_out, same, sems, False)
    for cp in remote:
        cp.wait_recv()
    for cp in remote:
        cp.wait_send()
    for cp in local:
        cp.wait()


def call_with_ride(body, *, name, grid, in_specs, args, out_specs, out_shape, scratch_shapes=(), ride=None, **cparams):
    out_specs, out_shape, scratch_shapes = list(out_specs), list(out_shape), list(scratch_shapes)
    if ride is None:
        res = pl.pallas_call(body, grid=grid, in_specs=list(in_specs), out_specs=out_specs, out_shape=out_shape,
                             scratch_shapes=scratch_shapes, name=name, compiler_params=_cparams(**cparams))(*args)
        return list(res), []
    arrays, same = ride
    n, n_in, n_out, n_scr = len(arrays), len(args), len(out_shape), len(scratch_shapes)
    hbm = pl.BlockSpec(memory_space=pl.ANY)
    land_shape = [jax.ShapeDtypeStruct((8,) + (a.shape if s else a.shape[1:]), a.dtype) for a, s in zip(arrays, same)]

    def wrapped(*refs):
        ins, ride_in = refs[:n_in], refs[n_in:n_in + n]
        outs, ride_out = refs[n_in + n:n_in + n + n_out], refs[n_in + n + n_out:n_in + 2 * n + n_out]
        scr, sems = refs[n_in + 2 * n + n_out:n_in + 2 * n + n_out + n_scr], refs[n_in + 2 * n + n_out + n_scr:]
        first, last = True, True
        for axis, size in enumerate(grid):
            first = jnp.logical_and(first, pl.program_id(axis) == 0)
            last = jnp.logical_and(last, pl.program_id(axis) == size - 1)
        if grid:
            pl.when(first)(lambda: _ride_start(ride_in, ride_out, same, sems))
        else:
            _ride_start(ride_in, ride_out, same, sems)
        body(*ins, *outs, *scr)
        if grid:
            pl.when(last)(lambda: _ride_wait(ride_in, ride_out, same, sems))
        else:
            _ride_wait(ride_in, ride_out, same, sems)

    sems = [pltpu.SemaphoreType.DMA((n, N_PEER)), pltpu.SemaphoreType.DMA((n, N_PEER)), pltpu.SemaphoreType.DMA((n,))]
    cparams["dimension_semantics"] = ("arbitrary",) * len(grid)
    res = pl.pallas_call(wrapped, grid=grid, in_specs=list(in_specs) + [hbm] * n, out_specs=out_specs + [hbm] * n,
                         out_shape=out_shape + land_shape, scratch_shapes=scratch_shapes + sems, name=name,
                         compiler_params=_cparams(**cparams))(*args, *arrays)
    return list(res[:n_out]), list(res[n_out:])


def exchange(name, ride):
    return call_with_ride(lambda: None, name=name, grid=(), in_specs=[], args=[], out_specs=[], out_shape=[], ride=ride)[1]


def _row_tile(r, mult, cap):
    best = None
    for t in range(mult, min(r, cap) + 1, mult):
        if r % t == 0:
            best = t
    return best if best is not None else r


def adamw(name, parts, w, m, v):
    r, c = w.shape
    tr = _row_tile(r, 16, 256)
    spec = pl.BlockSpec((tr, c), lambda i: (i, 0))

    def body(p_ref, w_ref, m_ref, v_ref, g_ref, d_ref, m2_ref, v2_ref):
        g = p_ref[0].astype(F32)
        for dev in range(1, 8):
            g = g + p_ref[dev].astype(F32)
        m2 = ADAM_B1 * m_ref[...] + (1.0 - ADAM_B1) * g
        v2 = ADAM_B2 * v_ref[...] + (1.0 - ADAM_B2) * jnp.square(g)
        m_hat = m2 / (1.0 - ADAM_B1 ** ADAM_STEP)
        v_hat = v2 / (1.0 - ADAM_B2 ** ADAM_STEP)
        g_ref[...] = g
        d_ref[...] = -ADAM_LR * (m_hat / (jnp.sqrt(v_hat) + ADAM_EPS) + ADAM_WD * w_ref[...])
        m2_ref[...] = m2
        v2_ref[...] = v2

    return pl.pallas_call(
        body, grid=(r // tr,), in_specs=[pl.BlockSpec((8, tr, c), lambda i: (0, i, 0)), spec, spec, spec],
        out_specs=[spec] * 4, out_shape=[jax.ShapeDtypeStruct((r, c), F32)] * 4, name=name, compiler_params=_cparams())(parts, w, m, v)


PACK_ROWS = 256


def _pack(arrs):
    flat = jnp.concatenate([a.reshape(-1).astype(F32) for a in arrs])
    quantum = PACK_ROWS * LANES
    total = -(-flat.shape[0] // quantum) * quantum
    return jnp.pad(flat, (0, total - flat.shape[0])).reshape(-1, LANES)


def _unpack(packed, shapes):
    flat = packed.reshape(-1)
    out, off = [], 0
    for s in shapes:
        n = int(np.prod(s))
        out.append(flat[off:off + n].reshape(s))
        off += n
    return out


def _pad_lanes(vec):
    return jnp.pad(vec.astype(F32), (0, LANES - vec.shape[0]))[None, :]


def make_step(cfg):
    I, BC, H, RW, SW, LW, F, D = (cfg.ssd_inner, cfg.ssd_bc, cfg.ssd_heads, cfg.ret_w, cfg.sb_w, cfg.lru_width,
                                  cfg.ffn_dim, cfg.d_model)
    ND = cfg.n_dev
    q_off = 2 * I + 2 * BC
    dt_off = q_off + 4 * RW
    dt_w = (-dt_off) % 1024 or 1024

    def gather_cols(g):
        return jnp.transpose(g, (1, 0, 2)).reshape(g.shape[1], -1)

    def scatter_cols(full):
        r, c = full.shape
        return jnp.transpose(full.reshape(r, ND, c // ND), (1, 0, 2))

    def step(p, m_in, v_in, x, loss_target):
        nb = x.shape[0]
        rows = nb * cfg.tp

        small_shapes = [p[n].shape for n in SMALL_SHARDED]
        wfull = {}

        halves = {}

        def gather_ride(names, extra=()):
            own = []
            for n in names:
                if isinstance(n, tuple):
                    half = p[n[0]].shape[0] // 2
                    own.append(p[n[0]][n[1] * half:(n[1] + 1) * half].astype(BF16))
                else:
                    own.append(p[n].astype(BF16))
            own += list(extra)
            return Ride(own, [True] * len(own))

        def gather_finish(names, landed):
            for n, g in zip(names, landed):
                if isinstance(n, tuple):
                    halves.setdefault(n[0], {})[n[1]] = g
                    if len(halves[n[0]]) < 2:
                        continue
                    n, g = n[0], jnp.concatenate([halves[n[0]][0], halves[n[0]][1]], axis=1)
                wfull[n] = gather_cols(g) if n in BIG_COL else g.reshape(-1, g.shape[-1])

        landed = exchange("gather_l0_mixer", gather_ride(['l0_w_in'], [_pack([p[n] for n in SMALL_SHARDED])]))
        gather_finish(['l0_w_in'], landed)
        small_parts = [_unpack(landed[-1][d], small_shapes) for d in range(ND)]
        sfull = {n: jnp.concatenate([small_parts[d][k] for d in range(ND)], axis=1) for k, n in enumerate(SMALL_SHARDED)}

        w0 = wfull['l0_w_in']
        w0a = jnp.concatenate([w0[:, :q_off], w0[:, q_off + H:], jnp.pad(w0[:, q_off:q_off + H], ((0, 0), (0, dt_w - H)))], axis=1)

        sel_hd = (jnp.arange(LANES)[:, None] == (jnp.arange(I)[None, :] // cfg.ssd_head_dim)).astype(F32)
        sel_lane = (jnp.arange(LANES)[:, None] == (jnp.arange(H * LANES)[None, :] // LANES)).astype(F32)
        rconst = _ret_consts(cfg)
        row2 = lambda vec: vec.astype(F32)[None, :]

        meta = jnp.broadcast_to(sfull['meta_tokens'][None], (nb, cfg.n_meta, D))
        h0 = jnp.concatenate([jnp.zeros((nb, cfg.pad, D), F32), meta, x], axis=1).reshape(rows, D)

        light = light_rows(rows)

        def norm_fwd(name, h, g):
            return stage_fwd(name, cfg, _norm_fn, [Row(h, D)], [Par(row2(g))], [(D, BF16)], tr=light)[0]

        def norm_bwd(name, h, g, dhn, dh_next):
            fn = lambda valid, hv, gv: (_norm_fn(valid, hv, gv)[0], hv)
            (dh,), (dg,) = stage_bwd(name, cfg, fn, [Row(h, D)], [Par(row2(g))], [[Row(dhn, D)], [Row(dh_next, D)]], [F32],
                                     tr=light)
            return dh, dg[0]

        hn0 = norm_fwd("l0_mix_norm", h0, p['l0_mix_norm'])
        u0, landed = matmul("l0_in_proj", hn0, w0a, ride=gather_ride(['l0_w_out']))
        gather_finish(['l0_w_out'], landed)
        cw, cb = sfull['l0_ssd_conv_w'], p['l0_ssd_conv_b']
        ssd_pre = _make_ssd_pre(cfg)
        ssd_post = _make_ssd_post(cfg)
        pre_rows = [Row(u0, I, 1, tail=True), Row(u0, BC, (2 * I) // BC, tail=True), Row(u0, BC, (2 * I) // BC + 1, tail=True),
                    Row(u0, LANES, dt_off // LANES)]
        pre_pars = [Par(cw[:, :I]), Par(cw[:, I:I + BC]), Par(cw[:, I + BC:]), Par(row2(cb[:I])), Par(row2(cb[I:I + BC])),
                    Par(row2(cb[I + BC:])), Par(_pad_lanes(p['l0_ssd_dt_bias'])), Par(_pad_lanes(p['l0_ssd_a_log'])),
                    Par(sel_hd, diff=False), Par(sel_lane, diff=False)]
        (xs, bm, cm, xdt, adt_b), landed = stage_fwd("ssd_pre", cfg, ssd_pre, pre_rows, pre_pars,
                                                     [(I, F32), (BC, F32), (BC, F32), (I, F32), (H * LANES, F32)],
                                                     ride=gather_ride([('l0_ffn_w_in', 0)]))
        gather_finish([('l0_ffn_w_in', 0)], landed)
        y_raw, ssd_states, landed = ssd_scan_fwd(cfg, xdt, bm, cm, adt_b, ride=gather_ride([('l0_ffn_w_in', 1)]))
        gather_finish([('l0_ffn_w_in', 1)], landed)
        post_rows = [Row(y_raw, I), Row(xs, I), Row(u0, I, 0)]
        post_pars = [Par(_pad_lanes(p['l0_ssd_d'])), Par(row2(p['l0_ssd_norm'])), Par(sel_hd, diff=False)]
        (y_ssd,) = stage_fwd("ssd_post", cfg, ssd_post, post_rows, post_pars, [(I, BF16)], tr=light)
        ret_g = row2(p['l0_ret_norm'])
        y_ret, landed = ret_fwd(cfg, u0, q_off, ret_g, rconst, ride=gather_ride(['l0_ffn_w_out']))
        gather_finish(['l0_ffn_w_out'], landed)
        ycat0 = jnp.concatenate([y_ssd, y_ret], axis=1)
        h1 = matmul("l0_out_proj", ycat0, wfull['l0_w_out'], resid=h0, cfg=cfg)

        def ffn_fwd(tag, h, norm_g, w_in, conv_w, conv_b, w_out, gather_names=(), act_gather_names=()):
            hn = norm_fwd(tag + "_ffn_norm", h, norm_g)
            if gather_names:
                u, landed = matmul(tag + "_ffn_in", hn, w_in, ride=gather_ride(list(gather_names)))
                gather_finish(gather_names, landed)
            else:
                u = matmul(tag + "_ffn_in", hn, w_in)
            rws = [Row(u, F, 0, tail=True), Row(u, F, 1, tail=True)]
            prs = [Par(conv_w[:, :F]), Par(conv_w[:, F:]), Par(row2(conv_b[:F])), Par(row2(conv_b[F:]))]
            if act_gather_names:
                (act,), landed = stage_fwd(tag + "_ffn_act", cfg, _ffn_act_fn, rws, prs, [(F, BF16)],
                                           ride=gather_ride(list(act_gather_names)))
                gather_finish(act_gather_names, landed)
            else:
                (act,) = stage_fwd(tag + "_ffn_act", cfg, _ffn_act_fn, rws, prs, [(F, BF16)])
            h_out = matmul(tag + "_ffn_out", act, w_out, resid=h, cfg=cfg)
            return h_out, (hn, u, rws, prs, act)

        h2, ffn0_saved = ffn_fwd("l0", h1, p['l0_ffn_norm'], wfull['l0_ffn_w_in'], sfull['l0_ffn_conv_w'], p['l0_ffn_conv_b'],
                                 wfull['l0_ffn_w_out'], gather_names=[('l1_w_in', 0)], act_gather_names=[('l1_w_in', 1)])

        hn2 = norm_fwd("l1_mix_norm", h2, p['l1_mix_norm'])
        u1 = matmul("l1_in_proj", hn2, wfull['l1_w_in'])
        late_names = ['l1_w_out', 'l1_ffn_w_in', 'l1_ffn_w_out']
        y_sb, sb_tot, landed = sb_fwd(cfg, u1, 0, ride=gather_ride(late_names))
        gather_finish(late_names, landed)
        lru_pre = _make_lru_pre(cfg)
        gate_blk = (3 * SW) // LW
        lpre_rows = [Row(u1, LW, gate_blk + 1, tail=True)]
        lpre_pars = [Par(sfull['l1_lru_conv_w']), Par(row2(p['l1_lru_conv_b'])), Par(p['l1_lru_wa']), Par(row2(p['l1_lru_ba'])),
                     Par(p['l1_lru_wx']), Par(row2(p['l1_lru_bx'])), Par(row2(p['l1_lru_lambda']))]
        lru_a, lru_b = stage_fwd("lru_pre", cfg, lru_pre, lpre_rows, lpre_pars, [(LW, F32), (LW, F32)])
        lru_h = lru_scan_fwd(cfg, lru_a, lru_b)
        lpost_rows = [Row(lru_h, LW), Row(u1, LW, gate_blk)]
        (y_lru,) = stage_fwd("lru_post", cfg, _lru_post_fn, lpost_rows, [], [(LW, BF16)], tr=light)
        ycat1 = jnp.concatenate([y_sb, y_lru], axis=1)
        h3 = matmul("l1_out_proj", ycat1, wfull['l1_w_out'], resid=h2, cfg=cfg)
        h4, ffn1_saved = ffn_fwd("l1", h3, p['l1_ffn_norm'], wfull['l1_ffn_w_in'], sfull['l1_ffn_conv_w'], p['l1_ffn_conv_b'],
                                 wfull['l1_ffn_w_out'])

        loss_part, dh4, d_final = final_loss(cfg, h4, loss_target.reshape(nb * cfg.seq, D), row2(p['final_norm']))
        loss = lax.psum(loss_part[0, 0], ("x", "y", "c"))
        gfull, grep = {}, {'final_norm': d_final[0]}

        def ffn_bwd(tag, h, norm_g, w_in, w_out, saved, dh_out, ride_names=(), act_ride_names=()):
            hn, u, rws, prs, act = saved
            dact = matmul(tag + "_ffn_out_dx", dh_out, w_out, tb=True, out_dtype=BF16)
            gfull[tag + '_ffn_w_out'] = matmul(tag + "_ffn_out_dw", act, dh_out, ta=True, out_dtype=BF16, tm_cap=512, tn_cap=512, tk_cap=4352)
            if act_ride_names:
                (dug, duu), (dwg, dwu, dbg, dbu), landed = stage_bwd(tag + "_ffn_act_bwd", cfg, _ffn_act_fn, rws, prs,
                                                                     [[Row(dact, F)]], [BF16, BF16],
                                                                     ride=grads_ride(list(act_ride_names)))
                parts_of.update(zip(act_ride_names, landed))
            else:
                (dug, duu), (dwg, dwu, dbg, dbu) = stage_bwd(tag + "_ffn_act_bwd", cfg, _ffn_act_fn, rws, prs,
                                                             [[Row(dact, F)]], [BF16, BF16])
            du = jnp.concatenate([dug, duu], axis=1)
            gfull[tag + '_ffn_conv_w'] = jnp.concatenate([dwg, dwu], axis=1)
            grep[tag + '_ffn_conv_b'] = jnp.concatenate([dbg, dbu], axis=1)[0]
            if ride_names:
                dhn, landed = matmul(tag + "_ffn_in_dx", du, w_in, tb=True, ride=grads_ride(list(ride_names)))
                parts_of.update(zip(ride_names, landed))
            else:
                dhn = matmul(tag + "_ffn_in_dx", du, w_in, tb=True)
            gfull[tag + '_ffn_w_in'] = matmul(tag + "_ffn_in_dw", hn, du, ta=True, out_dtype=BF16, tm_cap=512, tn_cap=512, tk_cap=4352)
            dh, grep[tag + '_ffn_norm'] = norm_bwd(tag + "_ffn_norm_bwd", h, norm_g, dhn, dh_out)
            return dh

        parts_of = {}

        def grads_ride(names, extra=(), extra_same=()):
            sends = [scatter_cols(gfull[n]) if n in BIG_COL else gfull[n].reshape(ND, -1, gfull[n].shape[-1]) for n in names]
            return Ride(sends + list(extra), [False] * len(names) + list(extra_same))

        dh3 = ffn_bwd("l1", h3, p['l1_ffn_norm'], wfull['l1_ffn_w_in'], wfull['l1_ffn_w_out'], ffn1_saved, dh4)

        dycat1 = matmul("l1_out_dx", dh3, wfull['l1_w_out'], tb=True, out_dtype=BF16)
        gfull['l1_w_out'] = matmul("l1_out_dw", ycat1, dh3, ta=True, out_dtype=BF16, tm_cap=512, tn_cap=512, tk_cap=4352)
        (d_lru_h, d_gate), _ = stage_bwd("lru_post_bwd", cfg, _lru_post_fn, lpost_rows, [], [[Row(dycat1, LW, SW // LW)]], [F32, BF16],
                                         tr=light)
        d_lru_a, d_lru_b = lru_scan_bwd(cfg, lru_a, lru_h, d_lru_h)
        (d_xr,), lgr = stage_bwd("lru_pre_bwd", cfg, lru_pre, lpre_rows, lpre_pars, [[Row(d_lru_a, LW)], [Row(d_lru_b, LW)]], [BF16])
        gfull['l1_lru_conv_w'] = lgr[0]
        grep.update({'l1_lru_conv_b': lgr[1][0], 'l1_lru_wa': lgr[2], 'l1_lru_ba': lgr[3][0], 'l1_lru_wx': lgr[4],
                     'l1_lru_bx': lgr[5][0], 'l1_lru_lambda': lgr[6][0]})
        names = ['l1_ffn_w_out', 'l1_ffn_w_in', 'l1_w_out']
        early = _pack([grep[n] for n in REP_EARLY])
        (d_q, d_k, d_v), landed = sb_bwd(cfg, u1, 0, sb_tot, dycat1, 0, ride=grads_ride(names, [early], [True]))
        parts_of.update(zip(names + ['rep_early'], landed))
        du1 = jnp.concatenate([d_q, d_k, d_v, d_gate, d_xr], axis=1)
        gfull['l1_w_in'] = matmul("l1_in_dw", hn2, du1, ta=True, out_dtype=BF16, tm_cap=512, tn_cap=512, tk_cap=4352)
        dhn2 = matmul("l1_in_dx", du1, wfull['l1_w_in'], tb=True)
        dh2, grep['l1_mix_norm'] = norm_bwd("l1_mix_norm_bwd", h2, p['l1_mix_norm'], dhn2, dh3)

        dh1 = ffn_bwd("l0", h1, p['l0_ffn_norm'], wfull['l0_ffn_w_in'], wfull['l0_ffn_w_out'], ffn0_saved, dh2,
                      ride_names=['l0_ffn_w_out'], act_ride_names=['l1_w_in'])

        dycat0 = matmul("l0_out_dx", dh1, wfull['l0_w_out'], tb=True, out_dtype=BF16)
        gfull['l0_w_out'] = matmul("l0_out_dw", ycat0, dh1, ta=True, out_dtype=BF16, tm_cap=512, tn_cap=512, tk_cap=4352)
        (d_yraw, d_xs1, d_z), (d_dskip, d_ssdnorm) = stage_bwd("ssd_post_bwd", cfg, ssd_post, post_rows, post_pars,
                                                               [[Row(dycat0, I, 0)]], [F32, F32, BF16])
        names = ['l0_ffn_w_in', 'l0_w_out']
        (d_xdt, d_bm, d_cm, d_adt), landed = ssd_scan_bwd(cfg, xdt, bm, cm, adt_b, ssd_states, d_yraw, ride=grads_ride(names))
        parts_of.update(zip(names, landed))
        (d_xs, d_b, d_c, d_dt), sgr = stage_bwd("ssd_pre_bwd", cfg, ssd_pre, pre_rows, pre_pars,
                                                 [[Row(d_xs1, I)], [Row(d_bm, BC)], [Row(d_cm, BC)], [Row(d_xdt, I)], [Row(d_adt, H * LANES)]],
                                                 [BF16] * 4)
        gfull['l0_ssd_conv_w'] = jnp.concatenate(sgr[0:3], axis=1)
        grep.update({'l0_ssd_conv_b': jnp.concatenate(sgr[3:6], axis=1)[0], 'l0_ssd_dt_bias': sgr[6][0, :H],
                     'l0_ssd_a_log': sgr[7][0, :H], 'l0_ssd_d': d_dskip[0, :H], 'l0_ssd_norm': d_ssdnorm[0]})
        (d_rq, d_rk, d_rv, d_rg), d_retnorm = ret_bwd(cfg, u0, q_off, ret_g, rconst, dycat0, I)
        grep['l0_ret_norm'] = d_retnorm[0]
        du0 = jnp.concatenate([d_z, d_xs, d_b, d_c, d_rq, d_rk, d_rv, d_rg, d_dt, jnp.zeros((rows, dt_w - LANES), BF16)], axis=1)
        dw0a = matmul("l0_in_dw", hn0, du0, ta=True, out_dtype=BF16, tm_cap=512, tn_cap=512, tk_cap=4352)
        gfull['l0_w_in'] = jnp.concatenate([dw0a[:, :q_off], dw0a[:, dt_off:dt_off + H], dw0a[:, q_off:dt_off]], axis=1)
        dhn0, landed = matmul("l0_in_dx", du0, w0a, tb=True, ride=grads_ride(['l0_w_in']))
        parts_of['l0_w_in'] = landed[0]
        dh0, grep['l0_mix_norm'] = norm_bwd("l0_mix_norm_bwd", h0, p['l0_mix_norm'], dhn0, dh1)

        dh0 = dh0.reshape(nb, cfg.tp, D)
        grad_x = dh0[:, cfg.chunk:, :]
        gfull['meta_tokens'] = jnp.sum(dh0[:, cfg.pad:cfg.chunk, :], axis=0)

        flat = jnp.concatenate([scatter_cols(gfull[n].astype(F32)).reshape(ND, -1) for n in SMALL_SHARDED], axis=1)
        quantum = PACK_ROWS * LANES
        small_send = jnp.pad(flat, ((0, 0), (0, -flat.shape[1] % quantum))).reshape(ND, -1, LANES)
        late = _pack([grep[n] for n in REP_LATE])
        landed = exchange("exchange_last_grads", Ride([small_send, late], [False, True]))
        parts_of.update(zip(['small', 'rep_late'], landed))

        grad, delta, new_m, new_v = {}, {}, {}, {}
        for n in BIG:
            grad[n], delta[n], new_m[n], new_v[n] = adamw("adamw_" + n, parts_of[n], p[n], m_in[n], v_in[n])
        for names, tag in ((SMALL_SHARDED, 'small'), (REP_EARLY, 'rep_early'), (REP_LATE, 'rep_late')):
            shapes = [p[n].shape for n in names]
            res = adamw("adamw_" + tag, parts_of[tag], _pack([p[n] for n in names]), _pack([m_in[n] for n in names]),
                        _pack([v_in[n] for n in names]))
            for dst, packed in zip((grad, delta, new_m, new_v), res):
                for n, a in zip(names, _unpack(packed, shapes)):
                    dst[n] = a
        return (loss, grad_x, *[grad[n] for n in WEIGHTS], *[delta[n] for n in WEIGHTS], *[new_m[n] for n in WEIGHTS],
                *[new_v[n] for n in WEIGHTS])

    return step


_STEP = make_step(CFG)


def kernel(x, meta_tokens, l0_mix_norm, l0_w_in, l0_ssd_conv_w, l0_ssd_conv_b, l0_ssd_dt_bias, l0_ssd_a_log, l0_ssd_d, l0_ssd_norm, l0_ret_norm, l0_w_out, l0_ffn_norm, l0_ffn_w_in, l0_ffn_conv_w, l0_ffn_conv_b, l0_ffn_w_out, l1_mix_norm, l1_w_in, l1_lru_conv_w, l1_lru_conv_b, l1_lru_wa, l1_lru_ba, l1_lru_wx, l1_lru_bx, l1_lru_lambda, l1_w_out, l1_ffn_norm, l1_ffn_w_in, l1_ffn_conv_w, l1_ffn_conv_b, l1_ffn_w_out, final_norm, loss_target, m_meta_tokens, m_l0_mix_norm, m_l0_w_in, m_l0_ssd_conv_w, m_l0_ssd_conv_b, m_l0_ssd_dt_bias, m_l0_ssd_a_log, m_l0_ssd_d, m_l0_ssd_norm, m_l0_ret_norm, m_l0_w_out, m_l0_ffn_norm, m_l0_ffn_w_in, m_l0_ffn_conv_w, m_l0_ffn_conv_b, m_l0_ffn_w_out, m_l1_mix_norm, m_l1_w_in, m_l1_lru_conv_w, m_l1_lru_conv_b, m_l1_lru_wa, m_l1_lru_ba, m_l1_lru_wx, m_l1_lru_bx, m_l1_lru_lambda, m_l1_w_out, m_l1_ffn_norm, m_l1_ffn_w_in, m_l1_ffn_conv_w, m_l1_ffn_conv_b, m_l1_ffn_w_out, m_final_norm, v_meta_tokens, v_l0_mix_norm, v_l0_w_in, v_l0_ssd_conv_w, v_l0_ssd_conv_b, v_l0_ssd_dt_bias, v_l0_ssd_a_log, v_l0_ssd_d, v_l0_ssd_norm, v_l0_ret_norm, v_l0_w_out, v_l0_ffn_norm, v_l0_ffn_w_in, v_l0_ffn_conv_w, v_l0_ffn_conv_b, v_l0_ffn_w_out, v_l1_mix_norm, v_l1_w_in, v_l1_lru_conv_w, v_l1_lru_conv_b, v_l1_lru_wa, v_l1_lru_ba, v_l1_lru_wx, v_l1_lru_bx, v_l1_lru_lambda, v_l1_w_out, v_l1_ffn_norm, v_l1_ffn_w_in, v_l1_ffn_conv_w, v_l1_ffn_conv_b, v_l1_ffn_w_out, v_final_norm):
    args = locals()
    p = {n: args[n] for n in WEIGHTS}
    m_in = {n: args["m_" + n] for n in WEIGHTS}
    v_in = {n: args["v_" + n] for n in WEIGHTS}
    return _STEP(p, m_in, v_in, x, loss_target)
```

```python
import functools
from typing import NamedTuple

import numpy as np
import jax
import jax.numpy as jnp
from jax import lax
from jax.experimental import pallas as pl
from jax.experimental.pallas import tpu as pltpu

F32 = jnp.float32
BF16 = jnp.bfloat16
EPS = 1e-6
LRU_C = 8.0
NEG = -1e30
SUBLANES = 8
LANES = 128
VMEM_LIMIT = 56 * 1024 * 1024
CHUNK_UNROLL = 4

ADAM_LR = 0.001
ADAM_B1 = 0.9
ADAM_B2 = 0.999
ADAM_EPS = 1e-08
ADAM_WD = 0.01
ADAM_STEP = 10

WEIGHTS = ['meta_tokens', 'l0_mix_norm', 'l0_w_in', 'l0_ssd_conv_w', 'l0_ssd_conv_b', 'l0_ssd_dt_bias', 'l0_ssd_a_log',
           'l0_ssd_d', 'l0_ssd_norm', 'l0_ret_norm', 'l0_w_out', 'l0_ffn_norm', 'l0_ffn_w_in', 'l0_ffn_conv_w',
           'l0_ffn_conv_b', 'l0_ffn_w_out', 'l1_mix_norm', 'l1_w_in', 'l1_lru_conv_w', 'l1_lru_conv_b', 'l1_lru_wa',
           'l1_lru_ba', 'l1_lru_wx', 'l1_lru_bx', 'l1_lru_lambda', 'l1_w_out', 'l1_ffn_norm', 'l1_ffn_w_in',
           'l1_ffn_conv_w', 'l1_ffn_conv_b', 'l1_ffn_w_out', 'final_norm']
BIG_COL = ['l0_w_in', 'l0_ffn_w_in', 'l1_w_in', 'l1_ffn_w_in']
BIG_ROW = ['l0_w_out', 'l0_ffn_w_out', 'l1_w_out', 'l1_ffn_w_out']
BIG = BIG_COL + BIG_ROW
SMALL_SHARDED = ['meta_tokens', 'l0_ssd_conv_w', 'l0_ffn_conv_w', 'l1_lru_conv_w', 'l1_ffn_conv_w']
REPLICATED = [w for w in WEIGHTS if w not in BIG and w not in SMALL_SHARDED]
REP_EARLY = ['final_norm', 'l1_ffn_norm', 'l1_ffn_conv_b', 'l1_lru_conv_b', 'l1_lru_wa', 'l1_lru_ba', 'l1_lru_wx',
             'l1_lru_bx', 'l1_lru_lambda']
REP_LATE = [w for w in REPLICATED if w not in REP_EARLY]


class Config(NamedTuple):
    d_model: int = 1024
    seq: int = 2048
    n_meta: int = 16
    chunk: int = 128
    ssd_heads: int = 16
    ssd_head_dim: int = 64
    ssd_groups: int = 4
    ssd_state: int = 128
    ret_heads: int = 4
    ret_dim: int = 256
    sb_heads: int = 16
    sb_head_dim: int = 64
    lru_width: int = 1024
    lru_blocks: int = 8
    ffn_dim: int = 2816
    n_dev: int = 8

    @property
    def t(self):
        return self.n_meta + self.seq

    @property
    def pad(self):
        return (-self.t) % self.chunk

    @property
    def tp(self):
        return self.t + self.pad

    @property
    def nc(self):
        return self.tp // self.chunk

    @property
    def ssd_inner(self):
        return self.ssd_heads * self.ssd_head_dim

    @property
    def ssd_gw(self):
        return self.ssd_inner // self.ssd_groups

    @property
    def ssd_bc(self):
        return self.ssd_groups * self.ssd_state

    @property
    def ret_w(self):
        return self.ret_heads * self.ret_dim

    @property
    def sb_w(self):
        return self.sb_heads * self.sb_head_dim

    @property
    def mix0_segs(self):
        return (self.ssd_inner, self.ssd_inner + 2 * self.ssd_bc, self.ssd_heads, self.ret_w, self.ret_w, self.ret_w, self.ret_w)


CFG = Config()


def _dg(a, b, ca, cb):
    return lax.dot_general(a, b, (((ca,), (cb,)), ((), ())), preferred_element_type=F32)


def _cot_a(b, ca, cb, g):
    return _dg(g, b, 1, 1 - cb) if ca == 1 else _dg(b, g, 1 - cb, 1)


def _cot_b(a, ca, cb, g):
    return _dg(a, g, 1 - ca, 0) if cb == 0 else _dg(g, a, 0, 1 - ca)


@functools.partial(jax.custom_vjp, nondiff_argnums=(2, 3))
def _bdot(a, b, ca, cb):
    return _dg(a.astype(BF16), b.astype(BF16), ca, cb)


def _bdot_fwd(a, b, ca, cb):
    return _bdot(a, b, ca, cb), (a, b)


def _bdot_bwd(ca, cb, res, g):
    a, b = res
    gb = g.astype(BF16)
    return _cot_a(b.astype(BF16), ca, cb, gb).astype(a.dtype), _cot_b(a.astype(BF16), ca, cb, gb).astype(b.dtype)


_bdot.defvjp(_bdot_fwd, _bdot_bwd)


def _dot(a, b):
    return _bdot(a, b, 1, 0)


def _dot_nt(a, b):
    return _bdot(a, b, 1, 1)


def _dot_tn(a, b):
    return _bdot(a, b, 0, 0)


def _split3(a):
    hi = a.astype(BF16)
    r1 = a - hi.astype(F32)
    mid = r1.astype(BF16)
    return hi, mid, (r1 - mid.astype(F32)).astype(BF16)


@functools.partial(jax.custom_vjp, nondiff_argnums=(2, 3, 4))
def _edot(a, b, ca, cb, const):
    if const == 'b':
        bb = b.astype(BF16)
        return sum(_dg(p, bb, ca, cb) for p in _split3(a))
    ab = a.astype(BF16)
    return sum(_dg(ab, p, ca, cb) for p in _split3(b))


def _edot_fwd(a, b, ca, cb, const):
    return _edot(a, b, ca, cb, const), (a, b)


def _edot_bwd(ca, cb, const, res, g):
    a, b = res
    if const == 'b':
        bb = b.astype(BF16)
        return sum(_cot_a(bb, ca, cb, p) for p in _split3(g)), jnp.zeros_like(b)
    ab = a.astype(BF16)
    return jnp.zeros_like(a), sum(_cot_b(ab, ca, cb, p) for p in _split3(g))


_edot.defvjp(_edot_fwd, _edot_bwd)


def _dot_exact01(a, sel):
    return _edot(a, sel, 1, 0, 'b')


def _silu(x):
    return x * jax.nn.sigmoid(x)


def _softplus(x):
    return jnp.maximum(x, 0.0) + jnp.log1p(jnp.exp(-jnp.abs(x)))


def _neg_expm1(x):
    series = -x * (1.0 + x * 0.5 * (1.0 + x / 3.0 * (1.0 + x * 0.25)))
    return jnp.where(x > -0.01, series, 1.0 - jnp.exp(x))


def _causal_conv(x, tail, w, b):
    taps, rows = w.shape[0], x.shape[0]
    xx = jnp.concatenate([tail, x], axis=0)
    y = b
    for k in range(taps):
        off = SUBLANES - (taps - 1 - k)
        y = y + w[k:k + 1, :] * xx[off:off + rows, :]
    return y


def _valid_chunk(cfg, chunk_idx):
    first = (chunk_idx % cfg.nc) == 0
    rid = lax.broadcasted_iota(jnp.int32, (cfg.chunk, 1), 0)
    return jnp.where(jnp.logical_and(first, rid < cfg.pad), 0.0, 1.0).astype(F32)


def _cparams(**kw):
    return pltpu.CompilerParams(vmem_limit_bytes=VMEM_LIMIT, **kw)


class Row(NamedTuple):
    arr: jax.Array
    width: int
    blk: int = 0
    tail: bool = False
    diff: bool = True


class Par(NamedTuple):
    arr: jax.Array
    diff: bool = True


def _row_specs(tr, rows, order):
    specs, args = [], []
    for r in rows:
        specs.append(pl.BlockSpec((tr, r.width), functools.partial(lambda i, b, o: (o(i), b), b=r.blk, o=order)))
        args.append(r.arr)
        if r.tail:
            per = tr // SUBLANES
            specs.append(pl.BlockSpec((SUBLANES, r.width),
                                      functools.partial(lambda i, b, o: (jnp.maximum(o(i) * per - 1, 0), b), b=r.blk, o=order)))
            args.append(r.arr)
    return specs, args


def _par_specs(pars):
    specs = [pl.BlockSpec(p.arr.shape, functools.partial(lambda i, nd: (0,) * nd, nd=p.arr.ndim)) for p in pars]
    return specs, [p.arr for p in pars]


def _valid_block(cfg, blk, tr, n_rows):
    row = blk * tr + lax.broadcasted_iota(jnp.int32, (tr, 1), 0)
    inpad = jnp.zeros((tr, 1), jnp.bool_)
    for s in range(n_rows // cfg.tp):
        inpad = jnp.logical_or(inpad, jnp.logical_and(row >= s * cfg.tp, row < s * cfg.tp + cfg.pad))
    return jnp.where(inpad, 0.0, 1.0).astype(F32)


def light_rows(n_rows):
    return n_rows // 8 if n_rows % (8 * 16) == 0 else None


def stage_fwd(name, cfg, fn, rows, pars, outs, tr=None, ride=None):
    tr = tr or cfg.chunk
    n_rows = rows[0].arr.shape[0]
    rspecs, rargs = _row_specs(tr, rows, lambda i: i)
    pspecs, pargs = _par_specs(pars)
    n_in = len(rargs) + len(pargs)

    def body(*refs):
        valid = _valid_block(cfg, pl.program_id(0), tr, n_rows)
        vals = [r[...].astype(F32) for r in refs[:n_in]]
        res = fn(valid, *vals)
        for o, v in zip(refs[n_in:], res):
            o[...] = v.astype(o.dtype)

    res, landed = call_with_ride(
        body, name=name, grid=(n_rows // tr,), in_specs=rspecs + pspecs, args=rargs + pargs,
        out_specs=[pl.BlockSpec((tr, w), lambda i: (i, 0)) for w, _ in outs],
        out_shape=[jax.ShapeDtypeStruct((n_rows, w), dt) for w, dt in outs], ride=ride)
    return (res, landed) if ride is not None else res


def stage_bwd(name, cfg, fn, rows, pars, douts, drow_dtypes, tr=None, ride=None):
    tr = tr or cfg.chunk
    n_rows = rows[0].arr.shape[0]
    n_blk = n_rows // tr
    order = lambda i: n_blk - 1 - i
    rspecs, rargs = _row_specs(tr, rows, order)
    pspecs, pargs = _par_specs(pars)
    pieces = [p for out in douts for p in out]
    dspecs, dargs = _row_specs(tr, pieces, order)
    n_r, n_p, n_d = len(rargs), len(pargs), len(dargs)
    diff_rows = [r for r in rows if r.diff]
    diff_pars = [p for p in pars if p.diff]
    tails = [r for r in diff_rows if r.tail]

    def body(*refs):
        in_refs = refs[:n_r + n_p]
        d_refs = refs[n_r + n_p:n_r + n_p + n_d]
        o_refs = refs[n_r + n_p + n_d:]
        drow_refs = o_refs[:len(diff_rows)]
        dpar_refs = o_refs[len(diff_rows):len(diff_rows) + len(diff_pars)]
        carry_refs = o_refs[len(diff_rows) + len(diff_pars):]
        step = pl.program_id(0)
        valid = _valid_block(cfg, order(step), tr, n_rows)
        vals = [r[...].astype(F32) for r in in_refs]
        slots, pos = [], 0
        for r in rows:
            if r.diff:
                slots.append(pos)
                if r.tail:
                    slots.append(pos + 1)
            pos += 2 if r.tail else 1
        for p in pars:
            if p.diff:
                slots.append(pos)
            pos += 1

        def g(*dv):
            full = list(vals)
            for s, v in zip(slots, dv):
                full[s] = v
            return tuple(fn(valid, *full))

        _, vjp = jax.vjp(g, *[vals[s] for s in slots])
        cts, k = [], 0
        for out in douts:
            parts = [d_refs[k + j][...].astype(F32) for j in range(len(out))]
            k += len(out)
            cts.append(parts[0] if len(parts) == 1 else jnp.concatenate(parts, axis=1))
        grads = list(vjp(tuple(cts)))

        @pl.when(step == 0)
        def _():
            for c in carry_refs:
                c[...] = jnp.zeros_like(c)
            for d in dpar_refs:
                d[...] = jnp.zeros_like(d)

        gi, ci = 0, 0
        for r, o in zip(diff_rows, drow_refs):
            dx = grads[gi]
            gi += 1
            if r.tail:
                dtail = grads[gi]
                gi += 1
                c = carry_refs[ci]
                ci += 1
                dx = dx + jnp.concatenate([jnp.zeros((tr - SUBLANES, r.width), F32), c[...]], axis=0)
                c[...] = dtail
            o[...] = dx.astype(o.dtype)
        for d in dpar_refs:
            d[...] += grads[gi]
            gi += 1

    out_specs = [pl.BlockSpec((tr, r.width), lambda i: (order(i), 0)) for r in diff_rows]
    out_shape = [jax.ShapeDtypeStruct((n_rows, r.width), dt) for r, dt in zip(diff_rows, drow_dtypes)]
    for p in diff_pars:
        out_specs.append(pl.BlockSpec(p.arr.shape, functools.partial(lambda i, nd: (0,) * nd, nd=p.arr.ndim)))
        out_shape.append(jax.ShapeDtypeStruct(p.arr.shape, F32))
    res, landed = call_with_ride(
        body, name=name, grid=(n_blk,), in_specs=rspecs + pspecs + dspecs, args=rargs + pargs + dargs, out_specs=out_specs,
        out_shape=out_shape, scratch_shapes=[pltpu.VMEM((SUBLANES, r.width), F32) for r in tails], ride=ride,
        dimension_semantics=("arbitrary",))
    grads = (list(res[:len(diff_rows)]), list(res[len(diff_rows):]))
    return grads + (landed,) if ride is not None else grads


def _tile(n, cap):
    if n % LANES:
        return n
    q = n // LANES
    best = 1
    for k in range(1, q + 1):
        if q % k == 0 and k * LANES <= cap:
            best = k
    return best * LANES


def _mm_vmem(tm, tn, tk, a_bytes, b_bytes, out_bytes, resid, nk):
    total = 2 * (tm * tk * a_bytes + tk * tn * b_bytes + tm * tn * out_bytes) + tm * tn * 4
    total += 2 * tm * tn * 4 if resid else 0
    total += (tm * tk * 2 if a_bytes == 4 else 0) + (tk * tn * 2 if b_bytes == 4 else 0)
    return total + (tm * tn * 4 if nk > 1 else 0)


def matmul(name, a, b, *, ta=False, tb=False, out_dtype=F32, resid=None, cfg=None, tm_cap=2176, tn_cap=1024, tk_cap=2816,
           ride=None):
    m, k = (a.shape[1], a.shape[0]) if ta else a.shape
    n = b.shape[0] if tb else b.shape[1]
    tm, tn, tk = _tile(m, tm_cap), _tile(n, tn_cap), _tile(k, tk_cap)
    budget = VMEM_LIMIT - 8 * 1024 * 1024
    sizes = (a.dtype.itemsize, b.dtype.itemsize, jnp.dtype(out_dtype).itemsize, resid is not None)
    while _mm_vmem(tm, tn, tk, *sizes, k // tk) > budget:
        if tm > 256:
            tm = _tile(m, tm - 1)
        elif tk > 512:
            tk = _tile(k, tk - 1)
        else:
            tn = _tile(n, tn - 1)
    nk = k // tk
    a_spec = pl.BlockSpec((tk, tm), lambda i, j, l: (l, i)) if ta else pl.BlockSpec((tm, tk), lambda i, j, l: (i, l))
    b_spec = pl.BlockSpec((tn, tk), lambda i, j, l: (j, l)) if tb else pl.BlockSpec((tk, tn), lambda i, j, l: (l, j))
    dims = (((0 if ta else 1,), (1 if tb else 0,)), ((), ()))
    in_specs, args = [a_spec, b_spec], [a, b]
    if resid is not None:
        in_specs.append(pl.BlockSpec((tm, tn), lambda i, j, l: (i, j)))
        args.append(resid)

    def body(*refs):
        a_ref, b_ref = refs[0], refs[1]
        o_ref = refs[3] if resid is not None else refs[2]
        part = lax.dot_general(a_ref[...].astype(BF16), b_ref[...].astype(BF16), dims, preferred_element_type=F32)

        def finish(res):
            if resid is not None:
                row = pl.program_id(0) * tm + lax.broadcasted_iota(jnp.int32, (tm, 1), 0)
                inpad = jnp.zeros((tm, 1), jnp.bool_)
                for s in range(m // cfg.tp):
                    inpad = jnp.logical_or(inpad, jnp.logical_and(row >= s * cfg.tp, row < s * cfg.tp + cfg.pad))
                res = refs[2][...] + jnp.where(inpad, 0.0, res)
            o_ref[...] = res.astype(o_ref.dtype)

        if nk == 1:
            finish(part)
            return
        acc = refs[-1]
        l = pl.program_id(2)

        @pl.when(l == 0)
        def _():
            acc[...] = part

        @pl.when(jnp.logical_and(l > 0, l < nk - 1))
        def _():
            acc[...] += part

        @pl.when(l == nk - 1)
        def _():
            finish(acc[...] + part)

    outs, landed = call_with_ride(
        body, name=name, grid=(m // tm, n // tn, nk), in_specs=in_specs, args=args,
        out_specs=[pl.BlockSpec((tm, tn), lambda i, j, l: (i, j))], out_shape=[jax.ShapeDtypeStruct((m, n), out_dtype)],
        scratch_shapes=[pltpu.VMEM((tm, tn), F32)] if nk > 1 else [], ride=ride,
        dimension_semantics=("parallel", "parallel", "arbitrary"))
    return (outs[0], landed) if ride is not None else outs[0]


def _norm_fn(valid, h, g):
    y = h * lax.rsqrt(jnp.mean(h * h, axis=-1, keepdims=True) + EPS)
    return (valid * (y * g),)


def _make_ssd_pre(cfg):
    nh = cfg.ssd_heads

    def fn(valid, xs, xs_t, bm, bm_t, cm, cm_t, dtr, w_xs, w_b, w_c, b_xs, b_b, b_c, dt_bias, a_log, sel_hd, sel_lane):
        xs = valid * _silu(_causal_conv(xs, xs_t, w_xs, b_xs))
        bm = valid * _silu(_causal_conv(bm, bm_t, w_b, b_b))
        cm = valid * _silu(_causal_conv(cm, cm_t, w_c, b_c))
        lane = lax.broadcasted_iota(jnp.int32, (1, LANES), 1)
        head = jnp.where(lane < nh, 1.0, 0.0).astype(F32)
        dt = valid * head * _softplus(dtr + dt_bias)
        adt = dt * (-jnp.exp(a_log))
        x = xs * _dot_exact01(dt, sel_hd)
        adt_b = _dot_exact01(adt, sel_lane)
        return xs, bm, cm, x, adt_b

    return fn


def _make_ssd_post(cfg):
    gw = cfg.ssd_gw

    def fn(valid, y_raw, xs, z, d_skip, norm_g, sel_hd):
        d_rep = _dot_exact01(jnp.broadcast_to(d_skip, (SUBLANES, LANES)), sel_hd)[0:1]
        y = (y_raw + xs * d_rep) * _silu(z)
        parts = []
        for g in range(cfg.ssd_groups):
            yg = y[:, g * gw:(g + 1) * gw]
            parts.append(yg * lax.rsqrt(jnp.mean(yg * yg, axis=-1, keepdims=True) + EPS))
        return (jnp.concatenate(parts, axis=1) * norm_g,)

    return fn


def _ffn_act_fn(valid, ug, ug_t, uu, uu_t, w_g, w_u, b_g, b_u):
    return (valid * _silu(_causal_conv(ug, ug_t, w_g, b_g)) * _causal_conv(uu, uu_t, w_u, b_u),)


def _make_lru_pre(cfg):
    nb = cfg.lru_blocks
    bw = cfg.lru_width // nb

    def fn(valid, xr, xr_t, w_conv, b_conv, wa, ba, wx, bx, lam):
        xc = _causal_conv(xr, xr_t, w_conv, b_conv)
        a_parts, b_parts = [], []
        for n in range(nb):
            sl = slice(n * bw, (n + 1) * bw)
            xn = xc[:, sl]
            r = jax.nn.sigmoid(_dot(xn, wa[n]) + ba[:, sl])
            i = jax.nn.sigmoid(_dot(xn, wx[n]) + bx[:, sl])
            log_a = -LRU_C * r * _softplus(-lam[:, sl])
            a_parts.append(jnp.exp(log_a))
            b_parts.append(valid * jnp.sqrt(jnp.maximum(_neg_expm1(2.0 * log_a), 0.0)) * (i * xn))
        return jnp.concatenate(a_parts, axis=1), jnp.concatenate(b_parts, axis=1)

    return fn


def _lru_post_fn(valid, hs, gate):
    return (hs * jax.nn.gelu(gate),)


def _tri_consts(n):
    r = lax.broadcasted_iota(jnp.int32, (n, n), 0)
    c = lax.broadcasted_iota(jnp.int32, (n, n), 1)
    return r, c


def _ssd_chunk(x, bm, cm, a, s, tril, lower):
    rows = x.shape[0]
    heads = a.shape[1] // LANES
    p = x.shape[1] // heads
    cb = _dot_nt(cm, bm)
    ys, ss = [], []
    for e in range(heads):
        ae = a[:, e * LANES:(e + 1) * LANES]
        cs = _edot(tril, ae, 1, 0, 'a')
        lmat = jnp.exp(jnp.where(lower, cs - cs.T, NEG))
        xe = x[:, e * p:(e + 1) * p]
        se = s[e * p:(e + 1) * p, :]
        tot = cs[rows - 1:rows, :]
        y_diag = _dot(cb * lmat, xe)
        st = _dot_tn(xe * jnp.exp(tot - cs)[:, :p], bm)
        y_off = _dot_nt(cm, se) * jnp.exp(cs)[:, :p]
        ys.append(y_diag + y_off)
        ss.append(jnp.exp(tot[:, :1]) * se + st)
    return jnp.concatenate(ys, axis=1), jnp.concatenate(ss, axis=0)


def _ssd_specs(cfg, x, bm, cm, adt_b):
    gw, ns = cfg.ssd_gw, cfg.ssd_state
    hpg = cfg.ssd_heads // cfg.ssd_groups
    specs = [pl.BlockSpec((cfg.tp, gw), lambda b, g: (b, g)),
             pl.BlockSpec((cfg.tp, ns), lambda b, g: (b, g)),
             pl.BlockSpec((cfg.tp, ns), lambda b, g: (b, g)),
             pl.BlockSpec((cfg.tp, hpg * LANES), lambda b, g: (b, g))]
    return specs, [x, bm, cm, adt_b]


def ssd_scan_fwd(cfg, x, bm, cm, adt_b, ride=None):
    nb = x.shape[0] // cfg.tp
    L, nc = cfg.chunk, cfg.nc
    specs, args = _ssd_specs(cfg, x, bm, cm, adt_b)

    def body(x_ref, b_ref, c_ref, a_ref, y_ref, states_ref, s_ref):
        r, c = _tri_consts(L)
        lower = r >= c
        tril = lower.astype(F32)
        s_ref[...] = jnp.zeros_like(s_ref)

        def step(ci, carry):
            rows = pl.ds(pl.multiple_of(ci * L, L), L)
            states_ref[ci] = s_ref[...]
            y, s_new = _ssd_chunk(x_ref[rows, :], b_ref[rows, :], c_ref[rows, :], a_ref[rows, :], s_ref[...], tril, lower)
            y_ref[rows, :] = y
            s_ref[...] = s_new
            return carry

        lax.fori_loop(0, nc, step, 0, unroll=CHUNK_UNROLL)

    gw, ns = cfg.ssd_gw, cfg.ssd_state
    outs, landed = call_with_ride(
        body, name="ssd_scan_fwd", grid=(nb, cfg.ssd_groups), in_specs=specs, args=args,
        out_specs=[specs[0], pl.BlockSpec((None, None, nc, gw, ns), lambda b, g: (b, g, 0, 0, 0))],
        out_shape=[jax.ShapeDtypeStruct(x.shape, F32), jax.ShapeDtypeStruct((nb, cfg.ssd_groups, nc, gw, ns), F32)],
        scratch_shapes=[pltpu.VMEM((gw, ns), F32)], ride=ride)
    return outs[0], outs[1], landed


def ssd_scan_bwd(cfg, x, bm, cm, adt_b, states, dy, ride=None):
    nb = x.shape[0] // cfg.tp
    L, nc = cfg.chunk, cfg.nc
    specs, args = _ssd_specs(cfg, x, bm, cm, adt_b)
    state_spec = pl.BlockSpec((None, None, nc, cfg.ssd_gw, cfg.ssd_state), lambda b, g: (b, g, 0, 0, 0))

    def body(x_ref, b_ref, c_ref, a_ref, s_all, dy_ref, dx_ref, db_ref, dc_ref, da_ref, ds_ref):
        r, c = _tri_consts(L)
        lower = r >= c
        tril = lower.astype(F32)
        chunk = functools.partial(_ssd_chunk, tril=tril, lower=lower)
        ds_ref[...] = jnp.zeros_like(ds_ref)

        def bwd(k, carry):
            ci = nc - 1 - k
            rows = pl.ds(pl.multiple_of(ci * L, L), L)
            _, vjp = jax.vjp(chunk, x_ref[rows, :], b_ref[rows, :], c_ref[rows, :], a_ref[rows, :], s_all[ci])
            dx, db, dc, da, ds = vjp((dy_ref[rows, :], ds_ref[...]))
            dx_ref[rows, :] = dx
            db_ref[rows, :] = db
            dc_ref[rows, :] = dc
            da_ref[rows, :] = da
            ds_ref[...] = ds
            return carry

        lax.fori_loop(0, nc, bwd, 0, unroll=CHUNK_UNROLL)

    return call_with_ride(
        body, name="ssd_scan_bwd", grid=(nb, cfg.ssd_groups), in_specs=specs + [state_spec, specs[0]], args=args + [states, dy],
        out_specs=specs, out_shape=[jax.ShapeDtypeStruct(t.shape, F32) for t in (x, bm, cm, adt_b)],
        scratch_shapes=[pltpu.VMEM((cfg.ssd_gw, cfg.ssd_state), F32)], ride=ride)


def _ret_chunk(q, k, v, g, norm_g, r_prev, cos, sin, valid, decay, zeta, xi, cdec, scale):
    half = q.shape[1] // 2

    def rot(t):
        t1, t2 = t[:, :half], t[:, half:]
        return jnp.concatenate([t1 * cos - t2 * sin, t1 * sin + t2 * cos], axis=1)

    qr = valid * rot(q)
    kr = valid * rot(k) * scale
    v = valid * v
    inner = _dot(_dot_nt(qr, kr) * decay, v)
    kv = _dot_tn(kr * zeta, v)
    cross = _dot(qr, r_prev) * xi
    o = inner + cross
    o = o - jnp.mean(o, axis=-1, keepdims=True)
    o = o * lax.rsqrt(jnp.mean(o * o, axis=-1, keepdims=True) + EPS)
    return _silu(g) * (o * norm_g), cdec * r_prev + kv


def _ret_consts(cfg):
    f32 = jnp.float32
    log_gamma = jnp.log1p(-jnp.exp2(-5.0 - jnp.arange(cfg.ret_heads, dtype=f32)))
    idx = jnp.arange(cfg.chunk, dtype=f32)
    diff = idx[:, None] - idx[None, :]
    decay = jnp.where(diff >= 0, jnp.exp(log_gamma[:, None, None] * jnp.maximum(diff, 0.0)), 0.0)
    zeta = jnp.exp(log_gamma[:, None] * (cfg.chunk - 1 - idx)[None, :])[..., None]
    xi = jnp.exp(log_gamma[:, None] * (idx + 1.0)[None, :])[..., None]
    cdec = jnp.exp(cfg.chunk * log_gamma)[:, None, None]
    half = cfg.ret_dim // 2
    inv_freq = 1.0 / (10000.0 ** (jnp.arange(half, dtype=f32) / (half - 1)))
    pos = jnp.arange(cfg.tp, dtype=f32) - cfg.pad
    ang = pos[:, None] * inv_freq[None, :]
    return dict(decay=decay, zeta=zeta, xi=xi, cdec=cdec, cos=jnp.cos(ang), sin=jnp.sin(ang))


def _ret_specs(cfg, u, col0, norm_g, consts):
    dk, nh, L = cfg.ret_dim, cfg.ret_heads, cfg.chunk
    base = col0 // dk
    seg = lambda s: pl.BlockSpec((cfg.tp, dk), functools.partial(lambda h, b, s: (b, base + s * nh + h), s=s))
    half = dk // 2
    specs = [seg(0), seg(1), seg(2), seg(3),
             pl.BlockSpec((1, dk), lambda h, b: (0, h)),
             pl.BlockSpec((cfg.tp, half), lambda h, b: (0, 0)),
             pl.BlockSpec((cfg.tp, half), lambda h, b: (0, 0)),
             pl.BlockSpec((None, L, L), lambda h, b: (h, 0, 0)),
             pl.BlockSpec((None, L, 1), lambda h, b: (h, 0, 0)),
             pl.BlockSpec((None, L, 1), lambda h, b: (h, 0, 0)),
             pl.BlockSpec((None, 1, 1), lambda h, b: (h, 0, 0))]
    args = [u, u, u, u, norm_g, consts["cos"], consts["sin"], consts["decay"], consts["zeta"], consts["xi"], consts["cdec"]]
    return specs, args


def _ret_chunk_at(cfg, refs, ci):
    L = cfg.chunk
    cos_ref, sin_ref, decay_ref, zeta_ref, xi_ref, cdec_ref = refs
    rows = pl.ds(pl.multiple_of(ci * L, L), L)
    fn = functools.partial(_ret_chunk, cos=cos_ref[rows, :], sin=sin_ref[rows, :], valid=_valid_chunk(cfg, ci),
                           decay=decay_ref[...], zeta=zeta_ref[...], xi=xi_ref[...], cdec=cdec_ref[...],
                           scale=cfg.ret_dim ** -0.5)
    return fn, rows


def ret_fwd(cfg, u, col0, norm_g, consts, ride=None):
    nb = u.shape[0] // cfg.tp
    dk, nc = cfg.ret_dim, cfg.nc
    specs, args = _ret_specs(cfg, u, col0, norm_g, consts)

    def body(q_ref, k_ref, v_ref, g_ref, ng_ref, *rest):
        y_ref, r_ref = rest[-2], rest[-1]
        r_ref[...] = jnp.zeros_like(r_ref)

        def step(ci, carry):
            fn, rows = _ret_chunk_at(cfg, rest[:6], ci)
            y, r_new = fn(q_ref[rows, :], k_ref[rows, :], v_ref[rows, :], g_ref[rows, :], ng_ref[...], r_ref[...])
            y_ref[rows, :] = y.astype(y_ref.dtype)
            r_ref[...] = r_new
            return carry

        lax.fori_loop(0, nc, step, 0, unroll=CHUNK_UNROLL)

    outs, landed = call_with_ride(
        body, name="ret_fwd", grid=(cfg.ret_heads, nb), in_specs=specs, args=args,
        out_specs=[pl.BlockSpec((cfg.tp, dk), lambda h, b: (b, h))],
        out_shape=[jax.ShapeDtypeStruct((u.shape[0], cfg.ret_w), BF16)], scratch_shapes=[pltpu.VMEM((dk, dk), F32)], ride=ride)
    return outs[0], landed


def ret_bwd(cfg, u, col0, norm_g, consts, dy, dy_col0):
    nb = u.shape[0] // cfg.tp
    dk, nc, nh = cfg.ret_dim, cfg.nc, cfg.ret_heads
    specs, args = _ret_specs(cfg, u, col0, norm_g, consts)
    dy_base = dy_col0 // dk
    specs.append(pl.BlockSpec((cfg.tp, dk), lambda h, b: (b, dy_base + h)))
    seg_out = lambda s: pl.BlockSpec((cfg.tp, dk), functools.partial(lambda h, b, s: (b, s * nh + h), s=s))

    def body(q_ref, k_ref, v_ref, g_ref, ng_ref, *rest):
        consts_refs, dy_ref = rest[:6], rest[6]
        dq_ref, dk_ref, dv_ref, dg_ref, dng_ref, r_all, dr_ref = rest[7:]
        dr_ref[...] = jnp.zeros_like(dr_ref)

        def fwd(ci, carry):
            fn, rows = _ret_chunk_at(cfg, consts_refs, ci)
            r_all[ci] = dr_ref[...]
            _, r_new = fn(q_ref[rows, :], k_ref[rows, :], v_ref[rows, :], g_ref[rows, :], ng_ref[...], dr_ref[...])
            dr_ref[...] = r_new
            return carry

        lax.fori_loop(0, nc, fwd, 0, unroll=CHUNK_UNROLL)
        dr_ref[...] = jnp.zeros_like(dr_ref)

        @pl.when(pl.program_id(1) == 0)
        def _():
            dng_ref[...] = jnp.zeros_like(dng_ref)

        def bwd(kk, carry):
            ci = nc - 1 - kk
            fn, rows = _ret_chunk_at(cfg, consts_refs, ci)
            _, vjp = jax.vjp(fn, q_ref[rows, :], k_ref[rows, :], v_ref[rows, :], g_ref[rows, :], ng_ref[...], r_all[ci])
            dq, dkk, dv, dg, dng, dr = vjp((dy_ref[rows, :].astype(F32), dr_ref[...]))
            dq_ref[rows, :] = dq.astype(dq_ref.dtype)
            dk_ref[rows, :] = dkk.astype(dk_ref.dtype)
            dv_ref[rows, :] = dv.astype(dv_ref.dtype)
            dg_ref[rows, :] = dg.astype(dg_ref.dtype)
            dng_ref[...] += dng
            dr_ref[...] = dr
            return carry

        lax.fori_loop(0, nc, bwd, 0, unroll=CHUNK_UNROLL)

    seg_shape = jax.ShapeDtypeStruct((u.shape[0], cfg.ret_w), BF16)
    res = pl.pallas_call(
        body, grid=(nh, nb), in_specs=specs,
        out_specs=[pl.BlockSpec((cfg.tp, dk), lambda h, b: (b, h))] * 4 + [pl.BlockSpec((1, dk), lambda h, b: (0, h))],
        out_shape=[seg_shape] * 4 + [jax.ShapeDtypeStruct((1, cfg.ret_w), F32)],
        scratch_shapes=[pltpu.VMEM((nc, dk, dk), F32), pltpu.VMEM((dk, dk), F32)],
        name="ret_bwd", compiler_params=_cparams(dimension_semantics=("arbitrary", "arbitrary")))(*args, dy)
    return res[:4], res[4]


def _dot2(a, tri):
    hi = a.astype(BF16)
    lo = (a - hi.astype(F32)).astype(BF16)
    return _dg(hi, tri, 1, 0) + _dg(lo, tri, 1, 0)


def _sb_group(cfg):
    return max(g for g in (4, 2, 1) if g <= cfg.nc)


def _sb_specs(cfg, u, col0):
    base = col0 // LANES
    per = cfg.sb_w // LANES
    seg = lambda s: pl.BlockSpec((cfg.tp, LANES), functools.partial(lambda b, p, s: (b, base + s * per + p), s=s))
    return [seg(0), seg(1), seg(2)], [u, u, u]


def _sb_scores(cfg, qh, ks, mask):
    z = _dg(qh, ks, 1, 1)
    sp = jnp.maximum(z, 0.0) + jnp.log(1.0 + jnp.exp(-jnp.abs(z)))
    return z, sp, jnp.where(mask, -sp, 0.0)


def _sb_running(x, tri, run, order):
    L = x.shape[0]
    parts = [None] * (x.shape[1] // L)
    for b in order:
        blk = x[:, b * L:(b + 1) * L]
        parts[b] = _dot2(blk, tri) + run
        run = run + jnp.sum(blk, axis=1, keepdims=True)
    return jnp.concatenate(parts, axis=1) if len(parts) > 1 else parts[0], run


def sb_fwd(cfg, u, col0, ride=None):
    nb = u.shape[0] // cfg.tp
    L, nc = cfg.chunk, cfg.nc
    hd = cfg.sb_head_dim
    scale = hd ** -0.5
    G = _sb_group(cfg)
    specs, args = _sb_specs(cfg, u, col0)

    def body(q_ref, k_ref, v_ref, y_ref, tot_ref):
        tr, tc = _tri_consts(L)
        after_tri = (tr > tc).astype(BF16)
        rid = lax.broadcasted_iota(jnp.int32, (L, G * L), 0)
        cid = lax.broadcasted_iota(jnp.int32, (L, G * L), 1)
        lane = lax.broadcasted_iota(jnp.int32, (1, LANES), 1)
        mine = [jnp.logical_and(lane >= hh * hd, lane < (hh + 1) * hd) for hh in range(2)]

        def qloop(i, carry):
            qrows = pl.ds(pl.multiple_of(i * L, L), L)
            q = q_ref[qrows, :]
            qhs = [jnp.where(mine[hh], q * scale, 0.0).astype(BF16) for hh in range(2)]

            def gloop(gg, c):
                accs, runs = c
                hi = i - gg * G
                start = jnp.maximum(hi - (G - 1), 0)
                krows = pl.ds(pl.multiple_of(start * L, L), G * L)
                ks = k_ref[krows, :].astype(BF16)
                vs = v_ref[krows, :].astype(BF16)
                kpos = start * L + cid
                mask = jnp.logical_and(jnp.logical_and(kpos < i * L + rid, kpos >= cfg.pad), kpos < (hi + 1) * L)
                new_accs, new_runs = [], []
                for hh in range(2):
                    z, sp, l1m = _sb_scores(cfg, qhs[hh], ks, mask)
                    after, run = _sb_running(l1m, after_tri, runs[hh], range(G - 1, -1, -1))
                    w = jnp.where(mask, jnp.exp(z - sp + after), 0.0)
                    new_accs.append(accs[hh] + _dg(w.astype(BF16), vs, 1, 0))
                    new_runs.append(run)
                return tuple(new_accs), tuple(new_runs)

            zero_acc, zero_run = jnp.zeros((L, LANES), F32), jnp.zeros((L, 1), F32)
            accs, runs = lax.fori_loop(0, (i + G) // G, gloop, ((zero_acc, zero_acc), (zero_run, zero_run)))
            for hh in range(2):
                tot_ref[0, hh, qrows, :] = runs[hh]
            y_ref[qrows, :] = jnp.where(lane < hd, accs[0], accs[1]).astype(y_ref.dtype)
            return carry

        lax.fori_loop(0, nc, qloop, 0)

    (y, tot), landed = call_with_ride(
        body, name="sb_fwd", grid=(nb, cfg.sb_w // LANES), in_specs=specs, args=args,
        out_specs=[pl.BlockSpec((cfg.tp, LANES), lambda b, p: (b, p)),
                   pl.BlockSpec((1, 2, cfg.tp, 1), lambda b, p: (b, p, 0, 0))],
        out_shape=[jax.ShapeDtypeStruct((u.shape[0], cfg.sb_w), BF16),
                   jax.ShapeDtypeStruct((nb, cfg.sb_heads, cfg.tp, 1), F32)], ride=ride)
    return y, tot, landed


def sb_bwd(cfg, u, col0, tot, dy, dy_col0, ride=None):
    nb = u.shape[0] // cfg.tp
    L, nc = cfg.chunk, cfg.nc
    hd = cfg.sb_head_dim
    scale = hd ** -0.5
    specs, args = _sb_specs(cfg, u, col0)
    dy_base = dy_col0 // LANES
    specs += [pl.BlockSpec((1, 2, cfg.tp, 1), lambda b, p: (b, p, 0, 0)),
              pl.BlockSpec((cfg.tp, LANES), lambda b, p: (b, dy_base + p))]

    G = _sb_group(cfg)

    def body(q_ref, k_ref, v_ref, tot_ref, dy_ref, dq_ref, dk_ref, dv_ref, dk_acc, dv_acc):
        tr, tc = _tri_consts(L)
        upto_tri = (tr <= tc).astype(BF16)
        before_tri = (tr < tc).astype(BF16)
        rid = lax.broadcasted_iota(jnp.int32, (L, G * L), 0)
        cid = lax.broadcasted_iota(jnp.int32, (L, G * L), 1)
        lane = lax.broadcasted_iota(jnp.int32, (1, LANES), 1)
        mine = [jnp.logical_and(lane >= hh * hd, lane < (hh + 1) * hd) for hh in range(2)]
        dk_acc[...] = jnp.zeros_like(dk_acc)
        dv_acc[...] = jnp.zeros_like(dv_acc)

        def qloop(i, carry):
            qrows = pl.ds(pl.multiple_of(i * L, L), L)
            q = q_ref[qrows, :]
            dy = dy_ref[qrows, :].astype(F32)
            qhs = [jnp.where(mine[hh], q * scale, 0.0).astype(BF16) for hh in range(2)]
            dyhs = [jnp.where(mine[hh], dy, 0.0).astype(BF16) for hh in range(2)]
            tots = [tot_ref[0, hh, qrows, :] for hh in range(2)]

            def gloop(g, c):
                dq, run_ls, run_as = c
                lo = g * G
                start = jnp.minimum(lo, nc - G)
                krows = pl.ds(pl.multiple_of(start * L, L), G * L)
                k = k_ref[krows, :]
                ks = k.astype(BF16)
                vs = v_ref[krows, :].astype(BF16)
                kpos = start * L + cid
                mask = jnp.logical_and(jnp.logical_and(kpos < i * L + rid, kpos >= cfg.pad), kpos >= lo * L)
                new_ls, new_as = [], []
                dv_add = jnp.zeros((G * L, LANES), F32)
                dk_add = jnp.zeros((G * L, LANES), F32)
                for hh in range(2):
                    z, sp, l1m = _sb_scores(cfg, qhs[hh], ks, mask)
                    upto, run_l = _sb_running(l1m, upto_tri, run_ls[hh], range(G))
                    w = jnp.where(mask, jnp.exp(z - sp + (tots[hh] - upto)), 0.0)
                    da = w * _dg(dyhs[hh], vs, 1, 1)
                    d_l1m, run_a = _sb_running(da, before_tri, run_as[hh], range(G))
                    sg = jnp.exp(z - sp)
                    dz = jnp.where(mask, da * (1.0 - sg) - d_l1m * sg, 0.0).astype(BF16)
                    dv_add = dv_add + _dg(w.astype(BF16), dyhs[hh], 0, 0)
                    dk_add = dk_add + _dg(dz, qhs[hh], 0, 0)
                    dq = dq + _dg(dz, jnp.where(mine[hh], k, 0.0).astype(BF16), 1, 0)
                    new_ls.append(run_l)
                    new_as.append(run_a)
                dv_acc[krows, :] += dv_add
                dk_acc[krows, :] += dk_add
                return dq, tuple(new_ls), tuple(new_as)

            zero = jnp.zeros((L, 1), F32)
            dq, _, _ = lax.fori_loop(0, (i + G) // G, gloop, (jnp.zeros((L, LANES), F32), (zero, zero), (zero, zero)))
            dq_ref[qrows, :] = (dq * scale).astype(dq_ref.dtype)
            return carry

        lax.fori_loop(0, nc, qloop, 0)
        dk_ref[...] = dk_acc[...].astype(dk_ref.dtype)
        dv_ref[...] = dv_acc[...].astype(dv_ref.dtype)

    out_spec = pl.BlockSpec((cfg.tp, LANES), lambda b, p: (b, p))
    seg_shape = jax.ShapeDtypeStruct((u.shape[0], cfg.sb_w), BF16)
    return call_with_ride(
        body, name="sb_bwd", grid=(nb, cfg.sb_w // LANES), in_specs=specs, args=args + [tot, dy], out_specs=[out_spec] * 3,
        out_shape=[seg_shape] * 3, scratch_shapes=[pltpu.VMEM((cfg.tp, LANES), F32), pltpu.VMEM((cfg.tp, LANES), F32)], ride=ride)


def _lru_cols(cfg):
    return _tile(cfg.lru_width, 256)


def lru_scan_fwd(cfg, a, b):
    nb = a.shape[0] // cfg.tp
    cw = _lru_cols(cfg)
    spec = pl.BlockSpec((cfg.tp, cw), lambda s, c: (s, c))

    def body(a_ref, b_ref, h_ref):
        def step(i, h):
            rows = pl.ds(pl.multiple_of(i * SUBLANES, SUBLANES), SUBLANES)
            at, bt = a_ref[rows, :], b_ref[rows, :]
            outs = []
            for r in range(SUBLANES):
                h = at[r:r + 1, :] * h + bt[r:r + 1, :]
                outs.append(h)
            h_ref[rows, :] = jnp.concatenate(outs, axis=0)
            return h

        lax.fori_loop(0, cfg.tp // SUBLANES, step, jnp.zeros((1, cw), F32))

    return pl.pallas_call(
        body, grid=(nb, cfg.lru_width // cw), in_specs=[spec, spec], out_specs=spec,
        out_shape=jax.ShapeDtypeStruct(a.shape, F32), name="lru_scan_fwd", compiler_params=_cparams())(a, b)


def lru_scan_bwd(cfg, a, h, dh):
    nb = a.shape[0] // cfg.tp
    cw = _lru_cols(cfg)
    nt = cfg.tp // SUBLANES
    spec = pl.BlockSpec((cfg.tp, cw), lambda s, c: (s, c))

    def body(a_ref, h_ref, dh_ref, da_ref, db_ref):
        def step(k, c):
            i = nt - 1 - k
            rows = pl.ds(pl.multiple_of(i * SUBLANES, SUBLANES), SUBLANES)
            prev = pl.ds(pl.multiple_of(jnp.maximum(i - 1, 0) * SUBLANES, SUBLANES), SUBLANES)
            at, ht, dht = a_ref[rows, :], h_ref[rows, :], dh_ref[rows, :]
            h_before = jnp.where(i > 0, h_ref[prev, :][SUBLANES - 1:SUBLANES, :], 0.0)
            das, dbs = [None] * SUBLANES, [None] * SUBLANES
            for r in range(SUBLANES - 1, -1, -1):
                g = dht[r:r + 1, :] + c
                dbs[r] = g
                das[r] = g * (ht[r - 1:r, :] if r > 0 else h_before)
                c = at[r:r + 1, :] * g
            da_ref[rows, :] = jnp.concatenate(das, axis=0)
            db_ref[rows, :] = jnp.concatenate(dbs, axis=0)
            return c

        lax.fori_loop(0, nt, step, jnp.zeros((1, cw), F32))

    return pl.pallas_call(
        body, grid=(nb, cfg.lru_width // cw), in_specs=[spec] * 3, out_specs=[spec] * 2,
        out_shape=[jax.ShapeDtypeStruct(a.shape, F32)] * 2, name="lru_scan_bwd", compiler_params=_cparams())(a, h, dh)


def final_loss(cfg, h, target, norm_g):
    assert cfg.seq % cfg.chunk == 0 and cfg.n_meta + cfg.pad == cfg.chunk
    L, nc, d = cfg.chunk, cfg.nc, cfg.d_model
    per_seq = cfg.seq // L

    def tgt_map(i):
        return ((i // nc) * per_seq + jnp.maximum(i % nc - 1, 0), 0)

    def body(h_ref, t_ref, g_ref, loss_ref, dh_ref, dg_ref):
        i = pl.program_id(0)
        real = jnp.where(i % nc == 0, 0.0, 1.0)
        tgt = t_ref[...]

        def loss_fn(hv, g):
            y = hv * lax.rsqrt(jnp.mean(hv * hv, axis=-1, keepdims=True) + EPS) * g
            return 0.5 * real * jnp.sum(jnp.mean(jnp.square(y - tgt), axis=-1))

        val, (dh, dg) = jax.value_and_grad(loss_fn, argnums=(0, 1))(h_ref[...], g_ref[...])

        @pl.when(i == 0)
        def _():
            loss_ref[...] = jnp.zeros_like(loss_ref)
            dg_ref[...] = jnp.zeros_like(dg_ref)

        loss_ref[...] += jnp.broadcast_to(val, loss_ref.shape)
        dg_ref[...] += dg
        dh_ref[...] = dh

    return pl.pallas_call(
        body, grid=(h.shape[0] // L,),
        in_specs=[pl.BlockSpec((L, d), lambda i: (i, 0)), pl.BlockSpec((L, d), tgt_map), pl.BlockSpec((1, d), lambda i: (0, 0))],
        out_specs=[pl.BlockSpec((SUBLANES, LANES), lambda i: (0, 0)), pl.BlockSpec((L, d), lambda i: (i, 0)),
                   pl.BlockSpec((1, d), lambda i: (0, 0))],
        out_shape=[jax.ShapeDtypeStruct((SUBLANES, LANES), F32), jax.ShapeDtypeStruct(h.shape, F32),
                   jax.ShapeDtypeStruct((1, d), F32)],
        name="final_loss", compiler_params=_cparams(dimension_semantics=("arbitrary",)))(h, target, norm_g)


N_PEER = 7


class Ride(NamedTuple):
    arrays: list
    same: list


def _ride_copies(ride_in, ride_out, same, sems, sending):
    send_sems, recv_sems, local_sems = sems
    n = len(ride_in)
    x, y, c = lax.axis_index("x"), lax.axis_index("y"), lax.axis_index("c")
    me = 4 * x + 2 * y + c

    def slab(w, dest):
        return ride_in[w] if same[w] else ride_in[w].at[dest]

    local = [pltpu.make_async_copy(slab(w, me), ride_out[w].at[me], local_sems.at[w]) for w in range(n)]
    remote = []
    for r in range(1, N_PEER + 1):
        peer = (1 - x if r & 4 else x, 1 - y if r & 2 else y, 1 - c if r & 1 else c)
        pidx = 4 * peer[0] + 2 * peer[1] + peer[2]
        for w in range(n):
            remote.append(pltpu.make_async_remote_copy(
                src_ref=slab(w, pidx), dst_ref=ride_out[w].at[me if sending else pidx], send_sem=send_sems.at[w, r - 1],
                recv_sem=recv_sems.at[w, r - 1], device_id=peer, device_id_type=pl.DeviceIdType.MESH))
    return local, remote


def _ride_start(ride_in, ride_out, same, sems):
    local, remote = _ride_copies(ride_in, ride_out, same, sems, True)
    for cp in local + remote:
        cp.start()


def _ride_wait(ride_in, ride_out, same, sems):
    local, remote = _ride_copies(ride_in, ride_out, same, sems, False)
    for cp in remote:
        cp.wait_recv()
    for cp in remote:
        cp.wait_send()
    for cp in local:
        cp.wait()


def call_with_ride(body, *, name, grid, in_specs, args, out_specs, out_shape, scratch_shapes=(), ride=None, **cparams):
    out_specs, out_shape, scratch_shapes = list(out_specs), list(out_shape), list(scratch_shapes)
    if ride is None:
        res = pl.pallas_call(body, grid=grid, in_specs=list(in_specs), out_specs=out_specs, out_shape=out_shape,
                             scratch_shapes=scratch_shapes, name=name, compiler_params=_cparams(**cparams))(*args)
        return list(res), []
    arrays, same = ride
    n, n_in, n_out, n_scr = len(arrays), len(args), len(out_shape), len(scratch_shapes)
    hbm = pl.BlockSpec(memory_space=pl.ANY)
    land_shape = [jax.ShapeDtypeStruct((8,) + (a.shape if s else a.shape[1:]), a.dtype) for a, s in zip(arrays, same)]

    def wrapped(*refs):
        ins, ride_in = refs[:n_in], refs[n_in:n_in + n]
        outs, ride_out = refs[n_in + n:n_in + n + n_out], refs[n_in + n + n_out:n_in + 2 * n + n_out]
        scr, sems = refs[n_in + 2 * n + n_out:n_in + 2 * n + n_out + n_scr], refs[n_in + 2 * n + n_out + n_scr:]
        first, last = True, True
        for axis, size in enumerate(grid):
            first = jnp.logical_and(first, pl.program_id(axis) == 0)
            last = jnp.logical_and(last, pl.program_id(axis) == size - 1)
        if grid:
            pl.when(first)(lambda: _ride_start(ride_in, ride_out, same, sems))
        else:
            _ride_start(ride_in, ride_out, same, sems)
        body(*ins, *outs, *scr)
        if grid:
            pl.when(last)(lambda: _ride_wait(ride_in, ride_out, same, sems))
        else:
            _ride_wait(ride_in, ride_out, same, sems)

    sems = [pltpu.SemaphoreType.DMA((n, N_PEER)), pltpu.SemaphoreType.DMA((n, N_PEER)), pltpu.SemaphoreType.DMA((n,))]
    cparams["dimension_semantics"] = ("arbitrary",) * len(grid)
    res = pl.pallas_call(wrapped, grid=grid, in_specs=list(in_specs) + [hbm] * n, out_specs=out_specs + [hbm] * n,
                         out_shape=out_shape + land_shape, scratch_shapes=scratch_shapes + sems, name=name,
                         compiler_params=_cparams(**cparams))(*args, *arrays)
    return list(res[:n_out]), list(res[n_out:])


def exchange(name, ride):
    return call_with_ride(lambda: None, name=name, grid=(), in_specs=[], args=[], out_specs=[], out_shape=[], ride=ride)[1]


def _row_tile(r, mult, cap):
    best = None
    for t in range(mult, min(r, cap) + 1, mult):
        if r % t == 0:
            best = t
    return best if best is not None else r


def adamw(name, parts, w, m, v):
    r, c = w.shape
    tr = _row_tile(r, 16, 256)
    spec = pl.BlockSpec((tr, c), lambda i: (i, 0))

    def body(p_ref, w_ref, m_ref, v_ref, g_ref, d_ref, m2_ref, v2_ref):
        g = p_ref[0].astype(F32)
        for dev in range(1, 8):
            g = g + p_ref[dev].astype(F32)
        m2 = ADAM_B1 * m_ref[...] + (1.0 - ADAM_B1) * g
        v2 = ADAM_B2 * v_ref[...] + (1.0 - ADAM_B2) * jnp.square(g)
        m_hat = m2 / (1.0 - ADAM_B1 ** ADAM_STEP)
        v_hat = v2 / (1.0 - ADAM_B2 ** ADAM_STEP)
        g_ref[...] = g
        d_ref[...] = -ADAM_LR * (m_hat / (jnp.sqrt(v_hat) + ADAM_EPS) + ADAM_WD * w_ref[...])
        m2_ref[...] = m2
        v2_ref[...] = v2

    return pl.pallas_call(
        body, grid=(r // tr,), in_specs=[pl.BlockSpec((8, tr, c), lambda i: (0, i, 0)), spec, spec, spec],
        out_specs=[spec] * 4, out_shape=[jax.ShapeDtypeStruct((r, c), F32)] * 4, name=name, compiler_params=_cparams())(parts, w, m, v)


PACK_ROWS = 256


def _pack(arrs):
    flat = jnp.concatenate([a.reshape(-1).astype(F32) for a in arrs])
    quantum = PACK_ROWS * LANES
    total = -(-flat.shape[0] // quantum) * quantum
    return jnp.pad(flat, (0, total - flat.shape[0])).reshape(-1, LANES)


def _unpack(packed, shapes):
    flat = packed.reshape(-1)
    out, off = [], 0
    for s in shapes:
        n = int(np.prod(s))
        out.append(flat[off:off + n].reshape(s))
        off += n
    return out


def _pad_lanes(vec):
    return jnp.pad(vec.astype(F32), (0, LANES - vec.shape[0]))[None, :]


def make_step(cfg):
    I, BC, H, RW, SW, LW, F, D = (cfg.ssd_inner, cfg.ssd_bc, cfg.ssd_heads, cfg.ret_w, cfg.sb_w, cfg.lru_width,
                                  cfg.ffn_dim, cfg.d_model)
    ND = cfg.n_dev
    q_off = 2 * I + 2 * BC
    dt_off = q_off + 4 * RW
    dt_w = (-dt_off) % 1024 or 1024

    def gather_cols(g):
        return jnp.transpose(g, (1, 0, 2)).reshape(g.shape[1], -1)

    def scatter_cols(full):
        r, c = full.shape
        return jnp.transpose(full.reshape(r, ND, c // ND), (1, 0, 2))

    def step(p, m_in, v_in, x, loss_target):
        nb = x.shape[0]
        rows = nb * cfg.tp

        small_shapes = [p[n].shape for n in SMALL_SHARDED]
        wfull = {}

        halves = {}

        def gather_ride(names, extra=()):
            own = []
            for n in names:
                if isinstance(n, tuple):
                    half = p[n[0]].shape[0] // 2
                    own.append(p[n[0]][n[1] * half:(n[1] + 1) * half].astype(BF16))
                else:
                    own.append(p[n].astype(BF16))
            own += list(extra)
            return Ride(own, [True] * len(own))

        def gather_finish(names, landed):
            for n, g in zip(names, landed):
                if isinstance(n, tuple):
                    halves.setdefault(n[0], {})[n[1]] = g
                    if len(halves[n[0]]) < 2:
                        continue
                    n, g = n[0], jnp.concatenate([halves[n[0]][0], halves[n[0]][1]], axis=1)
                wfull[n] = gather_cols(g) if n in BIG_COL else g.reshape(-1, g.shape[-1])

        landed = exchange("gather_l0_mixer", gather_ride(['l0_w_in'], [_pack([p[n] for n in SMALL_SHARDED])]))
        gather_finish(['l0_w_in'], landed)
        small_parts = [_unpack(landed[-1][d], small_shapes) for d in range(ND)]
        sfull = {n: jnp.concatenate([small_parts[d][k] for d in range(ND)], axis=1) for k, n in enumerate(SMALL_SHARDED)}

        w0 = wfull['l0_w_in']
        w0a = jnp.concatenate([w0[:, :q_off], w0[:, q_off + H:], jnp.pad(w0[:, q_off:q_off + H], ((0, 0), (0, dt_w - H)))], axis=1)

        sel_hd = (jnp.arange(LANES)[:, None] == (jnp.arange(I)[None, :] // cfg.ssd_head_dim)).astype(F32)
        sel_lane = (jnp.arange(LANES)[:, None] == (jnp.arange(H * LANES)[None, :] // LANES)).astype(F32)
        rconst = _ret_consts(cfg)
        row2 = lambda vec: vec.astype(F32)[None, :]

        meta = jnp.broadcast_to(sfull['meta_tokens'][None], (nb, cfg.n_meta, D))
        h0 = jnp.concatenate([jnp.zeros((nb, cfg.pad, D), F32), meta, x], axis=1).reshape(rows, D)

        light = light_rows(rows)

        def norm_fwd(name, h, g):
            return stage_fwd(name, cfg, _norm_fn, [Row(h, D)], [Par(row2(g))], [(D, BF16)], tr=light)[0]

        def norm_bwd(name, h, g, dhn, dh_next):
            fn = lambda valid, hv, gv: (_norm_fn(valid, hv, gv)[0], hv)
            (dh,), (dg,) = stage_bwd(name, cfg, fn, [Row(h, D)], [Par(row2(g))], [[Row(dhn, D)], [Row(dh_next, D)]], [F32],
                                     tr=light)
            return dh, dg[0]

        hn0 = norm_fwd("l0_mix_norm", h0, p['l0_mix_norm'])
        u0, landed = matmul("l0_in_proj", hn0, w0a, ride=gather_ride(['l0_w_out']))
        gather_finish(['l0_w_out'], landed)
        cw, cb = sfull['l0_ssd_conv_w'], p['l0_ssd_conv_b']
        ssd_pre = _make_ssd_pre(cfg)
        ssd_post = _make_ssd_post(cfg)
        pre_rows = [Row(u0, I, 1, tail=True), Row(u0, BC, (2 * I) // BC, tail=True), Row(u0, BC, (2 * I) // BC + 1, tail=True),
                    Row(u0, LANES, dt_off // LANES)]
        pre_pars = [Par(cw[:, :I]), Par(cw[:, I:I + BC]), Par(cw[:, I + BC:]), Par(row2(cb[:I])), Par(row2(cb[I:I + BC])),
                    Par(row2(cb[I + BC:])), Par(_pad_lanes(p['l0_ssd_dt_bias'])), Par(_pad_lanes(p['l0_ssd_a_log'])),
                    Par(sel_hd, diff=False), Par(sel_lane, diff=False)]
        (xs, bm, cm, xdt, adt_b), landed = stage_fwd("ssd_pre", cfg, ssd_pre, pre_rows, pre_pars,
                                                     [(I, F32), (BC, F32), (BC, F32), (I, F32), (H * LANES, F32)],
                                                     ride=gather_ride([('l0_ffn_w_in', 0)]))
        gather_finish([('l0_ffn_w_in', 0)], landed)
        y_raw, ssd_states, landed = ssd_scan_fwd(cfg, xdt, bm, cm, adt_b, ride=gather_ride([('l0_ffn_w_in', 1)]))
        gather_finish([('l0_ffn_w_in', 1)], landed)
        post_rows = [Row(y_raw, I), Row(xs, I), Row(u0, I, 0)]
        post_pars = [Par(_pad_lanes(p['l0_ssd_d'])), Par(row2(p['l0_ssd_norm'])), Par(sel_hd, diff=False)]
        (y_ssd,) = stage_fwd("ssd_post", cfg, ssd_post, post_rows, post_pars, [(I, BF16)], tr=light)
        ret_g = row2(p['l0_ret_norm'])
        y_ret, landed = ret_fwd(cfg, u0, q_off, ret_g, rconst, ride=gather_ride(['l0_ffn_w_out']))
        gather_finish(['l0_ffn_w_out'], landed)
        ycat0 = jnp.concatenate([y_ssd, y_ret], axis=1)
        h1 = matmul("l0_out_proj", ycat0, wfull['l0_w_out'], resid=h0, cfg=cfg)

        def ffn_fwd(tag, h, norm_g, w_in, conv_w, conv_b, w_out, gather_names=(), act_gather_names=()):
            hn = norm_fwd(tag + "_ffn_norm", h, norm_g)
            if gather_names:
                u, landed = matmul(tag + "_ffn_in", hn, w_in, ride=gather_ride(list(gather_names)))
                gather_finish(gather_names, landed)
            else:
                u = matmul(tag + "_ffn_in", hn, w_in)
            rws = [Row(u, F, 0, tail=True), Row(u, F, 1, tail=True)]
            prs = [Par(conv_w[:, :F]), Par(conv_w[:, F:]), Par(row2(conv_b[:F])), Par(row2(conv_b[F:]))]
            if act_gather_names:
                (act,), landed = stage_fwd(tag + "_ffn_act", cfg, _ffn_act_fn, rws, prs, [(F, BF16)],
                                           ride=gather_ride(list(act_gather_names)))
                gather_finish(act_gather_names, landed)
            else:
                (act,) = stage_fwd(tag + "_ffn_act", cfg, _ffn_act_fn, rws, prs, [(F, BF16)])
            h_out = matmul(tag + "_ffn_out", act, w_out, resid=h, cfg=cfg)
            return h_out, (hn, u, rws, prs, act)

        h2, ffn0_saved = ffn_fwd("l0", h1, p['l0_ffn_norm'], wfull['l0_ffn_w_in'], sfull['l0_ffn_conv_w'], p['l0_ffn_conv_b'],
                                 wfull['l0_ffn_w_out'], gather_names=[('l1_w_in', 0)], act_gather_names=[('l1_w_in', 1)])

        hn2 = norm_fwd("l1_mix_norm", h2, p['l1_mix_norm'])
        u1 = matmul("l1_in_proj", hn2, wfull['l1_w_in'])
        late_names = ['l1_w_out', 'l1_ffn_w_in', 'l1_ffn_w_out']
        y_sb, sb_tot, landed = sb_fwd(cfg, u1, 0, ride=gather_ride(late_names))
        gather_finish(late_names, landed)
        lru_pre = _make_lru_pre(cfg)
        gate_blk = (3 * SW) // LW
        lpre_rows = [Row(u1, LW, gate_blk + 1, tail=True)]
        lpre_pars = [Par(sfull['l1_lru_conv_w']), Par(row2(p['l1_lru_conv_b'])), Par(p['l1_lru_wa']), Par(row2(p['l1_lru_ba'])),
                     Par(p['l1_lru_wx']), Par(row2(p['l1_lru_bx'])), Par(row2(p['l1_lru_lambda']))]
        lru_a, lru_b = stage_fwd("lru_pre", cfg, lru_pre, lpre_rows, lpre_pars, [(LW, F32), (LW, F32)])
        lru_h = lru_scan_fwd(cfg, lru_a, lru_b)
        lpost_rows = [Row(lru_h, LW), Row(u1, LW, gate_blk)]
        (y_lru,) = stage_fwd("lru_post", cfg, _lru_post_fn, lpost_rows, [], [(LW, BF16)], tr=light)
        ycat1 = jnp.concatenate([y_sb, y_lru], axis=1)
        h3 = matmul("l1_out_proj", ycat1, wfull['l1_w_out'], resid=h2, cfg=cfg)
        h4, ffn1_saved = ffn_fwd("l1", h3, p['l1_ffn_norm'], wfull['l1_ffn_w_in'], sfull['l1_ffn_conv_w'], p['l1_ffn_conv_b'],
                                 wfull['l1_ffn_w_out'])

        loss_part, dh4, d_final = final_loss(cfg, h4, loss_target.reshape(nb * cfg.seq, D), row2(p['final_norm']))
        loss = lax.psum(loss_part[0, 0], ("x", "y", "c"))
        gfull, grep = {}, {'final_norm': d_final[0]}

        def ffn_bwd(tag, h, norm_g, w_in, w_out, saved, dh_out, ride_names=(), act_ride_names=()):
            hn, u, rws, prs, act = saved
            dh_b = dh_out.astype(BF16)
            dact = matmul(tag + "_ffn_out_dx", dh_b, w_out, tb=True, out_dtype=BF16)
            gfull[tag + '_ffn_w_out'] = matmul(tag + "_ffn_out_dw", act, dh_b, ta=True, out_dtype=BF16, tm_cap=512, tn_cap=512, tk_cap=4352)
            if act_ride_names:
                (dug, duu), (dwg, dwu, dbg, dbu), landed = stage_bwd(tag + "_ffn_act_bwd", cfg, _ffn_act_fn, rws, prs,
                                                                     [[Row(dact, F)]], [BF16, BF16],
                                                                     ride=grads_ride(list(act_ride_names)))
                parts_of.update(zip(act_ride_names, landed))
            else:
                (dug, duu), (dwg, dwu, dbg, dbu) = stage_bwd(tag + "_ffn_act_bwd", cfg, _ffn_act_fn, rws, prs,
                                                             [[Row(dact, F)]], [BF16, BF16])
            du = jnp.concatenate([dug, duu], axis=1)
            gfull[tag + '_ffn_conv_w'] = jnp.concatenate([dwg, dwu], axis=1)
            grep[tag + '_ffn_conv_b'] = jnp.concatenate([dbg, dbu], axis=1)[0]
            if ride_names:
                dhn, landed = matmul(tag + "_ffn_in_dx", du, w_in, tb=True, ride=grads_ride(list(ride_names)))
                parts_of.update(zip(ride_names, landed))
            else:
                dhn = matmul(tag + "_ffn_in_dx", du, w_in, tb=True)
            gfull[tag + '_ffn_w_in'] = matmul(tag + "_ffn_in_dw", hn, du, ta=True, out_dtype=BF16, tm_cap=512, tn_cap=512, tk_cap=4352)
            dh, grep[tag + '_ffn_norm'] = norm_bwd(tag + "_ffn_norm_bwd", h, norm_g, dhn, dh_out)
            return dh

        parts_of = {}

        def grads_ride(names, extra=(), extra_same=()):
            sends = [scatter_cols(gfull[n]) if n in BIG_COL else gfull[n].reshape(ND, -1, gfull[n].shape[-1]) for n in names]
            return Ride(sends + list(extra), [False] * len(names) + list(extra_same))

        dh3 = ffn_bwd("l1", h3, p['l1_ffn_norm'], wfull['l1_ffn_w_in'], wfull['l1_ffn_w_out'], ffn1_saved, dh4)

        dh3_b = dh3.astype(BF16)
        dycat1 = matmul("l1_out_dx", dh3_b, wfull['l1_w_out'], tb=True, out_dtype=BF16)
        gfull['l1_w_out'] = matmul("l1_out_dw", ycat1, dh3_b, ta=True, out_dtype=BF16, tm_cap=512, tn_cap=512, tk_cap=4352)
        (d_lru_h, d_gate), _ = stage_bwd("lru_post_bwd", cfg, _lru_post_fn, lpost_rows, [], [[Row(dycat1, LW, SW // LW)]], [F32, BF16],
                                         tr=light)
        d_lru_a, d_lru_b = lru_scan_bwd(cfg, lru_a, lru_h, d_lru_h)
        (d_xr,), lgr = stage_bwd("lru_pre_bwd", cfg, lru_pre, lpre_rows, lpre_pars, [[Row(d_lru_a, LW)], [Row(d_lru_b, LW)]], [BF16])
        gfull['l1_lru_conv_w'] = lgr[0]
        grep.update({'l1_lru_conv_b': lgr[1][0], 'l1_lru_wa': lgr[2], 'l1_lru_ba': lgr[3][0], 'l1_lru_wx': lgr[4],
                     'l1_lru_bx': lgr[5][0], 'l1_lru_lambda': lgr[6][0]})
        names = ['l1_ffn_w_out', 'l1_ffn_w_in', 'l1_w_out']
        early = _pack([grep[n] for n in REP_EARLY])
        (d_q, d_k, d_v), landed = sb_bwd(cfg, u1, 0, sb_tot, dycat1, 0, ride=grads_ride(names, [early], [True]))
        parts_of.update(zip(names + ['rep_early'], landed))
        du1 = jnp.concatenate([d_q, d_k, d_v, d_gate, d_xr], axis=1)
        gfull['l1_w_in'] = matmul("l1_in_dw", hn2, du1, ta=True, out_dtype=BF16, tm_cap=512, tn_cap=512, tk_cap=4352)
        dhn2 = matmul("l1_in_dx", du1, wfull['l1_w_in'], tb=True)
        dh2, grep['l1_mix_norm'] = norm_bwd("l1_mix_norm_bwd", h2, p['l1_mix_norm'], dhn2, dh3)

        dh1 = ffn_bwd("l0", h1, p['l0_ffn_norm'], wfull['l0_ffn_w_in'], wfull['l0_ffn_w_out'], ffn0_saved, dh2,
                      ride_names=['l0_ffn_w_out'], act_ride_names=['l1_w_in'])

        dh1_b = dh1.astype(BF16)
        dycat0 = matmul("l0_out_dx", dh1_b, wfull['l0_w_out'], tb=True, out_dtype=BF16)
        gfull['l0_w_out'] = matmul("l0_out_dw", ycat0, dh1_b, ta=True, out_dtype=BF16, tm_cap=512, tn_cap=512, tk_cap=4352)
        (d_yraw, d_xs1, d_z), (d_dskip, d_ssdnorm) = stage_bwd("ssd_post_bwd", cfg, ssd_post, post_rows, post_pars,
                                                               [[Row(dycat0, I, 0)]], [F32, F32, BF16])
        names = ['l0_ffn_w_in', 'l0_w_out']
        (d_xdt, d_bm, d_cm, d_adt), landed = ssd_scan_bwd(cfg, xdt, bm, cm, adt_b, ssd_states, d_yraw, ride=grads_ride(names))
        parts_of.update(zip(names, landed))
        (d_xs, d_b, d_c, d_dt), sgr = stage_bwd("ssd_pre_bwd", cfg, ssd_pre, pre_rows, pre_pars,
                                                 [[Row(d_xs1, I)], [Row(d_bm, BC)], [Row(d_cm, BC)], [Row(d_xdt, I)], [Row(d_adt, H * LANES)]],
                                                 [BF16] * 4)
        gfull['l0_ssd_conv_w'] = jnp.concatenate(sgr[0:3], axis=1)
        grep.update({'l0_ssd_conv_b': jnp.concatenate(sgr[3:6], axis=1)[0], 'l0_ssd_dt_bias': sgr[6][0, :H],
                     'l0_ssd_a_log': sgr[7][0, :H], 'l0_ssd_d': d_dskip[0, :H], 'l0_ssd_norm': d_ssdnorm[0]})
        (d_rq, d_rk, d_rv, d_rg), d_retnorm = ret_bwd(cfg, u0, q_off, ret_g, rconst, dycat0, I)
        grep['l0_ret_norm'] = d_retnorm[0]
        du0 = jnp.concatenate([d_z, d_xs, d_b, d_c, d_rq, d_rk, d_rv, d_rg, d_dt, jnp.zeros((rows, dt_w - LANES), BF16)], axis=1)
        dw0a = matmul("l0_in_dw", hn0, du0, ta=True, out_dtype=BF16, tm_cap=512, tn_cap=512, tk_cap=4352)
        gfull['l0_w_in'] = jnp.concatenate([dw0a[:, :q_off], dw0a[:, dt_off:dt_off + H], dw0a[:, q_off:dt_off]], axis=1)
        dhn0, landed = matmul("l0_in_dx", du0, w0a, tb=True, ride=grads_ride(['l0_w_in']))
        parts_of['l0_w_in'] = landed[0]
        dh0, grep['l0_mix_norm'] = norm_bwd("l0_mix_norm_bwd", h0, p['l0_mix_norm'], dhn0, dh1)

        dh0 = dh0.reshape(nb, cfg.tp, D)
        grad_x = dh0[:, cfg.chunk:, :]
        gfull['meta_tokens'] = jnp.sum(dh0[:, cfg.pad:cfg.chunk, :], axis=0)

        flat = jnp.concatenate([scatter_cols(gfull[n].astype(F32)).reshape(ND, -1) for n in SMALL_SHARDED], axis=1)
        quantum = PACK_ROWS * LANES
        small_send = jnp.pad(flat, ((0, 0), (0, -flat.shape[1] % quantum))).reshape(ND, -1, LANES)
        late = _pack([grep[n] for n in REP_LATE])
        landed = exchange("exchange_last_grads", Ride([small_send, late], [False, True]))
        parts_of.update(zip(['small', 'rep_late'], landed))

        grad, delta, new_m, new_v = {}, {}, {}, {}
        for n in BIG:
            grad[n], delta[n], new_m[n], new_v[n] = adamw("adamw_" + n, parts_of[n], p[n], m_in[n], v_in[n])
        for names, tag in ((SMALL_SHARDED, 'small'), (REP_EARLY, 'rep_early'), (REP_LATE, 'rep_late')):
            shapes = [p[n].shape for n in names]
            res = adamw("adamw_" + tag, parts_of[tag], _pack([p[n] for n in names]), _pack([m_in[n] for n in names]),
                        _pack([v_in[n] for n in names]))
            for dst, packed in zip((grad, delta, new_m, new_v), res):
                for n, a in zip(names, _unpack(packed, shapes)):
                    dst[n] = a
        return (loss, grad_x, *[grad[n] for n in WEIGHTS], *[delta[n] for n in WEIGHTS], *[new_m[n] for n in WEIGHTS],
                *[new_v[n] for n in WEIGHTS])

    return step


_STEP = make_step(CFG)


def kernel(x, meta_tokens, l0_mix_norm, l0_w_in, l0_ssd_conv_w, l0_ssd_conv_b, l0_ssd_dt_bias, l0_ssd_a_log, l0_ssd_d, l0_ssd_norm, l0_ret_norm, l0_w_out, l0_ffn_norm, l0_ffn_w_in, l0_ffn_conv_w, l0_ffn_conv_b, l0_ffn_w_out, l1_mix_norm, l1_w_in, l1_lru_conv_w, l1_lru_conv_b, l1_lru_wa, l1_lru_ba, l1_lru_wx, l1_lru_bx, l1_lru_lambda, l1_w_out, l1_ffn_norm, l1_ffn_w_in, l1_ffn_conv_w, l1_ffn_conv_b, l1_ffn_w_out, final_norm, loss_target, m_meta_tokens, m_l0_mix_norm, m_l0_w_in, m_l0_ssd_conv_w, m_l0_ssd_conv_b, m_l0_ssd_dt_bias, m_l0_ssd_a_log, m_l0_ssd_d, m_l0_ssd_norm, m_l0_ret_norm, m_l0_w_out, m_l0_ffn_norm, m_l0_ffn_w_in, m_l0_ffn_conv_w, m_l0_ffn_conv_b, m_l0_ffn_w_out, m_l1_mix_norm, m_l1_w_in, m_l1_lru_conv_w, m_l1_lru_conv_b, m_l1_lru_wa, m_l1_lru_ba, m_l1_lru_wx, m_l1_lru_bx, m_l1_lru_lambda, m_l1_w_out, m_l1_ffn_norm, m_l1_ffn_w_in, m_l1_ffn_conv_w, m_l1_ffn_conv_b, m_l1_ffn_w_out, m_final_norm, v_meta_tokens, v_l0_mix_norm, v_l0_w_in, v_l0_ssd_conv_w, v_l0_ssd_conv_b, v_l0_ssd_dt_bias, v_l0_ssd_a_log, v_l0_ssd_d, v_l0_ssd_norm, v_l0_ret_norm, v_l0_w_out, v_l0_ffn_norm, v_l0_ffn_w_in, v_l0_ffn_conv_w, v_l0_ffn_conv_b, v_l0_ffn_w_out, v_l1_mix_norm, v_l1_w_in, v_l1_lru_conv_w, v_l1_lru_conv_b, v_l1_lru_wa, v_l1_lru_ba, v_l1_lru_wx, v_l1_lru_bx, v_l1_lru_lambda, v_l1_w_out, v_l1_ffn_norm, v_l1_ffn_w_in, v_l1_ffn_conv_w, v_l1_ffn_conv_b, v_l1_ffn_w_out, v_final_norm):
    args = locals()
    p = {n: args[n] for n in WEIGHTS}
    m_in = {n: args["m_" + n] for n in WEIGHTS}
    v_in = {n: args["v_" + n] for n in WEIGHTS}
    return _STEP(p, m_in, v_in, x, loss_target)
```

```python
import functools
from typing import NamedTuple

import numpy as np
import jax
import jax.numpy as jnp
from jax import lax
from jax.experimental import pallas as pl
from jax.experimental.pallas import tpu as pltpu

F32 = jnp.float32
BF16 = jnp.bfloat16
EPS = 1e-6
LRU_C = 8.0
NEG = -1e30
SUBLANES = 8
LANES = 128
VMEM_LIMIT = 56 * 1024 * 1024
HALF_ROWS = 64
CHUNK_UNROLL = 4

ADAM_LR = 0.001
ADAM_B1 = 0.9
ADAM_B2 = 0.999
ADAM_EPS = 1e-08
ADAM_WD = 0.01
ADAM_STEP = 10

WEIGHTS = ['meta_tokens', 'l0_mix_norm', 'l0_w_in', 'l0_ssd_conv_w', 'l0_ssd_conv_b', 'l0_ssd_dt_bias', 'l0_ssd_a_log',
           'l0_ssd_d', 'l0_ssd_norm', 'l0_ret_norm', 'l0_w_out', 'l0_ffn_norm', 'l0_ffn_w_in', 'l0_ffn_conv_w',
           'l0_ffn_conv_b', 'l0_ffn_w_out', 'l1_mix_norm', 'l1_w_in', 'l1_lru_conv_w', 'l1_lru_conv_b', 'l1_lru_wa',
           'l1_lru_ba', 'l1_lru_wx', 'l1_lru_bx', 'l1_lru_lambda', 'l1_w_out', 'l1_ffn_norm', 'l1_ffn_w_in',
           'l1_ffn_conv_w', 'l1_ffn_conv_b', 'l1_ffn_w_out', 'final_norm']
BIG_COL = ['l0_w_in', 'l0_ffn_w_in', 'l1_w_in', 'l1_ffn_w_in']
BIG_ROW = ['l0_w_out', 'l0_ffn_w_out', 'l1_w_out', 'l1_ffn_w_out']
BIG = BIG_COL + BIG_ROW
SMALL_SHARDED = ['meta_tokens', 'l0_ssd_conv_w', 'l0_ffn_conv_w', 'l1_lru_conv_w', 'l1_ffn_conv_w']
REPLICATED = [w for w in WEIGHTS if w not in BIG and w not in SMALL_SHARDED]
REP_EARLY = ['final_norm', 'l1_ffn_norm', 'l1_ffn_conv_b', 'l1_lru_conv_b', 'l1_lru_wa', 'l1_lru_ba', 'l1_lru_wx',
             'l1_lru_bx', 'l1_lru_lambda']
REP_LATE = [w for w in REPLICATED if w not in REP_EARLY]


class Config(NamedTuple):
    d_model: int = 1024
    seq: int = 2048
    n_meta: int = 16
    chunk: int = 128
    ssd_heads: int = 16
    ssd_head_dim: int = 64
    ssd_groups: int = 4
    ssd_state: int = 128
    ret_heads: int = 4
    ret_dim: int = 256
    sb_heads: int = 16
    sb_head_dim: int = 64
    lru_width: int = 1024
    lru_blocks: int = 8
    ffn_dim: int = 2816
    n_dev: int = 8

    @property
    def t(self):
        return self.n_meta + self.seq

    @property
    def pad(self):
        return (-self.t) % self.chunk

    @property
    def tp(self):
        return self.t + self.pad

    @property
    def nc(self):
        return self.tp // self.chunk

    @property
    def ssd_inner(self):
        return self.ssd_heads * self.ssd_head_dim

    @property
    def ssd_gw(self):
        return self.ssd_inner // self.ssd_groups

    @property
    def ssd_bc(self):
        return self.ssd_groups * self.ssd_state

    @property
    def ret_w(self):
        return self.ret_heads * self.ret_dim

    @property
    def sb_w(self):
        return self.sb_heads * self.sb_head_dim

    @property
    def mix0_segs(self):
        return (self.ssd_inner, self.ssd_inner + 2 * self.ssd_bc, self.ssd_heads, self.ret_w, self.ret_w, self.ret_w, self.ret_w)


CFG = Config()


def _dg(a, b, ca, cb):
    return lax.dot_general(a, b, (((ca,), (cb,)), ((), ())), preferred_element_type=F32)


def _cot_a(b, ca, cb, g):
    return _dg(g, b, 1, 1 - cb) if ca == 1 else _dg(b, g, 1 - cb, 1)


def _cot_b(a, ca, cb, g):
    return _dg(a, g, 1 - ca, 0) if cb == 0 else _dg(g, a, 0, 1 - ca)


@functools.partial(jax.custom_vjp, nondiff_argnums=(2, 3))
def _bdot(a, b, ca, cb):
    return _dg(a.astype(BF16), b.astype(BF16), ca, cb)


def _bdot_fwd(a, b, ca, cb):
    return _bdot(a, b, ca, cb), (a, b)


def _bdot_bwd(ca, cb, res, g):
    a, b = res
    gb = g.astype(BF16)
    return _cot_a(b.astype(BF16), ca, cb, gb).astype(a.dtype), _cot_b(a.astype(BF16), ca, cb, gb).astype(b.dtype)


_bdot.defvjp(_bdot_fwd, _bdot_bwd)


def _dot(a, b):
    return _bdot(a, b, 1, 0)


def _dot_nt(a, b):
    return _bdot(a, b, 1, 1)


def _dot_tn(a, b):
    return _bdot(a, b, 0, 0)


def _split3(a):
    hi = a.astype(BF16)
    r1 = a - hi.astype(F32)
    mid = r1.astype(BF16)
    return hi, mid, (r1 - mid.astype(F32)).astype(BF16)


@functools.partial(jax.custom_vjp, nondiff_argnums=(2, 3, 4))
def _edot(a, b, ca, cb, const):
    if const == 'b':
        bb = b.astype(BF16)
        return sum(_dg(p, bb, ca, cb) for p in _split3(a))
    ab = a.astype(BF16)
    return sum(_dg(ab, p, ca, cb) for p in _split3(b))


def _edot_fwd(a, b, ca, cb, const):
    return _edot(a, b, ca, cb, const), (a, b)


def _edot_bwd(ca, cb, const, res, g):
    a, b = res
    if const == 'b':
        bb = b.astype(BF16)
        return sum(_cot_a(bb, ca, cb, p) for p in _split3(g)), jnp.zeros_like(b)
    ab = a.astype(BF16)
    return jnp.zeros_like(a), sum(_cot_b(ab, ca, cb, p) for p in _split3(g))


_edot.defvjp(_edot_fwd, _edot_bwd)


def _dot_exact01(a, sel):
    return _edot(a, sel, 1, 0, 'b')


def _silu(x):
    return x * jax.nn.sigmoid(x)


def _softplus(x):
    return jnp.maximum(x, 0.0) + jnp.log1p(jnp.exp(-jnp.abs(x)))


def _neg_expm1(x):
    series = -x * (1.0 + x * 0.5 * (1.0 + x / 3.0 * (1.0 + x * 0.25)))
    return jnp.where(x > -0.01, series, 1.0 - jnp.exp(x))


def _causal_conv(x, tail, w, b):
    taps, rows = w.shape[0], x.shape[0]
    xx = jnp.concatenate([tail, x], axis=0)
    y = b
    for k in range(taps):
        off = SUBLANES - (taps - 1 - k)
        y = y + w[k:k + 1, :] * xx[off:off + rows, :]
    return y


def _valid_chunk(cfg, chunk_idx):
    first = (chunk_idx % cfg.nc) == 0
    rid = lax.broadcasted_iota(jnp.int32, (cfg.chunk, 1), 0)
    return jnp.where(jnp.logical_and(first, rid < cfg.pad), 0.0, 1.0).astype(F32)


def _cparams(**kw):
    return pltpu.CompilerParams(vmem_limit_bytes=VMEM_LIMIT, **kw)


class Row(NamedTuple):
    arr: jax.Array
    width: int
    blk: int = 0
    tail: bool = False
    diff: bool = True


class Par(NamedTuple):
    arr: jax.Array
    diff: bool = True


def _row_specs(tr, rows, order):
    specs, args = [], []
    for r in rows:
        specs.append(pl.BlockSpec((tr, r.width), functools.partial(lambda i, b, o: (o(i), b), b=r.blk, o=order)))
        args.append(r.arr)
        if r.tail:
            per = tr // SUBLANES
            specs.append(pl.BlockSpec((SUBLANES, r.width),
                                      functools.partial(lambda i, b, o: (jnp.maximum(o(i) * per - 1, 0), b), b=r.blk, o=order)))
            args.append(r.arr)
    return specs, args


def _par_specs(pars):
    specs = [pl.BlockSpec(p.arr.shape, functools.partial(lambda i, nd: (0,) * nd, nd=p.arr.ndim)) for p in pars]
    return specs, [p.arr for p in pars]


def _valid_block(cfg, blk, tr, n_rows):
    row = blk * tr + lax.broadcasted_iota(jnp.int32, (tr, 1), 0)
    inpad = jnp.zeros((tr, 1), jnp.bool_)
    for s in range(n_rows // cfg.tp):
        inpad = jnp.logical_or(inpad, jnp.logical_and(row >= s * cfg.tp, row < s * cfg.tp + cfg.pad))
    return jnp.where(inpad, 0.0, 1.0).astype(F32)


def light_rows(n_rows):
    return n_rows // 8 if n_rows % (8 * 16) == 0 else None


def stage_fwd(name, cfg, fn, rows, pars, outs, tr=None, ride=None):
    tr = tr or cfg.chunk
    n_rows = rows[0].arr.shape[0]
    rspecs, rargs = _row_specs(tr, rows, lambda i: i)
    pspecs, pargs = _par_specs(pars)
    n_in = len(rargs) + len(pargs)

    def body(*refs):
        valid = _valid_block(cfg, pl.program_id(0), tr, n_rows)
        vals = [r[...].astype(F32) for r in refs[:n_in]]
        res = fn(valid, *vals)
        for o, v in zip(refs[n_in:], res):
            o[...] = v.astype(o.dtype)

    res, landed = call_with_ride(
        body, name=name, grid=(n_rows // tr,), in_specs=rspecs + pspecs, args=rargs + pargs,
        out_specs=[pl.BlockSpec((tr, w), lambda i: (i, 0)) for w, _ in outs],
        out_shape=[jax.ShapeDtypeStruct((n_rows, w), dt) for w, dt in outs], ride=ride)
    return (res, landed) if ride is not None else res


def stage_bwd(name, cfg, fn, rows, pars, douts, drow_dtypes, tr=None, ride=None):
    tr = tr or cfg.chunk
    n_rows = rows[0].arr.shape[0]
    n_blk = n_rows // tr
    order = lambda i: n_blk - 1 - i
    rspecs, rargs = _row_specs(tr, rows, order)
    pspecs, pargs = _par_specs(pars)
    pieces = [p for out in douts for p in out]
    dspecs, dargs = _row_specs(tr, pieces, order)
    n_r, n_p, n_d = len(rargs), len(pargs), len(dargs)
    diff_rows = [r for r in rows if r.diff]
    diff_pars = [p for p in pars if p.diff]
    tails = [r for r in diff_rows if r.tail]

    def body(*refs):
        in_refs = refs[:n_r + n_p]
        d_refs = refs[n_r + n_p:n_r + n_p + n_d]
        o_refs = refs[n_r + n_p + n_d:]
        drow_refs = o_refs[:len(diff_rows)]
        dpar_refs = o_refs[len(diff_rows):len(diff_rows) + len(diff_pars)]
        carry_refs = o_refs[len(diff_rows) + len(diff_pars):]
        step = pl.program_id(0)
        valid = _valid_block(cfg, order(step), tr, n_rows)
        vals = [r[...].astype(F32) for r in in_refs]
        slots, pos = [], 0
        for r in rows:
            if r.diff:
                slots.append(pos)
                if r.tail:
                    slots.append(pos + 1)
            pos += 2 if r.tail else 1
        for p in pars:
            if p.diff:
                slots.append(pos)
            pos += 1

        def g(*dv):
            full = list(vals)
            for s, v in zip(slots, dv):
                full[s] = v
            return tuple(fn(valid, *full))

        _, vjp = jax.vjp(g, *[vals[s] for s in slots])
        cts, k = [], 0
        for out in douts:
            parts = [d_refs[k + j][...].astype(F32) for j in range(len(out))]
            k += len(out)
            cts.append(parts[0] if len(parts) == 1 else jnp.concatenate(parts, axis=1))
        grads = list(vjp(tuple(cts)))

        @pl.when(step == 0)
        def _():
            for c in carry_refs:
                c[...] = jnp.zeros_like(c)
            for d in dpar_refs:
                d[...] = jnp.zeros_like(d)

        gi, ci = 0, 0
        for r, o in zip(diff_rows, drow_refs):
            dx = grads[gi]
            gi += 1
            if r.tail:
                dtail = grads[gi]
                gi += 1
                c = carry_refs[ci]
                ci += 1
                dx = dx + jnp.concatenate([jnp.zeros((tr - SUBLANES, r.width), F32), c[...]], axis=0)
                c[...] = dtail
            o[...] = dx.astype(o.dtype)
        for d in dpar_refs:
            d[...] += grads[gi]
            gi += 1

    out_specs = [pl.BlockSpec((tr, r.width), lambda i: (order(i), 0)) for r in diff_rows]
    out_shape = [jax.ShapeDtypeStruct((n_rows, r.width), dt) for r, dt in zip(diff_rows, drow_dtypes)]
    for p in diff_pars:
        out_specs.append(pl.BlockSpec(p.arr.shape, functools.partial(lambda i, nd: (0,) * nd, nd=p.arr.ndim)))
        out_shape.append(jax.ShapeDtypeStruct(p.arr.shape, F32))
    res, landed = call_with_ride(
        body, name=name, grid=(n_blk,), in_specs=rspecs + pspecs + dspecs, args=rargs + pargs + dargs, out_specs=out_specs,
        out_shape=out_shape, scratch_shapes=[pltpu.VMEM((SUBLANES, r.width), F32) for r in tails], ride=ride,
        dimension_semantics=("arbitrary",))
    grads = (list(res[:len(diff_rows)]), list(res[len(diff_rows):]))
    return grads + (landed,) if ride is not None else grads


def _tile(n, cap):
    if n % LANES:
        return n
    q = n // LANES
    best = 1
    for k in range(1, q + 1):
        if q % k == 0 and k * LANES <= cap:
            best = k
    return best * LANES


def _mm_vmem(tm, tn, tk, a_bytes, b_bytes, out_bytes, resid, nk):
    total = 2 * (tm * tk * a_bytes + tk * tn * b_bytes + tm * tn * out_bytes) + tm * tn * 4
    total += 2 * tm * tn * 4 if resid else 0
    total += (tm * tk * 2 if a_bytes == 4 else 0) + (tk * tn * 2 if b_bytes == 4 else 0)
    return total + (tm * tn * 4 if nk > 1 else 0)


def matmul(name, a, b, *, ta=False, tb=False, out_dtype=F32, resid=None, cfg=None, tm_cap=2176, tn_cap=1024, tk_cap=2816,
           ride=None):
    m, k = (a.shape[1], a.shape[0]) if ta else a.shape
    n = b.shape[0] if tb else b.shape[1]
    tm, tn, tk = _tile(m, tm_cap), _tile(n, tn_cap), _tile(k, tk_cap)
    budget = VMEM_LIMIT - 8 * 1024 * 1024
    sizes = (a.dtype.itemsize, b.dtype.itemsize, jnp.dtype(out_dtype).itemsize, resid is not None)
    while _mm_vmem(tm, tn, tk, *sizes, k // tk) > budget:
        if tm > 256:
            tm = _tile(m, tm - 1)
        elif tk > 512:
            tk = _tile(k, tk - 1)
        else:
            tn = _tile(n, tn - 1)
    nk = k // tk
    a_spec = pl.BlockSpec((tk, tm), lambda i, j, l: (l, i)) if ta else pl.BlockSpec((tm, tk), lambda i, j, l: (i, l))
    b_spec = pl.BlockSpec((tn, tk), lambda i, j, l: (j, l)) if tb else pl.BlockSpec((tk, tn), lambda i, j, l: (l, j))
    dims = (((0 if ta else 1,), (1 if tb else 0,)), ((), ()))
    in_specs, args = [a_spec, b_spec], [a, b]
    if resid is not None:
        in_specs.append(pl.BlockSpec((tm, tn), lambda i, j, l: (i, j)))
        args.append(resid)

    def body(*refs):
        a_ref, b_ref = refs[0], refs[1]
        o_ref = refs[3] if resid is not None else refs[2]
        part = lax.dot_general(a_ref[...].astype(BF16), b_ref[...].astype(BF16), dims, preferred_element_type=F32)

        def finish(res):
            if resid is not None:
                row = pl.program_id(0) * tm + lax.broadcasted_iota(jnp.int32, (tm, 1), 0)
                inpad = jnp.zeros((tm, 1), jnp.bool_)
                for s in range(m // cfg.tp):
                    inpad = jnp.logical_or(inpad, jnp.logical_and(row >= s * cfg.tp, row < s * cfg.tp + cfg.pad))
                res = refs[2][...] + jnp.where(inpad, 0.0, res)
            o_ref[...] = res.astype(o_ref.dtype)

        if nk == 1:
            finish(part)
            return
        acc = refs[-1]
        l = pl.program_id(2)

        @pl.when(l == 0)
        def _():
            acc[...] = part

        @pl.when(jnp.logical_and(l > 0, l < nk - 1))
        def _():
            acc[...] += part

        @pl.when(l == nk - 1)
        def _():
            finish(acc[...] + part)

    outs, landed = call_with_ride(
        body, name=name, grid=(m // tm, n // tn, nk), in_specs=in_specs, args=args,
        out_specs=[pl.BlockSpec((tm, tn), lambda i, j, l: (i, j))], out_shape=[jax.ShapeDtypeStruct((m, n), out_dtype)],
        scratch_shapes=[pltpu.VMEM((tm, tn), F32)] if nk > 1 else [], ride=ride,
        dimension_semantics=("parallel", "parallel", "arbitrary"))
    return (outs[0], landed) if ride is not None else outs[0]


def _norm_fn(valid, h, g):
    y = h * lax.rsqrt(jnp.mean(h * h, axis=-1, keepdims=True) + EPS)
    return (valid * (y * g),)


def _make_ssd_pre(cfg):
    nh = cfg.ssd_heads

    def fn(valid, xs, xs_t, bm, bm_t, cm, cm_t, dtr, w_xs, w_b, w_c, b_xs, b_b, b_c, dt_bias, a_log, sel_hd, sel_lane):
        xs = valid * _silu(_causal_conv(xs, xs_t, w_xs, b_xs))
        bm = valid * _silu(_causal_conv(bm, bm_t, w_b, b_b))
        cm = valid * _silu(_causal_conv(cm, cm_t, w_c, b_c))
        lane = lax.broadcasted_iota(jnp.int32, (1, LANES), 1)
        head = jnp.where(lane < nh, 1.0, 0.0).astype(F32)
        dt = valid * head * _softplus(dtr + dt_bias)
        adt = dt * (-jnp.exp(a_log))
        x = xs * _dot_exact01(dt, sel_hd)
        adt_b = _dot_exact01(adt, sel_lane)
        return xs, bm, cm, x, adt_b

    return fn


def _make_ssd_post(cfg):
    gw = cfg.ssd_gw

    def fn(valid, y_raw, xs, z, d_skip, norm_g, sel_hd):
        d_rep = _dot_exact01(jnp.broadcast_to(d_skip, (SUBLANES, LANES)), sel_hd)[0:1]
        y = (y_raw + xs * d_rep) * _silu(z)
        parts = []
        for g in range(cfg.ssd_groups):
            yg = y[:, g * gw:(g + 1) * gw]
            parts.append(yg * lax.rsqrt(jnp.mean(yg * yg, axis=-1, keepdims=True) + EPS))
        return (jnp.concatenate(parts, axis=1) * norm_g,)

    return fn


def _ffn_act_fn(valid, ug, ug_t, uu, uu_t, w_g, w_u, b_g, b_u):
    return (valid * _silu(_causal_conv(ug, ug_t, w_g, b_g)) * _causal_conv(uu, uu_t, w_u, b_u),)


def _make_lru_pre(cfg):
    nb = cfg.lru_blocks
    bw = cfg.lru_width // nb

    def fn(valid, xr, xr_t, w_conv, b_conv, wa, ba, wx, bx, lam):
        xc = _causal_conv(xr, xr_t, w_conv, b_conv)
        a_parts, b_parts = [], []
        for n in range(nb):
            sl = slice(n * bw, (n + 1) * bw)
            xn = xc[:, sl]
            r = jax.nn.sigmoid(_dot(xn, wa[n]) + ba[:, sl])
            i = jax.nn.sigmoid(_dot(xn, wx[n]) + bx[:, sl])
            log_a = -LRU_C * r * _softplus(-lam[:, sl])
            a_parts.append(jnp.exp(log_a))
            b_parts.append(valid * jnp.sqrt(jnp.maximum(_neg_expm1(2.0 * log_a), 0.0)) * (i * xn))
        return jnp.concatenate(a_parts, axis=1), jnp.concatenate(b_parts, axis=1)

    return fn


def _lru_post_fn(valid, hs, gate):
    return (hs * jax.nn.gelu(gate),)


def _tri_consts(n):
    r = lax.broadcasted_iota(jnp.int32, (n, n), 0)
    c = lax.broadcasted_iota(jnp.int32, (n, n), 1)
    return r, c


def _ssd_chunk(x, bm, cm, a, s, tril, lower):
    rows = x.shape[0]
    heads = a.shape[1] // LANES
    p = x.shape[1] // heads
    cb = _dot_nt(cm, bm)
    ys, ss = [], []
    for e in range(heads):
        ae = a[:, e * LANES:(e + 1) * LANES]
        cs = _edot(tril, ae, 1, 0, 'a')
        lmat = jnp.exp(jnp.where(lower, cs - cs.T, NEG))
        xe = x[:, e * p:(e + 1) * p]
        se = s[e * p:(e + 1) * p, :]
        tot = cs[rows - 1:rows, :]
        y_diag = _dot(cb * lmat, xe)
        st = _dot_tn(xe * jnp.exp(tot - cs)[:, :p], bm)
        y_off = _dot_nt(cm, se) * jnp.exp(cs)[:, :p]
        ys.append(y_diag + y_off)
        ss.append(jnp.exp(tot[:, :1]) * se + st)
    return jnp.concatenate(ys, axis=1), jnp.concatenate(ss, axis=0)


def _ssd_specs(cfg, x, bm, cm, adt_b):
    gw, ns = cfg.ssd_gw, cfg.ssd_state
    hpg = cfg.ssd_heads // cfg.ssd_groups
    specs = [pl.BlockSpec((cfg.tp, gw), lambda b, g: (b, g)),
             pl.BlockSpec((cfg.tp, ns), lambda b, g: (b, g)),
             pl.BlockSpec((cfg.tp, ns), lambda b, g: (b, g)),
             pl.BlockSpec((cfg.tp, hpg * LANES), lambda b, g: (b, g))]
    return specs, [x, bm, cm, adt_b]


def ssd_scan_fwd(cfg, x, bm, cm, adt_b, ride=None):
    nb = x.shape[0] // cfg.tp
    L, nc = cfg.chunk, cfg.nc
    specs, args = _ssd_specs(cfg, x, bm, cm, adt_b)

    def body(x_ref, b_ref, c_ref, a_ref, y_ref, states_ref, s_ref):
        r, c = _tri_consts(L)
        lower = r >= c
        tril = lower.astype(F32)
        s_ref[...] = jnp.zeros_like(s_ref)

        def step(ci, carry):
            rows = pl.ds(pl.multiple_of(ci * L, L), L)
            states_ref[ci] = s_ref[...]
            y, s_new = _ssd_chunk(x_ref[rows, :], b_ref[rows, :], c_ref[rows, :], a_ref[rows, :], s_ref[...], tril, lower)
            y_ref[rows, :] = y
            s_ref[...] = s_new
            return carry

        lax.fori_loop(0, nc, step, 0, unroll=CHUNK_UNROLL)

    gw, ns = cfg.ssd_gw, cfg.ssd_state
    outs, landed = call_with_ride(
        body, name="ssd_scan_fwd", grid=(nb, cfg.ssd_groups), in_specs=specs, args=args,
        out_specs=[specs[0], pl.BlockSpec((None, None, nc, gw, ns), lambda b, g: (b, g, 0, 0, 0))],
        out_shape=[jax.ShapeDtypeStruct(x.shape, F32), jax.ShapeDtypeStruct((nb, cfg.ssd_groups, nc, gw, ns), F32)],
        scratch_shapes=[pltpu.VMEM((gw, ns), F32)], ride=ride)
    return outs[0], outs[1], landed


def ssd_scan_bwd(cfg, x, bm, cm, adt_b, states, dy, ride=None):
    nb = x.shape[0] // cfg.tp
    L, nc = cfg.chunk, cfg.nc
    specs, args = _ssd_specs(cfg, x, bm, cm, adt_b)
    state_spec = pl.BlockSpec((None, None, nc, cfg.ssd_gw, cfg.ssd_state), lambda b, g: (b, g, 0, 0, 0))

    def body(x_ref, b_ref, c_ref, a_ref, s_all, dy_ref, dx_ref, db_ref, dc_ref, da_ref, ds_ref):
        r, c = _tri_consts(L)
        lower = r >= c
        tril = lower.astype(F32)
        chunk = functools.partial(_ssd_chunk, tril=tril, lower=lower)
        ds_ref[...] = jnp.zeros_like(ds_ref)

        def bwd(k, carry):
            ci = nc - 1 - k
            rows = pl.ds(pl.multiple_of(ci * L, L), L)
            _, vjp = jax.vjp(chunk, x_ref[rows, :], b_ref[rows, :], c_ref[rows, :], a_ref[rows, :], s_all[ci])
            dx, db, dc, da, ds = vjp((dy_ref[rows, :], ds_ref[...]))
            dx_ref[rows, :] = dx
            db_ref[rows, :] = db
            dc_ref[rows, :] = dc
            da_ref[rows, :] = da
            ds_ref[...] = ds
            return carry

        lax.fori_loop(0, nc, bwd, 0, unroll=CHUNK_UNROLL)

    return call_with_ride(
        body, name="ssd_scan_bwd", grid=(nb, cfg.ssd_groups), in_specs=specs + [state_spec, specs[0]], args=args + [states, dy],
        out_specs=specs, out_shape=[jax.ShapeDtypeStruct(t.shape, F32) for t in (x, bm, cm, adt_b)],
        scratch_shapes=[pltpu.VMEM((cfg.ssd_gw, cfg.ssd_state), F32)], ride=ride)


def _ret_chunk(q, k, v, g, norm_g, r_prev, cos, sin, valid, decay, zeta, xi, cdec, scale):
    half = q.shape[1] // 2

    def rot(t):
        t1, t2 = t[:, :half], t[:, half:]
        return jnp.concatenate([t1 * cos - t2 * sin, t1 * sin + t2 * cos], axis=1)

    qr = valid * rot(q)
    kr = valid * rot(k) * scale
    v = valid * v
    inner = _dot(_dot_nt(qr, kr) * decay, v)
    kv = _dot_tn(kr * zeta, v)
    cross = _dot(qr, r_prev) * xi
    o = inner + cross
    o = o - jnp.mean(o, axis=-1, keepdims=True)
    o = o * lax.rsqrt(jnp.mean(o * o, axis=-1, keepdims=True) + EPS)
    return _silu(g) * (o * norm_g), cdec * r_prev + kv


def _ret_consts(cfg):
    f32 = jnp.float32
    log_gamma = jnp.log1p(-jnp.exp2(-5.0 - jnp.arange(cfg.ret_heads, dtype=f32)))
    idx = jnp.arange(cfg.chunk, dtype=f32)
    diff = idx[:, None] - idx[None, :]
    decay = jnp.where(diff >= 0, jnp.exp(log_gamma[:, None, None] * jnp.maximum(diff, 0.0)), 0.0)
    zeta = jnp.exp(log_gamma[:, None] * (cfg.chunk - 1 - idx)[None, :])[..., None]
    xi = jnp.exp(log_gamma[:, None] * (idx + 1.0)[None, :])[..., None]
    cdec = jnp.exp(cfg.chunk * log_gamma)[:, None, None]
    half = cfg.ret_dim // 2
    inv_freq = 1.0 / (10000.0 ** (jnp.arange(half, dtype=f32) / (half - 1)))
    pos = jnp.arange(cfg.tp, dtype=f32) - cfg.pad
    ang = pos[:, None] * inv_freq[None, :]
    return dict(decay=decay, zeta=zeta, xi=xi, cdec=cdec, cos=jnp.cos(ang), sin=jnp.sin(ang))


def _ret_specs(cfg, u, col0, norm_g, consts):
    dk, nh, L = cfg.ret_dim, cfg.ret_heads, cfg.chunk
    base = col0 // dk
    seg = lambda s: pl.BlockSpec((cfg.tp, dk), functools.partial(lambda h, b, s: (b, base + s * nh + h), s=s))
    half = dk // 2
    specs = [seg(0), seg(1), seg(2), seg(3),
             pl.BlockSpec((1, dk), lambda h, b: (0, h)),
             pl.BlockSpec((cfg.tp, half), lambda h, b: (0, 0)),
             pl.BlockSpec((cfg.tp, half), lambda h, b: (0, 0)),
             pl.BlockSpec((None, L, L), lambda h, b: (h, 0, 0)),
             pl.BlockSpec((None, L, 1), lambda h, b: (h, 0, 0)),
             pl.BlockSpec((None, L, 1), lambda h, b: (h, 0, 0)),
             pl.BlockSpec((None, 1, 1), lambda h, b: (h, 0, 0))]
    args = [u, u, u, u, norm_g, consts["cos"], consts["sin"], consts["decay"], consts["zeta"], consts["xi"], consts["cdec"]]
    return specs, args


def _ret_chunk_at(cfg, refs, ci):
    L = cfg.chunk
    cos_ref, sin_ref, decay_ref, zeta_ref, xi_ref, cdec_ref = refs
    rows = pl.ds(pl.multiple_of(ci * L, L), L)
    fn = functools.partial(_ret_chunk, cos=cos_ref[rows, :], sin=sin_ref[rows, :], valid=_valid_chunk(cfg, ci),
                           decay=decay_ref[...], zeta=zeta_ref[...], xi=xi_ref[...], cdec=cdec_ref[...],
                           scale=cfg.ret_dim ** -0.5)
    return fn, rows


def ret_fwd(cfg, u, col0, norm_g, consts, ride=None):
    nb = u.shape[0] // cfg.tp
    dk, nc = cfg.ret_dim, cfg.nc
    specs, args = _ret_specs(cfg, u, col0, norm_g, consts)

    def body(q_ref, k_ref, v_ref, g_ref, ng_ref, *rest):
        y_ref, r_ref = rest[-2], rest[-1]
        r_ref[...] = jnp.zeros_like(r_ref)

        def step(ci, carry):
            fn, rows = _ret_chunk_at(cfg, rest[:6], ci)
            y, r_new = fn(q_ref[rows, :], k_ref[rows, :], v_ref[rows, :], g_ref[rows, :], ng_ref[...], r_ref[...])
            y_ref[rows, :] = y.astype(y_ref.dtype)
            r_ref[...] = r_new
            return carry

        lax.fori_loop(0, nc, step, 0, unroll=CHUNK_UNROLL)

    outs, landed = call_with_ride(
        body, name="ret_fwd", grid=(cfg.ret_heads, nb), in_specs=specs, args=args,
        out_specs=[pl.BlockSpec((cfg.tp, dk), lambda h, b: (b, h))],
        out_shape=[jax.ShapeDtypeStruct((u.shape[0], cfg.ret_w), BF16)], scratch_shapes=[pltpu.VMEM((dk, dk), F32)], ride=ride)
    return outs[0], landed


def ret_bwd(cfg, u, col0, norm_g, consts, dy, dy_col0):
    nb = u.shape[0] // cfg.tp
    dk, nc, nh = cfg.ret_dim, cfg.nc, cfg.ret_heads
    specs, args = _ret_specs(cfg, u, col0, norm_g, consts)
    dy_base = dy_col0 // dk
    specs.append(pl.BlockSpec((cfg.tp, dk), lambda h, b: (b, dy_base + h)))
    seg_out = lambda s: pl.BlockSpec((cfg.tp, dk), functools.partial(lambda h, b, s: (b, s * nh + h), s=s))

    def body(q_ref, k_ref, v_ref, g_ref, ng_ref, *rest):
        consts_refs, dy_ref = rest[:6], rest[6]
        dq_ref, dk_ref, dv_ref, dg_ref, dng_ref, r_all, dr_ref = rest[7:]
        dr_ref[...] = jnp.zeros_like(dr_ref)

        def fwd(ci, carry):
            fn, rows = _ret_chunk_at(cfg, consts_refs, ci)
            r_all[ci] = dr_ref[...]
            _, r_new = fn(q_ref[rows, :], k_ref[rows, :], v_ref[rows, :], g_ref[rows, :], ng_ref[...], dr_ref[...])
            dr_ref[...] = r_new
            return carry

        lax.fori_loop(0, nc, fwd, 0, unroll=CHUNK_UNROLL)
        dr_ref[...] = jnp.zeros_like(dr_ref)

        @pl.when(pl.program_id(1) == 0)
        def _():
            dng_ref[...] = jnp.zeros_like(dng_ref)

        def bwd(kk, carry):
            ci = nc - 1 - kk
            fn, rows = _ret_chunk_at(cfg, consts_refs, ci)
            _, vjp = jax.vjp(fn, q_ref[rows, :], k_ref[rows, :], v_ref[rows, :], g_ref[rows, :], ng_ref[...], r_all[ci])
            dq, dkk, dv, dg, dng, dr = vjp((dy_ref[rows, :].astype(F32), dr_ref[...]))
            dq_ref[rows, :] = dq.astype(dq_ref.dtype)
            dk_ref[rows, :] = dkk.astype(dk_ref.dtype)
            dv_ref[rows, :] = dv.astype(dv_ref.dtype)
            dg_ref[rows, :] = dg.astype(dg_ref.dtype)
            dng_ref[...] += dng
            dr_ref[...] = dr
            return carry

        lax.fori_loop(0, nc, bwd, 0, unroll=CHUNK_UNROLL)

    seg_shape = jax.ShapeDtypeStruct((u.shape[0], cfg.ret_w), BF16)
    res = pl.pallas_call(
        body, grid=(nh, nb), in_specs=specs,
        out_specs=[pl.BlockSpec((cfg.tp, dk), lambda h, b: (b, h))] * 4 + [pl.BlockSpec((1, dk), lambda h, b: (0, h))],
        out_shape=[seg_shape] * 4 + [jax.ShapeDtypeStruct((1, cfg.ret_w), F32)],
        scratch_shapes=[pltpu.VMEM((nc, dk, dk), F32), pltpu.VMEM((dk, dk), F32)],
        name="ret_bwd", compiler_params=_cparams(dimension_semantics=("arbitrary", "arbitrary")))(*args, dy)
    return res[:4], res[4]


def _dot2(a, tri):
    hi = a.astype(BF16)
    lo = (a - hi.astype(F32)).astype(BF16)
    return _dg(hi, tri, 1, 0) + _dg(lo, tri, 1, 0)


def _sb_group(cfg):
    return max(g for g in (4, 2, 1) if g <= cfg.nc)


def _sb_specs(cfg, u, col0):
    base = col0 // LANES
    per = cfg.sb_w // LANES
    seg = lambda s: pl.BlockSpec((cfg.tp, LANES), functools.partial(lambda b, p, s: (b, base + s * per + p), s=s))
    return [seg(0), seg(1), seg(2)], [u, u, u]


def _sb_scores(cfg, qh, ks, mask):
    z = _dg(qh, ks, 1, 1)
    sp = jnp.maximum(z, 0.0) + jnp.log(1.0 + jnp.exp(-jnp.abs(z)))
    return z, sp, jnp.where(mask, -sp, 0.0)


def _sb_running(x, tri, run, order):
    L = x.shape[0]
    parts = [None] * (x.shape[1] // L)
    for b in order:
        blk = x[:, b * L:(b + 1) * L]
        parts[b] = _dot2(blk, tri) + run
        run = run + jnp.sum(blk, axis=1, keepdims=True)
    return jnp.concatenate(parts, axis=1) if len(parts) > 1 else parts[0], run


def sb_fwd(cfg, u, col0, ride=None):
    nb = u.shape[0] // cfg.tp
    L, nc = cfg.chunk, cfg.nc
    hd = cfg.sb_head_dim
    scale = hd ** -0.5
    G = _sb_group(cfg)
    specs, args = _sb_specs(cfg, u, col0)

    def body(q_ref, k_ref, v_ref, y_ref, tot_ref):
        tr, tc = _tri_consts(L)
        after_tri = (tr > tc).astype(BF16)
        rid = lax.broadcasted_iota(jnp.int32, (L, G * L), 0)
        cid = lax.broadcasted_iota(jnp.int32, (L, G * L), 1)
        lane = lax.broadcasted_iota(jnp.int32, (1, LANES), 1)
        mine = [jnp.logical_and(lane >= hh * hd, lane < (hh + 1) * hd) for hh in range(2)]

        def qloop(i, carry):
            qrows = pl.ds(pl.multiple_of(i * L, L), L)
            q = q_ref[qrows, :]
            qhs = [jnp.where(mine[hh], q * scale, 0.0).astype(BF16) for hh in range(2)]

            def gloop(gg, c):
                accs, runs = c
                hi = i - gg * G
                start = jnp.maximum(hi - (G - 1), 0)
                krows = pl.ds(pl.multiple_of(start * L, L), G * L)
                ks = k_ref[krows, :].astype(BF16)
                vs = v_ref[krows, :].astype(BF16)
                kpos = start * L + cid
                mask = jnp.logical_and(jnp.logical_and(kpos < i * L + rid, kpos >= cfg.pad), kpos < (hi + 1) * L)
                new_accs, new_runs = [], []
                for hh in range(2):
                    z, sp, l1m = _sb_scores(cfg, qhs[hh], ks, mask)
                    after, run = _sb_running(l1m, after_tri, runs[hh], range(G - 1, -1, -1))
                    w = jnp.where(mask, jnp.exp(z - sp + after), 0.0)
                    new_accs.append(accs[hh] + _dg(w.astype(BF16), vs, 1, 0))
                    new_runs.append(run)
                return tuple(new_accs), tuple(new_runs)

            zero_acc, zero_run = jnp.zeros((L, LANES), F32), jnp.zeros((L, 1), F32)
            accs, runs = lax.fori_loop(0, (i + G) // G, gloop, ((zero_acc, zero_acc), (zero_run, zero_run)))
            for hh in range(2):
                tot_ref[0, hh, qrows, :] = runs[hh]
            y_ref[qrows, :] = jnp.where(lane < hd, accs[0], accs[1]).astype(y_ref.dtype)
            return carry

        lax.fori_loop(0, nc, qloop, 0)

    (y, tot), landed = call_with_ride(
        body, name="sb_fwd", grid=(nb, cfg.sb_w // LANES), in_specs=specs, args=args,
        out_specs=[pl.BlockSpec((cfg.tp, LANES), lambda b, p: (b, p)),
                   pl.BlockSpec((1, 2, cfg.tp, 1), lambda b, p: (b, p, 0, 0))],
        out_shape=[jax.ShapeDtypeStruct((u.shape[0], cfg.sb_w), BF16),
                   jax.ShapeDtypeStruct((nb, cfg.sb_heads, cfg.tp, 1), F32)], ride=ride)
    return y, tot, landed


def sb_bwd(cfg, u, col0, tot, dy, dy_col0, ride=None):
    nb = u.shape[0] // cfg.tp
    L, nc = cfg.chunk, cfg.nc
    hd = cfg.sb_head_dim
    scale = hd ** -0.5
    specs, args = _sb_specs(cfg, u, col0)
    dy_base = dy_col0 // LANES
    specs += [pl.BlockSpec((1, 2, cfg.tp, 1), lambda b, p: (b, p, 0, 0)),
              pl.BlockSpec((cfg.tp, LANES), lambda b, p: (b, dy_base + p))]

    G = _sb_group(cfg)

    def body(q_ref, k_ref, v_ref, tot_ref, dy_ref, dq_ref, dk_ref, dv_ref, dk_acc, dv_acc):
        tr, tc = _tri_consts(L)
        upto_tri = (tr <= tc).astype(BF16)
        before_tri = (tr < tc).astype(BF16)
        rid = lax.broadcasted_iota(jnp.int32, (L, G * L), 0)
        cid = lax.broadcasted_iota(jnp.int32, (L, G * L), 1)
        lane = lax.broadcasted_iota(jnp.int32, (1, LANES), 1)
        mine = [jnp.logical_and(lane >= hh * hd, lane < (hh + 1) * hd) for hh in range(2)]
        dk_acc[...] = jnp.zeros_like(dk_acc)
        dv_acc[...] = jnp.zeros_like(dv_acc)

        def qloop(i, carry):
            qrows = pl.ds(pl.multiple_of(i * L, L), L)
            q = q_ref[qrows, :]
            dy = dy_ref[qrows, :].astype(F32)
            qhs = [jnp.where(mine[hh], q * scale, 0.0).astype(BF16) for hh in range(2)]
            dyhs = [jnp.where(mine[hh], dy, 0.0).astype(BF16) for hh in range(2)]
            tots = [tot_ref[0, hh, qrows, :] for hh in range(2)]

            def gloop(g, c):
                dq, run_ls, run_as = c
                lo = g * G
                start = jnp.minimum(lo, nc - G)
                krows = pl.ds(pl.multiple_of(start * L, L), G * L)
                k = k_ref[krows, :]
                ks = k.astype(BF16)
                vs = v_ref[krows, :].astype(BF16)
                kpos = start * L + cid
                mask = jnp.logical_and(jnp.logical_and(kpos < i * L + rid, kpos >= cfg.pad), kpos >= lo * L)
                new_ls, new_as = [], []
                dv_add = jnp.zeros((G * L, LANES), F32)
                dk_add = jnp.zeros((G * L, LANES), F32)
                for hh in range(2):
                    z, sp, l1m = _sb_scores(cfg, qhs[hh], ks, mask)
                    upto, run_l = _sb_running(l1m, upto_tri, run_ls[hh], range(G))
                    w = jnp.where(mask, jnp.exp(z - sp + (tots[hh] - upto)), 0.0)
                    da = w * _dg(dyhs[hh], vs, 1, 1)
                    d_l1m, run_a = _sb_running(da, before_tri, run_as[hh], range(G))
                    sg = jnp.exp(z - sp)
                    dz = jnp.where(mask, da * (1.0 - sg) - d_l1m * sg, 0.0).astype(BF16)
                    dv_add = dv_add + _dg(w.astype(BF16), dyhs[hh], 0, 0)
                    dk_add = dk_add + _dg(dz, qhs[hh], 0, 0)
                    dq = dq + _dg(dz, jnp.where(mine[hh], k, 0.0).astype(BF16), 1, 0)
                    new_ls.append(run_l)
                    new_as.append(run_a)
                dv_acc[krows, :] += dv_add
                dk_acc[krows, :] += dk_add
                return dq, tuple(new_ls), tuple(new_as)

            zero = jnp.zeros((L, 1), F32)
            dq, _, _ = lax.fori_loop(0, (i + G) // G, gloop, (jnp.zeros((L, LANES), F32), (zero, zero), (zero, zero)))
            dq_ref[qrows, :] = (dq * scale).astype(dq_ref.dtype)
            return carry

        lax.fori_loop(0, nc, qloop, 0)
        dk_ref[...] = dk_acc[...].astype(dk_ref.dtype)
        dv_ref[...] = dv_acc[...].astype(dv_ref.dtype)

    out_spec = pl.BlockSpec((cfg.tp, LANES), lambda b, p: (b, p))
    seg_shape = jax.ShapeDtypeStruct((u.shape[0], cfg.sb_w), BF16)
    return call_with_ride(
        body, name="sb_bwd", grid=(nb, cfg.sb_w // LANES), in_specs=specs, args=args + [tot, dy], out_specs=[out_spec] * 3,
        out_shape=[seg_shape] * 3, scratch_shapes=[pltpu.VMEM((cfg.tp, LANES), F32), pltpu.VMEM((cfg.tp, LANES), F32)], ride=ride)


def _lru_cols(cfg):
    return _tile(cfg.lru_width, 256)


def lru_scan_fwd(cfg, a, b):
    nb = a.shape[0] // cfg.tp
    cw = _lru_cols(cfg)
    spec = pl.BlockSpec((cfg.tp, cw), lambda s, c: (s, c))

    def body(a_ref, b_ref, h_ref):
        def step(i, h):
            rows = pl.ds(pl.multiple_of(i * SUBLANES, SUBLANES), SUBLANES)
            at, bt = a_ref[rows, :], b_ref[rows, :]
            outs = []
            for r in range(SUBLANES):
                h = at[r:r + 1, :] * h + bt[r:r + 1, :]
                outs.append(h)
            h_ref[rows, :] = jnp.concatenate(outs, axis=0)
            return h

        lax.fori_loop(0, cfg.tp // SUBLANES, step, jnp.zeros((1, cw), F32))

    return pl.pallas_call(
        body, grid=(nb, cfg.lru_width // cw), in_specs=[spec, spec], out_specs=spec,
        out_shape=jax.ShapeDtypeStruct(a.shape, F32), name="lru_scan_fwd", compiler_params=_cparams())(a, b)


def lru_scan_bwd(cfg, a, h, dh):
    nb = a.shape[0] // cfg.tp
    cw = _lru_cols(cfg)
    nt = cfg.tp // SUBLANES
    spec = pl.BlockSpec((cfg.tp, cw), lambda s, c: (s, c))

    def body(a_ref, h_ref, dh_ref, da_ref, db_ref):
        def step(k, c):
            i = nt - 1 - k
            rows = pl.ds(pl.multiple_of(i * SUBLANES, SUBLANES), SUBLANES)
            prev = pl.ds(pl.multiple_of(jnp.maximum(i - 1, 0) * SUBLANES, SUBLANES), SUBLANES)
            at, ht, dht = a_ref[rows, :], h_ref[rows, :], dh_ref[rows, :]
            h_before = jnp.where(i > 0, h_ref[prev, :][SUBLANES - 1:SUBLANES, :], 0.0)
            das, dbs = [None] * SUBLANES, [None] * SUBLANES
            for r in range(SUBLANES - 1, -1, -1):
                g = dht[r:r + 1, :] + c
                dbs[r] = g
                das[r] = g * (ht[r - 1:r, :] if r > 0 else h_before)
                c = at[r:r + 1, :] * g
            da_ref[rows, :] = jnp.concatenate(das, axis=0)
            db_ref[rows, :] = jnp.concatenate(dbs, axis=0)
            return c

        lax.fori_loop(0, nt, step, jnp.zeros((1, cw), F32))

    return pl.pallas_call(
        body, grid=(nb, cfg.lru_width // cw), in_specs=[spec] * 3, out_specs=[spec] * 2,
        out_shape=[jax.ShapeDtypeStruct(a.shape, F32)] * 2, name="lru_scan_bwd", compiler_params=_cparams())(a, h, dh)


def final_loss(cfg, h, target, norm_g):
    assert cfg.seq % cfg.chunk == 0 and cfg.n_meta + cfg.pad == cfg.chunk
    L, nc, d = cfg.chunk, cfg.nc, cfg.d_model
    per_seq = cfg.seq // L

    def tgt_map(i):
        return ((i // nc) * per_seq + jnp.maximum(i % nc - 1, 0), 0)

    def body(h_ref, t_ref, g_ref, loss_ref, dh_ref, dg_ref):
        i = pl.program_id(0)
        real = jnp.where(i % nc == 0, 0.0, 1.0)
        tgt = t_ref[...]

        def loss_fn(hv, g):
            y = hv * lax.rsqrt(jnp.mean(hv * hv, axis=-1, keepdims=True) + EPS) * g
            return 0.5 * real * jnp.sum(jnp.mean(jnp.square(y - tgt), axis=-1))

        val, (dh, dg) = jax.value_and_grad(loss_fn, argnums=(0, 1))(h_ref[...], g_ref[...])

        @pl.when(i == 0)
        def _():
            loss_ref[...] = jnp.zeros_like(loss_ref)
            dg_ref[...] = jnp.zeros_like(dg_ref)

        loss_ref[...] += jnp.broadcast_to(val, loss_ref.shape)
        dg_ref[...] += dg
        dh_ref[...] = dh

    return pl.pallas_call(
        body, grid=(h.shape[0] // L,),
        in_specs=[pl.BlockSpec((L, d), lambda i: (i, 0)), pl.BlockSpec((L, d), tgt_map), pl.BlockSpec((1, d), lambda i: (0, 0))],
        out_specs=[pl.BlockSpec((SUBLANES, LANES), lambda i: (0, 0)), pl.BlockSpec((L, d), lambda i: (i, 0)),
                   pl.BlockSpec((1, d), lambda i: (0, 0))],
        out_shape=[jax.ShapeDtypeStruct((SUBLANES, LANES), F32), jax.ShapeDtypeStruct(h.shape, F32),
                   jax.ShapeDtypeStruct((1, d), F32)],
        name="final_loss", compiler_params=_cparams(dimension_semantics=("arbitrary",)))(h, target, norm_g)


N_PEER = 7


class Ride(NamedTuple):
    arrays: list
    same: list


def _ride_copies(ride_in, ride_out, same, sems, sending):
    send_sems, recv_sems, local_sems = sems
    n = len(ride_in)
    x, y, c = lax.axis_index("x"), lax.axis_index("y"), lax.axis_index("c")
    me = 4 * x + 2 * y + c

    def slab(w, dest):
        return ride_in[w] if same[w] else ride_in[w].at[dest]

    local = [pltpu.make_async_copy(slab(w, me), ride_out[w].at[me], local_sems.at[w]) for w in range(n)]
    remote = []
    for r in range(1, N_PEER + 1):
        peer = (1 - x if r & 4 else x, 1 - y if r & 2 else y, 1 - c if r & 1 else c)
        pidx = 4 * peer[0] + 2 * peer[1] + peer[2]
        for w in range(n):
            remote.append(pltpu.make_async_remote_copy(
                src_ref=slab(w, pidx), dst_ref=ride_out[w].at[me if sending else pidx], send_sem=send_sems.at[w, r - 1],
                recv_sem=recv_sems.at[w, r - 1], device_id=peer, device_id_type=pl.DeviceIdType.MESH))
    return local, remote


def _ride_start(ride_in, ride_out, same, sems):
    local, remote = _ride_copies(ride_in, ride_out, same, sems, True)
    for cp in local + remote:
        cp.start()


def _ride_wait(ride_in, ride_out, same, sems):
    local, remote = _ride_copies(ride_in, ride_out, same, sems, False)
    for cp in remote:
        cp.wait_recv()
    for cp in remote:
        cp.wait_send()
    for cp in local:
        cp.wait()


def call_with_ride(body, *, name, grid, in_specs, args, out_specs, out_shape, scratch_shapes=(), ride=None, **cparams):
    out_specs, out_shape, scratch_shapes = list(out_specs), list(out_shape), list(scratch_shapes)
    if ride is None:
        res = pl.pallas_call(body, grid=grid, in_specs=list(in_specs), out_specs=out_specs, out_shape=out_shape,
                             scratch_shapes=scratch_shapes, name=name, compiler_params=_cparams(**cparams))(*args)
        return list(res), []
    arrays, same = ride
    n, n_in, n_out, n_scr = len(arrays), len(args), len(out_shape), len(scratch_shapes)
    hbm = pl.BlockSpec(memory_space=pl.ANY)
    land_shape = [jax.ShapeDtypeStruct((8,) + (a.shape if s else a.shape[1:]), a.dtype) for a, s in zip(arrays, same)]

    def wrapped(*refs):
        ins, ride_in = refs[:n_in], refs[n_in:n_in + n]
        outs, ride_out = refs[n_in + n:n_in + n + n_out], refs[n_in + n + n_out:n_in + 2 * n + n_out]
        scr, sems = refs[n_in + 2 * n + n_out:n_in + 2 * n + n_out + n_scr], refs[n_in + 2 * n + n_out + n_scr:]
        first, last = True, True
        for axis, size in enumerate(grid):
            first = jnp.logical_and(first, pl.program_id(axis) == 0)
            last = jnp.logical_and(last, pl.program_id(axis) == size - 1)
        if grid:
            pl.when(first)(lambda: _ride_start(ride_in, ride_out, same, sems))
        else:
            _ride_start(ride_in, ride_out, same, sems)
        body(*ins, *outs, *scr)
        if grid:
            pl.when(last)(lambda: _ride_wait(ride_in, ride_out, same, sems))
        else:
            _ride_wait(ride_in, ride_out, same, sems)

    sems = [pltpu.SemaphoreType.DMA((n, N_PEER)), pltpu.SemaphoreType.DMA((n, N_PEER)), pltpu.SemaphoreType.DMA((n,))]
    cparams["dimension_semantics"] = ("arbitrary",) * len(grid)
    res = pl.pallas_call(wrapped, grid=grid, in_specs=list(in_specs) + [hbm] * n, out_specs=out_specs + [hbm] * n,
                         out_shape=out_shape + land_shape, scratch_shapes=scratch_shapes + sems, name=name,
                         compiler_params=_cparams(**cparams))(*args, *arrays)
    return list(res[:n_out]), list(res[n_out:])


def exchange(name, ride):
    return call_with_ride(lambda: None, name=name, grid=(), in_specs=[], args=[], out_specs=[], out_shape=[], ride=ride)[1]


def _row_tile(r, mult, cap):
    best = None
    for t in range(mult, min(r, cap) + 1, mult):
        if r % t == 0:
            best = t
    return best if best is not None else r


def adamw(name, parts, w, m, v):
    r, c = w.shape
    tr = _row_tile(r, 16, 256)
    spec = pl.BlockSpec((tr, c), lambda i: (i, 0))

    def body(p_ref, w_ref, m_ref, v_ref, g_ref, d_ref, m2_ref, v2_ref):
        g = p_ref[0].astype(F32)
        for dev in range(1, 8):
            g = g + p_ref[dev].astype(F32)
        m2 = ADAM_B1 * m_ref[...] + (1.0 - ADAM_B1) * g
        v2 = ADAM_B2 * v_ref[...] + (1.0 - ADAM_B2) * jnp.square(g)
        m_hat = m2 / (1.0 - ADAM_B1 ** ADAM_STEP)
        v_hat = v2 / (1.0 - ADAM_B2 ** ADAM_STEP)
        g_ref[...] = g
        d_ref[...] = -ADAM_LR * (m_hat / (jnp.sqrt(v_hat) + ADAM_EPS) + ADAM_WD * w_ref[...])
        m2_ref[...] = m2
        v2_ref[...] = v2

    return pl.pallas_call(
        body, grid=(r // tr,), in_specs=[pl.BlockSpec((8, tr, c), lambda i: (0, i, 0)), spec, spec, spec],
        out_specs=[spec] * 4, out_shape=[jax.ShapeDtypeStruct((r, c), F32)] * 4, name=name, compiler_params=_cparams())(parts, w, m, v)


PACK_ROWS = 256


def _pack(arrs):
    flat = jnp.concatenate([a.reshape(-1).astype(F32) for a in arrs])
    quantum = PACK_ROWS * LANES
    total = -(-flat.shape[0] // quantum) * quantum
    return jnp.pad(flat, (0, total - flat.shape[0])).reshape(-1, LANES)


def _unpack(packed, shapes):
    flat = packed.reshape(-1)
    out, off = [], 0
    for s in shapes:
        n = int(np.prod(s))
        out.append(flat[off:off + n].reshape(s))
        off += n
    return out


def _pad_lanes(vec):
    return jnp.pad(vec.astype(F32), (0, LANES - vec.shape[0]))[None, :]


def make_step(cfg):
    I, BC, H, RW, SW, LW, F, D = (cfg.ssd_inner, cfg.ssd_bc, cfg.ssd_heads, cfg.ret_w, cfg.sb_w, cfg.lru_width,
                                  cfg.ffn_dim, cfg.d_model)
    ND = cfg.n_dev
    q_off = 2 * I + 2 * BC
    dt_off = q_off + 4 * RW
    dt_w = (-dt_off) % 1024 or 1024

    def gather_cols(g):
        return jnp.transpose(g, (1, 0, 2)).reshape(g.shape[1], -1)

    def scatter_cols(full):
        r, c = full.shape
        return jnp.transpose(full.reshape(r, ND, c // ND), (1, 0, 2))

    def step(p, m_in, v_in, x, loss_target):
        nb = x.shape[0]
        rows = nb * cfg.tp

        small_shapes = [p[n].shape for n in SMALL_SHARDED]
        wfull = {}

        halves = {}

        def gather_ride(names, extra=()):
            own = []
            for n in names:
                if isinstance(n, tuple):
                    half = p[n[0]].shape[0] // 2
                    own.append(p[n[0]][n[1] * half:(n[1] + 1) * half].astype(BF16))
                else:
                    own.append(p[n].astype(BF16))
            own += list(extra)
            return Ride(own, [True] * len(own))

        def gather_finish(names, landed):
            for n, g in zip(names, landed):
                if isinstance(n, tuple):
                    halves.setdefault(n[0], {})[n[1]] = g
                    if len(halves[n[0]]) < 2:
                        continue
                    n, g = n[0], jnp.concatenate([halves[n[0]][0], halves[n[0]][1]], axis=1)
                wfull[n] = gather_cols(g) if n in BIG_COL else g.reshape(-1, g.shape[-1])

        landed = exchange("gather_l0_mixer", gather_ride(['l0_w_in'], [_pack([p[n] for n in SMALL_SHARDED])]))
        gather_finish(['l0_w_in'], landed)
        small_parts = [_unpack(landed[-1][d], small_shapes) for d in range(ND)]
        sfull = {n: jnp.concatenate([small_parts[d][k] for d in range(ND)], axis=1) for k, n in enumerate(SMALL_SHARDED)}

        w0 = wfull['l0_w_in']
        w0a = jnp.concatenate([w0[:, :q_off], w0[:, q_off + H:], jnp.pad(w0[:, q_off:q_off + H], ((0, 0), (0, dt_w - H)))], axis=1)

        sel_hd = (jnp.arange(LANES)[:, None] == (jnp.arange(I)[None, :] // cfg.ssd_head_dim)).astype(F32)
        sel_lane = (jnp.arange(LANES)[:, None] == (jnp.arange(H * LANES)[None, :] // LANES)).astype(F32)
        rconst = _ret_consts(cfg)
        row2 = lambda vec: vec.astype(F32)[None, :]

        meta = jnp.broadcast_to(sfull['meta_tokens'][None], (nb, cfg.n_meta, D))
        h0 = jnp.concatenate([jnp.zeros((nb, cfg.pad, D), F32), meta, x], axis=1).reshape(rows, D)

        light = light_rows(rows)

        def norm_fwd(name, h, g):
            return stage_fwd(name, cfg, _norm_fn, [Row(h, D)], [Par(row2(g))], [(D, BF16)], tr=light)[0]

        def norm_bwd(name, h, g, dhn, dh_next):
            fn = lambda valid, hv, gv: (_norm_fn(valid, hv, gv)[0], hv)
            (dh,), (dg,) = stage_bwd(name, cfg, fn, [Row(h, D)], [Par(row2(g))], [[Row(dhn, D)], [Row(dh_next, D)]], [F32],
                                     tr=light)
            return dh, dg[0]

        hn0 = norm_fwd("l0_mix_norm", h0, p['l0_mix_norm'])
        u0, landed = matmul("l0_in_proj", hn0, w0a, ride=gather_ride(['l0_w_out']))
        gather_finish(['l0_w_out'], landed)
        cw, cb = sfull['l0_ssd_conv_w'], p['l0_ssd_conv_b']
        ssd_pre = _make_ssd_pre(cfg)
        ssd_post = _make_ssd_post(cfg)
        pre_rows = [Row(u0, I, 1, tail=True), Row(u0, BC, (2 * I) // BC, tail=True), Row(u0, BC, (2 * I) // BC + 1, tail=True),
                    Row(u0, LANES, dt_off // LANES)]
        pre_pars = [Par(cw[:, :I]), Par(cw[:, I:I + BC]), Par(cw[:, I + BC:]), Par(row2(cb[:I])), Par(row2(cb[I:I + BC])),
                    Par(row2(cb[I + BC:])), Par(_pad_lanes(p['l0_ssd_dt_bias'])), Par(_pad_lanes(p['l0_ssd_a_log'])),
                    Par(sel_hd, diff=False), Par(sel_lane, diff=False)]
        (xs, bm, cm, xdt, adt_b), landed = stage_fwd("ssd_pre", cfg, ssd_pre, pre_rows, pre_pars,
                                                     [(I, F32), (BC, F32), (BC, F32), (I, F32), (H * LANES, F32)],
                                                     ride=gather_ride([('l0_ffn_w_in', 0)]))
        gather_finish([('l0_ffn_w_in', 0)], landed)
        y_raw, ssd_states, landed = ssd_scan_fwd(cfg, xdt, bm, cm, adt_b, ride=gather_ride([('l0_ffn_w_in', 1)]))
        gather_finish([('l0_ffn_w_in', 1)], landed)
        post_rows = [Row(y_raw, I), Row(xs, I), Row(u0, I, 0)]
        post_pars = [Par(_pad_lanes(p['l0_ssd_d'])), Par(row2(p['l0_ssd_norm'])), Par(sel_hd, diff=False)]
        (y_ssd,) = stage_fwd("ssd_post", cfg, ssd_post, post_rows, post_pars, [(I, BF16)], tr=light)
        ret_g = row2(p['l0_ret_norm'])
        y_ret, landed = ret_fwd(cfg, u0, q_off, ret_g, rconst, ride=gather_ride(['l0_ffn_w_out']))
        gather_finish(['l0_ffn_w_out'], landed)
        ycat0 = jnp.concatenate([y_ssd, y_ret], axis=1)
        h1 = matmul("l0_out_proj", ycat0, wfull['l0_w_out'], resid=h0, cfg=cfg)

        def ffn_fwd(tag, h, norm_g, w_in, conv_w, conv_b, w_out, gather_names=(), act_gather_names=()):
            hn = norm_fwd(tag + "_ffn_norm", h, norm_g)
            if gather_names:
                u, landed = matmul(tag + "_ffn_in", hn, w_in, ride=gather_ride(list(gather_names)))
                gather_finish(gather_names, landed)
            else:
                u = matmul(tag + "_ffn_in", hn, w_in)
            rws = [Row(u, F, 0, tail=True), Row(u, F, 1, tail=True)]
            prs = [Par(conv_w[:, :F]), Par(conv_w[:, F:]), Par(row2(conv_b[:F])), Par(row2(conv_b[F:]))]
            if act_gather_names:
                (act,), landed = stage_fwd(tag + "_ffn_act", cfg, _ffn_act_fn, rws, prs, [(F, BF16)],
                                           ride=gather_ride(list(act_gather_names)))
                gather_finish(act_gather_names, landed)
            else:
                (act,) = stage_fwd(tag + "_ffn_act", cfg, _ffn_act_fn, rws, prs, [(F, BF16)])
            h_out = matmul(tag + "_ffn_out", act, w_out, resid=h, cfg=cfg)
            return h_out, (hn, u, rws, prs, act)

        h2, ffn0_saved = ffn_fwd("l0", h1, p['l0_ffn_norm'], wfull['l0_ffn_w_in'], sfull['l0_ffn_conv_w'], p['l0_ffn_conv_b'],
                                 wfull['l0_ffn_w_out'], gather_names=[('l1_w_in', 0)], act_gather_names=[('l1_w_in', 1)])

        hn2 = norm_fwd("l1_mix_norm", h2, p['l1_mix_norm'])
        u1 = matmul("l1_in_proj", hn2, wfull['l1_w_in'])
        late_names = ['l1_w_out', 'l1_ffn_w_in', 'l1_ffn_w_out']
        y_sb, sb_tot, landed = sb_fwd(cfg, u1, 0, ride=gather_ride(late_names))
        gather_finish(late_names, landed)
        lru_pre = _make_lru_pre(cfg)
        gate_blk = (3 * SW) // LW
        lpre_rows = [Row(u1, LW, gate_blk + 1, tail=True)]
        lpre_pars = [Par(sfull['l1_lru_conv_w']), Par(row2(p['l1_lru_conv_b'])), Par(p['l1_lru_wa']), Par(row2(p['l1_lru_ba'])),
                     Par(p['l1_lru_wx']), Par(row2(p['l1_lru_bx'])), Par(row2(p['l1_lru_lambda']))]
        lru_a, lru_b = stage_fwd("lru_pre", cfg, lru_pre, lpre_rows, lpre_pars, [(LW, F32), (LW, F32)])
        lru_h = lru_scan_fwd(cfg, lru_a, lru_b)
        lpost_rows = [Row(lru_h, LW), Row(u1, LW, gate_blk)]
        (y_lru,) = stage_fwd("lru_post", cfg, _lru_post_fn, lpost_rows, [], [(LW, BF16)], tr=light)
        ycat1 = jnp.concatenate([y_sb, y_lru], axis=1)
        h3 = matmul("l1_out_proj", ycat1, wfull['l1_w_out'], resid=h2, cfg=cfg)
        h4, ffn1_saved = ffn_fwd("l1", h3, p['l1_ffn_norm'], wfull['l1_ffn_w_in'], sfull['l1_ffn_conv_w'], p['l1_ffn_conv_b'],
                                 wfull['l1_ffn_w_out'])

        loss_part, dh4, d_final = final_loss(cfg, h4, loss_target.reshape(nb * cfg.seq, D), row2(p['final_norm']))
        loss = lax.psum(loss_part[0, 0], ("x", "y", "c"))
        gfull, grep = {}, {'final_norm': d_final[0]}

        def ffn_bwd(tag, h, norm_g, w_in, w_out, saved, dh_out, ride_names=(), act_ride_names=()):
            hn, u, rws, prs, act = saved
            dact = matmul(tag + "_ffn_out_dx", dh_out, w_out, tb=True, out_dtype=BF16)
            gfull[tag + '_ffn_w_out'] = matmul(tag + "_ffn_out_dw", act, dh_out, ta=True, out_dtype=BF16, tm_cap=512, tn_cap=512, tk_cap=4352)
            if act_ride_names:
                (dug, duu), (dwg, dwu, dbg, dbu), landed = stage_bwd(tag + "_ffn_act_bwd", cfg, _ffn_act_fn, rws, prs,
                                                                     [[Row(dact, F)]], [BF16, BF16], tr=HALF_ROWS,
                                                                     ride=grads_ride(list(act_ride_names)))
                parts_of.update(zip(act_ride_names, landed))
            else:
                (dug, duu), (dwg, dwu, dbg, dbu) = stage_bwd(tag + "_ffn_act_bwd", cfg, _ffn_act_fn, rws, prs,
                                                             [[Row(dact, F)]], [BF16, BF16], tr=HALF_ROWS)
            du = jnp.concatenate([dug, duu], axis=1)
            gfull[tag + '_ffn_conv_w'] = jnp.concatenate([dwg, dwu], axis=1)
            grep[tag + '_ffn_conv_b'] = jnp.concatenate([dbg, dbu], axis=1)[0]
            if ride_names:
                dhn, landed = matmul(tag + "_ffn_in_dx", du, w_in, tb=True, ride=grads_ride(list(ride_names)))
                parts_of.update(zip(ride_names, landed))
            else:
                dhn = matmul(tag + "_ffn_in_dx", du, w_in, tb=True)
            gfull[tag + '_ffn_w_in'] = matmul(tag + "_ffn_in_dw", hn, du, ta=True, out_dtype=BF16, tm_cap=512, tn_cap=512, tk_cap=4352)
            dh, grep[tag + '_ffn_norm'] = norm_bwd(tag + "_ffn_norm_bwd", h, norm_g, dhn, dh_out)
            return dh

        parts_of = {}

        def grads_ride(names, extra=(), extra_same=()):
            sends = [scatter_cols(gfull[n]) if n in BIG_COL else gfull[n].reshape(ND, -1, gfull[n].shape[-1]) for n in names]
            return Ride(sends + list(extra), [False] * len(names) + list(extra_same))

        dh3 = ffn_bwd("l1", h3, p['l1_ffn_norm'], wfull['l1_ffn_w_in'], wfull['l1_ffn_w_out'], ffn1_saved, dh4)

        dycat1 = matmul("l1_out_dx", dh3, wfull['l1_w_out'], tb=True, out_dtype=BF16)
        gfull['l1_w_out'] = matmul("l1_out_dw", ycat1, dh3, ta=True, out_dtype=BF16, tm_cap=512, tn_cap=512, tk_cap=4352)
        (d_lru_h, d_gate), _ = stage_bwd("lru_post_bwd", cfg, _lru_post_fn, lpost_rows, [], [[Row(dycat1, LW, SW // LW)]], [F32, BF16],
                                         tr=light)
        d_lru_a, d_lru_b = lru_scan_bwd(cfg, lru_a, lru_h, d_lru_h)
        (d_xr,), lgr = stage_bwd("lru_pre_bwd", cfg, lru_pre, lpre_rows, lpre_pars, [[Row(d_lru_a, LW)], [Row(d_lru_b, LW)]], [BF16])
        gfull['l1_lru_conv_w'] = lgr[0]
        grep.update({'l1_lru_conv_b': lgr[1][0], 'l1_lru_wa': lgr[2], 'l1_lru_ba': lgr[3][0], 'l1_lru_wx': lgr[4],
                     'l1_lru_bx': lgr[5][0], 'l1_lru_lambda': lgr[6][0]})
        names = ['l1_ffn_w_out', 'l1_ffn_w_in', 'l1_w_out']
        early = _pack([grep[n] for n in REP_EARLY])
        (d_q, d_k, d_v), landed = sb_bwd(cfg, u1, 0, sb_tot, dycat1, 0, ride=grads_ride(names, [early], [True]))
        parts_of.update(zip(names + ['rep_early'], landed))
        du1 = jnp.concatenate([d_q, d_k, d_v, d_gate, d_xr], axis=1)
        gfull['l1_w_in'] = matmul("l1_in_dw", hn2, du1, ta=True, out_dtype=BF16, tm_cap=512, tn_cap=512, tk_cap=4352)
        dhn2 = matmul("l1_in_dx", du1, wfull['l1_w_in'], tb=True)
        dh2, grep['l1_mix_norm'] = norm_bwd("l1_mix_norm_bwd", h2, p['l1_mix_norm'], dhn2, dh3)

        dh1 = ffn_bwd("l0", h1, p['l0_ffn_norm'], wfull['l0_ffn_w_in'], wfull['l0_ffn_w_out'], ffn0_saved, dh2,
                      ride_names=['l0_ffn_w_out'], act_ride_names=['l1_w_in'])

        dycat0 = matmul("l0_out_dx", dh1, wfull['l0_w_out'], tb=True, out_dtype=BF16)
        gfull['l0_w_out'] = matmul("l0_out_dw", ycat0, dh1, ta=True, out_dtype=BF16, tm_cap=512, tn_cap=512, tk_cap=4352)
        (d_yraw, d_xs1, d_z), (d_dskip, d_ssdnorm) = stage_bwd("ssd_post_bwd", cfg, ssd_post, post_rows, post_pars,
                                                               [[Row(dycat0, I, 0)]], [F32, F32, BF16])
        names = ['l0_ffn_w_in', 'l0_w_out']
        (d_xdt, d_bm, d_cm, d_adt), landed = ssd_scan_bwd(cfg, xdt, bm, cm, adt_b, ssd_states, d_yraw, ride=grads_ride(names))
        parts_of.update(zip(names, landed))
        (d_xs, d_b, d_c, d_dt), sgr = stage_bwd("ssd_pre_bwd", cfg, ssd_pre, pre_rows, pre_pars,
                                                 [[Row(d_xs1, I)], [Row(d_bm, BC)], [Row(d_cm, BC)], [Row(d_xdt, I)], [Row(d_adt, H * LANES)]],
                                                 [BF16] * 4)
        gfull['l0_ssd_conv_w'] = jnp.concatenate(sgr[0:3], axis=1)
        grep.update({'l0_ssd_conv_b': jnp.concatenate(sgr[3:6], axis=1)[0], 'l0_ssd_dt_bias': sgr[6][0, :H],
                     'l0_ssd_a_log': sgr[7][0, :H], 'l0_ssd_d': d_dskip[0, :H], 'l0_ssd_norm': d_ssdnorm[0]})
        (d_rq, d_rk, d_rv, d_rg), d_retnorm = ret_bwd(cfg, u0, q_off, ret_g, rconst, dycat0, I)
        grep['l0_ret_norm'] = d_retnorm[0]
        du0 = jnp.concatenate([d_z, d_xs, d_b, d_c, d_rq, d_rk, d_rv, d_rg, d_dt, jnp.zeros((rows, dt_w - LANES), BF16)], axis=1)
        dw0a = matmul("l0_in_dw", hn0, du0, ta=True, out_dtype=BF16, tm_cap=512, tn_cap=512, tk_cap=4352)
        gfull['l0_w_in'] = jnp.concatenate([dw0a[:, :q_off], dw0a[:, dt_off:dt_off + H], dw0a[:, q_off:dt_off]], axis=1)
        dhn0, landed = matmul("l0_in_dx", du0, w0a, tb=True, ride=grads_ride(['l0_w_in']))
        parts_of['l0_w_in'] = landed[0]
        dh0, grep['l0_mix_norm'] = norm_bwd("l0_mix_norm_bwd", h0, p['l0_mix_norm'], dhn0, dh1)

        dh0 = dh0.reshape(nb, cfg.tp, D)
        grad_x = dh0[:, cfg.chunk:, :]
        gfull['meta_tokens'] = jnp.sum(dh0[:, cfg.pad:cfg.chunk, :], axis=0)

        flat = jnp.concatenate([scatter_cols(gfull[n].astype(F32)).reshape(ND, -1) for n in SMALL_SHARDED], axis=1)
        quantum = PACK_ROWS * LANES
        small_send = jnp.pad(flat, ((0, 0), (0, -flat.shape[1] % quantum))).reshape(ND, -1, LANES)
        late = _pack([grep[n] for n in REP_LATE])
        landed = exchange("exchange_last_grads", Ride([small_send, late], [False, True]))
        parts_of.update(zip(['small', 'rep_late'], landed))

        grad, delta, new_m, new_v = {}, {}, {}, {}
        for n in BIG:
            grad[n], delta[n], new_m[n], new_v[n] = adamw("adamw_" + n, parts_of[n], p[n], m_in[n], v_in[n])
        for names, tag in ((SMALL_SHARDED, 'small'), (REP_EARLY, 'rep_early'), (REP_LATE, 'rep_late')):
            shapes = [p[n].shape for n in names]
            res = adamw("adamw_" + tag, parts_of[tag], _pack([p[n] for n in names]), _pack([m_in[n] for n in names]),
                        _pack([v_in[n] for n in names]))
            for dst, packed in zip((grad, delta, new_m, new_v), res):
                for n, a in zip(names, _unpack(packed, shapes)):
                    dst[n] = a
        return (loss, grad_x, *[grad[n] for n in WEIGHTS], *[delta[n] for n in WEIGHTS], *[new_m[n] for n in WEIGHTS],
                *[new_v[n] for n in WEIGHTS])

    return step


_STEP = make_step(CFG)


def kernel(x, meta_tokens, l0_mix_norm, l0_w_in, l0_ssd_conv_w, l0_ssd_conv_b, l0_ssd_dt_bias, l0_ssd_a_log, l0_ssd_d, l0_ssd_norm, l0_ret_norm, l0_w_out, l0_ffn_norm, l0_ffn_w_in, l0_ffn_conv_w, l0_ffn_conv_b, l0_ffn_w_out, l1_mix_norm, l1_w_in, l1_lru_conv_w, l1_lru_conv_b, l1_lru_wa, l1_lru_ba, l1_lru_wx, l1_lru_bx, l1_lru_lambda, l1_w_out, l1_ffn_norm, l1_ffn_w_in, l1_ffn_conv_w, l1_ffn_conv_b, l1_ffn_w_out, final_norm, loss_target, m_meta_tokens, m_l0_mix_norm, m_l0_w_in, m_l0_ssd_conv_w, m_l0_ssd_conv_b, m_l0_ssd_dt_bias, m_l0_ssd_a_log, m_l0_ssd_d, m_l0_ssd_norm, m_l0_ret_norm, m_l0_w_out, m_l0_ffn_norm, m_l0_ffn_w_in, m_l0_ffn_conv_w, m_l0_ffn_conv_b, m_l0_ffn_w_out, m_l1_mix_norm, m_l1_w_in, m_l1_lru_conv_w, m_l1_lru_conv_b, m_l1_lru_wa, m_l1_lru_ba, m_l1_lru_wx, m_l1_lru_bx, m_l1_lru_lambda, m_l1_w_out, m_l1_ffn_norm, m_l1_ffn_w_in, m_l1_ffn_conv_w, m_l1_ffn_conv_b, m_l1_ffn_w_out, m_final_norm, v_meta_tokens, v_l0_mix_norm, v_l0_w_in, v_l0_ssd_conv_w, v_l0_ssd_conv_b, v_l0_ssd_dt_bias, v_l0_ssd_a_log, v_l0_ssd_d, v_l0_ssd_norm, v_l0_ret_norm, v_l0_w_out, v_l0_ffn_norm, v_l0_ffn_w_in, v_l0_ffn_conv_w, v_l0_ffn_conv_b, v_l0_ffn_w_out, v_l1_mix_norm, v_l1_w_in, v_l1_lru_conv_w, v_l1_lru_conv_b, v_l1_lru_wa, v_l1_lru_ba, v_l1_lru_wx, v_l1_lru_bx, v_l1_lru_lambda, v_l1_w_out, v_l1_ffn_norm, v_l1_ffn_w_in, v_l1_ffn_conv_w, v_l1_ffn_conv_b, v_l1_ffn_w_out, v_final_norm):
    args = locals()
    p = {n: args[n] for n in WEIGHTS}
    m_in = {n: args["m_" + n] for n in WEIGHTS}
    v_in = {n: args["v_" + n] for n in WEIGHTS}
    return _STEP(p, m_in, v_in, x, loss_target)
```
